```python
import math
import jax, jax.numpy as jnp
from jax import lax
import numpy as np

D_MODEL = 1024
BATCH = 32
SEQ = 2048
DEPTH = 2

DEEPNORM_ALPHA = (2 * DEPTH) ** 0.25
DEEPNORM_BETA = (8 * DEPTH) ** -0.25
LN_EPS = 1e-5
NEG_INF = -1e30

RET_HEADS = 4
RET_HEAD_DIM = D_MODEL // (2 * RET_HEADS)
RET_CHUNK = 128
ROPE_BASE = 10000.0

NSA_HEADS = 8
NSA_KV_HEADS = 2
NSA_GROUP = NSA_HEADS // NSA_KV_HEADS
NSA_HEAD_DIM = D_MODEL // (2 * NSA_HEADS)
CMP_BLOCK = 32
CMP_STRIDE = 16
CMP_HIDDEN = 4 * NSA_HEAD_DIM
SEL_BLOCK = 64
SEL_TOP_N = 16
SEL_QUERY_CHUNK = 16
SEL_FORCE_SCORE = 1e4
WIN_SIZE = 512
ATTN_BLOCK = 128

DIL_PATTERNS = ((128, 1), (512, 4), (2048, 16))
DIL_HEADS = 8
DIL_HEAD_DIM = D_MODEL // DIL_HEADS

REL_BUCKETS = 32
REL_MAX_DIST = 128
BIAS_HEADS = 8

D_FF = -(-8 * D_MODEL // 768) * 256

RET_W = RET_HEADS * RET_HEAD_DIM
NSA_QW = NSA_HEADS * NSA_HEAD_DIM
NSA_KVW = NSA_KV_HEADS * NSA_HEAD_DIM
AB_SPLITS = (RET_W, RET_W, RET_W, RET_W, NSA_QW) + (NSA_KVW,) * 6 + (3 * NSA_HEADS,)
AB_IN_COLS = sum(AB_SPLITS)
AB_OUT_COLS = RET_W + NSA_QW
DIL_WIDTH = DIL_HEADS * DIL_HEAD_DIM
DIL_IN_COLS = len(DIL_PATTERNS) * 3 * DIL_WIDTH

kernel_name = 'hybrid_retnet_nsa_dilated_trunk'


def layer_norm(x, g, b):
    xf = x.astype(jnp.float32)
    mu = jnp.mean(xf, axis=-1, keepdims=True)
    var = jnp.mean(jnp.square(xf - mu), axis=-1, keepdims=True)
    return ((xf - mu) * lax.rsqrt(var + LN_EPS) * g + b).astype(x.dtype)


def masked_softmax(s, valid):
    s = jnp.where(valid, s, NEG_INF)
    m = jnp.max(s, axis=-1, keepdims=True)
    e = jnp.where(valid, jnp.exp(s - m), 0.0)
    den = jnp.maximum(jnp.sum(e, axis=-1, keepdims=True), 1e-30)
    return e / den, (m + jnp.log(den))[..., 0]


def t5_bucket(dist):
    n = jnp.maximum(dist, 0)
    max_exact = REL_BUCKETS // 2
    nf = jnp.maximum(n, 1).astype(jnp.float32)
    large = max_exact + (jnp.log(nf / max_exact) / math.log(REL_MAX_DIST / max_exact)
                         * (REL_BUCKETS - max_exact)).astype(jnp.int32)
    large = jnp.minimum(large, REL_BUCKETS - 1)
    return jnp.where(n < max_exact, n, large)


def rope(t, pos):
    d = t.shape[-1]
    inv = ROPE_BASE ** (-jnp.arange(0, d, 2, dtype=jnp.float32) / d)
    ang = pos.astype(jnp.float32)[:, None] * inv[None, :]
    cos, sin = jnp.cos(ang), jnp.sin(ang)
    t1, t2 = t[..., : d // 2], t[..., d // 2:]
    return jnp.concatenate([t1 * cos - t2 * sin, t1 * sin + t2 * cos], axis=-1)


def split_heads(t, h):
    B, S, _ = t.shape
    return t.reshape(B, S, h, -1).transpose(0, 2, 1, 3)


def retention(q, k, v):
    B, H, S, dk = q.shape
    dv = v.shape[-1]
    C = RET_CHUNK
    N = S // C
    log_gamma = jnp.log1p(-jnp.exp2(-5.0 - jnp.arange(H, dtype=jnp.float32)))
    idx = jnp.arange(C, dtype=jnp.float32)
    diff = idx[:, None] - idx[None, :]
    inner_decay = jnp.where(diff >= 0, jnp.exp(log_gamma[:, None, None] * jnp.maximum(diff, 0.0)), 0.0)
    qc = (q * dk ** -0.5).reshape(B, H, N, C, dk)
    kc = k.reshape(B, H, N, C, dk)
    vc = v.reshape(B, H, N, C, dv)
    scores = jnp.einsum('bhncd,bhnkd->bhnck', qc, kc) * inner_decay[None, :, None]
    inner = jnp.einsum('bhnck,bhnke->bhnce', scores, vc)
    k_decay = jnp.exp(log_gamma[:, None] * (C - 1 - idx)[None, :])
    kv = jnp.einsum('bhnkd,bhnke->bhnde', kc * k_decay[None, :, None, :, None], vc)
    chunk_decay = jnp.exp(log_gamma * C)[None, :, None, None]

    def step(state, kv_n):
        return state * chunk_decay + kv_n, state

    _, prev = lax.scan(step, jnp.zeros((B, H, dk, dv), jnp.float32), jnp.moveaxis(kv, 2, 0))
    q_decay = jnp.exp(log_gamma[:, None] * (idx + 1.0)[None, :])
    cross = jnp.einsum('bhncd,nbhde->bhnce', qc * q_decay[None, :, None, :, None], prev)
    return (inner + cross).reshape(B, H, S, dv)


def banded_attention(q, k, v, max_dist, dist_scale, rel_bias):
    N, G, R, L, dh = q.shape
    scale = dh ** -0.5
    pad_end = (-L) % ATTN_BLOCK
    Lp = L + pad_end
    pad_front = -(-max_dist // ATTN_BLOCK) * ATTN_BLOCK
    span = pad_front + ATTN_BLOCK
    qp = jnp.pad(q, ((0, 0), (0, 0), (0, 0), (0, pad_end), (0, 0)))
    kp = jnp.pad(k, ((0, 0), (0, 0), (pad_front, pad_end), (0, 0)))
    vp = jnp.pad(v, ((0, 0), (0, 0), (pad_front, pad_end), (0, 0)))

    def block(i):
        start = i * ATTN_BLOCK
        qb = lax.dynamic_slice_in_dim(qp, start, ATTN_BLOCK, axis=3)
        kb = lax.dynamic_slice_in_dim(kp, start, span, axis=2)
        vb = lax.dynamic_slice_in_dim(vp, start, span, axis=2)
        qi = start + jnp.arange(ATTN_BLOCK)
        ki = start - pad_front + jnp.arange(span)
        dist = qi[:, None] - ki[None, :]
        valid = (dist >= 0) & (dist <= max_dist) & (ki[None, :] >= 0)
        bias = rel_bias[t5_bucket(dist * dist_scale)].reshape(ATTN_BLOCK, span, G, R).transpose(2, 3, 0, 1)
        s = jnp.einsum('ngrqd,ngkd->ngrqk', qb, kb).astype(jnp.float32) * scale + bias
        p, lse = masked_softmax(s, valid)
        return jnp.einsum('ngrqk,ngkd->ngrqd', p, vb.astype(jnp.float32)), lse

    o, lse = lax.map(block, jnp.arange(Lp // ATTN_BLOCK))
    o = jnp.moveaxis(o, 0, 3).reshape(N, G, R, Lp, dh)[..., :L, :]
    lse = jnp.moveaxis(lse, 0, 3).reshape(N, G, R, Lp)[..., :L]
    return o, lse


def nsa_attention(q, k_cmp, v_cmp, k_slc, v_slc, k_win, v_win, gates,
                  cmp_pos_k, cmp_pos_v, cmp_k_w1, cmp_k_w2, cmp_v_w1, cmp_v_w2, rel_bias):
    B, G, R, S, dh = q.shape
    scale = dh ** -0.5
    qpos = jnp.arange(S)

    n_cmp = (S - CMP_BLOCK) // CMP_STRIDE + 1
    cmp_start = jnp.arange(n_cmp) * CMP_STRIDE
    win_idx = cmp_start[:, None] + jnp.arange(CMP_BLOCK)[None, :]

    def compress(t, pos_emb, w1, w2):
        blocks = t[:, :, win_idx] + pos_emb
        return jax.nn.silu(blocks.reshape(B, G, n_cmp, CMP_BLOCK * dh) @ w1) @ w2

    kc = compress(k_cmp, cmp_pos_k, cmp_k_w1, cmp_k_w2)
    vc = compress(v_cmp, cmp_pos_v, cmp_v_w1, cmp_v_w2)
    cmp_end = cmp_start + CMP_BLOCK - 1
    cmp_valid = cmp_end[None, :] <= qpos[:, None]
    cmp_bias = rel_bias[t5_bucket(qpos[:, None] - cmp_end[None, :])].reshape(S, n_cmp, G, R).transpose(2, 3, 0, 1)
    s = jnp.einsum('bgrqd,bgcd->bgrqc', q, kc).astype(jnp.float32) * scale + cmp_bias
    p_cmp, _ = masked_softmax(s, cmp_valid)
    o_cmp = jnp.einsum('bgrqc,bgcd->bgrqd', p_cmp, vc.astype(jnp.float32))

    n_sel = S // SEL_BLOCK
    sel_start = jnp.arange(n_sel) * SEL_BLOCK
    overlap = jnp.clip(jnp.minimum(cmp_start[:, None] + CMP_BLOCK, sel_start[None, :] + SEL_BLOCK)
                       - jnp.maximum(cmp_start[:, None], sel_start[None, :]), 0, None).astype(jnp.float32) / CMP_BLOCK
    imp = jnp.einsum('bgrqc,cj->bgqj', p_cmp, overlap)
    blk_q = qpos // SEL_BLOCK
    j = jnp.arange(n_sel)
    sel_valid = j[None, :] <= blk_q[:, None]
    forced = (j[None, :] == 0) | (j[None, :] == blk_q[:, None]) | (j[None, :] == blk_q[:, None] - 1)
    sel_score = jnp.where(forced, SEL_FORCE_SCORE, jnp.where(sel_valid, imp, -1.0))
    top_n = min(SEL_TOP_N, n_sel)
    _, sel_idx = lax.top_k(sel_score, top_n)

    kb = k_slc.reshape(B, G, n_sel, SEL_BLOCK * dh)
    vb = v_slc.reshape(B, G, n_sel, SEL_BLOCK * dh)
    bias_gt = rel_bias.reshape(REL_BUCKETS, G, R).transpose(1, 0, 2)
    g_arr = jnp.arange(G)[None, :, None, None]
    Qc = SEL_QUERY_CHUNK
    n_keys = top_n * SEL_BLOCK

    def sel_chunk(i):
        start = i * Qc
        qc = lax.dynamic_slice_in_dim(q, start, Qc, axis=3)
        ic = lax.dynamic_slice_in_dim(sel_idx, start, Qc, axis=2)
        flat = ic.reshape(B, G, Qc * top_n)[..., None]
        kg = jnp.take_along_axis(kb, flat, axis=2).reshape(B, G, Qc, n_keys, dh)
        vg = jnp.take_along_axis(vb, flat, axis=2).reshape(B, G, Qc, n_keys, dh)
        kpos = (ic[..., None] * SEL_BLOCK + jnp.arange(SEL_BLOCK)).reshape(B, G, Qc, n_keys)
        qp = (start + jnp.arange(Qc))[None, None, :, None]
        valid = (kpos <= qp)[:, :, None]
        bias = jnp.moveaxis(bias_gt[g_arr, t5_bucket(qp - kpos)], -1, 2)
        s = jnp.einsum('bgrqd,bgqkd->bgrqk', qc, kg).astype(jnp.float32) * scale + bias
        p, _ = masked_softmax(s, valid)
        return jnp.einsum('bgrqk,bgqkd->bgrqd', p, vg.astype(jnp.float32))

    o_slc = lax.map(sel_chunk, jnp.arange(S // Qc))
    o_slc = jnp.moveaxis(o_slc, 0, 3).reshape(B, G, R, S, dh)

    o_win, _ = banded_attention(q, k_win, v_win, WIN_SIZE - 1, 1, rel_bias)

    return gates[..., 0:1] * o_cmp + gates[..., 1:2] * o_slc + gates[..., 2:3] * o_win


def mixer_retention_nsa(u, w_in, w_out, gn_g, gn_b, cmp_pos_k, cmp_pos_v,
                        cmp_k_w1, cmp_k_w2, cmp_v_w1, cmp_v_w2, rel_bias):
    B, S, _ = u.shape
    proj = u @ w_in
    (q_r, k_r, v_r, g_r, q_n, k_c, v_c, k_s, v_s, k_w, v_w, gates) = jnp.split(
        proj, np.cumsum(AB_SPLITS)[:-1].tolist(), axis=-1)

    pos = jnp.arange(S)
    qh = rope(split_heads(q_r, RET_HEADS).astype(jnp.float32), pos)
    kh = rope(split_heads(k_r, RET_HEADS).astype(jnp.float32), pos)
    vh = split_heads(v_r, RET_HEADS).astype(jnp.float32)
    y = retention(qh, kh, vh)
    mu = jnp.mean(y, axis=-1, keepdims=True)
    var = jnp.mean(jnp.square(y - mu), axis=-1, keepdims=True)
    y = ((y - mu) * lax.rsqrt(var + LN_EPS)).transpose(0, 2, 1, 3).reshape(B, S, RET_W)
    y_ret = (y * gn_g + gn_b) * jax.nn.silu(g_r.astype(jnp.float32))

    qn = q_n.reshape(B, S, NSA_KV_HEADS, NSA_GROUP, NSA_HEAD_DIM).transpose(0, 2, 3, 1, 4)
    kv_heads = lambda t: split_heads(t, NSA_KV_HEADS)
    gt = jax.nn.sigmoid(gates.astype(jnp.float32)).reshape(B, S, NSA_KV_HEADS, NSA_GROUP, 3).transpose(0, 2, 3, 1, 4)
    o = nsa_attention(qn, kv_heads(k_c), kv_heads(v_c), kv_heads(k_s), kv_heads(v_s),
                      kv_heads(k_w), kv_heads(v_w), gt, cmp_pos_k, cmp_pos_v,
                      cmp_k_w1, cmp_k_w2, cmp_v_w1, cmp_v_w2, rel_bias)
    y_nsa = o.transpose(0, 3, 1, 2, 4).reshape(B, S, NSA_QW)

    return jnp.concatenate([y_ret, y_nsa], axis=-1).astype(u.dtype) @ w_out


def dilated_group(q, k, v, window, dilation, rel_bias):
    B, H, S, dh = q.shape
    L = S // dilation
    to_res = lambda t: t.reshape(B, H, L, dilation, dh).transpose(3, 0, 1, 2, 4)
    qr, kr, vr = to_res(q)[:, :, :, None], to_res(k), to_res(v)
    o, lse = lax.map(lambda a: banded_attention(a[0], a[1], a[2], window // dilation, dilation, rel_bias),
                     (qr, kr, vr))
    o = o[:, :, :, 0].transpose(1, 2, 3, 0, 4).reshape(B, H, S, dh)
    lse = lse[:, :, :, 0].transpose(1, 2, 3, 0).reshape(B, H, S)
    return o, lse


def mixer_dilated(u, w_in, w_out, rel_bias):
    B, S, _ = u.shape
    proj = (u @ w_in).reshape(B, S, len(DIL_PATTERNS), 3, DIL_HEADS, DIL_HEAD_DIM)
    outs, lses = [], []
    for gi, (window, dilation) in enumerate(DIL_PATTERNS):
        q, k, v = (proj[:, :, gi, m].transpose(0, 2, 1, 3) for m in range(3))
        o, lse = dilated_group(q, k, v, window, dilation, rel_bias)
        outs.append(o)
        lses.append(lse)
    weight = jax.nn.softmax(jnp.stack(lses), axis=0)
    o = jnp.sum(weight[..., None] * jnp.stack(outs), axis=0)
    return o.transpose(0, 2, 1, 3).reshape(B, S, DIL_WIDTH).astype(u.dtype) @ w_out


def swiglu(u, w_gate, w_up, w_down):
    return (jax.nn.silu(u @ w_gate) * (u @ w_up)) @ w_down


def setup_inputs(seed: int = 0) -> dict:
    key = jax.random.key(seed)
    ks = jax.random.split(key, 24)
    n_even = (DEPTH + 1) // 2
    n_odd = DEPTH // 2
    nrm = lambda k, shape, s: jax.random.normal(k, shape, jnp.float32) * s
    return {
        'x': nrm(ks[0], (BATCH, SEQ, D_MODEL), 1.0),
        'c': nrm(ks[1], (BATCH, D_MODEL), 1.0),
        'rel_bias': nrm(ks[2], (REL_BUCKETS, BIAS_HEADS), 0.2),
        'ada_w': nrm(ks[3], (DEPTH, D_MODEL, 6 * D_MODEL), 0.5 * D_MODEL ** -0.5),
        'ada_b': nrm(ks[4], (DEPTH, 6 * D_MODEL), 0.02),
        'ln_g': 1.0 + nrm(ks[5], (DEPTH, 2, D_MODEL), 0.02),
        'ln_b': nrm(ks[6], (DEPTH, 2, D_MODEL), 0.02),
        'ab_w_in': nrm(ks[7], (n_even, D_MODEL, AB_IN_COLS), D_MODEL ** -0.5),
        'ab_w_out': nrm(ks[8], (n_even, AB_OUT_COLS, D_MODEL), DEEPNORM_BETA * AB_OUT_COLS ** -0.5),
        'ret_gn_g': 1.0 + nrm(ks[9], (n_even, RET_W), 0.02),
        'ret_gn_b': nrm(ks[10], (n_even, RET_W), 0.02),
        'cmp_pos_k': nrm(ks[11], (n_even, CMP_BLOCK, NSA_HEAD_DIM), 0.2),
        'cmp_pos_v': nrm(ks[12], (n_even, CMP_BLOCK, NSA_HEAD_DIM), 0.2),
        'cmp_k_w1': nrm(ks[13], (n_even, CMP_BLOCK * NSA_HEAD_DIM, CMP_HIDDEN), (CMP_BLOCK * NSA_HEAD_DIM) ** -0.5),
        'cmp_k_w2': nrm(ks[14], (n_even, CMP_HIDDEN, NSA_HEAD_DIM), CMP_HIDDEN ** -0.5),
        'cmp_v_w1': nrm(ks[15], (n_even, CMP_BLOCK * NSA_HEAD_DIM, CMP_HIDDEN), (CMP_BLOCK * NSA_HEAD_DIM) ** -0.5),
        'cmp_v_w2': nrm(ks[16], (n_even, CMP_HIDDEN, NSA_HEAD_DIM), CMP_HIDDEN ** -0.5),
        'dil_w_in': nrm(ks[17], (n_odd, D_MODEL, DIL_IN_COLS), D_MODEL ** -0.5),
        'dil_w_out': nrm(ks[18], (n_odd, DIL_WIDTH, D_MODEL), DEEPNORM_BETA * DIL_WIDTH ** -0.5),
        'ffn_w_gate': nrm(ks[19], (DEPTH, D_MODEL, D_FF), D_MODEL ** -0.5),
        'ffn_w_up': nrm(ks[20], (DEPTH, D_MODEL, D_FF), D_MODEL ** -0.5),
        'ffn_w_down': nrm(ks[21], (DEPTH, D_FF, D_MODEL), DEEPNORM_BETA * D_FF ** -0.5),
    }


def reference(x, c, rel_bias, ada_w, ada_b, ln_g, ln_b, ab_w_in, ab_w_out, ret_gn_g, ret_gn_b,
              cmp_pos_k, cmp_pos_v, cmp_k_w1, cmp_k_w2, cmp_v_w1, cmp_v_w2,
              dil_w_in, dil_w_out, ffn_w_gate, ffn_w_up, ffn_w_down):
    h = x
    for layer in range(DEPTH):
        mod = jax.nn.silu(c) @ ada_w[layer] + ada_b[layer]
        sh_m, sc_m, g_m, sh_f, sc_f, g_f = (t[:, None, :] for t in jnp.split(mod, 6, axis=-1))
        i = layer // 2
        u = h * (1.0 + sc_m) + sh_m
        if layer % 2 == 0:
            y = mixer_retention_nsa(u, ab_w_in[i], ab_w_out[i], ret_gn_g[i], ret_gn_b[i],
                                    cmp_pos_k[i], cmp_pos_v[i], cmp_k_w1[i], cmp_k_w2[i],
                                    cmp_v_w1[i], cmp_v_w2[i], rel_bias)
        else:
            y = mixer_dilated(u, dil_w_in[i], dil_w_out[i], rel_bias)
        h = layer_norm(DEEPNORM_ALPHA * h + g_m * y, ln_g[layer, 0], ln_b[layer, 0])
        u = h * (1.0 + sc_f) + sh_f
        y = swiglu(u, ffn_w_gate[layer], ffn_w_up[layer], ffn_w_down[layer])
        h = layer_norm(DEEPNORM_ALPHA * h + g_f * y, ln_g[layer, 1], ln_b[layer, 1])
    return h
```

```python
import functools
import math

import numpy as np
import jax
import jax.numpy as jnp
from jax import lax
from jax.experimental import pallas as pl
from jax.experimental.pallas import tpu as pltpu

F32 = jnp.float32
BF16 = jnp.bfloat16

D_MODEL = 1024
DEPTH = 2
DEEPNORM_ALPHA = (2 * DEPTH) ** 0.25
LN_EPS = 1e-5
NEG_INF = -1e30

RET_HEADS = 4
RET_HEAD_DIM = 128
RET_CHUNK = 128
ROPE_BASE = 10000.0
RET_W = RET_HEADS * RET_HEAD_DIM

NSA_HEADS = 8
NSA_KV_HEADS = 2
NSA_GROUP = 4
NSA_HEAD_DIM = 64
CMP_BLOCK = 32
CMP_STRIDE = 16
CMP_HIDDEN = 256
SEL_BLOCK = 64
SEL_TOP_N = 16
SEL_FORCE_SCORE = 1e4
WIN_SIZE = 512
NSA_QW = NSA_HEADS * NSA_HEAD_DIM
NSA_KVW = NSA_KV_HEADS * NSA_HEAD_DIM

DIL_PATTERNS = ((128, 1), (512, 4), (2048, 16))
DIL_HEADS = 8
DIL_HEAD_DIM = 128
DIL_WIDTH = DIL_HEADS * DIL_HEAD_DIM

REL_BUCKETS = 32
REL_MAX_DIST = 128
D_FF = 2816

LANES = 128
TILE = 128
VMEM_LIMIT_SMALL = 32 * 1024 * 1024
VMEM_LIMIT_LARGE = 56 * 1024 * 1024

PRE0_BF16_COLS = 3 * RET_W + NSA_QW + 4 * NSA_KVW
PRE0_F32_COLS = RET_W + 2 * NSA_KVW + LANES
FFN_CHUNKS = ((0, 768), (768, 768), (1536, 768), (2304, 512))


def _dot(a, b):
    return jnp.dot(a, b, preferred_element_type=F32)


def _dot_nt(a, b):
    return lax.dot_general(a, b, (((1,), (1,)), ((), ())), preferred_element_type=F32)


def _dot_tn(a, b):
    return lax.dot_general(a, b, (((0,), (0,)), ((), ())), preferred_element_type=F32)


def _split_bf16(x):
    hi = x.astype(BF16)
    lo = (x - hi.astype(F32)).astype(BF16)
    return hi, lo


def _silu(x):
    return x * jax.nn.sigmoid(x)


def _layer_norm(x, g, b):
    mu = jnp.mean(x, axis=-1, keepdims=True)
    xc = x - mu
    var = jnp.mean(xc * xc, axis=-1, keepdims=True)
    return xc * lax.rsqrt(var + LN_EPS) * g + b


def _resident(shape):
    return pl.BlockSpec(shape, lambda *_: (0,) * len(shape), pipeline_mode=pl.Buffered(1))


def _params(sem, vmem):
    return pltpu.CompilerParams(dimension_semantics=sem, vmem_limit_bytes=vmem)


def _ada_body(c_ref, w_ref, b_ref, o_ref):
    a_hi, a_lo = _split_bf16(_silu(c_ref[...]))
    w_hi, w_lo = _split_bf16(w_ref[0])
    o_ref[0] = _dot(a_hi, w_hi) + _dot(a_lo, w_hi) + _dot(a_hi, w_lo) + b_ref[0]


def _ada_call(c, ada_w, ada_b):
    B = c.shape[0]
    n_out = ada_w.shape[-1]
    tn = n_out // 4
    return pl.pallas_call(
        _ada_body,
        grid=(DEPTH, n_out // tn),
        in_specs=[pl.BlockSpec((B, D_MODEL), lambda l, n: (0, 0)),
                  pl.BlockSpec((1, D_MODEL, tn), lambda l, n: (l, 0, n)),
                  pl.BlockSpec((1, 1, tn), lambda l, n: (l, 0, n))],
        out_specs=pl.BlockSpec((1, B, tn), lambda l, n: (l, 0, n)),
        out_shape=jax.ShapeDtypeStruct((DEPTH, B, n_out), F32),
        compiler_params=_params(("arbitrary", "arbitrary"), VMEM_LIMIT_LARGE),
        name="ada_mod",
    )(c, ada_w, ada_b.reshape(DEPTH, 1, n_out))


def _pre_body(h_ref, mod_ref, w_ref, *o_refs):
    u = (h_ref[0] * (1.0 + mod_ref[0, 1:2, :]) + mod_ref[0, 0:1, :]).astype(BF16)
    off = 0
    for o_ref in o_refs:
        n = o_ref.shape[-1]
        o_ref[0] = _dot(u, w_ref[:, off:off + n]).astype(o_ref.dtype)
        off += n


def _pre_call(h, mod, w, out_cols_dtypes, tm, name):
    B, S, _ = h.shape
    n_total = w.shape[1]
    assert sum(n for n, _ in out_cols_dtypes) == n_total
    return pl.pallas_call(
        _pre_body,
        grid=(B, S // tm),
        in_specs=[pl.BlockSpec((1, tm, D_MODEL), lambda b, t: (b, t, 0)),
                  pl.BlockSpec((1, 6, D_MODEL), lambda b, t: (b, 0, 0)),
                  _resident((D_MODEL, n_total))],
        out_specs=[pl.BlockSpec((1, tm, n), lambda b, t: (b, t, 0)) for n, _ in out_cols_dtypes],
        out_shape=[jax.ShapeDtypeStruct((B, S, n), dt) for n, dt in out_cols_dtypes],
        compiler_params=_params(("arbitrary", "arbitrary"), VMEM_LIMIT_LARGE),
        name=name,
    )(h, mod, w)


def _ret_body(q_ref, k_ref, v_ref, g_ref, cos_ref, sin_ref, dec_ref, qd_ref, kd_ref, cd_ref,
              gng_ref, gnb_ref, o_ref):
    n_chunks = q_ref.shape[1] // RET_CHUNK
    state = jnp.zeros((RET_HEAD_DIM, RET_HEAD_DIM), F32)
    for n in range(n_chunks):
        sl = pl.ds(n * RET_CHUNK, RET_CHUNK)
        q = q_ref[0, sl, :].astype(F32)
        k = k_ref[0, sl, :].astype(F32)
        v = v_ref[0, sl, :]
        c2 = cos_ref[sl, :]
        s2 = sin_ref[sl, :]
        qr = (q * c2 + pltpu.roll(q, RET_HEAD_DIM // 2, 1) * s2) * (RET_HEAD_DIM ** -0.5)
        kr = k * c2 + pltpu.roll(k, RET_HEAD_DIM // 2, 1) * s2
        scores = _dot_nt(qr.astype(BF16), kr.astype(BF16)) * dec_ref[0]
        inner = _dot(scores.astype(BF16), v)
        cross = _dot((qr * qd_ref[0]).astype(BF16), state.astype(BF16))
        kv = _dot_tn((kr * kd_ref[0]).astype(BF16), v)
        state = state * cd_ref[0] + kv
        y = inner + cross
        mu = jnp.mean(y, axis=-1, keepdims=True)
        yc = y - mu
        var = jnp.mean(yc * yc, axis=-1, keepdims=True)
        yn = yc * lax.rsqrt(var + LN_EPS)
        gate = g_ref[0, sl, :]
        o_ref[0, sl, :] = ((yn * gng_ref[...] + gnb_ref[...]) * _silu(gate)).astype(o_ref.dtype)


def _ret_tables(S):
    d = RET_HEAD_DIM
    inv = ROPE_BASE ** (-jnp.arange(0, d, 2, dtype=F32) / d)
    ang = jnp.arange(S).astype(F32)[:, None] * inv[None, :]
    cos, sin = jnp.cos(ang), jnp.sin(ang)
    cos2 = jnp.concatenate([cos, cos], axis=-1)
    sin2 = jnp.concatenate([-sin, sin], axis=-1)
    C = RET_CHUNK
    log_gamma = jnp.log1p(-jnp.exp2(-5.0 - jnp.arange(RET_HEADS, dtype=F32)))
    idx = jnp.arange(C, dtype=F32)
    diff = idx[:, None] - idx[None, :]
    dec = jnp.where(diff >= 0, jnp.exp(log_gamma[:, None, None] * jnp.maximum(diff, 0.0)), 0.0)
    kd = jnp.exp(log_gamma[:, None] * (C - 1 - idx)[None, :])
    qd = jnp.exp(log_gamma[:, None] * (idx + 1.0)[None, :])
    cd = jnp.exp(log_gamma * C)
    bc = lambda t: jnp.broadcast_to(t[:, :, None], (RET_HEADS, C, d))
    cdb = jnp.broadcast_to(cd[:, None, None], (RET_HEADS, d, d))
    return cos2, sin2, dec, bc(qd), bc(kd), cdb


def _ret_call(pb, pf, gn_g, gn_b):
    B, S, _ = pb.shape
    cos2, sin2, dec, qd, kd, cd = _ret_tables(S)
    col = lambda off: pl.BlockSpec((1, S, RET_HEAD_DIM), lambda b, h: (b, 0, off + h))
    tab = pl.BlockSpec((1, RET_CHUNK, RET_HEAD_DIM), lambda b, h: (h, 0, 0))
    vec = pl.BlockSpec((1, RET_HEAD_DIM), lambda b, h: (0, h))
    return pl.pallas_call(
        _ret_body,
        grid=(B, RET_HEADS),
        in_specs=[col(0), col(RET_HEADS), col(2 * RET_HEADS), col(0),
                  _resident((S, RET_HEAD_DIM)), _resident((S, RET_HEAD_DIM)),
                  tab, tab, tab, tab, vec, vec],
        out_specs=pl.BlockSpec((1, S, RET_HEAD_DIM), lambda b, h: (b, 0, h)),
        out_shape=jax.ShapeDtypeStruct((B, S, RET_W), BF16),
        compiler_params=_params(("arbitrary", "arbitrary"), VMEM_LIMIT_SMALL),
        name="retention",
    )(pb, pb, pb, pf, cos2, sin2, dec, qd, kd, cd, gn_g.reshape(1, RET_W), gn_b.reshape(1, RET_W))


def _compress_body(xk_ref, xv_ref, pk_ref, pv_ref, w1k_ref, w1v_ref, w2k_ref, w2v_ref, kc_ref, vc_ref):
    for x_ref, p_ref, w1_ref, w2_ref, o_ref in ((xk_ref, pk_ref, w1k_ref, w2k_ref, kc_ref),
                                                (xv_ref, pv_ref, w1v_ref, w2v_ref, vc_ref)):
        acc = jnp.zeros(o_ref.shape[1:], F32)
        for g in range(NSA_KV_HEADS):
            x = x_ref[0, g]
            first = _dot((x + p_ref[0:1, :]).astype(BF16), w1_ref[0])
            second = _dot((x + p_ref[1:2, :]).astype(BF16), w1_ref[1])
            n_rows = x.shape[0]
            hid = _silu(first + pltpu.roll(second, n_rows - 1, 0))
            acc = acc + _dot(hid.astype(BF16), w2_ref[g])
        o_ref[0] = acc.astype(o_ref.dtype)


def _compress_call(xk, xv, pos_k, pos_v, w1k, w1v, w2k, w2v):
    B, G, n_half, flat = xk.shape
    half = lambda t: t.reshape(2, flat)
    w1 = lambda t: t.reshape(2, flat, CMP_HIDDEN).astype(BF16)

    def w2(t):
        z = jnp.zeros_like(t)
        return jnp.stack([jnp.concatenate([t, z], -1), jnp.concatenate([z, t], -1)]).astype(BF16)

    xspec = pl.BlockSpec((1, G, n_half, flat), lambda b: (b, 0, 0, 0))
    ospec = pl.BlockSpec((1, n_half, NSA_KVW), lambda b: (b, 0, 0))
    return pl.pallas_call(
        _compress_body,
        grid=(B,),
        in_specs=[xspec, xspec, _resident((2, flat)), _resident((2, flat)),
                  _resident((2, flat, CMP_HIDDEN)), _resident((2, flat, CMP_HIDDEN)),
                  _resident((2, CMP_HIDDEN, NSA_KVW)), _resident((2, CMP_HIDDEN, NSA_KVW))],
        out_specs=[ospec, ospec],
        out_shape=[jax.ShapeDtypeStruct((B, n_half, NSA_KVW), BF16)] * 2,
        compiler_params=_params(("arbitrary",), VMEM_LIMIT_SMALL),
        name="nsa_compress",
    )(xk, xv, half(pos_k), half(pos_v), w1(w1k), w1(w1v), w2(w2k), w2(w2v))


def _cmp_attn_body(q_ref, kc_ref, vc_ref, cb_ref, ov_ref, o_ref, sel_ref):
    i = pl.program_id(1)
    lane = lax.broadcasted_iota(jnp.int32, (TILE, LANES), 1)
    row = lax.broadcasted_iota(jnp.int32, (TILE, LANES), 0)
    lo_half = lane < NSA_HEAD_DIM
    kc = kc_ref[0]
    vc = vc_ref[0]
    zero = jnp.zeros_like(kc)
    kc_g = (jnp.where(lo_half, kc, zero), jnp.where(lo_half, zero, kc))
    valid = lane * CMP_STRIDE + (CMP_BLOCK - 1) <= i * TILE + row
    psum = [jnp.zeros((TILE, LANES), F32) for _ in range(NSA_KV_HEADS)]
    for r in range(NSA_GROUP):
        cols = slice(r * LANES, (r + 1) * LANES)
        q = q_ref[0, :, cols] * (NSA_HEAD_DIM ** -0.5)
        outs = []
        for g in range(NSA_KV_HEADS):
            s = _dot_nt(q, kc_g[g]) + cb_ref[g * NSA_GROUP + r]
            s = jnp.where(valid, s, NEG_INF)
            m = jnp.max(s, axis=-1, keepdims=True)
            e = jnp.where(valid, jnp.exp(s - m), 0.0)
            den = jnp.maximum(jnp.sum(e, axis=-1, keepdims=True), 1e-30)
            p = e / den
            psum[g] = psum[g] + p
            outs.append(_dot(p.astype(BF16), vc))
        o_ref[0, :, cols] = jnp.where(lo_half, outs[0], outs[1])

    n_sel = sel_ref.shape[2]
    blk = lax.broadcasted_iota(jnp.int32, (n_sel, TILE), 0)
    qblk = (i * TILE + lax.broadcasted_iota(jnp.int32, (n_sel, TILE), 1)) // SEL_BLOCK
    forced = jnp.where(blk == 0, 1.0, jnp.where(blk == qblk, 1.0, jnp.where(blk == qblk - 1, 1.0, 0.0)))
    for g in range(NSA_KV_HEADS):
        p_hi, p_lo = _split_bf16(psum[g])
        imp = _dot_nt(ov_ref[...], p_hi) + _dot_nt(ov_ref[...], p_lo)
        score = jnp.where(forced > 0.5, SEL_FORCE_SCORE, jnp.where(blk <= qblk, imp, -1.0))
        rank = jnp.zeros((n_sel, TILE), F32)
        for other in range(n_sel):
            so = score[other:other + 1, :]
            tie = jnp.where(blk > other, 1.0, 0.0)
            rank = rank + jnp.where(so > score, 1.0, jnp.where(so == score, tie, 0.0))
        sel_ref[0, g] = jnp.where(rank < float(min(SEL_TOP_N, n_sel)), 1.0, 0.0)


def _cmp_attn_call(pb, kc, vc, cmp_bias, overlap_t):
    B, S, _ = pb.shape
    n_sel = S // SEL_BLOCK
    q_block = (3 * RET_W) // NSA_QW
    return pl.pallas_call(
        _cmp_attn_body,
        grid=(B, S // TILE),
        in_specs=[pl.BlockSpec((1, TILE, NSA_QW), lambda b, i: (b, i, q_block)),
                  pl.BlockSpec((1,) + kc.shape[1:], lambda b, i: (b, 0, 0)),
                  pl.BlockSpec((1,) + vc.shape[1:], lambda b, i: (b, 0, 0)),
                  pl.BlockSpec((NSA_HEADS, TILE, LANES), lambda b, i: (0, i, 0)),
                  _resident((n_sel, LANES))],
        out_specs=[pl.BlockSpec((1, TILE, NSA_QW), lambda b, i: (b, i, 0)),
                   pl.BlockSpec((1, NSA_KV_HEADS, n_sel, TILE), lambda b, i: (b, 0, 0, i))],
        out_shape=[jax.ShapeDtypeStruct((B, S, NSA_QW), F32),
                   jax.ShapeDtypeStruct((B, NSA_KV_HEADS, n_sel, S), F32)],
        compiler_params=_params(("arbitrary", "arbitrary"), VMEM_LIMIT_SMALL),
        name="nsa_cmp_attn",
    )(pb, kc, vc, cmp_bias, overlap_t)


def _slc_body(far_ref, q_ref, ks_ref, vs_ref, sel_ref, ex_ref, d0_ref, d1_ref, o_ref, ksg_scr, mask_scr):
    i = pl.program_id(1)
    n_tiles = ks_ref.shape[1] // TILE
    lane = lax.broadcasted_iota(jnp.int32, (TILE, LANES), 1)
    lo_half = lane < NSA_HEAD_DIM

    @pl.when(i == 0)
    def _():
        for t in range(n_tiles):
            sl = pl.ds(t * TILE, TILE)
            kt = ks_ref[0, sl, :]
            zero = jnp.zeros_like(kt)
            ksg_scr[0, sl, :] = jnp.where(lo_half, kt, zero)
            ksg_scr[1, sl, :] = jnp.where(lo_half, zero, kt)

    for g in range(NSA_KV_HEADS):
        sel_t = sel_ref[0, g].astype(BF16)

        def fill(t, carry, g=g, sel_t=sel_t):
            mask_scr[g, t] = (_dot_tn(sel_t, ex_ref[t]) - 1.0) * (-NEG_INF)
            return carry

        lax.fori_loop(0, i + 1, fill, 0)

    prev = jnp.maximum(i - 1, 0)
    no_prev = jnp.where(i == 0, NEG_INF, 0.0)
    own_sl = pl.ds(pl.multiple_of(i * TILE, TILE), TILE)
    prev_sl = pl.ds(pl.multiple_of(prev * TILE, TILE), TILE)
    v_near = jnp.concatenate([vs_ref[0, prev_sl, :], vs_ref[0, own_sl, :]], axis=0)

    def update(carry, s, v):
        m, l, acc = carry
        m_new = jnp.maximum(m, jnp.max(s, axis=-1, keepdims=True))
        alpha = jnp.exp(m - m_new)
        p = jnp.exp(s - m_new)
        l = alpha * l + jnp.sum(p, axis=-1, keepdims=True)
        acc = alpha * acc + _dot(p.astype(BF16), v)
        return m_new, l, acc

    for r in range(NSA_GROUP):
        cols = slice(r * LANES, (r + 1) * LANES)
        q = q_ref[0, :, cols] * (NSA_HEAD_DIM ** -0.5)
        outs = []
        for g in range(NSA_KV_HEADS):
            head = g * NSA_GROUP + r
            far_bias = far_ref[head]

            def far_step(t, carry, g=g, q=q, far_bias=far_bias):
                sl = pl.ds(pl.multiple_of(t * TILE, TILE), TILE)
                s = _dot_nt(q, ksg_scr[g, sl, :]) + (mask_scr[g, t] + far_bias)
                return update(carry, s, vs_ref[0, sl, :])

            init = (jnp.full((TILE, 1), NEG_INF, F32), jnp.zeros((TILE, 1), F32),
                    jnp.zeros((TILE, LANES), F32))
            carry = lax.fori_loop(0, prev, far_step, init)
            k_near = jnp.concatenate([ksg_scr[g, prev_sl, :], ksg_scr[g, own_sl, :]], axis=0)
            bias = jnp.concatenate([d1_ref[head] + mask_scr[g, prev] + no_prev,
                                    d0_ref[head] + mask_scr[g, i]], axis=1)
            _, l, acc = update(carry, _dot_nt(q, k_near) + bias, v_near)
            outs.append(acc / l)
        o_ref[0, :, cols] = jnp.where(lo_half, outs[0], outs[1])


def _slc_call(pb, sel_t, expand, far, d0, d1):
    B, S, _ = pb.shape
    n_sel = S // SEL_BLOCK
    n_tiles = S // TILE
    q_block = (3 * RET_W) // NSA_QW
    ks_block = (3 * RET_W + NSA_QW) // NSA_KVW
    return pl.pallas_call(
        _slc_body,
        grid=(B, n_tiles),
        in_specs=[pl.BlockSpec(memory_space=pltpu.SMEM),
                  pl.BlockSpec((1, TILE, NSA_QW), lambda b, i: (b, i, q_block)),
                  pl.BlockSpec((1, S, NSA_KVW), lambda b, i: (b, 0, ks_block)),
                  pl.BlockSpec((1, S, NSA_KVW), lambda b, i: (b, 0, ks_block + 1)),
                  pl.BlockSpec((1, NSA_KV_HEADS, n_sel, TILE), lambda b, i: (b, 0, 0, i)),
                  _resident((n_tiles, n_sel, TILE)),
                  _resident((NSA_HEADS, TILE, TILE)), _resident((NSA_HEADS, TILE, TILE))],
        out_specs=pl.BlockSpec((1, TILE, NSA_QW), lambda b, i: (b, i, 0)),
        out_shape=jax.ShapeDtypeStruct((B, S, NSA_QW), F32),
        scratch_shapes=[pltpu.VMEM((NSA_KV_HEADS, S, NSA_KVW), BF16),
                        pltpu.VMEM((NSA_KV_HEADS, n_tiles, TILE, TILE), F32)],
        compiler_params=_params(("arbitrary", "arbitrary"), VMEM_LIMIT_SMALL),
        name="nsa_selected",
    )(far, pb, pb, pb, sel_t, expand, d0, d1)


def _win_body(q_ref, k_ref, v_ref, wb_ref, gate_ref, eg_ref, ocmp_ref, oslc_ref, y_ref):
    i = pl.program_id(1)
    n_prev = (WIN_SIZE - 1 + TILE - 1) // TILE
    lane = lax.broadcasted_iota(jnp.int32, (TILE, LANES), 1)
    lo_half = lane < NSA_HEAD_DIM
    k_tiles, v_tiles, negs = [], [], []
    for t in range(n_prev + 1):
        idx = i - n_prev + t
        sl = pl.ds(pl.multiple_of(jnp.maximum(idx, 0) * TILE, TILE), TILE)
        k_tiles.append(k_ref[0, sl, :])
        v_tiles.append(v_ref[0, sl, :])
        negs.append(jnp.full((TILE, TILE), jnp.where(idx < 0, NEG_INF, 0.0), F32))
    k_all = jnp.concatenate(k_tiles, axis=0)
    v_all = jnp.concatenate(v_tiles, axis=0)
    missing = jnp.concatenate(negs, axis=1)
    zero = jnp.zeros_like(k_all)
    lo_rows = lax.broadcasted_iota(jnp.int32, k_all.shape, 1) < NSA_HEAD_DIM
    k_g = (jnp.where(lo_rows, k_all, zero), jnp.where(lo_rows, zero, k_all))

    g_hi, g_lo = _split_bf16(jax.nn.sigmoid(gate_ref[0]))
    gates = _dot(g_hi, eg_ref[...]) + _dot(g_lo, eg_ref[...])

    for r in range(NSA_GROUP):
        cols = slice(r * LANES, (r + 1) * LANES)
        q = q_ref[0, :, cols] * (NSA_HEAD_DIM ** -0.5)
        outs = []
        for g in range(NSA_KV_HEADS):
            s = _dot_nt(q, k_g[g]) + (wb_ref[g * NSA_GROUP + r] + missing)
            m = jnp.max(s, axis=-1, keepdims=True)
            e = jnp.exp(s - m)
            den = jnp.sum(e, axis=-1, keepdims=True)
            outs.append(_dot(e.astype(BF16), v_all) / den)
        o_win = jnp.where(lo_half, outs[0], outs[1])
        y = (gates[:, r * LANES:(r + 1) * LANES] * ocmp_ref[0, :, cols]
             + gates[:, NSA_QW + r * LANES:NSA_QW + (r + 1) * LANES] * oslc_ref[0, :, cols]
             + gates[:, 2 * NSA_QW + r * LANES:2 * NSA_QW + (r + 1) * LANES] * o_win)
        y_ref[0, :, cols] = y.astype(y_ref.dtype)


def _win_call(pb, pf, win_bias, gate_expand, o_cmp, o_slc):
    B, S, _ = pb.shape
    q_block = (3 * RET_W) // NSA_QW
    kw_block = (3 * RET_W + NSA_QW) // NSA_KVW + 2
    gate_block = (RET_W + 2 * NSA_KVW) // LANES
    span = win_bias.shape[-1]
    tile_spec = pl.BlockSpec((1, TILE, NSA_QW), lambda b, i: (b, i, 0))
    return pl.pallas_call(
        _win_body,
        grid=(B, S // TILE),
        in_specs=[pl.BlockSpec((1, TILE, NSA_QW), lambda b, i: (b, i, q_block)),
                  pl.BlockSpec((1, S, NSA_KVW), lambda b, i: (b, 0, kw_block)),
                  pl.BlockSpec((1, S, NSA_KVW), lambda b, i: (b, 0, kw_block + 1)),
                  _resident((NSA_HEADS, TILE, span)),
                  pl.BlockSpec((1, TILE, LANES), lambda b, i: (b, i, gate_block)),
                  _resident((LANES, 3 * NSA_QW)),
                  tile_spec, tile_spec],
        out_specs=tile_spec,
        out_shape=jax.ShapeDtypeStruct((B, S, NSA_QW), BF16),
        compiler_params=_params(("arbitrary", "arbitrary"), VMEM_LIMIT_SMALL),
        name="nsa_window_combine",
    )(pb, pb, pb, win_bias, pf, gate_expand, o_cmp, o_slc)


def _post_body(h_ref, ma_ref, mb_ref, mod_ref, woa_ref, wob_ref, lng_ref, lnb_ref,
               wg_ref, wu_ref, wd_ref, o_ref):
    y = _dot(ma_ref[0], woa_ref[...]) + _dot(mb_ref[0], wob_ref[...])
    h1 = _layer_norm(DEEPNORM_ALPHA * h_ref[0] + mod_ref[0, 2:3, :] * y, lng_ref[0:1, :], lnb_ref[0:1, :])
    u = (h1 * (1.0 + mod_ref[0, 4:5, :]) + mod_ref[0, 3:4, :]).astype(BF16)
    acc = jnp.zeros(h1.shape, F32)
    for c0, cn in FFN_CHUNKS:
        gate = _dot(u, wg_ref[:, c0:c0 + cn])
        up = _dot(u, wu_ref[:, c0:c0 + cn])
        acc = acc + _dot((_silu(gate) * up).astype(BF16), wd_ref[c0:c0 + cn, :])
    o_ref[0] = _layer_norm(DEEPNORM_ALPHA * h1 + mod_ref[0, 5:6, :] * acc, lng_ref[1:2, :], lnb_ref[1:2, :])


def _post_call(h, mix_a, mix_b, col_a, col_b, mod, wo_a, wo_b, ln_g, ln_b, w_gate, w_up, w_down, tm, name):
    B, S, _ = h.shape
    half = D_MODEL // 2
    tok = pl.BlockSpec((1, tm, D_MODEL), lambda b, t: (b, t, 0))
    return pl.pallas_call(
        _post_body,
        grid=(B, S // tm),
        in_specs=[tok,
                  pl.BlockSpec((1, tm, half), lambda b, t: (b, t, col_a)),
                  pl.BlockSpec((1, tm, half), lambda b, t: (b, t, col_b)),
                  pl.BlockSpec((1, 6, D_MODEL), lambda b, t: (b, 0, 0)),
                  _resident((half, D_MODEL)), _resident((half, D_MODEL)),
                  _resident((2, D_MODEL)), _resident((2, D_MODEL)),
                  _resident((D_MODEL, D_FF)), _resident((D_MODEL, D_FF)), _resident((D_FF, D_MODEL))],
        out_specs=tok,
        out_shape=jax.ShapeDtypeStruct((B, S, D_MODEL), F32),
        compiler_params=_params(("arbitrary", "arbitrary"), VMEM_LIMIT_LARGE),
        name=name,
    )(h, mix_a, mix_b, mod, wo_a, wo_b, ln_g, ln_b, w_gate, w_up, w_down)


def _dil_body(q_ref, k_ref, v_ref, b_ref, o_ref, lse_ref, *, tiles_per_seg):
    i = pl.program_id(1)
    lane = lax.broadcasted_iota(jnp.int32, (TILE, LANES), 1)
    own_sl = pl.ds(pl.multiple_of(i * TILE, TILE), TILE)
    if tiles_per_seg > 1:
        prev_sl = pl.ds(pl.multiple_of(jnp.maximum(i - 1, 0) * TILE, TILE), TILE)
        no_prev = jnp.where(i % tiles_per_seg == 0, NEG_INF, 0.0)
    lse_tile = jnp.zeros((TILE, LANES), F32)
    for h in range(DIL_HEADS):
        cols = slice(h * DIL_HEAD_DIM, (h + 1) * DIL_HEAD_DIM)
        q = q_ref[0, :, cols]
        if tiles_per_seg > 1:
            k = jnp.concatenate([k_ref[0, prev_sl, cols], k_ref[0, own_sl, cols]], axis=0)
            v = jnp.concatenate([v_ref[0, prev_sl, cols], v_ref[0, own_sl, cols]], axis=0)
            bias = jnp.concatenate([b_ref[h, :, :TILE] + no_prev, b_ref[h, :, TILE:]], axis=1)
        else:
            k = k_ref[0, own_sl, cols]
            v = v_ref[0, own_sl, cols]
            bias = b_ref[h, :, TILE:]
        s = _dot_nt(q, k) * (DIL_HEAD_DIM ** -0.5) + bias
        m = jnp.max(s, axis=-1, keepdims=True)
        e = jnp.exp(s - m)
        den = jnp.sum(e, axis=-1, keepdims=True)
        o_ref[0, :, cols] = (_dot(e.astype(BF16), v) / den).astype(o_ref.dtype)
        lse_tile = jnp.where(lane == h, m + jnp.log(den), lse_tile)
    lse_ref[0] = lse_tile


def _dil_call(proj, bias, tiles_per_seg, name):
    B, S, _ = proj.shape
    seq = lambda c: pl.BlockSpec((1, S, DIL_WIDTH), lambda b, i: (b, 0, c))
    return pl.pallas_call(
        functools.partial(_dil_body, tiles_per_seg=tiles_per_seg),
        grid=(B, S // TILE),
        in_specs=[pl.BlockSpec((1, TILE, DIL_WIDTH), lambda b, i: (b, i, 0)), seq(1), seq(2),
                  _resident((DIL_HEADS, TILE, 2 * TILE))],
        out_specs=[pl.BlockSpec((1, TILE, DIL_WIDTH), lambda b, i: (b, i, 0)),
                   pl.BlockSpec((1, TILE, LANES), lambda b, i: (b, i, 0))],
        out_shape=[jax.ShapeDtypeStruct((B, S, DIL_WIDTH), BF16),
                   jax.ShapeDtypeStruct((B, S, LANES), F32)],
        compiler_params=_params(("arbitrary", "arbitrary"), VMEM_LIMIT_LARGE),
        name=name,
    )(proj, proj, proj, bias)


def _dil_mix_body(o0_ref, o1_ref, o2_ref, l0_ref, l1_ref, l2_ref, y_ref):
    o_refs = (o0_ref, o1_ref, o2_ref)
    l_refs = (l0_ref, l1_ref, l2_ref)
    rows = y_ref.shape[1]
    for h in range(DIL_HEADS):
        cols = slice(h * DIL_HEAD_DIM, (h + 1) * DIL_HEAD_DIM)
        lses = [jnp.broadcast_to(l_ref[0, :, h:h + 1], (rows, DIL_HEAD_DIM)) for l_ref in l_refs]
        m = jnp.maximum(jnp.maximum(lses[0], lses[1]), lses[2])
        ws = [jnp.exp(l - m) for l in lses]
        den = ws[0] + ws[1] + ws[2]
        y = sum((w / den) * o_ref[0, :, cols].astype(F32) for w, o_ref in zip(ws, o_refs))
        y_ref[0, :, cols] = y.astype(y_ref.dtype)


def _dil_mix_call(outs, lses, tm):
    B, S, _ = outs[0].shape
    ospec = pl.BlockSpec((1, tm, DIL_WIDTH), lambda b, t: (b, t, 0))
    lspec = pl.BlockSpec((1, tm, LANES), lambda b, t: (b, t, 0))
    return pl.pallas_call(
        _dil_mix_body,
        grid=(B, S // tm),
        in_specs=[ospec] * 3 + [lspec] * 3,
        out_specs=ospec,
        out_shape=jax.ShapeDtypeStruct((B, S, DIL_WIDTH), BF16),
        compiler_params=_params(("arbitrary", "arbitrary"), VMEM_LIMIT_SMALL),
        name="dilated_mix",
    )(*outs, *lses)


def _t5_bucket_np(dist):
    n = np.maximum(dist, 0)
    max_exact = REL_BUCKETS // 2
    nf = np.maximum(n, 1).astype(np.float64)
    val = np.log(nf / max_exact) / math.log(REL_MAX_DIST / max_exact) * (REL_BUCKETS - max_exact)
    frac = np.abs(val - np.round(val))
    on_edge = (frac < 1e-9) & (n > max_exact) & (n < REL_MAX_DIST)
    assert not on_edge.any()
    large = np.minimum(max_exact + np.floor(val + 1e-9).astype(np.int64), REL_BUCKETS - 1)
    return np.where(n < max_exact, n, large).astype(np.int32)


def _bias_tile(rel_bias, dist, valid):
    tab = jnp.take(rel_bias, jnp.asarray(_t5_bucket_np(dist).reshape(-1)), axis=0)
    tab = tab.reshape(dist.shape + (rel_bias.shape[1],)).transpose(2, 0, 1)
    return jnp.where(jnp.asarray(valid)[None], tab, NEG_INF)


def _nsa_tables(rel_bias, S):
    a = np.arange(TILE)[:, None]
    n_prev = (WIN_SIZE - 1 + TILE - 1) // TILE
    c = np.arange((n_prev + 1) * TILE)[None, :]
    dist = a - c + n_prev * TILE
    win_bias = _bias_tile(rel_bias, dist, (dist >= 0) & (dist <= WIN_SIZE - 1))
    c = np.arange(TILE)[None, :]
    d0 = _bias_tile(rel_bias, a - c, a - c >= 0)
    d1 = _bias_tile(rel_bias, a - c + TILE, np.ones((TILE, TILE), bool))
    far_bucket = _t5_bucket_np(np.arange(TILE + 1, S + TILE))
    assert (far_bucket == far_bucket[0]).all()
    far = rel_bias[int(far_bucket[0])]
    q = np.arange(S)[:, None]
    cend = np.arange(LANES)[None, :] * CMP_STRIDE + CMP_BLOCK - 1
    cmp_bias = _bias_tile(rel_bias, q - cend, np.ones((S, LANES), bool))
    n_cmp = (S - CMP_BLOCK) // CMP_STRIDE + 1
    n_sel = S // SEL_BLOCK
    cs = (np.arange(n_cmp) * CMP_STRIDE)[:, None]
    ss = (np.arange(n_sel) * SEL_BLOCK)[None, :]
    ov = np.clip(np.minimum(cs + CMP_BLOCK, ss + SEL_BLOCK) - np.maximum(cs, ss), 0, None) / CMP_BLOCK
    ov_t = np.zeros((n_sel, LANES), np.float32)
    ov_t[:, :n_cmp] = ov.T
    key_blk = np.arange(S) // SEL_BLOCK
    ex = (np.arange(n_sel)[:, None] == key_blk[None, :]).astype(np.float32)
    ex = ex.reshape(n_sel, S // TILE, TILE).transpose(1, 0, 2)
    eg = np.zeros((LANES, 3 * NSA_QW), np.float32)
    for g in range(NSA_KV_HEADS):
        for r in range(NSA_GROUP):
            for j in range(3):
                base = j * NSA_QW + r * LANES + g * NSA_HEAD_DIM
                eg[g * NSA_GROUP * 3 + r * 3 + j, base:base + NSA_HEAD_DIM] = 1.0
    return (win_bias, d0, d1, far, cmp_bias, jnp.asarray(ov_t, BF16), jnp.asarray(ex, BF16),
            jnp.asarray(eg, BF16))


def _dil_bias(rel_bias, dilation, max_dist):
    a = np.arange(TILE)[:, None]
    c = np.arange(2 * TILE)[None, :]
    dist = a - c + TILE
    return _bias_tile(rel_bias, dist * dilation, (dist >= 0) & (dist <= max_dist))


def _nsa_head_perm():
    perm = np.zeros(NSA_QW, np.int64)
    for r in range(NSA_GROUP):
        for g in range(NSA_KV_HEADS):
            new = r * LANES + g * NSA_HEAD_DIM
            old = (g * NSA_GROUP + r) * NSA_HEAD_DIM
            perm[new:new + NSA_HEAD_DIM] = np.arange(old, old + NSA_HEAD_DIM)
    return perm


def _to_residues(t, d):
    B, S, C = t.shape
    return t if d == 1 else t.reshape(B, S // d, d, C).transpose(0, 2, 1, 3).reshape(B, S, C)


def _from_residues(t, d):
    B, S, C = t.shape
    return t if d == 1 else t.reshape(B, d, S // d, C).transpose(0, 2, 1, 3).reshape(B, S, C)


def _layer0_mixer(h, mod, ab_w_in, rel_bias, gn_g, gn_b, pos_k, pos_v, w1k, w2k, w1v, w2v):
    B, S, _ = h.shape
    o = np.cumsum((0, RET_W, RET_W, RET_W, RET_W, NSA_QW) + (NSA_KVW,) * 6 + (3 * NSA_HEADS,))
    seg = lambda a: ab_w_in[:, o[a]:o[a + 1]]
    gate_w = jnp.pad(seg(11), ((0, 0), (0, LANES - 3 * NSA_HEADS)))
    w = jnp.concatenate([seg(0), seg(1), seg(2), seg(4)[:, _nsa_head_perm()], seg(7), seg(8), seg(9), seg(10),
                         seg(3), seg(5), seg(6), gate_w], axis=1).astype(BF16)
    pb, pf = _pre_call(h, mod, w, ((PRE0_BF16_COLS, BF16), (PRE0_F32_COLS, F32)), 512, "pre0")

    y_ret = _ret_call(pb, pf, gn_g, gn_b)

    win_bias, d0, d1, far, cmp_bias, ov_t, ex, eg = _nsa_tables(rel_bias, S)
    half = S // CMP_STRIDE

    def half_blocks(t):
        t = t.reshape(B, half, CMP_STRIDE, NSA_KV_HEADS, NSA_HEAD_DIM).transpose(0, 3, 1, 2, 4)
        return t.reshape(B, NSA_KV_HEADS, half, CMP_STRIDE * NSA_HEAD_DIM)

    kc, vc = _compress_call(half_blocks(pf[:, :, RET_W:RET_W + NSA_KVW]),
                            half_blocks(pf[:, :, RET_W + NSA_KVW:RET_W + 2 * NSA_KVW]),
                            pos_k, pos_v, w1k, w1v, w2k, w2v)
    o_cmp, sel_t = _cmp_attn_call(pb, kc, vc, cmp_bias, ov_t)
    o_slc = _slc_call(pb, sel_t, ex, far, d0, d1)
    y_nsa = _win_call(pb, pf, win_bias, eg, o_cmp, o_slc)
    return y_ret, y_nsa


def _layer1_mixer(h, mod, dil_w_in, rel_bias):
    B, S, _ = h.shape
    outs, lses = [], []
    for gi, (window, dilation) in enumerate(DIL_PATTERNS):
        w = dil_w_in[:, gi * 3 * DIL_WIDTH:(gi + 1) * 3 * DIL_WIDTH].astype(BF16)
        proj, = _pre_call(_to_residues(h, dilation), mod, w, ((3 * DIL_WIDTH, BF16),), 512, f"pre1_{gi}")
        bias = _dil_bias(rel_bias, dilation, window // dilation)
        o, lse = _dil_call(proj, bias, (S // dilation) // TILE, f"dilated_{gi}")
        outs.append(_from_residues(o, dilation))
        lses.append(_from_residues(lse, dilation))
    return _dil_mix_call(outs, lses, 256)


def kernel(x, c, rel_bias, ada_w, ada_b, ln_g, ln_b, ab_w_in, ab_w_out, ret_gn_g, ret_gn_b, cmp_pos_k, cmp_pos_v, cmp_k_w1, cmp_k_w2, cmp_v_w1, cmp_v_w2, dil_w_in, dil_w_out, ffn_w_gate, ffn_w_up, ffn_w_down):
    B = x.shape[0]
    mod = _ada_call(c, ada_w, ada_b).reshape(DEPTH, B, 6, D_MODEL)
    h = x
    for layer in range(DEPTH):
        i = layer // 2
        if layer % 2 == 0:
            mix_a, mix_b = _layer0_mixer(h, mod[layer], ab_w_in[i], rel_bias, ret_gn_g[i], ret_gn_b[i],
                                         cmp_pos_k[i], cmp_pos_v[i], cmp_k_w1[i], cmp_k_w2[i],
                                         cmp_v_w1[i], cmp_v_w2[i])
            col_a, col_b = 0, 0
            wo_a = ab_w_out[i, :RET_W]
            wo_b = ab_w_out[i, RET_W:][_nsa_head_perm()]
        else:
            mix_a = mix_b = _layer1_mixer(h, mod[layer], dil_w_in[i], rel_bias)
            col_a, col_b = 0, 1
            wo_a = dil_w_out[i, :D_MODEL // 2]
            wo_b = dil_w_out[i, D_MODEL // 2:]
        h = _post_call(h, mix_a, mix_b, col_a, col_b, mod[layer], wo_a.astype(BF16), wo_b.astype(BF16),
                       ln_g[layer], ln_b[layer], ffn_w_gate[layer].astype(BF16), ffn_w_up[layer].astype(BF16),
                       ffn_w_down[layer].astype(BF16), 512, f"post{layer}")
    return h
```

```python
import functools
import math

import numpy as np
import jax
import jax.numpy as jnp
from jax import lax
from jax.experimental import pallas as pl
from jax.experimental.pallas import tpu as pltpu

F32 = jnp.float32
BF16 = jnp.bfloat16

D_MODEL = 1024
DEPTH = 2
DEEPNORM_ALPHA = (2 * DEPTH) ** 0.25
LN_EPS = 1e-5
NEG_INF = -1e30

RET_HEADS = 4
RET_HEAD_DIM = 128
RET_CHUNK = 128
ROPE_BASE = 10000.0
RET_W = RET_HEADS * RET_HEAD_DIM

NSA_HEADS = 8
NSA_KV_HEADS = 2
NSA_GROUP = 4
NSA_HEAD_DIM = 64
CMP_BLOCK = 32
CMP_STRIDE = 16
CMP_HIDDEN = 256
SEL_BLOCK = 64
SEL_TOP_N = 16
SEL_FORCE_SCORE = 1e4
WIN_SIZE = 512
NSA_QW = NSA_HEADS * NSA_HEAD_DIM
NSA_KVW = NSA_KV_HEADS * NSA_HEAD_DIM

DIL_PATTERNS = ((128, 1), (512, 4), (2048, 16))
DIL_HEADS = 8
DIL_HEAD_DIM = 128
DIL_WIDTH = DIL_HEADS * DIL_HEAD_DIM

REL_BUCKETS = 32
REL_MAX_DIST = 128
D_FF = 2816

LANES = 128
TILE = 128
VMEM_LIMIT_SMALL = 32 * 1024 * 1024
VMEM_LIMIT_LARGE = 56 * 1024 * 1024

PRE0_BF16_COLS = 3 * RET_W + NSA_QW + 4 * NSA_KVW
PRE0_F32_COLS = RET_W + 2 * NSA_KVW + LANES
FFN_CHUNKS = ((0, 768), (768, 768), (1536, 768), (2304, 512))


def _dot(a, b):
    return jnp.dot(a, b, preferred_element_type=F32)


def _dot_nt(a, b):
    return lax.dot_general(a, b, (((1,), (1,)), ((), ())), preferred_element_type=F32)


def _dot_tn(a, b):
    return lax.dot_general(a, b, (((0,), (0,)), ((), ())), preferred_element_type=F32)


def _split_bf16(x):
    hi = x.astype(BF16)
    lo = (x - hi.astype(F32)).astype(BF16)
    return hi, lo


def _silu(x):
    return x * jax.nn.sigmoid(x)


def _layer_norm(x, g, b):
    mu = jnp.mean(x, axis=-1, keepdims=True)
    xc = x - mu
    var = jnp.mean(xc * xc, axis=-1, keepdims=True)
    return xc * lax.rsqrt(var + LN_EPS) * g + b


def _resident(shape):
    return pl.BlockSpec(shape, lambda *_: (0,) * len(shape), pipeline_mode=pl.Buffered(1))


def _params(sem, vmem):
    return pltpu.CompilerParams(dimension_semantics=sem, vmem_limit_bytes=vmem)


def _ada_body(c_ref, w_ref, b_ref, o_ref):
    a_hi, a_lo = _split_bf16(_silu(c_ref[...]))
    w_hi, w_lo = _split_bf16(w_ref[0])
    o_ref[0] = _dot(a_hi, w_hi) + _dot(a_lo, w_hi) + _dot(a_hi, w_lo) + b_ref[0]


def _ada_call(c, ada_w, ada_b):
    B = c.shape[0]
    n_out = ada_w.shape[-1]
    tn = n_out // 4
    return pl.pallas_call(
        _ada_body,
        grid=(DEPTH, n_out // tn),
        in_specs=[pl.BlockSpec((B, D_MODEL), lambda l, n: (0, 0)),
                  pl.BlockSpec((1, D_MODEL, tn), lambda l, n: (l, 0, n)),
                  pl.BlockSpec((1, 1, tn), lambda l, n: (l, 0, n))],
        out_specs=pl.BlockSpec((1, B, tn), lambda l, n: (l, 0, n)),
        out_shape=jax.ShapeDtypeStruct((DEPTH, B, n_out), F32),
        compiler_params=_params(("arbitrary", "arbitrary"), VMEM_LIMIT_LARGE),
        name="ada_mod",
    )(c, ada_w, ada_b.reshape(DEPTH, 1, n_out))


def _pre_body(h_ref, mod_ref, w_ref, *o_refs):
    n_res = h_ref.shape[2] // D_MODEL
    h = jnp.concatenate([h_ref[0, :, j * D_MODEL:(j + 1) * D_MODEL] for j in range(n_res)], axis=0)
    u = (h * (1.0 + mod_ref[0, 1:2, :]) + mod_ref[0, 0:1, :]).astype(BF16)
    off = 0
    for o_ref in o_refs:
        n = o_ref.shape[-1]
        o_ref[0] = _dot(u, w_ref[:, off:off + n]).astype(o_ref.dtype)
        off += n


def _pre_call(h, mod, w, out_cols_dtypes, tm, name, dilation=1):
    B, S, _ = h.shape
    n_total = w.shape[1]
    assert sum(n for n, _ in out_cols_dtypes) == n_total
    if dilation == 1:
        h_spec = pl.BlockSpec((1, tm, D_MODEL), lambda b, t: (b, t, 0))
    else:
        seg = S // dilation
        assert tm % seg == 0
        h = h.reshape(B, seg, dilation * D_MODEL)
        h_spec = pl.BlockSpec((1, seg, (tm // seg) * D_MODEL), lambda b, t: (b, 0, t))
    return pl.pallas_call(
        _pre_body,
        grid=(B, S // tm),
        in_specs=[h_spec,
                  pl.BlockSpec((1, 6, D_MODEL), lambda b, t: (b, 0, 0)),
                  _resident((D_MODEL, n_total))],
        out_specs=[pl.BlockSpec((1, tm, n), lambda b, t: (b, t, 0)) for n, _ in out_cols_dtypes],
        out_shape=[jax.ShapeDtypeStruct((B, S, n), dt) for n, dt in out_cols_dtypes],
        compiler_params=_params(("arbitrary", "arbitrary"), VMEM_LIMIT_LARGE),
        name=name,
    )(h, mod, w)


def _ret_body(q_ref, k_ref, v_ref, g_ref, cos_ref, sin_ref, dec_ref, qd_ref, kd_ref, cd_ref,
              gng_ref, gnb_ref, o_ref):
    n_chunks = q_ref.shape[1] // RET_CHUNK
    state = jnp.zeros((RET_HEAD_DIM, RET_HEAD_DIM), F32)
    for n in range(n_chunks):
        sl = pl.ds(n * RET_CHUNK, RET_CHUNK)
        q = q_ref[0, sl, :].astype(F32)
        k = k_ref[0, sl, :].astype(F32)
        v = v_ref[0, sl, :]
        c2 = cos_ref[sl, :]
        s2 = sin_ref[sl, :]
        qr = (q * c2 + pltpu.roll(q, RET_HEAD_DIM // 2, 1) * s2) * (RET_HEAD_DIM ** -0.5)
        kr = k * c2 + pltpu.roll(k, RET_HEAD_DIM // 2, 1) * s2
        scores = _dot_nt(qr.astype(BF16), kr.astype(BF16)) * dec_ref[0]
        inner = _dot(scores.astype(BF16), v)
        cross = _dot((qr * qd_ref[0]).astype(BF16), state.astype(BF16))
        kv = _dot_tn((kr * kd_ref[0]).astype(BF16), v)
        state = state * cd_ref[0] + kv
        y = inner + cross
        mu = jnp.mean(y, axis=-1, keepdims=True)
        yc = y - mu
        var = jnp.mean(yc * yc, axis=-1, keepdims=True)
        yn = yc * lax.rsqrt(var + LN_EPS)
        gate = g_ref[0, sl, :]
        o_ref[0, sl, :] = ((yn * gng_ref[...] + gnb_ref[...]) * _silu(gate)).astype(o_ref.dtype)


def _ret_tables(S):
    d = RET_HEAD_DIM
    inv = ROPE_BASE ** (-jnp.arange(0, d, 2, dtype=F32) / d)
    ang = jnp.arange(S).astype(F32)[:, None] * inv[None, :]
    cos, sin = jnp.cos(ang), jnp.sin(ang)
    cos2 = jnp.concatenate([cos, cos], axis=-1)
    sin2 = jnp.concatenate([-sin, sin], axis=-1)
    C = RET_CHUNK
    log_gamma = jnp.log1p(-jnp.exp2(-5.0 - jnp.arange(RET_HEADS, dtype=F32)))
    idx = jnp.arange(C, dtype=F32)
    diff = idx[:, None] - idx[None, :]
    dec = jnp.where(diff >= 0, jnp.exp(log_gamma[:, None, None] * jnp.maximum(diff, 0.0)), 0.0)
    kd = jnp.exp(log_gamma[:, None] * (C - 1 - idx)[None, :])
    qd = jnp.exp(log_gamma[:, None] * (idx + 1.0)[None, :])
    cd = jnp.exp(log_gamma * C)
    bc = lambda t: jnp.broadcast_to(t[:, :, None], (RET_HEADS, C, d))
    cdb = jnp.broadcast_to(cd[:, None, None], (RET_HEADS, d, d))
    return cos2, sin2, dec, bc(qd), bc(kd), cdb


def _ret_call(pb, pf, gn_g, gn_b):
    B, S, _ = pb.shape
    cos2, sin2, dec, qd, kd, cd = _ret_tables(S)
    col = lambda off: pl.BlockSpec((1, S, RET_HEAD_DIM), lambda b, h: (b, 0, off + h))
    tab = pl.BlockSpec((1, RET_CHUNK, RET_HEAD_DIM), lambda b, h: (h, 0, 0))
    vec = pl.BlockSpec((1, RET_HEAD_DIM), lambda b, h: (0, h))
    return pl.pallas_call(
        _ret_body,
        grid=(B, RET_HEADS),
        in_specs=[col(0), col(RET_HEADS), col(2 * RET_HEADS), col(0),
                  _resident((S, RET_HEAD_DIM)), _resident((S, RET_HEAD_DIM)),
                  tab, tab, tab, tab, vec, vec],
        out_specs=pl.BlockSpec((1, S, RET_HEAD_DIM), lambda b, h: (b, 0, h)),
        out_shape=jax.ShapeDtypeStruct((B, S, RET_W), BF16),
        compiler_params=_params(("arbitrary", "arbitrary"), VMEM_LIMIT_SMALL),
        name="retention",
    )(pb, pb, pb, pf, cos2, sin2, dec, qd, kd, cd, gn_g.reshape(1, RET_W), gn_b.reshape(1, RET_W))


def _compress_body(xk_ref, xv_ref, pk_ref, pv_ref, w1k_ref, w1v_ref, w2k_ref, w2v_ref, kc_ref, vc_ref):
    for x_ref, p_ref, w1_ref, w2_ref, o_ref in ((xk_ref, pk_ref, w1k_ref, w2k_ref, kc_ref),
                                                (xv_ref, pv_ref, w1v_ref, w2v_ref, vc_ref)):
        acc = jnp.zeros(o_ref.shape[1:], F32)
        for g in range(NSA_KV_HEADS):
            x = x_ref[0, g]
            first = _dot((x + p_ref[0:1, :]).astype(BF16), w1_ref[0])
            second = _dot((x + p_ref[1:2, :]).astype(BF16), w1_ref[1])
            n_rows = x.shape[0]
            hid = _silu(first + pltpu.roll(second, n_rows - 1, 0))
            acc = acc + _dot(hid.astype(BF16), w2_ref[g])
        o_ref[0] = acc.astype(o_ref.dtype)


def _compress_call(xk, xv, pos_k, pos_v, w1k, w1v, w2k, w2v):
    B, G, n_half, flat = xk.shape
    half = lambda t: t.reshape(2, flat)
    w1 = lambda t: t.reshape(2, flat, CMP_HIDDEN).astype(BF16)

    def w2(t):
        z = jnp.zeros_like(t)
        return jnp.stack([jnp.concatenate([t, z], -1), jnp.concatenate([z, t], -1)]).astype(BF16)

    xspec = pl.BlockSpec((1, G, n_half, flat), lambda b: (b, 0, 0, 0))
    ospec = pl.BlockSpec((1, n_half, NSA_KVW), lambda b: (b, 0, 0))
    return pl.pallas_call(
        _compress_body,
        grid=(B,),
        in_specs=[xspec, xspec, _resident((2, flat)), _resident((2, flat)),
                  _resident((2, flat, CMP_HIDDEN)), _resident((2, flat, CMP_HIDDEN)),
                  _resident((2, CMP_HIDDEN, NSA_KVW)), _resident((2, CMP_HIDDEN, NSA_KVW))],
        out_specs=[ospec, ospec],
        out_shape=[jax.ShapeDtypeStruct((B, n_half, NSA_KVW), BF16)] * 2,
        compiler_params=_params(("arbitrary",), VMEM_LIMIT_SMALL),
        name="nsa_compress",
    )(xk, xv, half(pos_k), half(pos_v), w1(w1k), w1(w1v), w2(w2k), w2(w2v))


def _cmp_attn_body(q_ref, kc_ref, vc_ref, cb_ref, ov_ref, o_ref, sel_ref):
    i = pl.program_id(1)
    lane = lax.broadcasted_iota(jnp.int32, (TILE, LANES), 1)
    row = lax.broadcasted_iota(jnp.int32, (TILE, LANES), 0)
    lo_half = lane < NSA_HEAD_DIM
    kc = kc_ref[0]
    vc = vc_ref[0]
    zero = jnp.zeros_like(kc)
    kc_g = (jnp.where(lo_half, kc, zero), jnp.where(lo_half, zero, kc))
    valid = lane * CMP_STRIDE + (CMP_BLOCK - 1) <= i * TILE + row
    psum = [jnp.zeros((TILE, LANES), F32) for _ in range(NSA_KV_HEADS)]
    for r in range(NSA_GROUP):
        cols = slice(r * LANES, (r + 1) * LANES)
        q = q_ref[0, :, cols] * (NSA_HEAD_DIM ** -0.5)
        outs = []
        for g in range(NSA_KV_HEADS):
            s = _dot_nt(q, kc_g[g]) + cb_ref[g * NSA_GROUP + r]
            s = jnp.where(valid, s, NEG_INF)
            m = jnp.max(s, axis=-1, keepdims=True)
            e = jnp.where(valid, jnp.exp(s - m), 0.0)
            den = jnp.maximum(jnp.sum(e, axis=-1, keepdims=True), 1e-30)
            p = e / den
            psum[g] = psum[g] + p
            outs.append(_dot(p.astype(BF16), vc))
        o_ref[0, :, cols] = jnp.where(lo_half, outs[0], outs[1])

    n_sel = sel_ref.shape[2]
    blk = lax.broadcasted_iota(jnp.int32, (n_sel, TILE), 0)
    qblk = (i * TILE + lax.broadcasted_iota(jnp.int32, (n_sel, TILE), 1)) // SEL_BLOCK
    forced = jnp.where(blk == 0, 1.0, jnp.where(blk == qblk, 1.0, jnp.where(blk == qblk - 1, 1.0, 0.0)))
    for g in range(NSA_KV_HEADS):
        p_hi, p_lo = _split_bf16(psum[g])
        imp = _dot_nt(ov_ref[...], p_hi) + _dot_nt(ov_ref[...], p_lo)
        score = jnp.where(forced > 0.5, SEL_FORCE_SCORE, jnp.where(blk <= qblk, imp, -1.0))
        rank = jnp.zeros((n_sel, TILE), F32)
        for other in range(n_sel):
            so = score[other:other + 1, :]
            tie = jnp.where(blk > other, 1.0, 0.0)
            rank = rank + jnp.where(so > score, 1.0, jnp.where(so == score, tie, 0.0))
        sel_ref[0, g] = jnp.where(rank < float(min(SEL_TOP_N, n_sel)), 1.0, 0.0)


def _cmp_attn_call(pb, kc, vc, cmp_bias, overlap_t):
    B, S, _ = pb.shape
    n_sel = S // SEL_BLOCK
    q_block = (3 * RET_W) // NSA_QW
    return pl.pallas_call(
        _cmp_attn_body,
        grid=(B, S // TILE),
        in_specs=[pl.BlockSpec((1, TILE, NSA_QW), lambda b, i: (b, i, q_block)),
                  pl.BlockSpec((1,) + kc.shape[1:], lambda b, i: (b, 0, 0)),
                  pl.BlockSpec((1,) + vc.shape[1:], lambda b, i: (b, 0, 0)),
                  pl.BlockSpec((NSA_HEADS, TILE, LANES), lambda b, i: (0, i, 0)),
                  _resident((n_sel, LANES))],
        out_specs=[pl.BlockSpec((1, TILE, NSA_QW), lambda b, i: (b, i, 0)),
                   pl.BlockSpec((1, NSA_KV_HEADS, n_sel, TILE), lambda b, i: (b, 0, 0, i))],
        out_shape=[jax.ShapeDtypeStruct((B, S, NSA_QW), F32),
                   jax.ShapeDtypeStruct((B, NSA_KV_HEADS, n_sel, S), F32)],
        compiler_params=_params(("arbitrary", "arbitrary"), VMEM_LIMIT_SMALL),
        name="nsa_cmp_attn",
    )(pb, kc, vc, cmp_bias, overlap_t)


SLC_CLASS_TILES = 4


def _slc_far_tiles(cls, n_tiles):
    return min(SLC_CLASS_TILES * cls + 2, n_tiles - 2)


def _slc_body(q_ref, ks_ref, vs_ref, sel_ref, exw_ref, ext_ref, near_ref, o_ref, ksg_scr):
    i = pl.program_id(1)
    n_tiles = ks_ref.shape[1] // TILE
    lane = lax.broadcasted_iota(jnp.int32, (TILE, LANES), 1)
    lo_half = lane < NSA_HEAD_DIM

    @pl.when(i == 0)
    def _():
        for t in range(n_tiles):
            sl = pl.ds(t * TILE, TILE)
            kt = ks_ref[0, sl, :]
            zero = jnp.zeros_like(kt)
            ksg_scr[0, sl, :] = jnp.where(lo_half, kt, zero)
            ksg_scr[1, sl, :] = jnp.where(lo_half, zero, kt)

    prev = jnp.maximum(i - 1, 0)
    no_prev = jnp.where(i == 0, NEG_INF, 0.0)
    own_sl = pl.ds(pl.multiple_of(i * TILE, TILE), TILE)
    prev_sl = pl.ds(pl.multiple_of(prev * TILE, TILE), TILE)
    rows = NSA_GROUP * TILE

    def tile_body(n_far):
        wf = n_far * TILE
        far_ok = lax.broadcasted_iota(jnp.int32, (TILE, wf), 1) < (i - 1) * TILE
        v_near = jnp.concatenate([vs_ref[0, prev_sl, :], vs_ref[0, own_sl, :]], axis=0)
        q4 = jnp.concatenate([q_ref[0, :, r * LANES:(r + 1) * LANES] for r in range(NSA_GROUP)], axis=0)
        q4 = q4 * (NSA_HEAD_DIM ** -0.5)
        outs = []
        for g in range(NSA_KV_HEADS):
            sel_t = sel_ref[0, g].astype(BF16)
            m_far = jnp.where(far_ok, (_dot_tn(sel_t, exw_ref[:, :wf]) - 1.0) * (-NEG_INF), NEG_INF)
            m_near = jnp.concatenate([(_dot_tn(sel_t, ext_ref[prev]) - 1.0) * (-NEG_INF) + no_prev,
                                      (_dot_tn(sel_t, ext_ref[i]) - 1.0) * (-NEG_INF)], axis=1)
            k_near = jnp.concatenate([ksg_scr[g, prev_sl, :], ksg_scr[g, own_sl, :]], axis=0)
            s_far = (_dot_nt(q4, ksg_scr[g, :wf, :]).reshape(NSA_GROUP, TILE, wf) + m_far[None]).reshape(rows, wf)
            s_near = ((_dot_nt(q4, k_near) + near_ref[g]).reshape(NSA_GROUP, TILE, 2 * TILE)
                      + m_near[None]).reshape(rows, 2 * TILE)
            m = jnp.maximum(jnp.max(s_far, axis=-1, keepdims=True), jnp.max(s_near, axis=-1, keepdims=True))
            e_far = jnp.exp(s_far - m)
            e_near = jnp.exp(s_near - m)
            den = jnp.sum(e_far, axis=-1, keepdims=True) + jnp.sum(e_near, axis=-1, keepdims=True)
            outs.append((_dot(e_far.astype(BF16), vs_ref[0, :wf, :]) + _dot(e_near.astype(BF16), v_near)) / den)
        for r in range(NSA_GROUP):
            rs = slice(r * TILE, (r + 1) * TILE)
            o_ref[0, :, r * LANES:(r + 1) * LANES] = jnp.where(lo_half, outs[0][rs], outs[1][rs])

    n_classes = -(-n_tiles // SLC_CLASS_TILES)
    for cls in range(n_classes):
        pl.when(i // SLC_CLASS_TILES == cls)(functools.partial(tile_body, _slc_far_tiles(cls, n_tiles)))


def _slc_call(pb, sel_t, expand_wide, expand_tiles, near):
    B, S, _ = pb.shape
    n_sel = S // SEL_BLOCK
    n_tiles = S // TILE
    q_block = (3 * RET_W) // NSA_QW
    ks_block = (3 * RET_W + NSA_QW) // NSA_KVW
    return pl.pallas_call(
        _slc_body,
        grid=(B, n_tiles),
        in_specs=[pl.BlockSpec((1, TILE, NSA_QW), lambda b, i: (b, i, q_block)),
                  pl.BlockSpec((1, S, NSA_KVW), lambda b, i: (b, 0, ks_block)),
                  pl.BlockSpec((1, S, NSA_KVW), lambda b, i: (b, 0, ks_block + 1)),
                  pl.BlockSpec((1, NSA_KV_HEADS, n_sel, TILE), lambda b, i: (b, 0, 0, i)),
                  _resident((n_sel, S)), _resident((n_tiles, n_sel, TILE)),
                  _resident((NSA_KV_HEADS, NSA_GROUP * TILE, 2 * TILE))],
        out_specs=pl.BlockSpec((1, TILE, NSA_QW), lambda b, i: (b, i, 0)),
        out_shape=jax.ShapeDtypeStruct((B, S, NSA_QW), F32),
        scratch_shapes=[pltpu.VMEM((NSA_KV_HEADS, S, NSA_KVW), BF16)],
        compiler_params=_params(("arbitrary", "arbitrary"), VMEM_LIMIT_LARGE),
        name="nsa_selected",
    )(pb, pb, pb, sel_t, expand_wide, expand_tiles, near)


def _win_body(q_ref, k_ref, v_ref, wb_ref, gate_ref, eg_ref, ocmp_ref, oslc_ref, y_ref):
    i = pl.program_id(1)
    n_prev = (WIN_SIZE - 1 + TILE - 1) // TILE
    lane = lax.broadcasted_iota(jnp.int32, (TILE, LANES), 1)
    lo_half = lane < NSA_HEAD_DIM
    k_tiles, v_tiles, negs = [], [], []
    for t in range(n_prev + 1):
        idx = i - n_prev + t
        sl = pl.ds(pl.multiple_of(jnp.maximum(idx, 0) * TILE, TILE), TILE)
        k_tiles.append(k_ref[0, sl, :])
        v_tiles.append(v_ref[0, sl, :])
        negs.append(jnp.full((TILE, TILE), jnp.where(idx < 0, NEG_INF, 0.0), F32))
    k_all = jnp.concatenate(k_tiles, axis=0)
    v_all = jnp.concatenate(v_tiles, axis=0)
    missing = jnp.concatenate(negs, axis=1)
    zero = jnp.zeros_like(k_all)
    lo_rows = lax.broadcasted_iota(jnp.int32, k_all.shape, 1) < NSA_HEAD_DIM
    k_g = (jnp.where(lo_rows, k_all, zero), jnp.where(lo_rows, zero, k_all))

    g_hi, g_lo = _split_bf16(jax.nn.sigmoid(gate_ref[0]))
    gates = _dot(g_hi, eg_ref[...]) + _dot(g_lo, eg_ref[...])

    for r in range(NSA_GROUP):
        cols = slice(r * LANES, (r + 1) * LANES)
        q = q_ref[0, :, cols] * (NSA_HEAD_DIM ** -0.5)
        outs = []
        for g in range(NSA_KV_HEADS):
            s = _dot_nt(q, k_g[g]) + (wb_ref[g * NSA_GROUP + r] + missing)
            m = jnp.max(s, axis=-1, keepdims=True)
            e = jnp.exp(s - m)
            den = jnp.sum(e, axis=-1, keepdims=True)
            outs.append(_dot(e.astype(BF16), v_all) / den)
        o_win = jnp.where(lo_half, outs[0], outs[1])
        y = (gates[:, r * LANES:(r + 1) * LANES] * ocmp_ref[0, :, cols]
             + gates[:, NSA_QW + r * LANES:NSA_QW + (r + 1) * LANES] * oslc_ref[0, :, cols]
             + gates[:, 2 * NSA_QW + r * LANES:2 * NSA_QW + (r + 1) * LANES] * o_win)
        y_ref[0, :, cols] = y.astype(y_ref.dtype)


def _win_call(pb, pf, win_bias, gate_expand, o_cmp, o_slc):
    B, S, _ = pb.shape
    q_block = (3 * RET_W) // NSA_QW
    kw_block = (3 * RET_W + NSA_QW) // NSA_KVW + 2
    gate_block = (RET_W + 2 * NSA_KVW) // LANES
    span = win_bias.shape[-1]
    tile_spec = pl.BlockSpec((1, TILE, NSA_QW), lambda b, i: (b, i, 0))
    return pl.pallas_call(
        _win_body,
        grid=(B, S // TILE),
        in_specs=[pl.BlockSpec((1, TILE, NSA_QW), lambda b, i: (b, i, q_block)),
                  pl.BlockSpec((1, S, NSA_KVW), lambda b, i: (b, 0, kw_block)),
                  pl.BlockSpec((1, S, NSA_KVW), lambda b, i: (b, 0, kw_block + 1)),
                  _resident((NSA_HEADS, TILE, span)),
                  pl.BlockSpec((1, TILE, LANES), lambda b, i: (b, i, gate_block)),
                  _resident((LANES, 3 * NSA_QW)),
                  tile_spec, tile_spec],
        out_specs=tile_spec,
        out_shape=jax.ShapeDtypeStruct((B, S, NSA_QW), BF16),
        compiler_params=_params(("arbitrary", "arbitrary"), VMEM_LIMIT_SMALL),
        name="nsa_window_combine",
    )(pb, pb, pb, win_bias, pf, gate_expand, o_cmp, o_slc)


def _post_body(h_ref, ma_ref, mb_ref, mod_ref, woa_ref, wob_ref, lng_ref, lnb_ref,
               wg_ref, wu_ref, wd_ref, o_ref):
    y = _dot(ma_ref[0], woa_ref[...]) + _dot(mb_ref[0], wob_ref[...])
    h1 = _layer_norm(DEEPNORM_ALPHA * h_ref[0] + mod_ref[0, 2:3, :] * y, lng_ref[0:1, :], lnb_ref[0:1, :])
    u = (h1 * (1.0 + mod_ref[0, 4:5, :]) + mod_ref[0, 3:4, :]).astype(BF16)
    acc = jnp.zeros(h1.shape, F32)
    for c0, cn in FFN_CHUNKS:
        gate = _dot(u, wg_ref[:, c0:c0 + cn])
        up = _dot(u, wu_ref[:, c0:c0 + cn])
        acc = acc + _dot((_silu(gate) * up).astype(BF16), wd_ref[c0:c0 + cn, :])
    o_ref[0] = _layer_norm(DEEPNORM_ALPHA * h1 + mod_ref[0, 5:6, :] * acc, lng_ref[1:2, :], lnb_ref[1:2, :])


def _post_call(h, mix_a, mix_b, col_a, col_b, mod, wo_a, wo_b, ln_g, ln_b, w_gate, w_up, w_down, tm, name):
    B, S, _ = h.shape
    half = D_MODEL // 2
    tok = pl.BlockSpec((1, tm, D_MODEL), lambda b, t: (b, t, 0))
    return pl.pallas_call(
        _post_body,
        grid=(B, S // tm),
        in_specs=[tok,
                  pl.BlockSpec((1, tm, half), lambda b, t: (b, t, col_a)),
                  pl.BlockSpec((1, tm, half), lambda b, t: (b, t, col_b)),
                  pl.BlockSpec((1, 6, D_MODEL), lambda b, t: (b, 0, 0)),
                  _resident((half, D_MODEL)), _resident((half, D_MODEL)),
                  _resident((2, D_MODEL)), _resident((2, D_MODEL)),
                  _resident((D_MODEL, D_FF)), _resident((D_MODEL, D_FF)), _resident((D_FF, D_MODEL))],
        out_specs=tok,
        out_shape=jax.ShapeDtypeStruct((B, S, D_MODEL), F32),
        compiler_params=_params(("arbitrary", "arbitrary"), VMEM_LIMIT_LARGE),
        name=name,
    )(h, mix_a, mix_b, mod, wo_a, wo_b, ln_g, ln_b, w_gate, w_up, w_down)


def _dil_body(q_ref, k_ref, v_ref, b_ref, o_ref, lse_ref, *, tiles_per_seg):
    i = pl.program_id(1)
    lane = lax.broadcasted_iota(jnp.int32, (TILE, LANES), 1)
    own_sl = pl.ds(pl.multiple_of(i * TILE, TILE), TILE)
    if tiles_per_seg > 1:
        prev_sl = pl.ds(pl.multiple_of(jnp.maximum(i - 1, 0) * TILE, TILE), TILE)
        no_prev = jnp.where(i % tiles_per_seg == 0, NEG_INF, 0.0)
    lse_tile = jnp.zeros((TILE, LANES), F32)
    for h in range(DIL_HEADS):
        cols = slice(h * DIL_HEAD_DIM, (h + 1) * DIL_HEAD_DIM)
        q = q_ref[0, :, cols]
        if tiles_per_seg > 1:
            k = jnp.concatenate([k_ref[0, prev_sl, cols], k_ref[0, own_sl, cols]], axis=0)
            v = jnp.concatenate([v_ref[0, prev_sl, cols], v_ref[0, own_sl, cols]], axis=0)
            bias = jnp.concatenate([b_ref[h, :, :TILE] + no_prev, b_ref[h, :, TILE:]], axis=1)
        else:
            k = k_ref[0, own_sl, cols]
            v = v_ref[0, own_sl, cols]
            bias = b_ref[h, :, TILE:]
        s = _dot_nt(q, k) * (DIL_HEAD_DIM ** -0.5) + bias
        m = jnp.max(s, axis=-1, keepdims=True)
        e = jnp.exp(s - m)
        den = jnp.sum(e, axis=-1, keepdims=True)
        o_ref[0, :, cols] = (_dot(e.astype(BF16), v) / den).astype(o_ref.dtype)
        lse_tile = jnp.where(lane == h, m + jnp.log(den), lse_tile)
    lse_ref[0] = lse_tile


def _dil_call(proj, bias, dilation, name):
    B, S, _ = proj.shape
    seg = S // dilation
    tiles_per_seg = seg // TILE
    seq = lambda c: pl.BlockSpec((1, S, DIL_WIDTH), lambda b, i: (b, 0, c))
    nat = lambda width: pl.BlockSpec((1, TILE, width), lambda b, i: (b, i % tiles_per_seg, i // tiles_per_seg))
    o, lse = pl.pallas_call(
        functools.partial(_dil_body, tiles_per_seg=tiles_per_seg),
        grid=(B, S // TILE),
        in_specs=[pl.BlockSpec((1, TILE, DIL_WIDTH), lambda b, i: (b, i, 0)), seq(1), seq(2),
                  _resident((DIL_HEADS, TILE, 2 * TILE))],
        out_specs=[nat(DIL_WIDTH), nat(LANES)],
        out_shape=[jax.ShapeDtypeStruct((B, seg, dilation * DIL_WIDTH), BF16),
                   jax.ShapeDtypeStruct((B, seg, dilation * LANES), F32)],
        compiler_params=_params(("arbitrary", "arbitrary"), VMEM_LIMIT_LARGE),
        name=name,
    )(proj, proj, proj, bias)
    return o.reshape(B, S, DIL_WIDTH), lse.reshape(B, S, LANES)


def _dil_mix_body(o0_ref, o1_ref, o2_ref, l0_ref, l1_ref, l2_ref, y_ref):
    o_refs = (o0_ref, o1_ref, o2_ref)
    l_refs = (l0_ref, l1_ref, l2_ref)
    rows = y_ref.shape[1]
    for h in range(DIL_HEADS):
        cols = slice(h * DIL_HEAD_DIM, (h + 1) * DIL_HEAD_DIM)
        lses = [jnp.broadcast_to(l_ref[0, :, h:h + 1], (rows, DIL_HEAD_DIM)) for l_ref in l_refs]
        m = jnp.maximum(jnp.maximum(lses[0], lses[1]), lses[2])
        ws = [jnp.exp(l - m) for l in lses]
        den = ws[0] + ws[1] + ws[2]
        y = sum((w / den) * o_ref[0, :, cols].astype(F32) for w, o_ref in zip(ws, o_refs))
        y_ref[0, :, cols] = y.astype(y_ref.dtype)


def _dil_mix_call(outs, lses, tm):
    B, S, _ = outs[0].shape
    ospec = pl.BlockSpec((1, tm, DIL_WIDTH), lambda b, t: (b, t, 0))
    lspec = pl.BlockSpec((1, tm, LANES), lambda b, t: (b, t, 0))
    return pl.pallas_call(
        _dil_mix_body,
        grid=(B, S // tm),
        in_specs=[ospec] * 3 + [lspec] * 3,
        out_specs=ospec,
        out_shape=jax.ShapeDtypeStruct((B, S, DIL_WIDTH), BF16),
        compiler_params=_params(("arbitrary", "arbitrary"), VMEM_LIMIT_SMALL),
        name="dilated_mix",
    )(*outs, *lses)


def _t5_bucket_np(dist):
    n = np.maximum(dist, 0)
    max_exact = REL_BUCKETS // 2
    nf = np.maximum(n, 1).astype(np.float64)
    val = np.log(nf / max_exact) / math.log(REL_MAX_DIST / max_exact) * (REL_BUCKETS - max_exact)
    frac = np.abs(val - np.round(val))
    on_edge = (frac < 1e-9) & (n > max_exact) & (n < REL_MAX_DIST)
    assert not on_edge.any()
    large = np.minimum(max_exact + np.floor(val + 1e-9).astype(np.int64), REL_BUCKETS - 1)
    return np.where(n < max_exact, n, large).astype(np.int32)


def _shift_table(rel_bias, rows, cols, step, dist_fn, valid_fn):
    u = np.concatenate([np.arange(cols), np.arange(-(rows - 1) * step, 0)])
    period = u.size
    vals = jnp.take(rel_bias, jnp.asarray(_t5_bucket_np(dist_fn(u))), axis=0).T
    vals = jnp.where(jnp.asarray(valid_fn(u))[None], vals, NEG_INF)
    t = jnp.tile(vals, (1, rows))[:, :rows * (period - step)].reshape(vals.shape[0], rows, period - step)
    return t[:, :, :cols]


def _nsa_tables(rel_bias, S):
    always = lambda u: np.ones(u.shape, bool)
    n_prev = (WIN_SIZE - 1 + TILE - 1) // TILE
    win_dist = lambda u: n_prev * TILE - u
    win_bias = _shift_table(rel_bias, TILE, (n_prev + 1) * TILE, 1, win_dist,
                            lambda u: (win_dist(u) >= 0) & (win_dist(u) <= WIN_SIZE - 1))
    d0 = _shift_table(rel_bias, TILE, TILE, 1, lambda u: -u, lambda u: u <= 0)
    d1 = _shift_table(rel_bias, TILE, TILE, 1, lambda u: TILE - u, always)
    far_bucket = _t5_bucket_np(np.arange(TILE + 1, S + TILE))
    assert (far_bucket == far_bucket[0]).all()
    far = rel_bias[int(far_bucket[0])][:, None, None]
    near = jnp.concatenate([d1 - far, d0 - far], axis=2)
    near = near.reshape(NSA_KV_HEADS, NSA_GROUP * TILE, 2 * TILE)
    cmp_bias = _shift_table(rel_bias, LANES, S, CMP_STRIDE, lambda u: u - (CMP_BLOCK - 1), always)
    cmp_bias = cmp_bias.transpose(0, 2, 1)
    n_cmp = (S - CMP_BLOCK) // CMP_STRIDE + 1
    n_sel = S // SEL_BLOCK
    cs = (np.arange(n_cmp) * CMP_STRIDE)[:, None]
    ss = (np.arange(n_sel) * SEL_BLOCK)[None, :]
    ov = np.clip(np.minimum(cs + CMP_BLOCK, ss + SEL_BLOCK) - np.maximum(cs, ss), 0, None) / CMP_BLOCK
    ov_t = np.zeros((n_sel, LANES), np.float32)
    ov_t[:, :n_cmp] = ov.T
    key_blk = np.arange(S) // SEL_BLOCK
    ex_wide = (np.arange(n_sel)[:, None] == key_blk[None, :]).astype(np.float32)
    ex_tiles = ex_wide.reshape(n_sel, S // TILE, TILE).transpose(1, 0, 2)
    eg = np.zeros((LANES, 3 * NSA_QW), np.float32)
    for g in range(NSA_KV_HEADS):
        for r in range(NSA_GROUP):
            for j in range(3):
                base = j * NSA_QW + r * LANES + g * NSA_HEAD_DIM
                eg[g * NSA_GROUP * 3 + r * 3 + j, base:base + NSA_HEAD_DIM] = 1.0
    return (win_bias, near, cmp_bias, jnp.asarray(ov_t, BF16), jnp.asarray(ex_wide, BF16),
            jnp.asarray(ex_tiles, BF16), jnp.asarray(eg, BF16))


def _dil_bias(rel_bias, dilation, max_dist):
    dist = lambda u: TILE - u
    return _shift_table(rel_bias, TILE, 2 * TILE, 1, lambda u: dist(u) * dilation,
                        lambda u: (dist(u) >= 0) & (dist(u) <= max_dist))


def _nsa_head_perm():
    perm = np.zeros(NSA_QW, np.int64)
    for r in range(NSA_GROUP):
        for g in range(NSA_KV_HEADS):
            new = r * LANES + g * NSA_HEAD_DIM
            old = (g * NSA_GROUP + r) * NSA_HEAD_DIM
            perm[new:new + NSA_HEAD_DIM] = np.arange(old, old + NSA_HEAD_DIM)
    return perm


def _layer0_mixer(h, mod, ab_w_in, rel_bias, gn_g, gn_b, pos_k, pos_v, w1k, w2k, w1v, w2v):
    B, S, _ = h.shape
    o = np.cumsum((0, RET_W, RET_W, RET_W, RET_W, NSA_QW) + (NSA_KVW,) * 6 + (3 * NSA_HEADS,))
    seg = lambda a: ab_w_in[:, o[a]:o[a + 1]]
    gate_w = jnp.pad(seg(11), ((0, 0), (0, LANES - 3 * NSA_HEADS)))
    w = jnp.concatenate([seg(0), seg(1), seg(2), seg(4)[:, _nsa_head_perm()], seg(7), seg(8), seg(9), seg(10),
                         seg(3), seg(5), seg(6), gate_w], axis=1).astype(BF16)
    pb, pf = _pre_call(h, mod, w, ((PRE0_BF16_COLS, BF16), (PRE0_F32_COLS, F32)), 512, "pre0")

    y_ret = _ret_call(pb, pf, gn_g, gn_b)

    win_bias, near, cmp_bias, ov_t, ex_wide, ex_tiles, eg = _nsa_tables(rel_bias, S)
    half = S // CMP_STRIDE

    def half_blocks(t):
        t = t.reshape(B, half, CMP_STRIDE, NSA_KV_HEADS, NSA_HEAD_DIM).transpose(0, 3, 1, 2, 4)
        return t.reshape(B, NSA_KV_HEADS, half, CMP_STRIDE * NSA_HEAD_DIM)

    kc, vc = _compress_call(half_blocks(pf[:, :, RET_W:RET_W + NSA_KVW]),
                            half_blocks(pf[:, :, RET_W + NSA_KVW:RET_W + 2 * NSA_KVW]),
                            pos_k, pos_v, w1k, w1v, w2k, w2v)
    o_cmp, sel_t = _cmp_attn_call(pb, kc, vc, cmp_bias, ov_t)
    o_slc = _slc_call(pb, sel_t, ex_wide, ex_tiles, near)
    y_nsa = _win_call(pb, pf, win_bias, eg, o_cmp, o_slc)
    return y_ret, y_nsa


def _layer1_mixer(h, mod, dil_w_in, rel_bias):
    B, S, _ = h.shape
    outs, lses = [], []
    for gi, (window, dilation) in enumerate(DIL_PATTERNS):
        w = dil_w_in[:, gi * 3 * DIL_WIDTH:(gi + 1) * 3 * DIL_WIDTH].astype(BF16)
        proj, = _pre_call(h, mod, w, ((3 * DIL_WIDTH, BF16),), 512, f"pre1_{gi}", dilation)
        bias = _dil_bias(rel_bias, dilation, window // dilation)
        o, lse = _dil_call(proj, bias, dilation, f"dilated_{gi}")
        outs.append(o)
        lses.append(lse)
    return _dil_mix_call(outs, lses, 256)


def kernel(x, c, rel_bias, ada_w, ada_b, ln_g, ln_b, ab_w_in, ab_w_out, ret_gn_g, ret_gn_b, cmp_pos_k, cmp_pos_v, cmp_k_w1, cmp_k_w2, cmp_v_w1, cmp_v_w2, dil_w_in, dil_w_out, ffn_w_gate, ffn_w_up, ffn_w_down):
    B = x.shape[0]
    mod = _ada_call(c, ada_w, ada_b).reshape(DEPTH, B, 6, D_MODEL)
    h = x
    for layer in range(DEPTH):
        i = layer // 2
        if layer % 2 == 0:
            mix_a, mix_b = _layer0_mixer(h, mod[layer], ab_w_in[i], rel_bias, ret_gn_g[i], ret_gn_b[i],
                                         cmp_pos_k[i], cmp_pos_v[i], cmp_k_w1[i], cmp_k_w2[i],
                                         cmp_v_w1[i], cmp_v_w2[i])
            col_a, col_b = 0, 0
            wo_a = ab_w_out[i, :RET_W]
            wo_b = ab_w_out[i, RET_W:][_nsa_head_perm()]
        else:
            mix_a = mix_b = _layer1_mixer(h, mod[layer], dil_w_in[i], rel_bias)
            col_a, col_b = 0, 1
            wo_a = dil_w_out[i, :D_MODEL // 2]
            wo_b = dil_w_out[i, D_MODEL // 2:]
        h = _post_call(h, mix_a, mix_b, col_a, col_b, mod[layer], wo_a.astype(BF16), wo_b.astype(BF16),
                       ln_g[layer], ln_b[layer], ffn_w_gate[layer].astype(BF16), ffn_w_up[layer].astype(BF16),
                       ffn_w_down[layer].astype(BF16), 512, f"post{layer}")
    return h
```

```python
import functools
import math

import numpy as np
import jax
import jax.numpy as jnp
from jax import lax
from jax.experimental import pallas as pl
from jax.experimental.pallas import tpu as pltpu

F32 = jnp.float32
BF16 = jnp.bfloat16

D_MODEL = 1024
DEPTH = 2
DEEPNORM_ALPHA = (2 * DEPTH) ** 0.25
LN_EPS = 1e-5
NEG_INF = -1e30

RET_HEADS = 4
RET_HEAD_DIM = 128
RET_CHUNK = 128
ROPE_BASE = 10000.0
RET_W = RET_HEADS * RET_HEAD_DIM

NSA_HEADS = 8
NSA_KV_HEADS = 2
NSA_GROUP = 4
NSA_HEAD_DIM = 64
CMP_BLOCK = 32
CMP_STRIDE = 16
CMP_HIDDEN = 256
SEL_BLOCK = 64
SEL_TOP_N = 16
SEL_FORCE_SCORE = 1e4
WIN_SIZE = 512
NSA_QW = NSA_HEADS * NSA_HEAD_DIM
NSA_KVW = NSA_KV_HEADS * NSA_HEAD_DIM

DIL_PATTERNS = ((128, 1), (512, 4), (2048, 16))
DIL_HEADS = 8
DIL_HEAD_DIM = 128
DIL_WIDTH = DIL_HEADS * DIL_HEAD_DIM

REL_BUCKETS = 32
REL_MAX_DIST = 128
D_FF = 2816

LANES = 128
TILE = 128
VMEM_LIMIT_SMALL = 32 * 1024 * 1024
VMEM_LIMIT_LARGE = 56 * 1024 * 1024

PRE0_BF16_COLS = 3 * RET_W + NSA_QW + 4 * NSA_KVW
PRE0_F32_COLS = RET_W + 2 * NSA_KVW + LANES
FFN_CHUNKS = ((0, 768), (768, 768), (1536, 768), (2304, 512))


def _dot(a, b):
    return jnp.dot(a, b, preferred_element_type=F32)


def _dot_nt(a, b):
    return lax.dot_general(a, b, (((1,), (1,)), ((), ())), preferred_element_type=F32)


def _dot_tn(a, b):
    return lax.dot_general(a, b, (((0,), (0,)), ((), ())), preferred_element_type=F32)


def _split_bf16(x):
    hi = x.astype(BF16)
    lo = (x - hi.astype(F32)).astype(BF16)
    return hi, lo


def _silu(x):
    return x * jax.nn.sigmoid(x)


def _layer_norm(x, g, b):
    mu = jnp.mean(x, axis=-1, keepdims=True)
    xc = x - mu
    var = jnp.mean(xc * xc, axis=-1, keepdims=True)
    return xc * lax.rsqrt(var + LN_EPS) * g + b


def _resident(shape):
    return pl.BlockSpec(shape, lambda *_: (0,) * len(shape), pipeline_mode=pl.Buffered(1))


def _params(sem, vmem):
    return pltpu.CompilerParams(dimension_semantics=sem, vmem_limit_bytes=vmem)


def _ada_body(c_ref, w_ref, b_ref, o_ref):
    a_hi, a_lo = _split_bf16(_silu(c_ref[...]))
    w_hi, w_lo = _split_bf16(w_ref[0])
    o_ref[0] = _dot(a_hi, w_hi) + _dot(a_lo, w_hi) + _dot(a_hi, w_lo) + b_ref[0]


def _ada_call(c, ada_w, ada_b):
    B = c.shape[0]
    n_out = ada_w.shape[-1]
    tn = n_out // 4
    return pl.pallas_call(
        _ada_body,
        grid=(DEPTH, n_out // tn),
        in_specs=[pl.BlockSpec((B, D_MODEL), lambda l, n: (0, 0)),
                  pl.BlockSpec((1, D_MODEL, tn), lambda l, n: (l, 0, n)),
                  pl.BlockSpec((1, 1, tn), lambda l, n: (l, 0, n))],
        out_specs=pl.BlockSpec((1, B, tn), lambda l, n: (l, 0, n)),
        out_shape=jax.ShapeDtypeStruct((DEPTH, B, n_out), F32),
        compiler_params=_params(("arbitrary", "arbitrary"), VMEM_LIMIT_LARGE),
        name="ada_mod",
    )(c, ada_w, ada_b.reshape(DEPTH, 1, n_out))


def _pre_body(h_ref, mod_ref, w_ref, *refs, dilation):
    if dilation == 1:
        o_refs, h = refs, h_ref[0]
    else:
        o_refs, h_scr = refs[:-1], refs[-1]
        tm = h_ref.shape[1]
        for c in range(h_scr.shape[0]):
            h_scr[c] = h_ref[0, :, c * LANES:(c + 1) * LANES]
        h = jnp.concatenate(
            [jnp.concatenate([h_scr[c, pl.ds(r, tm // dilation, stride=dilation), :]
                              for c in range(h_scr.shape[0])], axis=1) for r in range(dilation)], axis=0)
    u = (h * (1.0 + mod_ref[0, 1:2, :]) + mod_ref[0, 0:1, :]).astype(BF16)
    off = 0
    for o_ref in o_refs:
        n = o_ref.shape[-1]
        o_ref[0] = _dot(u, w_ref[:, off:off + n]).astype(o_ref.dtype).reshape(o_ref.shape[1:])
        off += n


def _pre_call(h, mod, w, out_cols_dtypes, tm, name, dilation=1):
    B, S, _ = h.shape
    n_total = w.shape[1]
    assert sum(n for n, _ in out_cols_dtypes) == n_total
    if dilation == 1:
        out_specs = [pl.BlockSpec((1, tm, n), lambda b, t: (b, t, 0)) for n, _ in out_cols_dtypes]
        out_shape = [jax.ShapeDtypeStruct((B, S, n), dt) for n, dt in out_cols_dtypes]
        scratch = []
    else:
        out_specs = [pl.BlockSpec((1, dilation, tm // dilation, n), lambda b, t: (b, 0, t, 0))
                     for n, _ in out_cols_dtypes]
        out_shape = [jax.ShapeDtypeStruct((B, dilation, S // dilation, n), dt) for n, dt in out_cols_dtypes]
        scratch = [pltpu.VMEM((D_MODEL // LANES, tm, LANES), F32)]
    outs = pl.pallas_call(
        functools.partial(_pre_body, dilation=dilation),
        grid=(B, S // tm),
        in_specs=[pl.BlockSpec((1, tm, D_MODEL), lambda b, t: (b, t, 0)),
                  pl.BlockSpec((1, 6, D_MODEL), lambda b, t: (b, 0, 0)),
                  _resident((D_MODEL, n_total))],
        out_specs=out_specs,
        out_shape=out_shape,
        scratch_shapes=scratch,
        compiler_params=_params(("arbitrary", "arbitrary"), VMEM_LIMIT_LARGE),
        name=name,
    )(h, mod, w)
    return [o.reshape(B, S, o.shape[-1]) for o in outs]


def _ret_body(q_ref, k_ref, v_ref, g_ref, cos_ref, sin_ref, dec_ref, qd_ref, kd_ref, cd_ref,
              gng_ref, gnb_ref, o_ref):
    n_chunks = q_ref.shape[1] // RET_CHUNK
    state = jnp.zeros((RET_HEAD_DIM, RET_HEAD_DIM), F32)
    for n in range(n_chunks):
        sl = pl.ds(n * RET_CHUNK, RET_CHUNK)
        q = q_ref[0, sl, :].astype(F32)
        k = k_ref[0, sl, :].astype(F32)
        v = v_ref[0, sl, :]
        c2 = cos_ref[sl, :]
        s2 = sin_ref[sl, :]
        qr = (q * c2 + pltpu.roll(q, RET_HEAD_DIM // 2, 1) * s2) * (RET_HEAD_DIM ** -0.5)
        kr = k * c2 + pltpu.roll(k, RET_HEAD_DIM // 2, 1) * s2
        scores = _dot_nt(qr.astype(BF16), kr.astype(BF16)) * dec_ref[0]
        inner = _dot(scores.astype(BF16), v)
        cross = _dot((qr * qd_ref[0]).astype(BF16), state.astype(BF16))
        kv = _dot_tn((kr * kd_ref[0]).astype(BF16), v)
        state = state * cd_ref[0] + kv
        y = inner + cross
        mu = jnp.mean(y, axis=-1, keepdims=True)
        yc = y - mu
        var = jnp.mean(yc * yc, axis=-1, keepdims=True)
        yn = yc * lax.rsqrt(var + LN_EPS)
        gate = g_ref[0, sl, :]
        o_ref[0, sl, :] = ((yn * gng_ref[...] + gnb_ref[...]) * _silu(gate)).astype(o_ref.dtype)


def _ret_tables(S):
    d = RET_HEAD_DIM
    inv = ROPE_BASE ** (-jnp.arange(0, d, 2, dtype=F32) / d)
    ang = jnp.arange(S).astype(F32)[:, None] * inv[None, :]
    cos, sin = jnp.cos(ang), jnp.sin(ang)
    cos2 = jnp.concatenate([cos, cos], axis=-1)
    sin2 = jnp.concatenate([-sin, sin], axis=-1)
    C = RET_CHUNK
    log_gamma = jnp.log1p(-jnp.exp2(-5.0 - jnp.arange(RET_HEADS, dtype=F32)))
    idx = jnp.arange(C, dtype=F32)
    diff = idx[:, None] - idx[None, :]
    dec = jnp.where(diff >= 0, jnp.exp(log_gamma[:, None, None] * jnp.maximum(diff, 0.0)), 0.0)
    kd = jnp.exp(log_gamma[:, None] * (C - 1 - idx)[None, :])
    qd = jnp.exp(log_gamma[:, None] * (idx + 1.0)[None, :])
    cd = jnp.exp(log_gamma * C)
    bc = lambda t: jnp.broadcast_to(t[:, :, None], (RET_HEADS, C, d))
    cdb = jnp.broadcast_to(cd[:, None, None], (RET_HEADS, d, d))
    return cos2, sin2, dec, bc(qd), bc(kd), cdb


def _ret_call(pb, pf, gn_g, gn_b):
    B, S, _ = pb.shape
    cos2, sin2, dec, qd, kd, cd = _ret_tables(S)
    col = lambda off: pl.BlockSpec((1, S, RET_HEAD_DIM), lambda b, h: (b, 0, off + h))
    tab = pl.BlockSpec((1, RET_CHUNK, RET_HEAD_DIM), lambda b, h: (h, 0, 0))
    vec = pl.BlockSpec((1, RET_HEAD_DIM), lambda b, h: (0, h))
    return pl.pallas_call(
        _ret_body,
        grid=(B, RET_HEADS),
        in_specs=[col(0), col(RET_HEADS), col(2 * RET_HEADS), col(0),
                  _resident((S, RET_HEAD_DIM)), _resident((S, RET_HEAD_DIM)),
                  tab, tab, tab, tab, vec, vec],
        out_specs=pl.BlockSpec((1, S, RET_HEAD_DIM), lambda b, h: (b, 0, h)),
        out_shape=jax.ShapeDtypeStruct((B, S, RET_W), BF16),
        compiler_params=_params(("arbitrary", "arbitrary"), VMEM_LIMIT_SMALL),
        name="retention",
    )(pb, pb, pb, pf, cos2, sin2, dec, qd, kd, cd, gn_g.reshape(1, RET_W), gn_b.reshape(1, RET_W))


def _compress_body(xk_ref, xv_ref, pk_ref, pv_ref, w1k_ref, w1v_ref, w2k_ref, w2v_ref, kc_ref, vc_ref):
    for x_ref, p_ref, w1_ref, w2_ref, o_ref in ((xk_ref, pk_ref, w1k_ref, w2k_ref, kc_ref),
                                                (xv_ref, pv_ref, w1v_ref, w2v_ref, vc_ref)):
        acc = jnp.zeros(o_ref.shape[1:], F32)
        for g in range(NSA_KV_HEADS):
            x = x_ref[0, g]
            first = _dot((x + p_ref[0:1, :]).astype(BF16), w1_ref[0])
            second = _dot((x + p_ref[1:2, :]).astype(BF16), w1_ref[1])
            n_rows = x.shape[0]
            hid = _silu(first + pltpu.roll(second, n_rows - 1, 0))
            acc = acc + _dot(hid.astype(BF16), w2_ref[g])
        o_ref[0] = acc.astype(o_ref.dtype)


def _compress_call(xk, xv, pos_k, pos_v, w1k, w1v, w2k, w2v):
    B, G, n_half, flat = xk.shape
    half = lambda t: t.reshape(2, flat)
    w1 = lambda t: t.reshape(2, flat, CMP_HIDDEN).astype(BF16)

    def w2(t):
        z = jnp.zeros_like(t)
        return jnp.stack([jnp.concatenate([t, z], -1), jnp.concatenate([z, t], -1)]).astype(BF16)

    xspec = pl.BlockSpec((1, G, n_half, flat), lambda b: (b, 0, 0, 0))
    ospec = pl.BlockSpec((1, n_half, NSA_KVW), lambda b: (b, 0, 0))
    return pl.pallas_call(
        _compress_body,
        grid=(B,),
        in_specs=[xspec, xspec, _resident((2, flat)), _resident((2, flat)),
                  _resident((2, flat, CMP_HIDDEN)), _resident((2, flat, CMP_HIDDEN)),
                  _resident((2, CMP_HIDDEN, NSA_KVW)), _resident((2, CMP_HIDDEN, NSA_KVW))],
        out_specs=[ospec, ospec],
        out_shape=[jax.ShapeDtypeStruct((B, n_half, NSA_KVW), BF16)] * 2,
        compiler_params=_params(("arbitrary",), VMEM_LIMIT_SMALL),
        name="nsa_compress",
    )(xk, xv, half(pos_k), half(pos_v), w1(w1k), w1(w1v), w2(w2k), w2(w2v))


CMP_Q_ROWS = 512


def _cmp_attn_body(q_ref, kc_ref, vc_ref, cb_ref, ov_ref, o_ref, sel_ref):
    i = pl.program_id(1)
    tq = q_ref.shape[1]
    lane = lax.broadcasted_iota(jnp.int32, (tq, LANES), 1)
    row = lax.broadcasted_iota(jnp.int32, (tq, LANES), 0)
    lo_half = lane < NSA_HEAD_DIM
    kc = kc_ref[0]
    vc = vc_ref[0]
    zero = jnp.zeros_like(kc)
    kc_lo = lax.broadcasted_iota(jnp.int32, kc.shape, 1) < NSA_HEAD_DIM
    kc_g = (jnp.where(kc_lo, kc, zero), jnp.where(kc_lo, zero, kc))
    valid = (lane * CMP_STRIDE + (CMP_BLOCK - 1) <= i * tq + row)[None]
    q4 = jnp.concatenate([q_ref[0, :, r * LANES:(r + 1) * LANES] for r in range(NSA_GROUP)], axis=0)
    q4 = q4 * (NSA_HEAD_DIM ** -0.5)
    psum, outs = [], []
    for g in range(NSA_KV_HEADS):
        s = _dot_nt(q4, kc_g[g]).reshape(NSA_GROUP, tq, LANES) + cb_ref[g * NSA_GROUP:(g + 1) * NSA_GROUP]
        s = jnp.where(valid, s, NEG_INF)
        m = jnp.max(s, axis=-1, keepdims=True)
        e = jnp.where(valid, jnp.exp(s - m), 0.0)
        den = jnp.maximum(jnp.sum(e, axis=-1, keepdims=True), 1e-30)
        p = e / den
        psum.append(jnp.sum(p, axis=0))
        outs.append(_dot(p.reshape(NSA_GROUP * tq, LANES).astype(BF16), vc))
    for r in range(NSA_GROUP):
        rs = slice(r * tq, (r + 1) * tq)
        o_ref[0, :, r * LANES:(r + 1) * LANES] = jnp.where(lo_half, outs[0][rs], outs[1][rs])

    n_sel = sel_ref.shape[2]
    blk = lax.broadcasted_iota(jnp.int32, (n_sel, tq), 0)
    qblk = (i * tq + lax.broadcasted_iota(jnp.int32, (n_sel, tq), 1)) // SEL_BLOCK
    forced = jnp.where(blk == 0, 1.0, jnp.where(blk == qblk, 1.0, jnp.where(blk == qblk - 1, 1.0, 0.0)))
    for g in range(NSA_KV_HEADS):
        p_hi, p_lo = _split_bf16(psum[g])
        imp = _dot_nt(ov_ref[...], p_hi) + _dot_nt(ov_ref[...], p_lo)
        score = jnp.where(forced > 0.5, SEL_FORCE_SCORE, jnp.where(blk <= qblk, imp, -1.0))
        rank = jnp.zeros((n_sel, tq), F32)
        for other in range(n_sel):
            so = score[other:other + 1, :]
            tie = jnp.where(blk > other, 1.0, 0.0)
            rank = rank + jnp.where(so > score, 1.0, jnp.where(so == score, tie, 0.0))
        sel_ref[0, g] = jnp.where(rank < float(min(SEL_TOP_N, n_sel)), 1.0, 0.0)


def _cmp_attn_call(pb, kc, vc, cmp_bias, overlap_t):
    B, S, _ = pb.shape
    n_sel = S // SEL_BLOCK
    q_block = (3 * RET_W) // NSA_QW
    tq = CMP_Q_ROWS
    return pl.pallas_call(
        _cmp_attn_body,
        grid=(B, S // tq),
        in_specs=[pl.BlockSpec((1, tq, NSA_QW), lambda b, i: (b, i, q_block)),
                  pl.BlockSpec((1,) + kc.shape[1:], lambda b, i: (b, 0, 0)),
                  pl.BlockSpec((1,) + vc.shape[1:], lambda b, i: (b, 0, 0)),
                  pl.BlockSpec((NSA_HEADS, tq, LANES), lambda b, i: (0, i, 0)),
                  _resident((n_sel, LANES))],
        out_specs=[pl.BlockSpec((1, tq, NSA_QW), lambda b, i: (b, i, 0)),
                   pl.BlockSpec((1, NSA_KV_HEADS, n_sel, tq), lambda b, i: (b, 0, 0, i))],
        out_shape=[jax.ShapeDtypeStruct((B, S, NSA_QW), F32),
                   jax.ShapeDtypeStruct((B, NSA_KV_HEADS, n_sel, S), F32)],
        compiler_params=_params(("arbitrary", "arbitrary"), VMEM_LIMIT_SMALL),
        name="nsa_cmp_attn",
    )(pb, kc, vc, cmp_bias, overlap_t)


SLC_CLASS_TILES = 4


def _slc_far_tiles(cls, n_tiles):
    return min(SLC_CLASS_TILES * cls + 2, n_tiles - 2)


def _slc_body(q_ref, ks_ref, vs_ref, sel_ref, exw_ref, ext_ref, near_ref, o_ref, ksg_scr):
    i = pl.program_id(1)
    n_tiles = ks_ref.shape[1] // TILE
    lane = lax.broadcasted_iota(jnp.int32, (TILE, LANES), 1)
    lo_half = lane < NSA_HEAD_DIM

    @pl.when(i == 0)
    def _():
        for t in range(n_tiles):
            sl = pl.ds(t * TILE, TILE)
            kt = ks_ref[0, sl, :]
            zero = jnp.zeros_like(kt)
            ksg_scr[0, sl, :] = jnp.where(lo_half, kt, zero)
            ksg_scr[1, sl, :] = jnp.where(lo_half, zero, kt)

    prev = jnp.maximum(i - 1, 0)
    no_prev = jnp.where(i == 0, NEG_INF, 0.0)
    own_sl = pl.ds(pl.multiple_of(i * TILE, TILE), TILE)
    prev_sl = pl.ds(pl.multiple_of(prev * TILE, TILE), TILE)
    rows = NSA_GROUP * TILE

    def tile_body(n_far):
        wf = n_far * TILE
        far_ok = lax.broadcasted_iota(jnp.int32, (TILE, wf), 1) < (i - 1) * TILE
        v_near = jnp.concatenate([vs_ref[0, prev_sl, :], vs_ref[0, own_sl, :]], axis=0)
        q4 = jnp.concatenate([q_ref[0, :, r * LANES:(r + 1) * LANES] for r in range(NSA_GROUP)], axis=0)
        q4 = q4 * (NSA_HEAD_DIM ** -0.5)
        outs = []
        for g in range(NSA_KV_HEADS):
            sel_t = sel_ref[0, g].astype(BF16)
            m_far = jnp.where(far_ok, (_dot_tn(sel_t, exw_ref[:, :wf]) - 1.0) * (-NEG_INF), NEG_INF)
            m_near = jnp.concatenate([(_dot_tn(sel_t, ext_ref[prev]) - 1.0) * (-NEG_INF) + no_prev,
                                      (_dot_tn(sel_t, ext_ref[i]) - 1.0) * (-NEG_INF)], axis=1)
            k_near = jnp.concatenate([ksg_scr[g, prev_sl, :], ksg_scr[g, own_sl, :]], axis=0)
            s_far = (_dot_nt(q4, ksg_scr[g, :wf, :]).reshape(NSA_GROUP, TILE, wf) + m_far[None]).reshape(rows, wf)
            s_near = ((_dot_nt(q4, k_near) + near_ref[g]).reshape(NSA_GROUP, TILE, 2 * TILE)
                      + m_near[None]).reshape(rows, 2 * TILE)
            m = jnp.maximum(jnp.max(s_far, axis=-1, keepdims=True), jnp.max(s_near, axis=-1, keepdims=True))
            e_far = jnp.exp(s_far - m)
            e_near = jnp.exp(s_near - m)
            den = jnp.sum(e_far, axis=-1, keepdims=True) + jnp.sum(e_near, axis=-1, keepdims=True)
            outs.append((_dot(e_far.astype(BF16), vs_ref[0, :wf, :]) + _dot(e_near.astype(BF16), v_near)) / den)
        for r in range(NSA_GROUP):
            rs = slice(r * TILE, (r + 1) * TILE)
            o_ref[0, :, r * LANES:(r + 1) * LANES] = jnp.where(lo_half, outs[0][rs], outs[1][rs])

    n_classes = -(-n_tiles // SLC_CLASS_TILES)
    for cls in range(n_classes):
        pl.when(i // SLC_CLASS_TILES == cls)(functools.partial(tile_body, _slc_far_tiles(cls, n_tiles)))


def _slc_call(pb, sel_t, expand_wide, expand_tiles, near):
    B, S, _ = pb.shape
    n_sel = S // SEL_BLOCK
    n_tiles = S // TILE
    q_block = (3 * RET_W) // NSA_QW
    ks_block = (3 * RET_W + NSA_QW) // NSA_KVW
    return pl.pallas_call(
        _slc_body,
        grid=(B, n_tiles),
        in_specs=[pl.BlockSpec((1, TILE, NSA_QW), lambda b, i: (b, i, q_block)),
                  pl.BlockSpec((1, S, NSA_KVW), lambda b, i: (b, 0, ks_block)),
                  pl.BlockSpec((1, S, NSA_KVW), lambda b, i: (b, 0, ks_block + 1)),
                  pl.BlockSpec((1, NSA_KV_HEADS, n_sel, TILE), lambda b, i: (b, 0, 0, i)),
                  _resident((n_sel, S)), _resident((n_tiles, n_sel, TILE)),
                  _resident((NSA_KV_HEADS, NSA_GROUP * TILE, 2 * TILE))],
        out_specs=pl.BlockSpec((1, TILE, NSA_QW), lambda b, i: (b, i, 0)),
        out_shape=jax.ShapeDtypeStruct((B, S, NSA_QW), F32),
        scratch_shapes=[pltpu.VMEM((NSA_KV_HEADS, S, NSA_KVW), BF16)],
        compiler_params=_params(("arbitrary", "arbitrary"), VMEM_LIMIT_LARGE),
        name="nsa_selected",
    )(pb, pb, pb, sel_t, expand_wide, expand_tiles, near)


WIN_PREV_TILES = (WIN_SIZE - 1 + TILE - 1) // TILE
WIN_Q_TILES = 2


def _win_body(q_ref, k_ref, v_ref, wb_ref, gate_ref, eg_ref, ocmp_ref, oslc_ref, y_ref):
    i = pl.program_id(1)
    tq = q_ref.shape[1]
    n_span = WIN_PREV_TILES + WIN_Q_TILES
    first = i * WIN_Q_TILES - WIN_PREV_TILES
    lo_half = lax.broadcasted_iota(jnp.int32, (tq, LANES), 1) < NSA_HEAD_DIM

    def attend(early):
        if early:
            k_tiles, v_tiles, negs = [], [], []
            for t in range(n_span):
                sl = pl.ds(pl.multiple_of(jnp.maximum(first + t, 0) * TILE, TILE), TILE)
                k_tiles.append(k_ref[0, sl, :])
                v_tiles.append(v_ref[0, sl, :])
                negs.append(jnp.full((tq, TILE), jnp.where(first + t < 0, NEG_INF, 0.0), F32))
            k_all = jnp.concatenate(k_tiles, axis=0)
            v_all = jnp.concatenate(v_tiles, axis=0)
            missing = jnp.concatenate(negs, axis=1)[None]
        else:
            sl = pl.ds(pl.multiple_of(first * TILE, TILE), n_span * TILE)
            k_all = k_ref[0, sl, :]
            v_all = v_ref[0, sl, :]
        zero = jnp.zeros_like(k_all)
        k_lo = lax.broadcasted_iota(jnp.int32, k_all.shape, 1) < NSA_HEAD_DIM
        k_g = (jnp.where(k_lo, k_all, zero), jnp.where(k_lo, zero, k_all))

        g_hi, g_lo = _split_bf16(jax.nn.sigmoid(gate_ref[0]))
        gates = _dot(g_hi, eg_ref[...]) + _dot(g_lo, eg_ref[...])
        q4 = jnp.concatenate([q_ref[0, :, r * LANES:(r + 1) * LANES] for r in range(NSA_GROUP)], axis=0)
        q4 = q4 * (NSA_HEAD_DIM ** -0.5)
        outs = []
        for g in range(NSA_KV_HEADS):
            s = _dot_nt(q4, k_g[g]).reshape(NSA_GROUP, tq, n_span * TILE)
            s = s + wb_ref[g * NSA_GROUP:(g + 1) * NSA_GROUP]
            if early:
                s = s + missing
            m = jnp.max(s, axis=-1, keepdims=True)
            e = jnp.exp(s - m)
            den = jnp.sum(e, axis=-1, keepdims=True).reshape(NSA_GROUP * tq, 1)
            outs.append(_dot(e.reshape(NSA_GROUP * tq, n_span * TILE).astype(BF16), v_all) / den)
        for r in range(NSA_GROUP):
            cols = slice(r * LANES, (r + 1) * LANES)
            rs = slice(r * tq, (r + 1) * tq)
            o_win = jnp.where(lo_half, outs[0][rs], outs[1][rs])
            y = (gates[:, r * LANES:(r + 1) * LANES] * ocmp_ref[0, :, cols]
                 + gates[:, NSA_QW + r * LANES:NSA_QW + (r + 1) * LANES] * oslc_ref[0, :, cols]
                 + gates[:, 2 * NSA_QW + r * LANES:2 * NSA_QW + (r + 1) * LANES] * o_win)
            y_ref[0, :, cols] = y.astype(y_ref.dtype)

    pl.when(first >= 0)(functools.partial(attend, False))
    pl.when(first < 0)(functools.partial(attend, True))


def _win_call(pb, pf, win_bias, gate_expand, o_cmp, o_slc):
    B, S, _ = pb.shape
    q_block = (3 * RET_W) // NSA_QW
    kw_block = (3 * RET_W + NSA_QW) // NSA_KVW + 2
    gate_block = (RET_W + 2 * NSA_KVW) // LANES
    tq = WIN_Q_TILES * TILE
    tile_spec = pl.BlockSpec((1, tq, NSA_QW), lambda b, i: (b, i, 0))
    return pl.pallas_call(
        _win_body,
        grid=(B, S // tq),
        in_specs=[pl.BlockSpec((1, tq, NSA_QW), lambda b, i: (b, i, q_block)),
                  pl.BlockSpec((1, S, NSA_KVW), lambda b, i: (b, 0, kw_block)),
                  pl.BlockSpec((1, S, NSA_KVW), lambda b, i: (b, 0, kw_block + 1)),
                  _resident(win_bias.shape),
                  pl.BlockSpec((1, tq, LANES), lambda b, i: (b, i, gate_block)),
                  _resident((LANES, 3 * NSA_QW)),
                  tile_spec, tile_spec],
        out_specs=tile_spec,
        out_shape=jax.ShapeDtypeStruct((B, S, NSA_QW), BF16),
        compiler_params=_params(("arbitrary", "arbitrary"), VMEM_LIMIT_LARGE),
        name="nsa_window_combine",
    )(pb, pb, pb, win_bias, pf, gate_expand, o_cmp, o_slc)


def _post_body(h_ref, ma_ref, mb_ref, mod_ref, woa_ref, wob_ref, lng_ref, lnb_ref,
               wg_ref, wu_ref, wd_ref, o_ref):
    y = _dot(ma_ref[0], woa_ref[...]) + _dot(mb_ref[0], wob_ref[...])
    h1 = _layer_norm(DEEPNORM_ALPHA * h_ref[0] + mod_ref[0, 2:3, :] * y, lng_ref[0:1, :], lnb_ref[0:1, :])
    u = (h1 * (1.0 + mod_ref[0, 4:5, :]) + mod_ref[0, 3:4, :]).astype(BF16)
    acc = jnp.zeros(h1.shape, F32)
    for c0, cn in FFN_CHUNKS:
        gate = _dot(u, wg_ref[:, c0:c0 + cn])
        up = _dot(u, wu_ref[:, c0:c0 + cn])
        acc = acc + _dot((_silu(gate) * up).astype(BF16), wd_ref[c0:c0 + cn, :])
    o_ref[0] = _layer_norm(DEEPNORM_ALPHA * h1 + mod_ref[0, 5:6, :] * acc, lng_ref[1:2, :], lnb_ref[1:2, :])


def _post_call(h, mix_a, mix_b, col_a, col_b, mod, wo_a, wo_b, ln_g, ln_b, w_gate, w_up, w_down, tm, name):
    B, S, _ = h.shape
    half = D_MODEL // 2
    tok = pl.BlockSpec((1, tm, D_MODEL), lambda b, t: (b, t, 0))
    return pl.pallas_call(
        _post_body,
        grid=(B, S // tm),
        in_specs=[tok,
                  pl.BlockSpec((1, tm, half), lambda b, t: (b, t, col_a)),
                  pl.BlockSpec((1, tm, half), lambda b, t: (b, t, col_b)),
                  pl.BlockSpec((1, 6, D_MODEL), lambda b, t: (b, 0, 0)),
                  _resident((half, D_MODEL)), _resident((half, D_MODEL)),
                  _resident((2, D_MODEL)), _resident((2, D_MODEL)),
                  _resident((D_MODEL, D_FF)), _resident((D_MODEL, D_FF)), _resident((D_FF, D_MODEL))],
        out_specs=tok,
        out_shape=jax.ShapeDtypeStruct((B, S, D_MODEL), F32),
        compiler_params=_params(("arbitrary", "arbitrary"), VMEM_LIMIT_LARGE),
        name=name,
    )(h, mix_a, mix_b, mod, wo_a, wo_b, ln_g, ln_b, w_gate, w_up, w_down)


DIL_STEP_TILES = 4


def _dil_body(q_ref, k_ref, v_ref, b_ref, o_ref, lse_ref, *, tiles_per_seg):
    t = pl.program_id(1)
    res_tiles = min(tiles_per_seg, DIL_STEP_TILES)
    with_prev = tiles_per_seg > 1
    base = t * (DIL_STEP_TILES * TILE)
    lane = lax.broadcasted_iota(jnp.int32, (TILE, LANES), 1)
    if tiles_per_seg > DIL_STEP_TILES:
        first_prev = jnp.where((t * DIL_STEP_TILES) % tiles_per_seg == 0, NEG_INF, 0.0)
    else:
        first_prev = NEG_INF

    def span(ref, j, cols):
        own = pl.ds(pl.multiple_of(base + j * TILE, TILE), TILE)
        if not with_prev:
            return ref[0, own, cols]
        if j % res_tiles != 0:
            return ref[0, pl.ds(pl.multiple_of(base + (j - 1) * TILE, TILE), 2 * TILE), cols]
        prev = pl.ds(pl.multiple_of(jnp.maximum(base + (j - 1) * TILE, 0), TILE), TILE)
        return jnp.concatenate([ref[0, prev, cols], ref[0, own, cols]], axis=0)

    def tile_bias(h, j):
        if not with_prev:
            return b_ref[h, :, TILE:]
        if j % res_tiles != 0:
            return b_ref[h]
        gone = first_prev if j == 0 else NEG_INF
        return jnp.concatenate([b_ref[h, :, :TILE] + gone, b_ref[h, :, TILE:]], axis=1)

    tiles = [(h, j) for h in range(DIL_HEADS) for j in range(DIL_STEP_TILES)]
    head_cols = lambda h: slice(h * DIL_HEAD_DIM, (h + 1) * DIL_HEAD_DIM)
    scores = [_dot_nt(q_ref[0, j * TILE:(j + 1) * TILE, head_cols(h)], span(k_ref, j, head_cols(h)))
              + tile_bias(h, j) for h, j in tiles]
    s = jnp.concatenate(scores, axis=0)
    m = jnp.max(s, axis=-1, keepdims=True)
    e = jnp.exp(s - m)
    den = jnp.sum(e, axis=-1, keepdims=True)
    p = e.astype(BF16)
    inv = 1.0 / den
    lse = m + jnp.log(den)
    lse_tiles = [jnp.zeros((TILE, LANES), F32) for _ in range(DIL_STEP_TILES)]
    for n, (h, j) in enumerate(tiles):
        rows = slice(n * TILE, (n + 1) * TILE)
        r0 = (j % res_tiles) * TILE
        c0 = (j // res_tiles) * DIL_WIDTH + h * DIL_HEAD_DIM
        o = _dot(p[rows], span(v_ref, j, head_cols(h))) * inv[rows]
        o_ref[0, r0:r0 + TILE, c0:c0 + DIL_HEAD_DIM] = o.astype(o_ref.dtype)
        lse_tiles[j] = jnp.where(lane == h, lse[rows], lse_tiles[j])
    for j in range(DIL_STEP_TILES):
        r0 = (j % res_tiles) * TILE
        c0 = (j // res_tiles) * LANES
        lse_ref[0, r0:r0 + TILE, c0:c0 + LANES] = lse_tiles[j]


def _dil_call(proj, bias, dilation, name):
    B, S, _ = proj.shape
    seg = S // dilation
    tiles_per_seg = seg // TILE
    step = DIL_STEP_TILES * TILE
    n_res = max(1, step // seg)
    seg_steps = max(1, seg // step)
    seq = lambda c: pl.BlockSpec((1, S, DIL_WIDTH), lambda b, i: (b, 0, c))
    nat = lambda width: pl.BlockSpec((1, step // n_res, n_res * width),
                                     lambda b, i: (b, i % seg_steps, i // seg_steps))
    o, lse = pl.pallas_call(
        functools.partial(_dil_body, tiles_per_seg=tiles_per_seg),
        grid=(B, S // step),
        in_specs=[pl.BlockSpec((1, step, DIL_WIDTH), lambda b, i: (b, i, 0)), seq(1), seq(2),
                  _resident((DIL_HEADS, TILE, 2 * TILE))],
        out_specs=[nat(DIL_WIDTH), nat(LANES)],
        out_shape=[jax.ShapeDtypeStruct((B, seg, dilation * DIL_WIDTH), BF16),
                   jax.ShapeDtypeStruct((B, seg, dilation * LANES), F32)],
        compiler_params=_params(("arbitrary", "arbitrary"), VMEM_LIMIT_LARGE),
        name=name,
    )(proj, proj, proj, bias)
    return o.reshape(B, S, DIL_WIDTH), lse.reshape(B, S, LANES)


def _dil_mix_body(o0_ref, o1_ref, o2_ref, l0_ref, l1_ref, l2_ref, y_ref):
    o_refs = (o0_ref, o1_ref, o2_ref)
    l_refs = (l0_ref, l1_ref, l2_ref)
    rows = y_ref.shape[1]
    for h in range(DIL_HEADS):
        cols = slice(h * DIL_HEAD_DIM, (h + 1) * DIL_HEAD_DIM)
        lses = [jnp.broadcast_to(l_ref[0, :, h:h + 1], (rows, DIL_HEAD_DIM)) for l_ref in l_refs]
        m = jnp.maximum(jnp.maximum(lses[0], lses[1]), lses[2])
        ws = [jnp.exp(l - m) for l in lses]
        den = ws[0] + ws[1] + ws[2]
        y = sum((w / den) * o_ref[0, :, cols].astype(F32) for w, o_ref in zip(ws, o_refs))
        y_ref[0, :, cols] = y.astype(y_ref.dtype)


def _dil_mix_call(outs, lses, tm):
    B, S, _ = outs[0].shape
    ospec = pl.BlockSpec((1, tm, DIL_WIDTH), lambda b, t: (b, t, 0))
    lspec = pl.BlockSpec((1, tm, LANES), lambda b, t: (b, t, 0))
    return pl.pallas_call(
        _dil_mix_body,
        grid=(B, S // tm),
        in_specs=[ospec] * 3 + [lspec] * 3,
        out_specs=ospec,
        out_shape=jax.ShapeDtypeStruct((B, S, DIL_WIDTH), BF16),
        compiler_params=_params(("arbitrary", "arbitrary"), VMEM_LIMIT_SMALL),
        name="dilated_mix",
    )(*outs, *lses)


def _t5_bucket_np(dist):
    n = np.maximum(dist, 0)
    max_exact = REL_BUCKETS // 2
    nf = np.maximum(n, 1).astype(np.float64)
    val = np.log(nf / max_exact) / math.log(REL_MAX_DIST / max_exact) * (REL_BUCKETS - max_exact)
    frac = np.abs(val - np.round(val))
    on_edge = (frac < 1e-9) & (n > max_exact) & (n < REL_MAX_DIST)
    assert not on_edge.any()
    large = np.minimum(max_exact + np.floor(val + 1e-9).astype(np.int64), REL_BUCKETS - 1)
    return np.where(n < max_exact, n, large).astype(np.int32)


def _shift_table(rel_bias, rows, cols, step, dist_fn, valid_fn):
    u = np.concatenate([np.arange(cols), np.arange(-(rows - 1) * step, 0)])
    period = u.size
    vals = jnp.take(rel_bias, jnp.asarray(_t5_bucket_np(dist_fn(u))), axis=0).T
    vals = jnp.where(jnp.asarray(valid_fn(u))[None], vals, NEG_INF)
    t = jnp.tile(vals, (1, rows))[:, :rows * (period - step)].reshape(vals.shape[0], rows, period - step)
    return t[:, :, :cols]


def _nsa_tables(rel_bias, S):
    always = lambda u: np.ones(u.shape, bool)
    win_dist = lambda u: WIN_PREV_TILES * TILE - u
    win_bias = _shift_table(rel_bias, WIN_Q_TILES * TILE, (WIN_PREV_TILES + WIN_Q_TILES) * TILE, 1, win_dist,
                            lambda u: (win_dist(u) >= 0) & (win_dist(u) <= WIN_SIZE - 1))
    d0 = _shift_table(rel_bias, TILE, TILE, 1, lambda u: -u, lambda u: u <= 0)
    d1 = _shift_table(rel_bias, TILE, TILE, 1, lambda u: TILE - u, always)
    far_bucket = _t5_bucket_np(np.arange(TILE + 1, S + TILE))
    assert (far_bucket == far_bucket[0]).all()
    far = rel_bias[int(far_bucket[0])][:, None, None]
    near = jnp.concatenate([d1 - far, d0 - far], axis=2)
    near = near.reshape(NSA_KV_HEADS, NSA_GROUP * TILE, 2 * TILE)
    cmp_bias = _shift_table(rel_bias, LANES, S, CMP_STRIDE, lambda u: u - (CMP_BLOCK - 1), always)
    cmp_bias = cmp_bias.transpose(0, 2, 1)
    n_cmp = (S - CMP_BLOCK) // CMP_STRIDE + 1
    n_sel = S // SEL_BLOCK
    cs = (np.arange(n_cmp) * CMP_STRIDE)[:, None]
    ss = (np.arange(n_sel) * SEL_BLOCK)[None, :]
    ov = np.clip(np.minimum(cs + CMP_BLOCK, ss + SEL_BLOCK) - np.maximum(cs, ss), 0, None) / CMP_BLOCK
    ov_t = np.zeros((n_sel, LANES), np.float32)
    ov_t[:, :n_cmp] = ov.T
    key_blk = np.arange(S) // SEL_BLOCK
    ex_wide = (np.arange(n_sel)[:, None] == key_blk[None, :]).astype(np.float32)
    ex_tiles = ex_wide.reshape(n_sel, S // TILE, TILE).transpose(1, 0, 2)
    eg = np.zeros((LANES, 3 * NSA_QW), np.float32)
    for g in range(NSA_KV_HEADS):
        for r in range(NSA_GROUP):
            for j in range(3):
                base = j * NSA_QW + r * LANES + g * NSA_HEAD_DIM
                eg[g * NSA_GROUP * 3 + r * 3 + j, base:base + NSA_HEAD_DIM] = 1.0
    return (win_bias, near, cmp_bias, jnp.asarray(ov_t, BF16), jnp.asarray(ex_wide, BF16),
            jnp.asarray(ex_tiles, BF16), jnp.asarray(eg, BF16))


def _dil_bias(rel_bias, dilation, max_dist):
    dist = lambda u: TILE - u
    return _shift_table(rel_bias, TILE, 2 * TILE, 1, lambda u: dist(u) * dilation,
                        lambda u: (dist(u) >= 0) & (dist(u) <= max_dist))


def _nsa_head_perm():
    perm = np.zeros(NSA_QW, np.int64)
    for r in range(NSA_GROUP):
        for g in range(NSA_KV_HEADS):
            new = r * LANES + g * NSA_HEAD_DIM
            old = (g * NSA_GROUP + r) * NSA_HEAD_DIM
            perm[new:new + NSA_HEAD_DIM] = np.arange(old, old + NSA_HEAD_DIM)
    return perm


def _layer0_mixer(h, mod, ab_w_in, rel_bias, gn_g, gn_b, pos_k, pos_v, w1k, w2k, w1v, w2v):
    B, S, _ = h.shape
    o = np.cumsum((0, RET_W, RET_W, RET_W, RET_W, NSA_QW) + (NSA_KVW,) * 6 + (3 * NSA_HEADS,))
    seg = lambda a: ab_w_in[:, o[a]:o[a + 1]]
    gate_w = jnp.pad(seg(11), ((0, 0), (0, LANES - 3 * NSA_HEADS)))
    w = jnp.concatenate([seg(0), seg(1), seg(2), seg(4)[:, _nsa_head_perm()], seg(7), seg(8), seg(9), seg(10),
                         seg(3), seg(5), seg(6), gate_w], axis=1).astype(BF16)
    pb, pf = _pre_call(h, mod, w, ((PRE0_BF16_COLS, BF16), (PRE0_F32_COLS, F32)), 512, "pre0")

    y_ret = _ret_call(pb, pf, gn_g, gn_b)

    win_bias, near, cmp_bias, ov_t, ex_wide, ex_tiles, eg = _nsa_tables(rel_bias, S)
    half = S // CMP_STRIDE

    def half_blocks(t):
        t = t.reshape(B, half, CMP_STRIDE, NSA_KV_HEADS, NSA_HEAD_DIM).transpose(0, 3, 1, 2, 4)
        return t.reshape(B, NSA_KV_HEADS, half, CMP_STRIDE * NSA_HEAD_DIM)

    kc, vc = _compress_call(half_blocks(pf[:, :, RET_W:RET_W + NSA_KVW]),
                            half_blocks(pf[:, :, RET_W + NSA_KVW:RET_W + 2 * NSA_KVW]),
                            pos_k, pos_v, w1k, w1v, w2k, w2v)
    o_cmp, sel_t = _cmp_attn_call(pb, kc, vc, cmp_bias, ov_t)
    o_slc = _slc_call(pb, sel_t, ex_wide, ex_tiles, near)
    y_nsa = _win_call(pb, pf, win_bias, eg, o_cmp, o_slc)
    return y_ret, y_nsa


def _layer1_mixer(h, mod, dil_w_in, rel_bias):
    B, S, _ = h.shape
    outs, lses = [], []
    for gi, (window, dilation) in enumerate(DIL_PATTERNS):
        w = dil_w_in[:, gi * 3 * DIL_WIDTH:(gi + 1) * 3 * DIL_WIDTH]
        w = jnp.concatenate([w[:, :DIL_WIDTH] * (DIL_HEAD_DIM ** -0.5), w[:, DIL_WIDTH:]], axis=1).astype(BF16)
        proj, = _pre_call(h, mod, w, ((3 * DIL_WIDTH, BF16),), 512, f"pre1_{gi}", dilation)
        bias = _dil_bias(rel_bias, dilation, window // dilation)
        o, lse = _dil_call(proj, bias, dilation, f"dilated_{gi}")
        outs.append(o)
        lses.append(lse)
    return _dil_mix_call(outs, lses, 256)


def kernel(x, c, rel_bias, ada_w, ada_b, ln_g, ln_b, ab_w_in, ab_w_out, ret_gn_g, ret_gn_b, cmp_pos_k, cmp_pos_v, cmp_k_w1, cmp_k_w2, cmp_v_w1, cmp_v_w2, dil_w_in, dil_w_out, ffn_w_gate, ffn_w_up, ffn_w_down):
    B = x.shape[0]
    mod = _ada_call(c, ada_w, ada_b).reshape(DEPTH, B, 6, D_MODEL)
    h = x
    for layer in range(DEPTH):
        i = layer // 2
        if layer % 2 == 0:
            mix_a, mix_b = _layer0_mixer(h, mod[layer], ab_w_in[i], rel_bias, ret_gn_g[i], ret_gn_b[i],
                                         cmp_pos_k[i], cmp_pos_v[i], cmp_k_w1[i], cmp_k_w2[i],
                                         cmp_v_w1[i], cmp_v_w2[i])
            col_a, col_b = 0, 0
            wo_a = ab_w_out[i, :RET_W]
            wo_b = ab_w_out[i, RET_W:][_nsa_head_perm()]
        else:
            mix_a = mix_b = _layer1_mixer(h, mod[layer], dil_w_in[i], rel_bias)
            col_a, col_b = 0, 1
            wo_a = dil_w_out[i, :D_MODEL // 2]
            wo_b = dil_w_out[i, D_MODEL // 2:]
        h = _post_call(h, mix_a, mix_b, col_a, col_b, mod[layer], wo_a.astype(BF16), wo_b.astype(BF16),
                       ln_g[layer], ln_b[layer], ffn_w_gate[layer].astype(BF16), ffn_w_up[layer].astype(BF16),
                       ffn_w_down[layer].astype(BF16), 512, f"post{layer}")
    return h
```

```python
import functools
import math

import numpy as np
import jax
import jax.numpy as jnp
from jax import lax
from jax.experimental import pallas as pl
from jax.experimental.pallas import tpu as pltpu

F32 = jnp.float32
BF16 = jnp.bfloat16

D_MODEL = 1024
DEPTH = 2
DEEPNORM_ALPHA = (2 * DEPTH) ** 0.25
LN_EPS = 1e-5
NEG_INF = -1e30

RET_HEADS = 4
RET_HEAD_DIM = 128
RET_CHUNK = 128
ROPE_BASE = 10000.0
RET_W = RET_HEADS * RET_HEAD_DIM

NSA_HEADS = 8
NSA_KV_HEADS = 2
NSA_GROUP = 4
NSA_HEAD_DIM = 64
CMP_BLOCK = 32
CMP_STRIDE = 16
CMP_HIDDEN = 256
SEL_BLOCK = 64
SEL_TOP_N = 16
SEL_FORCE_SCORE = 1e4
WIN_SIZE = 512
NSA_QW = NSA_HEADS * NSA_HEAD_DIM
NSA_KVW = NSA_KV_HEADS * NSA_HEAD_DIM

DIL_PATTERNS = ((128, 1), (512, 4), (2048, 16))
DIL_HEADS = 8
DIL_HEAD_DIM = 128
DIL_WIDTH = DIL_HEADS * DIL_HEAD_DIM

REL_BUCKETS = 32
REL_MAX_DIST = 128
D_FF = 2816

LANES = 128
TILE = 128
VMEM_LIMIT_SMALL = 32 * 1024 * 1024
VMEM_LIMIT_LARGE = 56 * 1024 * 1024

PRE0_BF16_COLS = 3 * RET_W + NSA_QW + 4 * NSA_KVW
PRE0_F32_COLS = RET_W + 2 * NSA_KVW + LANES
FFN_CHUNKS = ((0, 768), (768, 768), (1536, 768), (2304, 512))


def _dot(a, b):
    return jnp.dot(a, b, preferred_element_type=F32)


def _dot_nt(a, b):
    return lax.dot_general(a, b, (((1,), (1,)), ((), ())), preferred_element_type=F32)


def _dot_tn(a, b):
    return lax.dot_general(a, b, (((0,), (0,)), ((), ())), preferred_element_type=F32)


def _split_bf16(x):
    hi = x.astype(BF16)
    lo = (x - hi.astype(F32)).astype(BF16)
    return hi, lo


def _silu(x):
    return x * jax.nn.sigmoid(x)


def _layer_norm(x, g, b):
    mu = jnp.mean(x, axis=-1, keepdims=True)
    xc = x - mu
    var = jnp.mean(xc * xc, axis=-1, keepdims=True)
    return xc * lax.rsqrt(var + LN_EPS) * g + b


def _resident(shape):
    return pl.BlockSpec(shape, lambda *_: (0,) * len(shape), pipeline_mode=pl.Buffered(1))


def _params(sem, vmem):
    return pltpu.CompilerParams(dimension_semantics=sem, vmem_limit_bytes=vmem)


def _ada_body(c_ref, w_ref, b_ref, o_ref):
    a_hi, a_lo = _split_bf16(_silu(c_ref[...]))
    w_hi, w_lo = _split_bf16(w_ref[0])
    o_ref[0] = _dot(a_hi, w_hi) + _dot(a_lo, w_hi) + _dot(a_hi, w_lo) + b_ref[0]


def _ada_call(c, ada_w, ada_b):
    B = c.shape[0]
    n_out = ada_w.shape[-1]
    tn = n_out // 4
    return pl.pallas_call(
        _ada_body,
        grid=(DEPTH, n_out // tn),
        in_specs=[pl.BlockSpec((B, D_MODEL), lambda l, n: (0, 0)),
                  pl.BlockSpec((1, D_MODEL, tn), lambda l, n: (l, 0, n)),
                  pl.BlockSpec((1, 1, tn), lambda l, n: (l, 0, n))],
        out_specs=pl.BlockSpec((1, B, tn), lambda l, n: (l, 0, n)),
        out_shape=jax.ShapeDtypeStruct((DEPTH, B, n_out), F32),
        compiler_params=_params(("arbitrary", "arbitrary"), VMEM_LIMIT_LARGE),
        name="ada_mod",
    )(c, ada_w, ada_b.reshape(DEPTH, 1, n_out))


def _pre_body(h_ref, mod_ref, w_ref, *refs, dilation):
    if dilation == 1:
        o_refs, h = refs, h_ref[0]
    else:
        o_refs, h_scr = refs[:-1], refs[-1]
        tm = h_ref.shape[1]
        for c in range(h_scr.shape[0]):
            h_scr[c] = h_ref[0, :, c * LANES:(c + 1) * LANES]
        h = jnp.concatenate(
            [jnp.concatenate([h_scr[c, pl.ds(r, tm // dilation, stride=dilation), :]
                              for c in range(h_scr.shape[0])], axis=1) for r in range(dilation)], axis=0)
    u = (h * (1.0 + mod_ref[0, 1:2, :]) + mod_ref[0, 0:1, :]).astype(BF16)
    off = 0
    for o_ref in o_refs:
        n = o_ref.shape[-1]
        o_ref[0] = _dot(u, w_ref[:, off:off + n]).astype(o_ref.dtype).reshape(o_ref.shape[1:])
        off += n


def _pre_call(h, mod, w, out_cols_dtypes, tm, name, dilation=1):
    B, S, _ = h.shape
    n_total = w.shape[1]
    assert sum(n for n, _ in out_cols_dtypes) == n_total
    if dilation == 1:
        out_specs = [pl.BlockSpec((1, tm, n), lambda b, t: (b, t, 0)) for n, _ in out_cols_dtypes]
        out_shape = [jax.ShapeDtypeStruct((B, S, n), dt) for n, dt in out_cols_dtypes]
        scratch = []
    else:
        out_specs = [pl.BlockSpec((1, dilation, tm // dilation, n), lambda b, t: (b, 0, t, 0))
                     for n, _ in out_cols_dtypes]
        out_shape = [jax.ShapeDtypeStruct((B, dilation, S // dilation, n), dt) for n, dt in out_cols_dtypes]
        scratch = [pltpu.VMEM((D_MODEL // LANES, tm, LANES), F32)]
    outs = pl.pallas_call(
        functools.partial(_pre_body, dilation=dilation),
        grid=(B, S // tm),
        in_specs=[pl.BlockSpec((1, tm, D_MODEL), lambda b, t: (b, t, 0)),
                  pl.BlockSpec((1, 6, D_MODEL), lambda b, t: (b, 0, 0)),
                  _resident((D_MODEL, n_total))],
        out_specs=out_specs,
        out_shape=out_shape,
        scratch_shapes=scratch,
        compiler_params=_params(("arbitrary", "arbitrary"), VMEM_LIMIT_LARGE),
        name=name,
    )(h, mod, w)
    return [o.reshape(B, S, o.shape[-1]) for o in outs]


def _ret_body(q_ref, k_ref, v_ref, g_ref, cos_ref, sin_ref, dec_ref, qd_ref, kd_ref, cd_ref,
              gng_ref, gnb_ref, o_ref):
    n_chunks = q_ref.shape[1] // RET_CHUNK
    state = jnp.zeros((RET_HEAD_DIM, RET_HEAD_DIM), F32)
    for n in range(n_chunks):
        sl = pl.ds(n * RET_CHUNK, RET_CHUNK)
        q = q_ref[0, sl, :].astype(F32)
        k = k_ref[0, sl, :].astype(F32)
        v = v_ref[0, sl, :]
        c2 = cos_ref[sl, :]
        s2 = sin_ref[sl, :]
        qr = (q * c2 + pltpu.roll(q, RET_HEAD_DIM // 2, 1) * s2) * (RET_HEAD_DIM ** -0.5)
        kr = k * c2 + pltpu.roll(k, RET_HEAD_DIM // 2, 1) * s2
        scores = _dot_nt(qr.astype(BF16), kr.astype(BF16)) * dec_ref[0]
        inner = _dot(scores.astype(BF16), v)
        cross = _dot((qr * qd_ref[0]).astype(BF16), state.astype(BF16))
        kv = _dot_tn((kr * kd_ref[0]).astype(BF16), v)
        state = state * cd_ref[0] + kv
        y = inner + cross
        mu = jnp.mean(y, axis=-1, keepdims=True)
        yc = y - mu
        var = jnp.mean(yc * yc, axis=-1, keepdims=True)
        yn = yc * lax.rsqrt(var + LN_EPS)
        gate = g_ref[0, sl, :]
        o_ref[0, sl, :] = ((yn * gng_ref[...] + gnb_ref[...]) * _silu(gate)).astype(o_ref.dtype)


def _ret_tables(S):
    d = RET_HEAD_DIM
    inv = ROPE_BASE ** (-jnp.arange(0, d, 2, dtype=F32) / d)
    ang = jnp.arange(S).astype(F32)[:, None] * inv[None, :]
    cos, sin = jnp.cos(ang), jnp.sin(ang)
    cos2 = jnp.concatenate([cos, cos], axis=-1)
    sin2 = jnp.concatenate([-sin, sin], axis=-1)
    C = RET_CHUNK
    log_gamma = jnp.log1p(-jnp.exp2(-5.0 - jnp.arange(RET_HEADS, dtype=F32)))
    idx = jnp.arange(C, dtype=F32)
    diff = idx[:, None] - idx[None, :]
    dec = jnp.where(diff >= 0, jnp.exp(log_gamma[:, None, None] * jnp.maximum(diff, 0.0)), 0.0)
    kd = jnp.exp(log_gamma[:, None] * (C - 1 - idx)[None, :])
    qd = jnp.exp(log_gamma[:, None] * (idx + 1.0)[None, :])
    cd = jnp.exp(log_gamma * C)
    bc = lambda t: jnp.broadcast_to(t[:, :, None], (RET_HEADS, C, d))
    cdb = jnp.broadcast_to(cd[:, None, None], (RET_HEADS, d, d))
    return cos2, sin2, dec, bc(qd), bc(kd), cdb


def _ret_call(pb, pf, gn_g, gn_b):
    B, S, _ = pb.shape
    cos2, sin2, dec, qd, kd, cd = _ret_tables(S)
    col = lambda off: pl.BlockSpec((1, S, RET_HEAD_DIM), lambda b, h: (b, 0, off + h))
    tab = pl.BlockSpec((1, RET_CHUNK, RET_HEAD_DIM), lambda b, h: (h, 0, 0))
    vec = pl.BlockSpec((1, RET_HEAD_DIM), lambda b, h: (0, h))
    return pl.pallas_call(
        _ret_body,
        grid=(B, RET_HEADS),
        in_specs=[col(0), col(RET_HEADS), col(2 * RET_HEADS), col(0),
                  _resident((S, RET_HEAD_DIM)), _resident((S, RET_HEAD_DIM)),
                  tab, tab, tab, tab, vec, vec],
        out_specs=pl.BlockSpec((1, S, RET_HEAD_DIM), lambda b, h: (b, 0, h)),
        out_shape=jax.ShapeDtypeStruct((B, S, RET_W), BF16),
        compiler_params=_params(("arbitrary", "arbitrary"), VMEM_LIMIT_SMALL),
        name="retention",
    )(pb, pb, pb, pf, cos2, sin2, dec, qd, kd, cd, gn_g.reshape(1, RET_W), gn_b.reshape(1, RET_W))


def _compress_body(xk_ref, xv_ref, pk_ref, pv_ref, w1k_ref, w1v_ref, w2k_ref, w2v_ref, kc_ref, vc_ref):
    n_blk = kc_ref.shape[1]
    for x_ref, p_ref, w1_ref, w2_ref, o_ref in ((xk_ref, pk_ref, w1k_ref, w2k_ref, kc_ref),
                                                (xv_ref, pv_ref, w1v_ref, w2v_ref, vc_ref)):
        first = jnp.zeros((n_blk, w1_ref.shape[-1]), F32)
        second = jnp.zeros((n_blk, w1_ref.shape[-1]), F32)
        for t in range(CMP_STRIDE):
            x = x_ref[0, pl.ds(t, n_blk, stride=CMP_STRIDE), :]
            first = first + _dot((x + p_ref[t:t + 1, :]).astype(BF16), w1_ref[t])
            second = second + _dot((x + p_ref[CMP_STRIDE + t:CMP_STRIDE + t + 1, :]).astype(BF16),
                                   w1_ref[CMP_STRIDE + t])
        hid = _silu(first + pltpu.roll(second, n_blk - 1, 0))
        o_ref[0] = _dot(hid.astype(BF16), w2_ref[...]).astype(o_ref.dtype)


def _compress_call(pf, pos_k, pos_v, w1k, w1v, w2k, w2v):
    B, S, _ = pf.shape
    n_blk = S // CMP_STRIDE
    assert CMP_BLOCK == 2 * CMP_STRIDE

    def both_heads(t):
        z = jnp.zeros_like(t)
        return jnp.concatenate([jnp.concatenate([t, z], -1), jnp.concatenate([z, t], -1)], -2).astype(BF16)

    pos = lambda t: jnp.concatenate([t, t], axis=-1)
    w1 = lambda t: both_heads(t.reshape(CMP_BLOCK, NSA_HEAD_DIM, CMP_HIDDEN))
    kc_block = RET_W // NSA_KVW
    xspec = lambda c: pl.BlockSpec((1, S, NSA_KVW), lambda b: (b, 0, c))
    ospec = pl.BlockSpec((1, n_blk, NSA_KVW), lambda b: (b, 0, 0))
    return pl.pallas_call(
        _compress_body,
        grid=(B,),
        in_specs=[xspec(kc_block), xspec(kc_block + 1),
                  _resident((CMP_BLOCK, NSA_KVW)), _resident((CMP_BLOCK, NSA_KVW)),
                  _resident((CMP_BLOCK, NSA_KVW, 2 * CMP_HIDDEN)), _resident((CMP_BLOCK, NSA_KVW, 2 * CMP_HIDDEN)),
                  _resident((2 * CMP_HIDDEN, NSA_KVW)), _resident((2 * CMP_HIDDEN, NSA_KVW))],
        out_specs=[ospec, ospec],
        out_shape=[jax.ShapeDtypeStruct((B, n_blk, NSA_KVW), BF16)] * 2,
        compiler_params=_params(("arbitrary",), VMEM_LIMIT_SMALL),
        name="nsa_compress",
    )(pf, pf, pos(pos_k), pos(pos_v), w1(w1k), w1(w1v), both_heads(w2k), both_heads(w2v))


CMP_Q_ROWS = 512


def _cmp_attn_body(q_ref, kc_ref, vc_ref, cb_ref, ov_ref, o_ref, sel_ref):
    i = pl.program_id(1)
    tq = q_ref.shape[1]
    lane = lax.broadcasted_iota(jnp.int32, (tq, LANES), 1)
    row = lax.broadcasted_iota(jnp.int32, (tq, LANES), 0)
    lo_half = lane < NSA_HEAD_DIM
    kc = kc_ref[0]
    vc = vc_ref[0]
    zero = jnp.zeros_like(kc)
    kc_lo = lax.broadcasted_iota(jnp.int32, kc.shape, 1) < NSA_HEAD_DIM
    kc_g = (jnp.where(kc_lo, kc, zero), jnp.where(kc_lo, zero, kc))
    valid = (lane * CMP_STRIDE + (CMP_BLOCK - 1) <= i * tq + row)[None]
    q4 = jnp.concatenate([q_ref[0, :, r * LANES:(r + 1) * LANES] for r in range(NSA_GROUP)], axis=0)
    q4 = q4 * (NSA_HEAD_DIM ** -0.5)
    psum, outs = [], []
    for g in range(NSA_KV_HEADS):
        s = _dot_nt(q4, kc_g[g]).reshape(NSA_GROUP, tq, LANES) + cb_ref[g * NSA_GROUP:(g + 1) * NSA_GROUP]
        s = jnp.where(valid, s, NEG_INF)
        m = jnp.max(s, axis=-1, keepdims=True)
        e = jnp.where(valid, jnp.exp(s - m), 0.0)
        den = jnp.maximum(jnp.sum(e, axis=-1, keepdims=True), 1e-30)
        p = e / den
        psum.append(jnp.sum(p, axis=0))
        outs.append(_dot(p.reshape(NSA_GROUP * tq, LANES).astype(BF16), vc))
    for r in range(NSA_GROUP):
        rs = slice(r * tq, (r + 1) * tq)
        o_ref[0, :, r * LANES:(r + 1) * LANES] = jnp.where(lo_half, outs[0][rs], outs[1][rs])

    n_sel = sel_ref.shape[2]
    blk = lax.broadcasted_iota(jnp.int32, (n_sel, tq), 0)
    qblk = (i * tq + lax.broadcasted_iota(jnp.int32, (n_sel, tq), 1)) // SEL_BLOCK
    forced = jnp.where(blk == 0, 1.0, jnp.where(blk == qblk, 1.0, jnp.where(blk == qblk - 1, 1.0, 0.0)))
    for g in range(NSA_KV_HEADS):
        p_hi, p_lo = _split_bf16(psum[g])
        imp = _dot_nt(ov_ref[...], p_hi) + _dot_nt(ov_ref[...], p_lo)
        score = jnp.where(forced > 0.5, SEL_FORCE_SCORE, jnp.where(blk <= qblk, imp, -1.0))
        rank = jnp.zeros((n_sel, tq), F32)
        for other in range(n_sel):
            so = score[other:other + 1, :]
            tie = jnp.where(blk > other, 1.0, 0.0)
            rank = rank + jnp.where(so > score, 1.0, jnp.where(so == score, tie, 0.0))
        sel_ref[0, g] = jnp.where(rank < float(min(SEL_TOP_N, n_sel)), 1.0, 0.0)


def _cmp_attn_call(pb, kc, vc, cmp_bias, overlap_t):
    B, S, _ = pb.shape
    n_sel = S // SEL_BLOCK
    q_block = (3 * RET_W) // NSA_QW
    tq = CMP_Q_ROWS
    return pl.pallas_call(
        _cmp_attn_body,
        grid=(B, S // tq),
        in_specs=[pl.BlockSpec((1, tq, NSA_QW), lambda b, i: (b, i, q_block)),
                  pl.BlockSpec((1,) + kc.shape[1:], lambda b, i: (b, 0, 0)),
                  pl.BlockSpec((1,) + vc.shape[1:], lambda b, i: (b, 0, 0)),
                  pl.BlockSpec((NSA_HEADS, tq, LANES), lambda b, i: (0, i, 0)),
                  _resident((n_sel, LANES))],
        out_specs=[pl.BlockSpec((1, tq, NSA_QW), lambda b, i: (b, i, 0)),
                   pl.BlockSpec((1, NSA_KV_HEADS, n_sel, tq), lambda b, i: (b, 0, 0, i))],
        out_shape=[jax.ShapeDtypeStruct((B, S, NSA_QW), F32),
                   jax.ShapeDtypeStruct((B, NSA_KV_HEADS, n_sel, S), F32)],
        compiler_params=_params(("arbitrary", "arbitrary"), VMEM_LIMIT_SMALL),
        name="nsa_cmp_attn",
    )(pb, kc, vc, cmp_bias, overlap_t)


SLC_CLASS_TILES = 4


def _slc_far_tiles(cls, n_tiles):
    return min(SLC_CLASS_TILES * cls + 2, n_tiles - 2)


def _slc_body(q_ref, ks_ref, vs_ref, sel_ref, exw_ref, ext_ref, near_ref, o_ref, ksg_scr):
    i = pl.program_id(1)
    n_tiles = ks_ref.shape[1] // TILE
    lane = lax.broadcasted_iota(jnp.int32, (TILE, LANES), 1)
    lo_half = lane < NSA_HEAD_DIM

    @pl.when(i == 0)
    def _():
        for t in range(n_tiles):
            sl = pl.ds(t * TILE, TILE)
            kt = ks_ref[0, sl, :]
            zero = jnp.zeros_like(kt)
            ksg_scr[0, sl, :] = jnp.where(lo_half, kt, zero)
            ksg_scr[1, sl, :] = jnp.where(lo_half, zero, kt)

    prev = jnp.maximum(i - 1, 0)
    no_prev = jnp.where(i == 0, NEG_INF, 0.0)
    own_sl = pl.ds(pl.multiple_of(i * TILE, TILE), TILE)
    prev_sl = pl.ds(pl.multiple_of(prev * TILE, TILE), TILE)
    rows = NSA_GROUP * TILE

    def tile_body(n_far):
        wf = n_far * TILE
        far_ok = lax.broadcasted_iota(jnp.int32, (TILE, wf), 1) < (i - 1) * TILE
        v_near = jnp.concatenate([vs_ref[0, prev_sl, :], vs_ref[0, own_sl, :]], axis=0)
        q4 = jnp.concatenate([q_ref[0, :, r * LANES:(r + 1) * LANES] for r in range(NSA_GROUP)], axis=0)
        q4 = q4 * (NSA_HEAD_DIM ** -0.5)
        outs = []
        for g in range(NSA_KV_HEADS):
            sel_t = sel_ref[0, g].astype(BF16)
            m_far = jnp.where(far_ok, (_dot_tn(sel_t, exw_ref[:, :wf]) - 1.0) * (-NEG_INF), NEG_INF)
            m_near = jnp.concatenate([(_dot_tn(sel_t, ext_ref[prev]) - 1.0) * (-NEG_INF) + no_prev,
                                      (_dot_tn(sel_t, ext_ref[i]) - 1.0) * (-NEG_INF)], axis=1)
            k_near = jnp.concatenate([ksg_scr[g, prev_sl, :], ksg_scr[g, own_sl, :]], axis=0)
            s_far = (_dot_nt(q4, ksg_scr[g, :wf, :]).reshape(NSA_GROUP, TILE, wf) + m_far[None]).reshape(rows, wf)
            s_near = ((_dot_nt(q4, k_near) + near_ref[g]).reshape(NSA_GROUP, TILE, 2 * TILE)
                      + m_near[None]).reshape(rows, 2 * TILE)
            m = jnp.maximum(jnp.max(s_far, axis=-1, keepdims=True), jnp.max(s_near, axis=-1, keepdims=True))
            e_far = jnp.exp(s_far - m)
            e_near = jnp.exp(s_near - m)
            den = jnp.sum(e_far, axis=-1, keepdims=True) + jnp.sum(e_near, axis=-1, keepdims=True)
            outs.append((_dot(e_far.astype(BF16), vs_ref[0, :wf, :]) + _dot(e_near.astype(BF16), v_near)) / den)
        for r in range(NSA_GROUP):
            rs = slice(r * TILE, (r + 1) * TILE)
            o_ref[0, :, r * LANES:(r + 1) * LANES] = jnp.where(lo_half, outs[0][rs], outs[1][rs])

    n_classes = -(-n_tiles // SLC_CLASS_TILES)
    for cls in range(n_classes):
        pl.when(i // SLC_CLASS_TILES == cls)(functools.partial(tile_body, _slc_far_tiles(cls, n_tiles)))


def _slc_call(pb, sel_t, expand_wide, expand_tiles, near):
    B, S, _ = pb.shape
    n_sel = S // SEL_BLOCK
    n_tiles = S // TILE
    q_block = (3 * RET_W) // NSA_QW
    ks_block = (3 * RET_W + NSA_QW) // NSA_KVW
    return pl.pallas_call(
        _slc_body,
        grid=(B, n_tiles),
        in_specs=[pl.BlockSpec((1, TILE, NSA_QW), lambda b, i: (b, i, q_block)),
                  pl.BlockSpec((1, S, NSA_KVW), lambda b, i: (b, 0, ks_block)),
                  pl.BlockSpec((1, S, NSA_KVW), lambda b, i: (b, 0, ks_block + 1)),
                  pl.BlockSpec((1, NSA_KV_HEADS, n_sel, TILE), lambda b, i: (b, 0, 0, i)),
                  _resident((n_sel, S)), _resident((n_tiles, n_sel, TILE)),
                  _resident((NSA_KV_HEADS, NSA_GROUP * TILE, 2 * TILE))],
        out_specs=pl.BlockSpec((1, TILE, NSA_QW), lambda b, i: (b, i, 0)),
        out_shape=jax.ShapeDtypeStruct((B, S, NSA_QW), F32),
        scratch_shapes=[pltpu.VMEM((NSA_KV_HEADS, S, NSA_KVW), BF16)],
        compiler_params=_params(("arbitrary", "arbitrary"), VMEM_LIMIT_LARGE),
        name="nsa_selected",
    )(pb, pb, pb, sel_t, expand_wide, expand_tiles, near)


WIN_PREV_TILES = (WIN_SIZE - 1 + TILE - 1) // TILE
WIN_Q_TILES = 2


def _win_body(q_ref, k_ref, v_ref, wb_ref, gate_ref, eg_ref, ocmp_ref, oslc_ref, y_ref):
    i = pl.program_id(1)
    tq = q_ref.shape[1]
    n_span = WIN_PREV_TILES + WIN_Q_TILES
    first = i * WIN_Q_TILES - WIN_PREV_TILES
    lo_half = lax.broadcasted_iota(jnp.int32, (tq, LANES), 1) < NSA_HEAD_DIM

    def attend(early):
        if early:
            k_tiles, v_tiles, negs = [], [], []
            for t in range(n_span):
                sl = pl.ds(pl.multiple_of(jnp.maximum(first + t, 0) * TILE, TILE), TILE)
                k_tiles.append(k_ref[0, sl, :])
                v_tiles.append(v_ref[0, sl, :])
                negs.append(jnp.full((tq, TILE), jnp.where(first + t < 0, NEG_INF, 0.0), F32))
            k_all = jnp.concatenate(k_tiles, axis=0)
            v_all = jnp.concatenate(v_tiles, axis=0)
            missing = jnp.concatenate(negs, axis=1)[None]
        else:
            sl = pl.ds(pl.multiple_of(first * TILE, TILE), n_span * TILE)
            k_all = k_ref[0, sl, :]
            v_all = v_ref[0, sl, :]
        zero = jnp.zeros_like(k_all)
        k_lo = lax.broadcasted_iota(jnp.int32, k_all.shape, 1) < NSA_HEAD_DIM
        k_g = (jnp.where(k_lo, k_all, zero), jnp.where(k_lo, zero, k_all))

        g_hi, g_lo = _split_bf16(jax.nn.sigmoid(gate_ref[0]))
        gates = _dot(g_hi, eg_ref[...]) + _dot(g_lo, eg_ref[...])
        q4 = jnp.concatenate([q_ref[0, :, r * LANES:(r + 1) * LANES] for r in range(NSA_GROUP)], axis=0)
        q4 = q4 * (NSA_HEAD_DIM ** -0.5)
        outs = []
        for g in range(NSA_KV_HEADS):
            s = _dot_nt(q4, k_g[g]).reshape(NSA_GROUP, tq, n_span * TILE)
            s = s + wb_ref[g * NSA_GROUP:(g + 1) * NSA_GROUP]
            if early:
                s = s + missing
            m = jnp.max(s, axis=-1, keepdims=True)
            e = jnp.exp(s - m)
            den = jnp.sum(e, axis=-1, keepdims=True).reshape(NSA_GROUP * tq, 1)
            outs.append(_dot(e.reshape(NSA_GROUP * tq, n_span * TILE).astype(BF16), v_all) / den)
        for r in range(NSA_GROUP):
            cols = slice(r * LANES, (r + 1) * LANES)
            rs = slice(r * tq, (r + 1) * tq)
            o_win = jnp.where(lo_half, outs[0][rs], outs[1][rs])
            y = (gates[:, r * LANES:(r + 1) * LANES] * ocmp_ref[0, :, cols]
                 + gates[:, NSA_QW + r * LANES:NSA_QW + (r + 1) * LANES] * oslc_ref[0, :, cols]
                 + gates[:, 2 * NSA_QW + r * LANES:2 * NSA_QW + (r + 1) * LANES] * o_win)
            y_ref[0, :, cols] = y.astype(y_ref.dtype)

    pl.when(first >= 0)(functools.partial(attend, False))
    pl.when(first < 0)(functools.partial(attend, True))


def _win_call(pb, pf, win_bias, gate_expand, o_cmp, o_slc):
    B, S, _ = pb.shape
    q_block = (3 * RET_W) // NSA_QW
    kw_block = (3 * RET_W + NSA_QW) // NSA_KVW + 2
    gate_block = (RET_W + 2 * NSA_KVW) // LANES
    tq = WIN_Q_TILES * TILE
    tile_spec = pl.BlockSpec((1, tq, NSA_QW), lambda b, i: (b, i, 0))
    return pl.pallas_call(
        _win_body,
        grid=(B, S // tq),
        in_specs=[pl.BlockSpec((1, tq, NSA_QW), lambda b, i: (b, i, q_block)),
                  pl.BlockSpec((1, S, NSA_KVW), lambda b, i: (b, 0, kw_block)),
                  pl.BlockSpec((1, S, NSA_KVW), lambda b, i: (b, 0, kw_block + 1)),
                  _resident(win_bias.shape),
                  pl.BlockSpec((1, tq, LANES), lambda b, i: (b, i, gate_block)),
                  _resident((LANES, 3 * NSA_QW)),
                  tile_spec, tile_spec],
        out_specs=tile_spec,
        out_shape=jax.ShapeDtypeStruct((B, S, NSA_QW), BF16),
        compiler_params=_params(("arbitrary", "arbitrary"), VMEM_LIMIT_LARGE),
        name="nsa_window_combine",
    )(pb, pb, pb, win_bias, pf, gate_expand, o_cmp, o_slc)


def _post_body(h_ref, ma_ref, mb_ref, mod_ref, woa_ref, wob_ref, lng_ref, lnb_ref,
               wg_ref, wu_ref, wd_ref, o_ref):
    y = _dot(ma_ref[0], woa_ref[...]) + _dot(mb_ref[0], wob_ref[...])
    h1 = _layer_norm(DEEPNORM_ALPHA * h_ref[0] + mod_ref[0, 2:3, :] * y, lng_ref[0:1, :], lnb_ref[0:1, :])
    u = (h1 * (1.0 + mod_ref[0, 4:5, :]) + mod_ref[0, 3:4, :]).astype(BF16)
    acc = jnp.zeros(h1.shape, F32)
    for c0, cn in FFN_CHUNKS:
        gate = _dot(u, wg_ref[:, c0:c0 + cn])
        up = _dot(u, wu_ref[:, c0:c0 + cn])
        acc = acc + _dot((_silu(gate) * up).astype(BF16), wd_ref[c0:c0 + cn, :])
    o_ref[0] = _layer_norm(DEEPNORM_ALPHA * h1 + mod_ref[0, 5:6, :] * acc, lng_ref[1:2, :], lnb_ref[1:2, :])


def _post_call(h, mix_a, mix_b, col_a, col_b, mod, wo_a, wo_b, ln_g, ln_b, w_gate, w_up, w_down, tm, name):
    B, S, _ = h.shape
    half = D_MODEL // 2
    tok = pl.BlockSpec((1, tm, D_MODEL), lambda b, t: (b, t, 0))
    return pl.pallas_call(
        _post_body,
        grid=(B, S // tm),
        in_specs=[tok,
                  pl.BlockSpec((1, tm, half), lambda b, t: (b, t, col_a)),
                  pl.BlockSpec((1, tm, half), lambda b, t: (b, t, col_b)),
                  pl.BlockSpec((1, 6, D_MODEL), lambda b, t: (b, 0, 0)),
                  _resident((half, D_MODEL)), _resident((half, D_MODEL)),
                  _resident((2, D_MODEL)), _resident((2, D_MODEL)),
                  _resident((D_MODEL, D_FF)), _resident((D_MODEL, D_FF)), _resident((D_FF, D_MODEL))],
        out_specs=tok,
        out_shape=jax.ShapeDtypeStruct((B, S, D_MODEL), F32),
        compiler_params=_params(("arbitrary", "arbitrary"), VMEM_LIMIT_LARGE),
        name=name,
    )(h, mix_a, mix_b, mod, wo_a, wo_b, ln_g, ln_b, w_gate, w_up, w_down)


DIL_STEP_TILES = 4


def _dil_body(q_ref, k_ref, v_ref, b_ref, o_ref, lse_ref, *, tiles_per_seg):
    t = pl.program_id(1)
    res_tiles = min(tiles_per_seg, DIL_STEP_TILES)
    with_prev = tiles_per_seg > 1
    base = t * (DIL_STEP_TILES * TILE)
    lane = lax.broadcasted_iota(jnp.int32, (TILE, LANES), 1)
    if tiles_per_seg > DIL_STEP_TILES:
        first_prev = jnp.where((t * DIL_STEP_TILES) % tiles_per_seg == 0, NEG_INF, 0.0)
    else:
        first_prev = NEG_INF

    def span(ref, j, cols):
        own = pl.ds(pl.multiple_of(base + j * TILE, TILE), TILE)
        if not with_prev:
            return ref[0, own, cols]
        if j % res_tiles != 0:
            return ref[0, pl.ds(pl.multiple_of(base + (j - 1) * TILE, TILE), 2 * TILE), cols]
        prev = pl.ds(pl.multiple_of(jnp.maximum(base + (j - 1) * TILE, 0), TILE), TILE)
        return jnp.concatenate([ref[0, prev, cols], ref[0, own, cols]], axis=0)

    def tile_bias(h, j):
        if not with_prev:
            return b_ref[h, :, TILE:]
        if j % res_tiles != 0:
            return b_ref[h]
        gone = first_prev if j == 0 else NEG_INF
        return jnp.concatenate([b_ref[h, :, :TILE] + gone, b_ref[h, :, TILE:]], axis=1)

    tiles = [(h, j) for h in range(DIL_HEADS) for j in range(DIL_STEP_TILES)]
    head_cols = lambda h: slice(h * DIL_HEAD_DIM, (h + 1) * DIL_HEAD_DIM)
    scores = [_dot_nt(q_ref[0, j * TILE:(j + 1) * TILE, head_cols(h)], span(k_ref, j, head_cols(h)))
              + tile_bias(h, j) for h, j in tiles]
    s = jnp.concatenate(scores, axis=0)
    m = jnp.max(s, axis=-1, keepdims=True)
    e = jnp.exp(s - m)
    den = jnp.sum(e, axis=-1, keepdims=True)
    p = e.astype(BF16)
    inv = 1.0 / den
    lse = m + jnp.log(den)
    lse_tiles = [jnp.zeros((TILE, LANES), F32) for _ in range(DIL_STEP_TILES)]
    for n, (h, j) in enumerate(tiles):
        rows = slice(n * TILE, (n + 1) * TILE)
        o = _dot(p[rows], span(v_ref, j, head_cols(h))) * inv[rows]
        o_ref[0, j * TILE:(j + 1) * TILE, head_cols(h)] = o.astype(o_ref.dtype)
        lse_tiles[j] = jnp.where(lane == h, lse[rows], lse_tiles[j])
    for j in range(DIL_STEP_TILES):
        lse_ref[0, j * TILE:(j + 1) * TILE, :] = lse_tiles[j]


def _dil_call(proj, bias, dilation, name):
    B, S, _ = proj.shape
    step = DIL_STEP_TILES * TILE
    seq = lambda c: pl.BlockSpec((1, S, DIL_WIDTH), lambda b, i: (b, 0, c))
    tile = lambda width: pl.BlockSpec((1, step, width), lambda b, i: (b, i, 0))
    return pl.pallas_call(
        functools.partial(_dil_body, tiles_per_seg=(S // dilation) // TILE),
        grid=(B, S // step),
        in_specs=[tile(DIL_WIDTH), seq(1), seq(2), _resident((DIL_HEADS, TILE, 2 * TILE))],
        out_specs=[tile(DIL_WIDTH), tile(LANES)],
        out_shape=[jax.ShapeDtypeStruct((B, S, DIL_WIDTH), BF16), jax.ShapeDtypeStruct((B, S, LANES), F32)],
        compiler_params=_params(("arbitrary", "arbitrary"), VMEM_LIMIT_LARGE),
        name=name,
    )(proj, proj, proj, bias)


def _dil_mix_body(o0_ref, o1_ref, o2_ref, l0_ref, l1_ref, l2_ref, y_ref, o_scr, l_scr):
    o_refs = (o0_ref, o1_ref, o2_ref)
    l_refs = (l0_ref, l1_ref, l2_ref)
    tm = y_ref.shape[1]
    for gi, (o_ref, l_ref) in enumerate(zip(o_refs, l_refs)):
        dilation = o_ref.shape[1]
        for r in range(dilation):
            rows = pl.ds(r, tm // dilation, stride=dilation)
            l_scr[gi, rows, :] = l_ref[0, r]
            for h in range(DIL_HEADS):
                o_scr[gi, h, rows, :] = o_ref[0, r, :, h * DIL_HEAD_DIM:(h + 1) * DIL_HEAD_DIM].astype(F32)
    for h in range(DIL_HEADS):
        lses = [jnp.broadcast_to(l_scr[gi, :, h:h + 1], (tm, DIL_HEAD_DIM)) for gi in range(len(o_refs))]
        m = jnp.maximum(jnp.maximum(lses[0], lses[1]), lses[2])
        ws = [jnp.exp(l - m) for l in lses]
        den = ws[0] + ws[1] + ws[2]
        y = sum((w / den) * o_scr[gi, h] for gi, w in enumerate(ws))
        y_ref[0, :, h * DIL_HEAD_DIM:(h + 1) * DIL_HEAD_DIM] = y.astype(y_ref.dtype)


def _dil_mix_call(outs, lses, tm):
    B, S, _ = outs[0].shape
    dilations = [d for _, d in DIL_PATTERNS]
    by_residue = lambda t, d: t.reshape(B, d, S // d, t.shape[-1])
    spec = lambda d, width: pl.BlockSpec((1, d, tm // d, width), lambda b, t: (b, 0, t, 0))
    return pl.pallas_call(
        _dil_mix_body,
        grid=(B, S // tm),
        in_specs=[spec(d, DIL_WIDTH) for d in dilations] + [spec(d, LANES) for d in dilations],
        out_specs=pl.BlockSpec((1, tm, DIL_WIDTH), lambda b, t: (b, t, 0)),
        out_shape=jax.ShapeDtypeStruct((B, S, DIL_WIDTH), BF16),
        scratch_shapes=[pltpu.VMEM((len(dilations), DIL_HEADS, tm, DIL_HEAD_DIM), F32),
                        pltpu.VMEM((len(dilations), tm, LANES), F32)],
        compiler_params=_params(("arbitrary", "arbitrary"), VMEM_LIMIT_SMALL),
        name="dilated_mix",
    )(*[by_residue(o, d) for o, d in zip(outs, dilations)], *[by_residue(l, d) for l, d in zip(lses, dilations)])


def _t5_bucket_np(dist):
    n = np.maximum(dist, 0)
    max_exact = REL_BUCKETS // 2
    nf = np.maximum(n, 1).astype(np.float64)
    val = np.log(nf / max_exact) / math.log(REL_MAX_DIST / max_exact) * (REL_BUCKETS - max_exact)
    frac = np.abs(val - np.round(val))
    on_edge = (frac < 1e-9) & (n > max_exact) & (n < REL_MAX_DIST)
    assert not on_edge.any()
    large = np.minimum(max_exact + np.floor(val + 1e-9).astype(np.int64), REL_BUCKETS - 1)
    return np.where(n < max_exact, n, large).astype(np.int32)


def _shift_table(rel_bias, rows, cols, step, dist_fn, valid_fn):
    u = np.concatenate([np.arange(cols), np.arange(-(rows - 1) * step, 0)])
    period = u.size
    vals = jnp.take(rel_bias, jnp.asarray(_t5_bucket_np(dist_fn(u))), axis=0).T
    vals = jnp.where(jnp.asarray(valid_fn(u))[None], vals, NEG_INF)
    t = jnp.tile(vals, (1, rows))[:, :rows * (period - step)].reshape(vals.shape[0], rows, period - step)
    return t[:, :, :cols]


def _nsa_tables(rel_bias, S):
    always = lambda u: np.ones(u.shape, bool)
    win_dist = lambda u: WIN_PREV_TILES * TILE - u
    win_bias = _shift_table(rel_bias, WIN_Q_TILES * TILE, (WIN_PREV_TILES + WIN_Q_TILES) * TILE, 1, win_dist,
                            lambda u: (win_dist(u) >= 0) & (win_dist(u) <= WIN_SIZE - 1))
    d0 = _shift_table(rel_bias, TILE, TILE, 1, lambda u: -u, lambda u: u <= 0)
    d1 = _shift_table(rel_bias, TILE, TILE, 1, lambda u: TILE - u, always)
    far_bucket = _t5_bucket_np(np.arange(TILE + 1, S + TILE))
    assert (far_bucket == far_bucket[0]).all()
    far = rel_bias[int(far_bucket[0])][:, None, None]
    near = jnp.concatenate([d1 - far, d0 - far], axis=2)
    near = near.reshape(NSA_KV_HEADS, NSA_GROUP * TILE, 2 * TILE)
    cmp_bias = _shift_table(rel_bias, LANES, S, CMP_STRIDE, lambda u: u - (CMP_BLOCK - 1), always)
    cmp_bias = cmp_bias.transpose(0, 2, 1)
    n_cmp = (S - CMP_BLOCK) // CMP_STRIDE + 1
    n_sel = S // SEL_BLOCK
    cs = (np.arange(n_cmp) * CMP_STRIDE)[:, None]
    ss = (np.arange(n_sel) * SEL_BLOCK)[None, :]
    ov = np.clip(np.minimum(cs + CMP_BLOCK, ss + SEL_BLOCK) - np.maximum(cs, ss), 0, None) / CMP_BLOCK
    ov_t = np.zeros((n_sel, LANES), np.float32)
    ov_t[:, :n_cmp] = ov.T
    key_blk = np.arange(S) // SEL_BLOCK
    ex_wide = (np.arange(n_sel)[:, None] == key_blk[None, :]).astype(np.float32)
    ex_tiles = ex_wide.reshape(n_sel, S // TILE, TILE).transpose(1, 0, 2)
    eg = np.zeros((LANES, 3 * NSA_QW), np.float32)
    for g in range(NSA_KV_HEADS):
        for r in range(NSA_GROUP):
            for j in range(3):
                base = j * NSA_QW + r * LANES + g * NSA_HEAD_DIM
                eg[g * NSA_GROUP * 3 + r * 3 + j, base:base + NSA_HEAD_DIM] = 1.0
    return (win_bias, near, cmp_bias, jnp.asarray(ov_t, BF16), jnp.asarray(ex_wide, BF16),
            jnp.asarray(ex_tiles, BF16), jnp.asarray(eg, BF16))


def _dil_bias(rel_bias, dilation, max_dist):
    dist = lambda u: TILE - u
    return _shift_table(rel_bias, TILE, 2 * TILE, 1, lambda u: dist(u) * dilation,
                        lambda u: (dist(u) >= 0) & (dist(u) <= max_dist))


def _nsa_head_perm():
    perm = np.zeros(NSA_QW, np.int64)
    for r in range(NSA_GROUP):
        for g in range(NSA_KV_HEADS):
            new = r * LANES + g * NSA_HEAD_DIM
            old = (g * NSA_GROUP + r) * NSA_HEAD_DIM
            perm[new:new + NSA_HEAD_DIM] = np.arange(old, old + NSA_HEAD_DIM)
    return perm


def _layer0_mixer(h, mod, ab_w_in, rel_bias, gn_g, gn_b, pos_k, pos_v, w1k, w2k, w1v, w2v):
    B, S, _ = h.shape
    o = np.cumsum((0, RET_W, RET_W, RET_W, RET_W, NSA_QW) + (NSA_KVW,) * 6 + (3 * NSA_HEADS,))
    seg = lambda a: ab_w_in[:, o[a]:o[a + 1]]
    gate_w = jnp.pad(seg(11), ((0, 0), (0, LANES - 3 * NSA_HEADS)))
    w = jnp.concatenate([seg(0), seg(1), seg(2), seg(4)[:, _nsa_head_perm()], seg(7), seg(8), seg(9), seg(10),
                         seg(3), seg(5), seg(6), gate_w], axis=1).astype(BF16)
    pb, pf = _pre_call(h, mod, w, ((PRE0_BF16_COLS, BF16), (PRE0_F32_COLS, F32)), 512, "pre0")

    y_ret = _ret_call(pb, pf, gn_g, gn_b)

    win_bias, near, cmp_bias, ov_t, ex_wide, ex_tiles, eg = _nsa_tables(rel_bias, S)
    kc, vc = _compress_call(pf, pos_k, pos_v, w1k, w1v, w2k, w2v)
    o_cmp, sel_t = _cmp_attn_call(pb, kc, vc, cmp_bias, ov_t)
    o_slc = _slc_call(pb, sel_t, ex_wide, ex_tiles, near)
    y_nsa = _win_call(pb, pf, win_bias, eg, o_cmp, o_slc)
    return y_ret, y_nsa


def _layer1_mixer(h, mod, dil_w_in, rel_bias):
    B, S, _ = h.shape
    outs, lses = [], []
    for gi, (window, dilation) in enumerate(DIL_PATTERNS):
        w = dil_w_in[:, gi * 3 * DIL_WIDTH:(gi + 1) * 3 * DIL_WIDTH]
        w = jnp.concatenate([w[:, :DIL_WIDTH] * (DIL_HEAD_DIM ** -0.5), w[:, DIL_WIDTH:]], axis=1).astype(BF16)
        proj, = _pre_call(h, mod, w, ((3 * DIL_WIDTH, BF16),), 512, f"pre1_{gi}", dilation)
        bias = _dil_bias(rel_bias, dilation, window // dilation)
        o, lse = _dil_call(proj, bias, dilation, f"dilated_{gi}")
        outs.append(o)
        lses.append(lse)
    return _dil_mix_call(outs, lses, 256)


def kernel(x, c, rel_bias, ada_w, ada_b, ln_g, ln_b, ab_w_in, ab_w_out, ret_gn_g, ret_gn_b, cmp_pos_k, cmp_pos_v, cmp_k_w1, cmp_k_w2, cmp_v_w1, cmp_v_w2, dil_w_in, dil_w_out, ffn_w_gate, ffn_w_up, ffn_w_down):
    B = x.shape[0]
    mod = _ada_call(c, ada_w, ada_b).reshape(DEPTH, B, 6, D_MODEL)
    h = x
    for layer in range(DEPTH):
        i = layer // 2
        if layer % 2 == 0:
            mix_a, mix_b = _layer0_mixer(h, mod[layer], ab_w_in[i], rel_bias, ret_gn_g[i], ret_gn_b[i],
                                         cmp_pos_k[i], cmp_pos_v[i], cmp_k_w1[i], cmp_k_w2[i],
                                         cmp_v_w1[i], cmp_v_w2[i])
            col_a, col_b = 0, 0
            wo_a = ab_w_out[i, :RET_W]
            wo_b = ab_w_out[i, RET_W:][_nsa_head_perm()]
        else:
            mix_a = mix_b = _layer1_mixer(h, mod[layer], dil_w_in[i], rel_bias)
            col_a, col_b = 0, 1
            wo_a = dil_w_out[i, :D_MODEL // 2]
            wo_b = dil_w_out[i, D_MODEL // 2:]
        h = _post_call(h, mix_a, mix_b, col_a, col_b, mod[layer], wo_a.astype(BF16), wo_b.astype(BF16),
                       ln_g[layer], ln_b[layer], ffn_w_gate[layer].astype(BF16), ffn_w_up[layer].astype(BF16),
                       ffn_w_down[layer].astype(BF16), 512, f"post{layer}")
    return h
```

```python
import functools
import math

import numpy as np
import jax
import jax.numpy as jnp
from jax import lax
from jax.experimental import pallas as pl
from jax.experimental.pallas import tpu as pltpu

F32 = jnp.float32
BF16 = jnp.bfloat16

D_MODEL = 1024
DEPTH = 2
DEEPNORM_ALPHA = (2 * DEPTH) ** 0.25
LN_EPS = 1e-5
NEG_INF = -1e30
LOG2_E = math.log2(math.e)

RET_HEADS = 4
RET_HEAD_DIM = 128
RET_CHUNK = 128
ROPE_BASE = 10000.0
RET_W = RET_HEADS * RET_HEAD_DIM

NSA_HEADS = 8
NSA_KV_HEADS = 2
NSA_GROUP = 4
NSA_HEAD_DIM = 64
CMP_BLOCK = 32
CMP_STRIDE = 16
CMP_HIDDEN = 256
SEL_BLOCK = 64
SEL_TOP_N = 16
SEL_FORCE_SCORE = 1e4
WIN_SIZE = 512
NSA_QW = NSA_HEADS * NSA_HEAD_DIM
NSA_KVW = NSA_KV_HEADS * NSA_HEAD_DIM

DIL_PATTERNS = ((128, 1), (512, 4), (2048, 16))
DIL_HEADS = 8
DIL_HEAD_DIM = 128
DIL_WIDTH = DIL_HEADS * DIL_HEAD_DIM

REL_BUCKETS = 32
REL_MAX_DIST = 128
D_FF = 2816

LANES = 128
TILE = 128
VMEM_LIMIT_SMALL = 32 * 1024 * 1024
VMEM_LIMIT_LARGE = 56 * 1024 * 1024

PRE0_BF16_COLS = 3 * RET_W + NSA_QW + 4 * NSA_KVW
PRE0_F32_COLS = RET_W + 2 * NSA_KVW + LANES
FFN_CHUNKS = ((0, 768), (768, 768), (1536, 768), (2304, 512))


def _dot(a, b):
    return jnp.dot(a, b, preferred_element_type=F32)


def _dot_nt(a, b):
    return lax.dot_general(a, b, (((1,), (1,)), ((), ())), preferred_element_type=F32)


def _dot_tn(a, b):
    return lax.dot_general(a, b, (((0,), (0,)), ((), ())), preferred_element_type=F32)


def _split_bf16(x):
    hi = x.astype(BF16)
    lo = (x - hi.astype(F32)).astype(BF16)
    return hi, lo


def _silu(x):
    return x * jax.nn.sigmoid(x)


def _layer_norm(x, g, b):
    mu = jnp.mean(x, axis=-1, keepdims=True)
    xc = x - mu
    var = jnp.mean(xc * xc, axis=-1, keepdims=True)
    return xc * lax.rsqrt(var + LN_EPS) * g + b


def _resident(shape):
    return pl.BlockSpec(shape, lambda *_: (0,) * len(shape), pipeline_mode=pl.Buffered(1))


def _params(sem, vmem):
    return pltpu.CompilerParams(dimension_semantics=sem, vmem_limit_bytes=vmem)


def _ada_body(c_ref, w_ref, b_ref, o_ref):
    a_hi, a_lo = _split_bf16(_silu(c_ref[...]))
    w_hi, w_lo = _split_bf16(w_ref[0])
    o_ref[0] = _dot(a_hi, w_hi) + _dot(a_lo, w_hi) + _dot(a_hi, w_lo) + b_ref[0]


def _ada_call(c, ada_w, ada_b):
    B = c.shape[0]
    n_out = ada_w.shape[-1]
    tn = n_out // 4
    return pl.pallas_call(
        _ada_body,
        grid=(DEPTH, n_out // tn),
        in_specs=[pl.BlockSpec((B, D_MODEL), lambda l, n: (0, 0)),
                  pl.BlockSpec((1, D_MODEL, tn), lambda l, n: (l, 0, n)),
                  pl.BlockSpec((1, 1, tn), lambda l, n: (l, 0, n))],
        out_specs=pl.BlockSpec((1, B, tn), lambda l, n: (l, 0, n)),
        out_shape=jax.ShapeDtypeStruct((DEPTH, B, n_out), F32),
        compiler_params=_params(("arbitrary", "arbitrary"), VMEM_LIMIT_LARGE),
        name="ada_mod",
    )(c, ada_w, ada_b.reshape(DEPTH, 1, n_out))


def _pre_body(h_ref, mod_ref, w_ref, *refs, dilation):
    if dilation == 1:
        o_refs, h = refs, h_ref[0]
    else:
        o_refs, h_scr = refs[:-1], refs[-1]
        tm = h_ref.shape[1]
        for c in range(h_scr.shape[0]):
            h_scr[c] = h_ref[0, :, c * LANES:(c + 1) * LANES]
        h = jnp.concatenate(
            [jnp.concatenate([h_scr[c, pl.ds(r, tm // dilation, stride=dilation), :]
                              for c in range(h_scr.shape[0])], axis=1) for r in range(dilation)], axis=0)
    u = (h * (1.0 + mod_ref[0, 1:2, :]) + mod_ref[0, 0:1, :]).astype(BF16)
    off = 0
    for o_ref in o_refs:
        n = o_ref.shape[-1]
        o_ref[0] = _dot(u, w_ref[:, off:off + n]).astype(o_ref.dtype).reshape(o_ref.shape[1:])
        off += n


def _pre_call(h, mod, w, out_cols_dtypes, tm, name, dilation=1):
    B, S, _ = h.shape
    n_total = w.shape[1]
    assert sum(n for n, _ in out_cols_dtypes) == n_total
    if dilation == 1:
        out_specs = [pl.BlockSpec((1, tm, n), lambda b, t: (b, t, 0)) for n, _ in out_cols_dtypes]
        out_shape = [jax.ShapeDtypeStruct((B, S, n), dt) for n, dt in out_cols_dtypes]
        scratch = []
    else:
        out_specs = [pl.BlockSpec((1, dilation, tm // dilation, n), lambda b, t: (b, 0, t, 0))
                     for n, _ in out_cols_dtypes]
        out_shape = [jax.ShapeDtypeStruct((B, dilation, S // dilation, n), dt) for n, dt in out_cols_dtypes]
        scratch = [pltpu.VMEM((D_MODEL // LANES, tm, LANES), F32)]
    outs = pl.pallas_call(
        functools.partial(_pre_body, dilation=dilation),
        grid=(B, S // tm),
        in_specs=[pl.BlockSpec((1, tm, D_MODEL), lambda b, t: (b, t, 0)),
                  pl.BlockSpec((1, 6, D_MODEL), lambda b, t: (b, 0, 0)),
                  _resident((D_MODEL, n_total))],
        out_specs=out_specs,
        out_shape=out_shape,
        scratch_shapes=scratch,
        compiler_params=_params(("arbitrary", "arbitrary"), VMEM_LIMIT_LARGE),
        name=name,
    )(h, mod, w)
    return [o.reshape(B, S, o.shape[-1]) for o in outs]


def _ret_body(q_ref, k_ref, v_ref, g_ref, cos_ref, sin_ref, dec_ref, qd_ref, kd_ref, cd_ref,
              gng_ref, gnb_ref, o_ref):
    n_chunks = q_ref.shape[1] // RET_CHUNK
    state = jnp.zeros((RET_HEAD_DIM, RET_HEAD_DIM), F32)
    for n in range(n_chunks):
        sl = pl.ds(n * RET_CHUNK, RET_CHUNK)
        q = q_ref[0, sl, :].astype(F32)
        k = k_ref[0, sl, :].astype(F32)
        v = v_ref[0, sl, :]
        c2 = cos_ref[sl, :]
        s2 = sin_ref[sl, :]
        qr = (q * c2 + pltpu.roll(q, RET_HEAD_DIM // 2, 1) * s2) * (RET_HEAD_DIM ** -0.5)
        kr = k * c2 + pltpu.roll(k, RET_HEAD_DIM // 2, 1) * s2
        scores = _dot_nt(qr.astype(BF16), kr.astype(BF16)) * dec_ref[0]
        inner = _dot(scores.astype(BF16), v)
        cross = _dot((qr * qd_ref[0]).astype(BF16), state.astype(BF16))
        kv = _dot_tn((kr * kd_ref[0]).astype(BF16), v)
        state = state * cd_ref[0] + kv
        y = inner + cross
        mu = jnp.mean(y, axis=-1, keepdims=True)
        yc = y - mu
        var = jnp.mean(yc * yc, axis=-1, keepdims=True)
        yn = yc * lax.rsqrt(var + LN_EPS)
        gate = g_ref[0, sl, :]
        o_ref[0, sl, :] = ((yn * gng_ref[...] + gnb_ref[...]) * _silu(gate)).astype(o_ref.dtype)


def _ret_tables(S):
    d = RET_HEAD_DIM
    inv = ROPE_BASE ** (-jnp.arange(0, d, 2, dtype=F32) / d)
    ang = jnp.arange(S).astype(F32)[:, None] * inv[None, :]
    cos, sin = jnp.cos(ang), jnp.sin(ang)
    cos2 = jnp.concatenate([cos, cos], axis=-1)
    sin2 = jnp.concatenate([-sin, sin], axis=-1)
    C = RET_CHUNK
    log_gamma = jnp.log1p(-jnp.exp2(-5.0 - jnp.arange(RET_HEADS, dtype=F32)))
    idx = jnp.arange(C, dtype=F32)
    diff = idx[:, None] - idx[None, :]
    dec = jnp.where(diff >= 0, jnp.exp(log_gamma[:, None, None] * jnp.maximum(diff, 0.0)), 0.0)
    kd = jnp.exp(log_gamma[:, None] * (C - 1 - idx)[None, :])
    qd = jnp.exp(log_gamma[:, None] * (idx + 1.0)[None, :])
    cd = jnp.exp(log_gamma * C)
    bc = lambda t: jnp.broadcast_to(t[:, :, None], (RET_HEADS, C, d))
    cdb = jnp.broadcast_to(cd[:, None, None], (RET_HEADS, d, d))
    return cos2, sin2, dec, bc(qd), bc(kd), cdb


def _ret_call(pb, pf, gn_g, gn_b):
    B, S, _ = pb.shape
    cos2, sin2, dec, qd, kd, cd = _ret_tables(S)
    col = lambda off: pl.BlockSpec((1, S, RET_HEAD_DIM), lambda b, h: (b, 0, off + h))
    tab = pl.BlockSpec((1, RET_CHUNK, RET_HEAD_DIM), lambda b, h: (h, 0, 0))
    vec = pl.BlockSpec((1, RET_HEAD_DIM), lambda b, h: (0, h))
    return pl.pallas_call(
        _ret_body,
        grid=(B, RET_HEADS),
        in_specs=[col(0), col(RET_HEADS), col(2 * RET_HEADS), col(0),
                  _resident((S, RET_HEAD_DIM)), _resident((S, RET_HEAD_DIM)),
                  tab, tab, tab, tab, vec, vec],
        out_specs=pl.BlockSpec((1, S, RET_HEAD_DIM), lambda b, h: (b, 0, h)),
        out_shape=jax.ShapeDtypeStruct((B, S, RET_W), BF16),
        compiler_params=_params(("arbitrary", "arbitrary"), VMEM_LIMIT_SMALL),
        name="retention",
    )(pb, pb, pb, pf, cos2, sin2, dec, qd, kd, cd, gn_g.reshape(1, RET_W), gn_b.reshape(1, RET_W))


def _compress_body(xk_ref, xv_ref, pk_ref, pv_ref, w1k_ref, w1v_ref, w2k_ref, w2v_ref, kc_ref, vc_ref):
    n_blk = kc_ref.shape[1]
    for x_ref, p_ref, w1_ref, w2_ref, o_ref in ((xk_ref, pk_ref, w1k_ref, w2k_ref, kc_ref),
                                                (xv_ref, pv_ref, w1v_ref, w2v_ref, vc_ref)):
        first = jnp.zeros((n_blk, w1_ref.shape[-1]), F32)
        second = jnp.zeros((n_blk, w1_ref.shape[-1]), F32)
        for t in range(CMP_STRIDE):
            x = x_ref[0, pl.ds(t, n_blk, stride=CMP_STRIDE), :]
            first = first + _dot((x + p_ref[t:t + 1, :]).astype(BF16), w1_ref[t])
            second = second + _dot((x + p_ref[CMP_STRIDE + t:CMP_STRIDE + t + 1, :]).astype(BF16),
                                   w1_ref[CMP_STRIDE + t])
        hid = _silu(first + pltpu.roll(second, n_blk - 1, 0))
        o_ref[0] = _dot(hid.astype(BF16), w2_ref[...]).astype(o_ref.dtype)


def _compress_call(pf, pos_k, pos_v, w1k, w1v, w2k, w2v):
    B, S, _ = pf.shape
    n_blk = S // CMP_STRIDE
    assert CMP_BLOCK == 2 * CMP_STRIDE

    def both_heads(t):
        z = jnp.zeros_like(t)
        return jnp.concatenate([jnp.concatenate([t, z], -1), jnp.concatenate([z, t], -1)], -2).astype(BF16)

    pos = lambda t: jnp.concatenate([t, t], axis=-1)
    w1 = lambda t: both_heads(t.reshape(CMP_BLOCK, NSA_HEAD_DIM, CMP_HIDDEN))
    kc_block = RET_W // NSA_KVW
    xspec = lambda c: pl.BlockSpec((1, S, NSA_KVW), lambda b: (b, 0, c))
    ospec = pl.BlockSpec((1, n_blk, NSA_KVW), lambda b: (b, 0, 0))
    return pl.pallas_call(
        _compress_body,
        grid=(B,),
        in_specs=[xspec(kc_block), xspec(kc_block + 1),
                  _resident((CMP_BLOCK, NSA_KVW)), _resident((CMP_BLOCK, NSA_KVW)),
                  _resident((CMP_BLOCK, NSA_KVW, 2 * CMP_HIDDEN)), _resident((CMP_BLOCK, NSA_KVW, 2 * CMP_HIDDEN)),
                  _resident((2 * CMP_HIDDEN, NSA_KVW)), _resident((2 * CMP_HIDDEN, NSA_KVW))],
        out_specs=[ospec, ospec],
        out_shape=[jax.ShapeDtypeStruct((B, n_blk, NSA_KVW), BF16)] * 2,
        compiler_params=_params(("arbitrary",), VMEM_LIMIT_SMALL),
        name="nsa_compress",
    )(pf, pf, pos(pos_k), pos(pos_v), w1(w1k), w1(w1v), both_heads(w2k), both_heads(w2v))


CMP_Q_ROWS = 512


def _cmp_attn_body(q_ref, kc_ref, vc_ref, cb_ref, ov_ref, o_ref, sel_ref):
    i = pl.program_id(1)
    tq = q_ref.shape[1]
    lane = lax.broadcasted_iota(jnp.int32, (tq, LANES), 1)
    row = lax.broadcasted_iota(jnp.int32, (tq, LANES), 0)
    lo_half = lane < NSA_HEAD_DIM
    kc = kc_ref[0]
    vc = vc_ref[0]
    zero = jnp.zeros_like(kc)
    kc_lo = lax.broadcasted_iota(jnp.int32, kc.shape, 1) < NSA_HEAD_DIM
    kc_g = (jnp.where(kc_lo, kc, zero), jnp.where(kc_lo, zero, kc))
    valid = (lane * CMP_STRIDE + (CMP_BLOCK - 1) <= i * tq + row)[None]
    q4 = jnp.concatenate([q_ref[0, :, r * LANES:(r + 1) * LANES] for r in range(NSA_GROUP)], axis=0)
    q4 = q4 * (NSA_HEAD_DIM ** -0.5)
    psum, outs = [], []
    for g in range(NSA_KV_HEADS):
        s = _dot_nt(q4, kc_g[g]).reshape(NSA_GROUP, tq, LANES) + cb_ref[g * NSA_GROUP:(g + 1) * NSA_GROUP]
        s = jnp.where(valid, s, NEG_INF)
        m = jnp.max(s, axis=-1, keepdims=True)
        e = jnp.where(valid, jnp.exp2(s - m), 0.0)
        den = jnp.maximum(jnp.sum(e, axis=-1, keepdims=True), 1e-30)
        p = e / den
        psum.append(jnp.sum(p, axis=0))
        outs.append(_dot(p.reshape(NSA_GROUP * tq, LANES).astype(BF16), vc))
    for r in range(NSA_GROUP):
        rs = slice(r * tq, (r + 1) * tq)
        o_ref[0, :, r * LANES:(r + 1) * LANES] = jnp.where(lo_half, outs[0][rs], outs[1][rs])

    n_sel = sel_ref.shape[2]
    blk = lax.broadcasted_iota(jnp.int32, (n_sel, tq), 0)
    qblk = (i * tq + lax.broadcasted_iota(jnp.int32, (n_sel, tq), 1)) // SEL_BLOCK
    forced = jnp.where(blk == 0, 1.0, jnp.where(blk == qblk, 1.0, jnp.where(blk == qblk - 1, 1.0, 0.0)))
    for g in range(NSA_KV_HEADS):
        p_hi, p_lo = _split_bf16(psum[g])
        imp = _dot_nt(ov_ref[...], p_hi) + _dot_nt(ov_ref[...], p_lo)
        score = jnp.where(forced > 0.5, SEL_FORCE_SCORE, jnp.where(blk <= qblk, imp, -1.0))
        rank = jnp.zeros((n_sel, tq), F32)
        for other in range(n_sel):
            so = score[other:other + 1, :]
            tie = jnp.where(blk > other, 1.0, 0.0)
            rank = rank + jnp.where(so > score, 1.0, jnp.where(so == score, tie, 0.0))
        sel_ref[0, g] = jnp.where(rank < float(min(SEL_TOP_N, n_sel)), 1.0, 0.0)


def _cmp_attn_call(pb, kc, vc, cmp_bias, overlap_t):
    B, S, _ = pb.shape
    n_sel = S // SEL_BLOCK
    q_block = (3 * RET_W) // NSA_QW
    tq = CMP_Q_ROWS
    return pl.pallas_call(
        _cmp_attn_body,
        grid=(B, S // tq),
        in_specs=[pl.BlockSpec((1, tq, NSA_QW), lambda b, i: (b, i, q_block)),
                  pl.BlockSpec((1,) + kc.shape[1:], lambda b, i: (b, 0, 0)),
                  pl.BlockSpec((1,) + vc.shape[1:], lambda b, i: (b, 0, 0)),
                  pl.BlockSpec((NSA_HEADS, tq, LANES), lambda b, i: (0, i, 0)),
                  _resident((n_sel, LANES))],
        out_specs=[pl.BlockSpec((1, tq, NSA_QW), lambda b, i: (b, i, 0)),
                   pl.BlockSpec((1, NSA_KV_HEADS, n_sel, tq), lambda b, i: (b, 0, 0, i))],
        out_shape=[jax.ShapeDtypeStruct((B, S, NSA_QW), F32),
                   jax.ShapeDtypeStruct((B, NSA_KV_HEADS, n_sel, S), F32)],
        compiler_params=_params(("arbitrary", "arbitrary"), VMEM_LIMIT_SMALL),
        name="nsa_cmp_attn",
    )(pb, kc, vc, cmp_bias, overlap_t)


SLC_CLASS_TILES = 4


def _slc_far_tiles(cls, n_tiles):
    return min(SLC_CLASS_TILES * cls + 2, n_tiles - 2)


def _slc_body(q_ref, ks_ref, vs_ref, sel_ref, tag_ref, near_ref, o_ref, kaug_scr):
    i = pl.program_id(1)
    n_tiles = ks_ref.shape[1] // TILE
    n_sel = sel_ref.shape[2]
    lane = lax.broadcasted_iota(jnp.int32, (TILE, LANES), 1)
    lo_half = lane < NSA_HEAD_DIM

    @pl.when(i == 0)
    def _():
        for t in range(n_tiles):
            sl = pl.ds(t * TILE, TILE)
            kt = ks_ref[0, sl, :]
            zero = jnp.zeros_like(kt)
            kaug_scr[0, sl, :] = jnp.concatenate([jnp.where(lo_half, kt, zero), tag_ref[sl, :]], axis=1)
            kaug_scr[1, sl, :] = jnp.concatenate([jnp.where(lo_half, zero, kt), tag_ref[sl, :]], axis=1)

    prev = jnp.maximum(i - 1, 0)
    own_sl = pl.ds(pl.multiple_of(i * TILE, TILE), TILE)
    prev_sl = pl.ds(pl.multiple_of(prev * TILE, TILE), TILE)
    no_prev = jnp.where(lax.broadcasted_iota(jnp.int32, (1, 2 * TILE), 1) < TILE,
                        jnp.where(i == 0, NEG_INF, 0.0), 0.0)
    tile_pen = jnp.where(lane - n_sel < i - 1, 0.0, NEG_INF)
    sel_pad = jnp.zeros((LANES - n_sel, TILE), F32)

    def tile_body(n_far):
        wf = n_far * TILE
        v_near = jnp.concatenate([vs_ref[0, prev_sl, :], vs_ref[0, own_sl, :]], axis=0)
        q4 = jnp.concatenate([q_ref[0, :, r * LANES:(r + 1) * LANES] for r in range(NSA_GROUP)], axis=0)
        q4 = q4 * (NSA_HEAD_DIM ** -0.5)
        outs = []
        for g in range(NSA_KV_HEADS):
            sel_q = jnp.concatenate([sel_ref[0, g], sel_pad], axis=0).T
            blk_pen = (sel_q - 1.0) * (-NEG_INF)
            pen_near = jnp.where(lane < n_sel, blk_pen, 0.0).astype(BF16)
            pen_far = jnp.where(lane < n_sel, blk_pen,
                                jnp.where(lane < n_sel + n_tiles, tile_pen, 0.0)).astype(BF16)
            q_far = jnp.concatenate([q4, jnp.concatenate([pen_far] * NSA_GROUP, axis=0)], axis=1)
            q_near = jnp.concatenate([q4, jnp.concatenate([pen_near] * NSA_GROUP, axis=0)], axis=1)
            k_near = jnp.concatenate([kaug_scr[g, prev_sl, :], kaug_scr[g, own_sl, :]], axis=0)
            s_far = _dot_nt(q_far, kaug_scr[g, :wf, :])
            s_near = _dot_nt(q_near, k_near) + (near_ref[g] + no_prev)
            m = jnp.maximum(jnp.max(s_far, axis=-1, keepdims=True), jnp.max(s_near, axis=-1, keepdims=True))
            e_far = jnp.exp2(s_far - m)
            e_near = jnp.exp2(s_near - m)
            den = jnp.sum(e_far, axis=-1, keepdims=True) + jnp.sum(e_near, axis=-1, keepdims=True)
            outs.append((_dot(e_far.astype(BF16), vs_ref[0, :wf, :]) + _dot(e_near.astype(BF16), v_near)) / den)
        for r in range(NSA_GROUP):
            rs = slice(r * TILE, (r + 1) * TILE)
            o_ref[0, :, r * LANES:(r + 1) * LANES] = jnp.where(lo_half, outs[0][rs], outs[1][rs])

    n_classes = -(-n_tiles // SLC_CLASS_TILES)
    for cls in range(n_classes):
        pl.when(i // SLC_CLASS_TILES == cls)(functools.partial(tile_body, _slc_far_tiles(cls, n_tiles)))


def _slc_call(pb, sel_t, key_tags, near):
    B, S, _ = pb.shape
    n_sel = S // SEL_BLOCK
    n_tiles = S // TILE
    assert n_sel + n_tiles <= LANES
    q_block = (3 * RET_W) // NSA_QW
    ks_block = (3 * RET_W + NSA_QW) // NSA_KVW
    return pl.pallas_call(
        _slc_body,
        grid=(B, n_tiles),
        in_specs=[pl.BlockSpec((1, TILE, NSA_QW), lambda b, i: (b, i, q_block)),
                  pl.BlockSpec((1, S, NSA_KVW), lambda b, i: (b, 0, ks_block)),
                  pl.BlockSpec((1, S, NSA_KVW), lambda b, i: (b, 0, ks_block + 1)),
                  pl.BlockSpec((1, NSA_KV_HEADS, n_sel, TILE), lambda b, i: (b, 0, 0, i)),
                  _resident(key_tags.shape), _resident(near.shape)],
        out_specs=pl.BlockSpec((1, TILE, NSA_QW), lambda b, i: (b, i, 0)),
        out_shape=jax.ShapeDtypeStruct((B, S, NSA_QW), F32),
        scratch_shapes=[pltpu.VMEM((NSA_KV_HEADS, S, NSA_KVW + LANES), BF16)],
        compiler_params=_params(("arbitrary", "arbitrary"), VMEM_LIMIT_LARGE),
        name="nsa_selected",
    )(pb, pb, pb, sel_t, key_tags, near)


WIN_PREV_TILES = (WIN_SIZE - 1 + TILE - 1) // TILE
WIN_Q_TILES = 2


def _win_body(q_ref, k_ref, v_ref, wb_ref, gate_ref, eg_ref, ocmp_ref, oslc_ref, y_ref):
    i = pl.program_id(1)
    tq = q_ref.shape[1]
    n_span = WIN_PREV_TILES + WIN_Q_TILES
    first = i * WIN_Q_TILES - WIN_PREV_TILES
    lo_half = lax.broadcasted_iota(jnp.int32, (tq, LANES), 1) < NSA_HEAD_DIM

    def attend(early):
        if early:
            k_tiles, v_tiles, negs = [], [], []
            for t in range(n_span):
                sl = pl.ds(pl.multiple_of(jnp.maximum(first + t, 0) * TILE, TILE), TILE)
                k_tiles.append(k_ref[0, sl, :])
                v_tiles.append(v_ref[0, sl, :])
                negs.append(jnp.full((tq, TILE), jnp.where(first + t < 0, NEG_INF, 0.0), F32))
            k_all = jnp.concatenate(k_tiles, axis=0)
            v_all = jnp.concatenate(v_tiles, axis=0)
            missing = jnp.concatenate(negs, axis=1)[None]
        else:
            sl = pl.ds(pl.multiple_of(first * TILE, TILE), n_span * TILE)
            k_all = k_ref[0, sl, :]
            v_all = v_ref[0, sl, :]
        zero = jnp.zeros_like(k_all)
        k_lo = lax.broadcasted_iota(jnp.int32, k_all.shape, 1) < NSA_HEAD_DIM
        k_g = (jnp.where(k_lo, k_all, zero), jnp.where(k_lo, zero, k_all))

        g_hi, g_lo = _split_bf16(jax.nn.sigmoid(gate_ref[0]))
        gates = _dot(g_hi, eg_ref[...]) + _dot(g_lo, eg_ref[...])
        q4 = jnp.concatenate([q_ref[0, :, r * LANES:(r + 1) * LANES] for r in range(NSA_GROUP)], axis=0)
        q4 = q4 * (NSA_HEAD_DIM ** -0.5)
        outs = []
        for g in range(NSA_KV_HEADS):
            s = _dot_nt(q4, k_g[g]).reshape(NSA_GROUP, tq, n_span * TILE)
            s = s + wb_ref[g * NSA_GROUP:(g + 1) * NSA_GROUP]
            if early:
                s = s + missing
            m = jnp.max(s, axis=-1, keepdims=True)
            e = jnp.exp2(s - m)
            den = jnp.sum(e, axis=-1, keepdims=True).reshape(NSA_GROUP * tq, 1)
            outs.append(_dot(e.reshape(NSA_GROUP * tq, n_span * TILE).astype(BF16), v_all) / den)
        for r in range(NSA_GROUP):
            cols = slice(r * LANES, (r + 1) * LANES)
            rs = slice(r * tq, (r + 1) * tq)
            o_win = jnp.where(lo_half, outs[0][rs], outs[1][rs])
            y = (gates[:, r * LANES:(r + 1) * LANES] * ocmp_ref[0, :, cols]
                 + gates[:, NSA_QW + r * LANES:NSA_QW + (r + 1) * LANES] * oslc_ref[0, :, cols]
                 + gates[:, 2 * NSA_QW + r * LANES:2 * NSA_QW + (r + 1) * LANES] * o_win)
            y_ref[0, :, cols] = y.astype(y_ref.dtype)

    pl.when(first >= 0)(functools.partial(attend, False))
    pl.when(first < 0)(functools.partial(attend, True))


def _win_call(pb, pf, win_bias, gate_expand, o_cmp, o_slc):
    B, S, _ = pb.shape
    q_block = (3 * RET_W) // NSA_QW
    kw_block = (3 * RET_W + NSA_QW) // NSA_KVW + 2
    gate_block = (RET_W + 2 * NSA_KVW) // LANES
    tq = WIN_Q_TILES * TILE
    tile_spec = pl.BlockSpec((1, tq, NSA_QW), lambda b, i: (b, i, 0))
    return pl.pallas_call(
        _win_body,
        grid=(B, S // tq),
        in_specs=[pl.BlockSpec((1, tq, NSA_QW), lambda b, i: (b, i, q_block)),
                  pl.BlockSpec((1, S, NSA_KVW), lambda b, i: (b, 0, kw_block)),
                  pl.BlockSpec((1, S, NSA_KVW), lambda b, i: (b, 0, kw_block + 1)),
                  _resident(win_bias.shape),
                  pl.BlockSpec((1, tq, LANES), lambda b, i: (b, i, gate_block)),
                  _resident((LANES, 3 * NSA_QW)),
                  tile_spec, tile_spec],
        out_specs=tile_spec,
        out_shape=jax.ShapeDtypeStruct((B, S, NSA_QW), BF16),
        compiler_params=_params(("arbitrary", "arbitrary"), VMEM_LIMIT_LARGE),
        name="nsa_window_combine",
    )(pb, pb, pb, win_bias, pf, gate_expand, o_cmp, o_slc)


def _post_body(h_ref, ma_ref, mb_ref, mod_ref, woa_ref, wob_ref, lng_ref, lnb_ref,
               wg_ref, wu_ref, wd_ref, o_ref):
    y = _dot(ma_ref[0], woa_ref[...]) + _dot(mb_ref[0], wob_ref[...])
    h1 = _layer_norm(DEEPNORM_ALPHA * h_ref[0] + mod_ref[0, 2:3, :] * y, lng_ref[0:1, :], lnb_ref[0:1, :])
    u = (h1 * (1.0 + mod_ref[0, 4:5, :]) + mod_ref[0, 3:4, :]).astype(BF16)
    acc = jnp.zeros(h1.shape, F32)
    for c0, cn in FFN_CHUNKS:
        gate = _dot(u, wg_ref[:, c0:c0 + cn])
        up = _dot(u, wu_ref[:, c0:c0 + cn])
        acc = acc + _dot((_silu(gate) * up).astype(BF16), wd_ref[c0:c0 + cn, :])
    o_ref[0] = _layer_norm(DEEPNORM_ALPHA * h1 + mod_ref[0, 5:6, :] * acc, lng_ref[1:2, :], lnb_ref[1:2, :])


def _post_call(h, mix_a, mix_b, col_a, col_b, mod, wo_a, wo_b, ln_g, ln_b, w_gate, w_up, w_down, tm, name):
    B, S, _ = h.shape
    half = D_MODEL // 2
    tok = pl.BlockSpec((1, tm, D_MODEL), lambda b, t: (b, t, 0))
    return pl.pallas_call(
        _post_body,
        grid=(B, S // tm),
        in_specs=[tok,
                  pl.BlockSpec((1, tm, half), lambda b, t: (b, t, col_a)),
                  pl.BlockSpec((1, tm, half), lambda b, t: (b, t, col_b)),
                  pl.BlockSpec((1, 6, D_MODEL), lambda b, t: (b, 0, 0)),
                  _resident((half, D_MODEL)), _resident((half, D_MODEL)),
                  _resident((2, D_MODEL)), _resident((2, D_MODEL)),
                  _resident((D_MODEL, D_FF)), _resident((D_MODEL, D_FF)), _resident((D_FF, D_MODEL))],
        out_specs=tok,
        out_shape=jax.ShapeDtypeStruct((B, S, D_MODEL), F32),
        compiler_params=_params(("arbitrary", "arbitrary"), VMEM_LIMIT_LARGE),
        name=name,
    )(h, mix_a, mix_b, mod, wo_a, wo_b, ln_g, ln_b, w_gate, w_up, w_down)


DIL_STEP_TILES = 4


def _dil_body(q_ref, k_ref, kp_ref, v_ref, vp_ref, b_ref, o_ref, lse_ref, *, tiles_per_seg):
    t = pl.program_id(1)
    res_tiles = min(tiles_per_seg, DIL_STEP_TILES)
    with_prev = tiles_per_seg > 1
    if tiles_per_seg > DIL_STEP_TILES:
        first_prev = jnp.where((t * DIL_STEP_TILES) % tiles_per_seg == 0, NEG_INF, 0.0)
    else:
        first_prev = NEG_INF

    def span(ref, before_ref, j, cols):
        if not with_prev:
            return ref[0, j * TILE:(j + 1) * TILE, cols]
        if j > 0:
            return ref[0, (j - 1) * TILE:(j + 1) * TILE, cols]
        return jnp.concatenate([before_ref[0, :, cols], ref[0, :TILE, cols]], axis=0)

    def tile_bias(h, j):
        if not with_prev:
            return b_ref[h, TILE:, :]
        if j % res_tiles != 0:
            return b_ref[h]
        gone = first_prev if j == 0 else NEG_INF
        return jnp.concatenate([b_ref[h, :TILE, :] + gone, b_ref[h, TILE:, :]], axis=0)

    tiles = [(j, h) for j in range(DIL_STEP_TILES) for h in range(DIL_HEADS)]
    head_cols = lambda h: slice(h * DIL_HEAD_DIM, (h + 1) * DIL_HEAD_DIM)
    scores = [_dot_nt(span(k_ref, kp_ref, j, head_cols(h)), q_ref[0, j * TILE:(j + 1) * TILE, head_cols(h)])
              + tile_bias(h, j) for j, h in tiles]
    s = jnp.concatenate(scores, axis=1)
    m = jnp.max(s, axis=0, keepdims=True)
    e = jnp.exp2(s - m)
    den = jnp.sum(e, axis=0, keepdims=True)
    p = (e * (1.0 / den)).astype(BF16)
    lse = (m + jnp.log2(den)) * math.log(2.0)
    for n, (j, h) in enumerate(tiles):
        o = _dot_tn(p[:, n * TILE:(n + 1) * TILE], span(v_ref, vp_ref, j, head_cols(h)))
        o_ref[0, j * TILE:(j + 1) * TILE, head_cols(h)] = o.astype(o_ref.dtype)
    pad = jnp.zeros((LANES - DIL_HEADS, TILE), F32)
    for j in range(DIL_STEP_TILES):
        by_head = [lse[:, (j * DIL_HEADS + h) * TILE:(j * DIL_HEADS + h + 1) * TILE] for h in range(DIL_HEADS)]
        lse_ref[0, j * TILE:(j + 1) * TILE, :] = jnp.concatenate(by_head + [pad], axis=0).T


def _dil_call(proj, bias, dilation, name):
    B, S, _ = proj.shape
    step = DIL_STEP_TILES * TILE
    own = lambda c: pl.BlockSpec((1, step, DIL_WIDTH), lambda b, i: (b, i, c))
    before = lambda c: pl.BlockSpec((1, TILE, DIL_WIDTH),
                                    lambda b, i: (b, jnp.maximum(i * DIL_STEP_TILES - 1, 0), c))
    tile = lambda width: pl.BlockSpec((1, step, width), lambda b, i: (b, i, 0))
    return pl.pallas_call(
        functools.partial(_dil_body, tiles_per_seg=(S // dilation) // TILE),
        grid=(B, S // step),
        in_specs=[own(0), own(1), before(1), own(2), before(2), _resident(bias.shape)],
        out_specs=[tile(DIL_WIDTH), tile(LANES)],
        out_shape=[jax.ShapeDtypeStruct((B, S, DIL_WIDTH), BF16), jax.ShapeDtypeStruct((B, S, LANES), F32)],
        compiler_params=_params(("arbitrary", "arbitrary"), VMEM_LIMIT_LARGE),
        name=name,
    )(proj, proj, proj, proj, proj, bias)


def _dil_mix_body(o0_ref, o1_ref, o2_ref, l0_ref, l1_ref, l2_ref, y_ref, o_scr, l_scr):
    o_refs = (o0_ref, o1_ref, o2_ref)
    l_refs = (l0_ref, l1_ref, l2_ref)
    tm = y_ref.shape[1]
    for gi, (o_ref, l_ref) in enumerate(zip(o_refs, l_refs)):
        dilation = o_ref.shape[1]
        for r in range(dilation):
            rows = pl.ds(r, tm // dilation, stride=dilation)
            l_scr[gi, rows, :] = l_ref[0, r]
            for h in range(DIL_HEADS):
                o_scr[gi, h, rows, :] = o_ref[0, r, :, h * DIL_HEAD_DIM:(h + 1) * DIL_HEAD_DIM].astype(F32)
    for h in range(DIL_HEADS):
        lses = [jnp.broadcast_to(l_scr[gi, :, h:h + 1], (tm, DIL_HEAD_DIM)) for gi in range(len(o_refs))]
        m = jnp.maximum(jnp.maximum(lses[0], lses[1]), lses[2])
        ws = [jnp.exp(l - m) for l in lses]
        den = ws[0] + ws[1] + ws[2]
        y = sum((w / den) * o_scr[gi, h] for gi, w in enumerate(ws))
        y_ref[0, :, h * DIL_HEAD_DIM:(h + 1) * DIL_HEAD_DIM] = y.astype(y_ref.dtype)


def _dil_mix_call(outs, lses, tm):
    B, S, _ = outs[0].shape
    dilations = [d for _, d in DIL_PATTERNS]
    by_residue = lambda t, d: t.reshape(B, d, S // d, t.shape[-1])
    spec = lambda d, width: pl.BlockSpec((1, d, tm // d, width), lambda b, t: (b, 0, t, 0))
    return pl.pallas_call(
        _dil_mix_body,
        grid=(B, S // tm),
        in_specs=[spec(d, DIL_WIDTH) for d in dilations] + [spec(d, LANES) for d in dilations],
        out_specs=pl.BlockSpec((1, tm, DIL_WIDTH), lambda b, t: (b, t, 0)),
        out_shape=jax.ShapeDtypeStruct((B, S, DIL_WIDTH), BF16),
        scratch_shapes=[pltpu.VMEM((len(dilations), DIL_HEADS, tm, DIL_HEAD_DIM), F32),
                        pltpu.VMEM((len(dilations), tm, LANES), F32)],
        compiler_params=_params(("arbitrary", "arbitrary"), VMEM_LIMIT_SMALL),
        name="dilated_mix",
    )(*[by_residue(o, d) for o, d in zip(outs, dilations)], *[by_residue(l, d) for l, d in zip(lses, dilations)])


def _t5_bucket_np(dist):
    n = np.maximum(dist, 0)
    max_exact = REL_BUCKETS // 2
    nf = np.maximum(n, 1).astype(np.float64)
    val = np.log(nf / max_exact) / math.log(REL_MAX_DIST / max_exact) * (REL_BUCKETS - max_exact)
    frac = np.abs(val - np.round(val))
    on_edge = (frac < 1e-9) & (n > max_exact) & (n < REL_MAX_DIST)
    assert not on_edge.any()
    large = np.minimum(max_exact + np.floor(val + 1e-9).astype(np.int64), REL_BUCKETS - 1)
    return np.where(n < max_exact, n, large).astype(np.int32)


def _shift_table(rel_bias, rows, cols, step, dist_fn, valid_fn):
    u = np.concatenate([np.arange(cols), np.arange(-(rows - 1) * step, 0)])
    period = u.size
    vals = jnp.take(rel_bias, jnp.asarray(_t5_bucket_np(dist_fn(u))), axis=0).T
    vals = jnp.where(jnp.asarray(valid_fn(u))[None], vals, NEG_INF)
    t = jnp.tile(vals, (1, rows))[:, :rows * (period - step)].reshape(vals.shape[0], rows, period - step)
    return t[:, :, :cols]


def _nsa_tables(rel_bias, S):
    rel_bias = rel_bias * LOG2_E
    always = lambda u: np.ones(u.shape, bool)
    win_dist = lambda u: WIN_PREV_TILES * TILE - u
    win_bias = _shift_table(rel_bias, WIN_Q_TILES * TILE, (WIN_PREV_TILES + WIN_Q_TILES) * TILE, 1, win_dist,
                            lambda u: (win_dist(u) >= 0) & (win_dist(u) <= WIN_SIZE - 1))
    d0 = _shift_table(rel_bias, TILE, TILE, 1, lambda u: -u, lambda u: u <= 0)
    d1 = _shift_table(rel_bias, TILE, TILE, 1, lambda u: TILE - u, always)
    far_bucket = _t5_bucket_np(np.arange(TILE + 1, S + TILE))
    assert (far_bucket == far_bucket[0]).all()
    far = rel_bias[int(far_bucket[0])][:, None, None]
    near = jnp.concatenate([d1 - far, d0 - far], axis=2)
    near = near.reshape(NSA_KV_HEADS, NSA_GROUP * TILE, 2 * TILE)
    cmp_bias = _shift_table(rel_bias, LANES, S, CMP_STRIDE, lambda u: u - (CMP_BLOCK - 1), always)
    cmp_bias = cmp_bias.transpose(0, 2, 1)
    n_cmp = (S - CMP_BLOCK) // CMP_STRIDE + 1
    n_sel = S // SEL_BLOCK
    cs = (np.arange(n_cmp) * CMP_STRIDE)[:, None]
    ss = (np.arange(n_sel) * SEL_BLOCK)[None, :]
    ov = np.clip(np.minimum(cs + CMP_BLOCK, ss + SEL_BLOCK) - np.maximum(cs, ss), 0, None) / CMP_BLOCK
    ov_t = np.zeros((n_sel, LANES), np.float32)
    ov_t[:, :n_cmp] = ov.T
    key = np.arange(S)[:, None]
    lane = np.arange(LANES)[None, :]
    tags = ((lane == key // SEL_BLOCK) | (lane == n_sel + key // TILE)).astype(np.float32)
    eg = np.zeros((LANES, 3 * NSA_QW), np.float32)
    for g in range(NSA_KV_HEADS):
        for r in range(NSA_GROUP):
            for j in range(3):
                base = j * NSA_QW + r * LANES + g * NSA_HEAD_DIM
                eg[g * NSA_GROUP * 3 + r * 3 + j, base:base + NSA_HEAD_DIM] = 1.0
    return win_bias, near, cmp_bias, jnp.asarray(ov_t, BF16), jnp.asarray(tags, BF16), jnp.asarray(eg, BF16)


def _dil_bias(rel_bias, dilation, max_dist):
    dist = lambda u: TILE + u
    return _shift_table(rel_bias * LOG2_E, 2 * TILE, TILE, 1, lambda u: dist(u) * dilation,
                        lambda u: (dist(u) >= 0) & (dist(u) <= max_dist))


def _nsa_head_perm():
    perm = np.zeros(NSA_QW, np.int64)
    for r in range(NSA_GROUP):
        for g in range(NSA_KV_HEADS):
            new = r * LANES + g * NSA_HEAD_DIM
            old = (g * NSA_GROUP + r) * NSA_HEAD_DIM
            perm[new:new + NSA_HEAD_DIM] = np.arange(old, old + NSA_HEAD_DIM)
    return perm


def _layer0_mixer(h, mod, ab_w_in, rel_bias, gn_g, gn_b, pos_k, pos_v, w1k, w2k, w1v, w2v):
    B, S, _ = h.shape
    o = np.cumsum((0, RET_W, RET_W, RET_W, RET_W, NSA_QW) + (NSA_KVW,) * 6 + (3 * NSA_HEADS,))
    seg = lambda a: ab_w_in[:, o[a]:o[a + 1]]
    gate_w = jnp.pad(seg(11), ((0, 0), (0, LANES - 3 * NSA_HEADS)))
    q_nsa = seg(4)[:, _nsa_head_perm()] * LOG2_E
    w = jnp.concatenate([seg(0), seg(1), seg(2), q_nsa, seg(7), seg(8), seg(9), seg(10),
                         seg(3), seg(5), seg(6), gate_w], axis=1).astype(BF16)
    pb, pf = _pre_call(h, mod, w, ((PRE0_BF16_COLS, BF16), (PRE0_F32_COLS, F32)), 512, "pre0")

    y_ret = _ret_call(pb, pf, gn_g, gn_b)

    win_bias, near, cmp_bias, ov_t, key_tags, eg = _nsa_tables(rel_bias, S)
    kc, vc = _compress_call(pf, pos_k, pos_v, w1k, w1v, w2k, w2v)
    o_cmp, sel_t = _cmp_attn_call(pb, kc, vc, cmp_bias, ov_t)
    o_slc = _slc_call(pb, sel_t, key_tags, near)
    y_nsa = _win_call(pb, pf, win_bias, eg, o_cmp, o_slc)
    return y_ret, y_nsa


def _layer1_mixer(h, mod, dil_w_in, rel_bias):
    B, S, _ = h.shape
    outs, lses = [], []
    for gi, (window, dilation) in enumerate(DIL_PATTERNS):
        w = dil_w_in[:, gi * 3 * DIL_WIDTH:(gi + 1) * 3 * DIL_WIDTH]
        w = jnp.concatenate([w[:, :DIL_WIDTH] * (DIL_HEAD_DIM ** -0.5 * LOG2_E), w[:, DIL_WIDTH:]], axis=1).astype(BF16)
        proj, = _pre_call(h, mod, w, ((3 * DIL_WIDTH, BF16),), 512, f"pre1_{gi}", dilation)
        bias = _dil_bias(rel_bias, dilation, window // dilation)
        o, lse = _dil_call(proj, bias, dilation, f"dilated_{gi}")
        outs.append(o)
        lses.append(lse)
    return _dil_mix_call(outs, lses, 256)


def kernel(x, c, rel_bias, ada_w, ada_b, ln_g, ln_b, ab_w_in, ab_w_out, ret_gn_g, ret_gn_b, cmp_pos_k, cmp_pos_v, cmp_k_w1, cmp_k_w2, cmp_v_w1, cmp_v_w2, dil_w_in, dil_w_out, ffn_w_gate, ffn_w_up, ffn_w_down):
    B = x.shape[0]
    mod = _ada_call(c, ada_w, ada_b).reshape(DEPTH, B, 6, D_MODEL)
    h = x
    for layer in range(DEPTH):
        i = layer // 2
        if layer % 2 == 0:
            mix_a, mix_b = _layer0_mixer(h, mod[layer], ab_w_in[i], rel_bias, ret_gn_g[i], ret_gn_b[i],
                                         cmp_pos_k[i], cmp_pos_v[i], cmp_k_w1[i], cmp_k_w2[i],
                                         cmp_v_w1[i], cmp_v_w2[i])
            col_a, col_b = 0, 0
            wo_a = ab_w_out[i, :RET_W]
            wo_b = ab_w_out[i, RET_W:][_nsa_head_perm()]
        else:
            mix_a = mix_b = _layer1_mixer(h, mod[layer], dil_w_in[i], rel_bias)
            col_a, col_b = 0, 1
            wo_a = dil_w_out[i, :D_MODEL // 2]
            wo_b = dil_w_out[i, D_MODEL // 2:]
        h = _post_call(h, mix_a, mix_b, col_a, col_b, mod[layer], wo_a.astype(BF16), wo_b.astype(BF16),
                       ln_g[layer], ln_b[layer], ffn_w_gate[layer].astype(BF16), ffn_w_up[layer].astype(BF16),
                       ffn_w_down[layer].astype(BF16), 512, f"post{layer}")
    return h
```

```python
import functools
import math

import numpy as np
import jax
import jax.numpy as jnp
from jax import lax
from jax.experimental import pallas as pl
from jax.experimental.pallas import tpu as pltpu

F32 = jnp.float32
BF16 = jnp.bfloat16

D_MODEL = 1024
DEPTH = 2
DEEPNORM_ALPHA = (2 * DEPTH) ** 0.25
LN_EPS = 1e-5
NEG_INF = -1e30
LOG2_E = math.log2(math.e)

RET_HEADS = 4
RET_HEAD_DIM = 128
RET_CHUNK = 128
ROPE_BASE = 10000.0
RET_W = RET_HEADS * RET_HEAD_DIM

NSA_HEADS = 8
NSA_KV_HEADS = 2
NSA_GROUP = 4
NSA_HEAD_DIM = 64
CMP_BLOCK = 32
CMP_STRIDE = 16
CMP_HIDDEN = 256
SEL_BLOCK = 64
SEL_TOP_N = 16
SEL_FORCE_SCORE = 1e4
WIN_SIZE = 512
NSA_QW = NSA_HEADS * NSA_HEAD_DIM
NSA_KVW = NSA_KV_HEADS * NSA_HEAD_DIM

DIL_PATTERNS = ((128, 1), (512, 4), (2048, 16))
DIL_HEADS = 8
DIL_HEAD_DIM = 128
DIL_WIDTH = DIL_HEADS * DIL_HEAD_DIM

REL_BUCKETS = 32
REL_MAX_DIST = 128
D_FF = 2816

LANES = 128
TILE = 128
VMEM_LIMIT_SMALL = 32 * 1024 * 1024
VMEM_LIMIT_LARGE = 56 * 1024 * 1024

PRE0_BF16_COLS = 3 * RET_W + NSA_QW + 4 * NSA_KVW
PRE0_F32_COLS = RET_W + 2 * NSA_KVW + LANES
FFN_CHUNKS = ((0, 768), (768, 768), (1536, 768), (2304, 512))


def _dot(a, b):
    return jnp.dot(a, b, preferred_element_type=F32)


def _dot_nt(a, b):
    return lax.dot_general(a, b, (((1,), (1,)), ((), ())), preferred_element_type=F32)


def _dot_tn(a, b):
    return lax.dot_general(a, b, (((0,), (0,)), ((), ())), preferred_element_type=F32)


def _split_bf16(x):
    hi = x.astype(BF16)
    lo = (x - hi.astype(F32)).astype(BF16)
    return hi, lo


def _silu(x):
    return x * jax.nn.sigmoid(x)


def _layer_norm(x, g, b):
    mu = jnp.mean(x, axis=-1, keepdims=True)
    xc = x - mu
    var = jnp.mean(xc * xc, axis=-1, keepdims=True)
    return xc * lax.rsqrt(var + LN_EPS) * g + b


def _resident(shape):
    return pl.BlockSpec(shape, lambda *_: (0,) * len(shape), pipeline_mode=pl.Buffered(1))


def _params(sem, vmem):
    return pltpu.CompilerParams(dimension_semantics=sem, vmem_limit_bytes=vmem)


def _ada_body(c_ref, w_ref, b_ref, o_ref):
    a_hi, a_lo = _split_bf16(_silu(c_ref[...]))
    w_hi, w_lo = _split_bf16(w_ref[0])
    o_ref[0] = _dot(a_hi, w_hi) + _dot(a_lo, w_hi) + _dot(a_hi, w_lo) + b_ref[0]


def _ada_call(c, ada_w, ada_b):
    B = c.shape[0]
    n_out = ada_w.shape[-1]
    tn = n_out // 4
    return pl.pallas_call(
        _ada_body,
        grid=(DEPTH, n_out // tn),
        in_specs=[pl.BlockSpec((B, D_MODEL), lambda l, n: (0, 0)),
                  pl.BlockSpec((1, D_MODEL, tn), lambda l, n: (l, 0, n)),
                  pl.BlockSpec((1, 1, tn), lambda l, n: (l, 0, n))],
        out_specs=pl.BlockSpec((1, B, tn), lambda l, n: (l, 0, n)),
        out_shape=jax.ShapeDtypeStruct((DEPTH, B, n_out), F32),
        compiler_params=_params(("arbitrary", "arbitrary"), VMEM_LIMIT_LARGE),
        name="ada_mod",
    )(c, ada_w, ada_b.reshape(DEPTH, 1, n_out))


def _pre_body(h_ref, mod_ref, w_ref, *refs, dilation):
    if dilation == 1:
        o_refs, h = refs, h_ref[0]
    else:
        o_refs, h_scr = refs[:-1], refs[-1]
        tm = h_ref.shape[1]
        for c in range(h_scr.shape[0]):
            h_scr[c] = h_ref[0, :, c * LANES:(c + 1) * LANES]
        h = jnp.concatenate(
            [jnp.concatenate([h_scr[c, pl.ds(r, tm // dilation, stride=dilation), :]
                              for c in range(h_scr.shape[0])], axis=1) for r in range(dilation)], axis=0)
    u = (h * (1.0 + mod_ref[0, 1:2, :]) + mod_ref[0, 0:1, :]).astype(BF16)
    off = 0
    for o_ref in o_refs:
        n = o_ref.shape[-1]
        o_ref[0] = _dot(u, w_ref[:, off:off + n]).astype(o_ref.dtype).reshape(o_ref.shape[1:])
        off += n


def _pre_call(h, mod, w, out_cols_dtypes, tm, name, dilation=1):
    B, S, _ = h.shape
    n_total = w.shape[1]
    assert sum(n for n, _ in out_cols_dtypes) == n_total
    if dilation == 1:
        out_specs = [pl.BlockSpec((1, tm, n), lambda b, t: (b, t, 0)) for n, _ in out_cols_dtypes]
        out_shape = [jax.ShapeDtypeStruct((B, S, n), dt) for n, dt in out_cols_dtypes]
        scratch = []
    else:
        out_specs = [pl.BlockSpec((1, dilation, tm // dilation, n), lambda b, t: (b, 0, t, 0))
                     for n, _ in out_cols_dtypes]
        out_shape = [jax.ShapeDtypeStruct((B, dilation, S // dilation, n), dt) for n, dt in out_cols_dtypes]
        scratch = [pltpu.VMEM((D_MODEL // LANES, tm, LANES), F32)]
    outs = pl.pallas_call(
        functools.partial(_pre_body, dilation=dilation),
        grid=(B, S // tm),
        in_specs=[pl.BlockSpec((1, tm, D_MODEL), lambda b, t: (b, t, 0)),
                  pl.BlockSpec((1, 6, D_MODEL), lambda b, t: (b, 0, 0)),
                  _resident((D_MODEL, n_total))],
        out_specs=out_specs,
        out_shape=out_shape,
        scratch_shapes=scratch,
        compiler_params=_params(("arbitrary", "arbitrary"), VMEM_LIMIT_LARGE),
        name=name,
    )(h, mod, w)
    return [o.reshape(B, S, o.shape[-1]) for o in outs]


def _ret_body(q_ref, k_ref, v_ref, g_ref, cos_ref, sin_ref, dec_ref, qd_ref, kd_ref, cd_ref,
              gng_ref, gnb_ref, o_ref):
    n_chunks = q_ref.shape[1] // RET_CHUNK
    state = jnp.zeros((RET_HEAD_DIM, RET_HEAD_DIM), F32)
    for n in range(n_chunks):
        sl = pl.ds(n * RET_CHUNK, RET_CHUNK)
        q = q_ref[0, sl, :].astype(F32)
        k = k_ref[0, sl, :].astype(F32)
        v = v_ref[0, sl, :]
        c2 = cos_ref[sl, :]
        s2 = sin_ref[sl, :]
        qr = (q * c2 + pltpu.roll(q, RET_HEAD_DIM // 2, 1) * s2) * (RET_HEAD_DIM ** -0.5)
        kr = k * c2 + pltpu.roll(k, RET_HEAD_DIM // 2, 1) * s2
        scores = _dot_nt(qr.astype(BF16), kr.astype(BF16)) * dec_ref[0]
        inner = _dot(scores.astype(BF16), v)
        cross = _dot((qr * qd_ref[0]).astype(BF16), state.astype(BF16))
        kv = _dot_tn((kr * kd_ref[0]).astype(BF16), v)
        state = state * cd_ref[0] + kv
        y = inner + cross
        mu = jnp.mean(y, axis=-1, keepdims=True)
        yc = y - mu
        var = jnp.mean(yc * yc, axis=-1, keepdims=True)
        yn = yc * lax.rsqrt(var + LN_EPS)
        gate = g_ref[0, sl, :]
        o_ref[0, sl, :] = ((yn * gng_ref[...] + gnb_ref[...]) * _silu(gate)).astype(o_ref.dtype)


def _ret_tables(S):
    d = RET_HEAD_DIM
    inv = ROPE_BASE ** (-jnp.arange(0, d, 2, dtype=F32) / d)
    ang = jnp.arange(S).astype(F32)[:, None] * inv[None, :]
    cos, sin = jnp.cos(ang), jnp.sin(ang)
    cos2 = jnp.concatenate([cos, cos], axis=-1)
    sin2 = jnp.concatenate([-sin, sin], axis=-1)
    C = RET_CHUNK
    log_gamma = jnp.log1p(-jnp.exp2(-5.0 - jnp.arange(RET_HEADS, dtype=F32)))
    idx = jnp.arange(C, dtype=F32)
    diff = idx[:, None] - idx[None, :]
    dec = jnp.where(diff >= 0, jnp.exp(log_gamma[:, None, None] * jnp.maximum(diff, 0.0)), 0.0)
    kd = jnp.exp(log_gamma[:, None] * (C - 1 - idx)[None, :])
    qd = jnp.exp(log_gamma[:, None] * (idx + 1.0)[None, :])
    cd = jnp.exp(log_gamma * C)
    bc = lambda t: jnp.broadcast_to(t[:, :, None], (RET_HEADS, C, d))
    cdb = jnp.broadcast_to(cd[:, None, None], (RET_HEADS, d, d))
    return cos2, sin2, dec, bc(qd), bc(kd), cdb


def _ret_call(pb, pf, gn_g, gn_b):
    B, S, _ = pb.shape
    cos2, sin2, dec, qd, kd, cd = _ret_tables(S)
    col = lambda off: pl.BlockSpec((1, S, RET_HEAD_DIM), lambda b, h: (b, 0, off + h))
    tab = pl.BlockSpec((1, RET_CHUNK, RET_HEAD_DIM), lambda b, h: (h, 0, 0))
    vec = pl.BlockSpec((1, RET_HEAD_DIM), lambda b, h: (0, h))
    return pl.pallas_call(
        _ret_body,
        grid=(B, RET_HEADS),
        in_specs=[col(0), col(RET_HEADS), col(2 * RET_HEADS), col(0),
                  _resident((S, RET_HEAD_DIM)), _resident((S, RET_HEAD_DIM)),
                  tab, tab, tab, tab, vec, vec],
        out_specs=pl.BlockSpec((1, S, RET_HEAD_DIM), lambda b, h: (b, 0, h)),
        out_shape=jax.ShapeDtypeStruct((B, S, RET_W), BF16),
        compiler_params=_params(("arbitrary", "arbitrary"), VMEM_LIMIT_SMALL),
        name="retention",
    )(pb, pb, pb, pf, cos2, sin2, dec, qd, kd, cd, gn_g.reshape(1, RET_W), gn_b.reshape(1, RET_W))


def _compress_body(xk_ref, xv_ref, pk_ref, pv_ref, w1k_ref, w1v_ref, w2k_ref, w2v_ref, kc_ref, vc_ref):
    n_blk = kc_ref.shape[1]
    for x_ref, p_ref, w1_ref, w2_ref, o_ref in ((xk_ref, pk_ref, w1k_ref, w2k_ref, kc_ref),
                                                (xv_ref, pv_ref, w1v_ref, w2v_ref, vc_ref)):
        first = jnp.zeros((n_blk, w1_ref.shape[-1]), F32)
        second = jnp.zeros((n_blk, w1_ref.shape[-1]), F32)
        for t in range(CMP_STRIDE):
            x = x_ref[0, pl.ds(t, n_blk, stride=CMP_STRIDE), :]
            first = first + _dot((x + p_ref[t:t + 1, :]).astype(BF16), w1_ref[t])
            second = second + _dot((x + p_ref[CMP_STRIDE + t:CMP_STRIDE + t + 1, :]).astype(BF16),
                                   w1_ref[CMP_STRIDE + t])
        hid = _silu(first + pltpu.roll(second, n_blk - 1, 0))
        o_ref[0] = _dot(hid.astype(BF16), w2_ref[...]).astype(o_ref.dtype)


def _compress_call(pf, pos_k, pos_v, w1k, w1v, w2k, w2v):
    B, S, _ = pf.shape
    n_blk = S // CMP_STRIDE
    assert CMP_BLOCK == 2 * CMP_STRIDE

    def both_heads(t):
        z = jnp.zeros_like(t)
        return jnp.concatenate([jnp.concatenate([t, z], -1), jnp.concatenate([z, t], -1)], -2).astype(BF16)

    pos = lambda t: jnp.concatenate([t, t], axis=-1)
    w1 = lambda t: both_heads(t.reshape(CMP_BLOCK, NSA_HEAD_DIM, CMP_HIDDEN))
    kc_block = RET_W // NSA_KVW
    xspec = lambda c: pl.BlockSpec((1, S, NSA_KVW), lambda b: (b, 0, c))
    ospec = pl.BlockSpec((1, n_blk, NSA_KVW), lambda b: (b, 0, 0))
    return pl.pallas_call(
        _compress_body,
        grid=(B,),
        in_specs=[xspec(kc_block), xspec(kc_block + 1),
                  _resident((CMP_BLOCK, NSA_KVW)), _resident((CMP_BLOCK, NSA_KVW)),
                  _resident((CMP_BLOCK, NSA_KVW, 2 * CMP_HIDDEN)), _resident((CMP_BLOCK, NSA_KVW, 2 * CMP_HIDDEN)),
                  _resident((2 * CMP_HIDDEN, NSA_KVW)), _resident((2 * CMP_HIDDEN, NSA_KVW))],
        out_specs=[ospec, ospec],
        out_shape=[jax.ShapeDtypeStruct((B, n_blk, NSA_KVW), BF16)] * 2,
        compiler_params=_params(("arbitrary",), VMEM_LIMIT_SMALL),
        name="nsa_compress",
    )(pf, pf, pos(pos_k), pos(pos_v), w1(w1k), w1(w1v), both_heads(w2k), both_heads(w2v))


CMP_Q_ROWS = 512


def _cmp_attn_body(q_ref, kc_ref, vc_ref, cb_ref, ov_ref, o_ref, sel_ref):
    i = pl.program_id(1)
    tq = q_ref.shape[1]
    lane = lax.broadcasted_iota(jnp.int32, (tq, LANES), 1)
    row = lax.broadcasted_iota(jnp.int32, (tq, LANES), 0)
    lo_half = lane < NSA_HEAD_DIM
    kc = kc_ref[0]
    vc = vc_ref[0]
    zero = jnp.zeros_like(kc)
    kc_lo = lax.broadcasted_iota(jnp.int32, kc.shape, 1) < NSA_HEAD_DIM
    kc_g = (jnp.where(kc_lo, kc, zero), jnp.where(kc_lo, zero, kc))
    valid = (lane * CMP_STRIDE + (CMP_BLOCK - 1) <= i * tq + row)[None]
    q4 = jnp.concatenate([q_ref[0, :, r * LANES:(r + 1) * LANES] for r in range(NSA_GROUP)], axis=0)
    q4 = q4 * (NSA_HEAD_DIM ** -0.5)
    psum, outs = [], []
    for g in range(NSA_KV_HEADS):
        s = _dot_nt(q4, kc_g[g]).reshape(NSA_GROUP, tq, LANES) + cb_ref[g * NSA_GROUP:(g + 1) * NSA_GROUP]
        s = jnp.where(valid, s, NEG_INF)
        m = jnp.max(s, axis=-1, keepdims=True)
        e = jnp.where(valid, jnp.exp2(s - m), 0.0)
        den = jnp.maximum(jnp.sum(e, axis=-1, keepdims=True), 1e-30)
        p = e / den
        psum.append(jnp.sum(p, axis=0))
        outs.append(_dot(p.reshape(NSA_GROUP * tq, LANES).astype(BF16), vc))
    for r in range(NSA_GROUP):
        rs = slice(r * tq, (r + 1) * tq)
        o_ref[0, :, r * LANES:(r + 1) * LANES] = jnp.where(lo_half, outs[0][rs], outs[1][rs])

    n_sel = sel_ref.shape[2]
    blk = lax.broadcasted_iota(jnp.int32, (n_sel, tq), 0)
    qblk = (i * tq + lax.broadcasted_iota(jnp.int32, (n_sel, tq), 1)) // SEL_BLOCK
    forced = jnp.where(blk == 0, 1.0, jnp.where(blk == qblk, 1.0, jnp.where(blk == qblk - 1, 1.0, 0.0)))
    for g in range(NSA_KV_HEADS):
        p_hi, p_lo = _split_bf16(psum[g])
        imp = _dot_nt(ov_ref[...], p_hi) + _dot_nt(ov_ref[...], p_lo)
        score = jnp.where(forced > 0.5, SEL_FORCE_SCORE, jnp.where(blk <= qblk, imp, -1.0))
        rank = jnp.zeros((n_sel, tq), F32)
        for other in range(n_sel):
            so = score[other:other + 1, :]
            tie = jnp.where(blk > other, 1.0, 0.0)
            rank = rank + jnp.where(so > score, 1.0, jnp.where(so == score, tie, 0.0))
        sel_ref[0, g] = jnp.where(rank < float(min(SEL_TOP_N, n_sel)), 1.0, 0.0)


def _cmp_attn_call(pb, kc, vc, cmp_bias, overlap_t):
    B, S, _ = pb.shape
    n_sel = S // SEL_BLOCK
    q_block = (3 * RET_W) // NSA_QW
    tq = CMP_Q_ROWS
    return pl.pallas_call(
        _cmp_attn_body,
        grid=(B, S // tq),
        in_specs=[pl.BlockSpec((1, tq, NSA_QW), lambda b, i: (b, i, q_block)),
                  pl.BlockSpec((1,) + kc.shape[1:], lambda b, i: (b, 0, 0)),
                  pl.BlockSpec((1,) + vc.shape[1:], lambda b, i: (b, 0, 0)),
                  pl.BlockSpec((NSA_HEADS, tq, LANES), lambda b, i: (0, i, 0)),
                  _resident((n_sel, LANES))],
        out_specs=[pl.BlockSpec((1, tq, NSA_QW), lambda b, i: (b, i, 0)),
                   pl.BlockSpec((1, NSA_KV_HEADS, n_sel, tq), lambda b, i: (b, 0, 0, i))],
        out_shape=[jax.ShapeDtypeStruct((B, S, NSA_QW), F32),
                   jax.ShapeDtypeStruct((B, NSA_KV_HEADS, n_sel, S), F32)],
        compiler_params=_params(("arbitrary", "arbitrary"), VMEM_LIMIT_SMALL),
        name="nsa_cmp_attn",
    )(pb, kc, vc, cmp_bias, overlap_t)


SLC_CLASS_TILES = 4


def _slc_far_tiles(cls, n_tiles):
    return min(SLC_CLASS_TILES * cls + 2, n_tiles - 2)


def _slc_body(q_ref, ks_ref, vs_ref, sel_ref, tag_ref, near_ref, o_ref, kaug_scr, vaug_scr):
    i = pl.program_id(1)
    n_tiles = ks_ref.shape[1] // TILE
    n_sel = sel_ref.shape[2]
    lane = lax.broadcasted_iota(jnp.int32, (TILE, LANES), 1)
    lo_half = lane < NSA_HEAD_DIM

    @pl.when(i == 0)
    def _():
        for t in range(n_tiles):
            sl = pl.ds(t * TILE, TILE)
            kt = ks_ref[0, sl, :]
            zero = jnp.zeros_like(kt)
            kaug_scr[0, sl, :] = jnp.concatenate([jnp.where(lo_half, kt, zero), tag_ref[sl, :]], axis=1)
            kaug_scr[1, sl, :] = jnp.concatenate([jnp.where(lo_half, zero, kt), tag_ref[sl, :]], axis=1)
            vaug_scr[sl, :] = jnp.concatenate([vs_ref[0, sl, :], tag_ref[sl, :]], axis=1)

    prev = jnp.maximum(i - 1, 0)
    own_sl = pl.ds(pl.multiple_of(i * TILE, TILE), TILE)
    prev_sl = pl.ds(pl.multiple_of(prev * TILE, TILE), TILE)
    no_prev = jnp.where(lax.broadcasted_iota(jnp.int32, (1, 2 * TILE), 1) < TILE,
                        jnp.where(i == 0, NEG_INF, 0.0), 0.0)
    tile_pen = jnp.where(lane - n_sel < i - 1, 0.0, NEG_INF)
    sel_pad = jnp.zeros((LANES - n_sel, TILE), F32)

    def tile_body(n_far):
        wf = n_far * TILE
        v_near = jnp.concatenate([vaug_scr[prev_sl, :], vaug_scr[own_sl, :]], axis=0)
        q4 = jnp.concatenate([q_ref[0, :, r * LANES:(r + 1) * LANES] for r in range(NSA_GROUP)], axis=0)
        q4 = q4 * (NSA_HEAD_DIM ** -0.5)
        q_far, q_near, k_near = [], [], []
        for g in range(NSA_KV_HEADS):
            sel_q = jnp.concatenate([sel_ref[0, g], sel_pad], axis=0).T
            blk_pen = (sel_q - 1.0) * (-NEG_INF)
            pen_near = jnp.where(lane < n_sel, blk_pen, 0.0).astype(BF16)
            pen_far = jnp.where(lane < n_sel, blk_pen,
                                jnp.where(lane < n_sel + n_tiles, tile_pen, 0.0)).astype(BF16)
            q_far.append(jnp.concatenate([q4, jnp.concatenate([pen_far] * NSA_GROUP, axis=0)], axis=1))
            q_near.append(jnp.concatenate([q4, jnp.concatenate([pen_near] * NSA_GROUP, axis=0)], axis=1))
            k_near.append(jnp.concatenate([kaug_scr[g, prev_sl, :], kaug_scr[g, own_sl, :]], axis=0))

        n_split = 2
        rows = NSA_GROUP * TILE // n_split
        chains = [(g, part) for g in range(NSA_KV_HEADS) for part in range(n_split)]

        def scores(g, part):
            rs = slice(part * rows, (part + 1) * rows)
            return (_dot_nt(q_far[g][rs], kaug_scr[g, :wf, :]),
                    _dot_nt(q_near[g][rs], k_near[g]) + (near_ref[g, rs, :] + no_prev))

        def weights(s_far, s_near):
            m = jnp.maximum(jnp.max(s_far, axis=-1, keepdims=True), jnp.max(s_near, axis=-1, keepdims=True))
            return jnp.exp2(s_far - m).astype(BF16), jnp.exp2(s_near - m).astype(BF16)

        def values(p_far, p_near):
            acc = _dot(p_far, vaug_scr[:wf, :]) + _dot(p_near, v_near)
            return acc[:, :NSA_KVW] / acc[:, NSA_KVW + LANES - 1:]

        s, p, o = {}, {}, {}
        for step in range(len(chains) + 2):
            if step < len(chains):
                s[step] = scores(*chains[step])
            if 0 <= step - 2 < len(chains):
                o[step - 2] = values(*p.pop(step - 2))
            if 0 <= step - 1 < len(chains):
                p[step - 1] = weights(*s.pop(step - 1))
        for r in range(NSA_GROUP):
            part, rs = r // (NSA_GROUP // n_split), slice((r % (NSA_GROUP // n_split)) * TILE,
                                                          (r % (NSA_GROUP // n_split) + 1) * TILE)
            o_ref[0, :, r * LANES:(r + 1) * LANES] = jnp.where(lo_half, o[part][rs], o[n_split + part][rs])

    n_classes = -(-n_tiles // SLC_CLASS_TILES)
    for cls in range(n_classes):
        pl.when(i // SLC_CLASS_TILES == cls)(functools.partial(tile_body, _slc_far_tiles(cls, n_tiles)))


def _slc_call(pb, sel_t, key_tags, near):
    B, S, _ = pb.shape
    n_sel = S // SEL_BLOCK
    n_tiles = S // TILE
    assert n_sel + n_tiles < LANES
    q_block = (3 * RET_W) // NSA_QW
    ks_block = (3 * RET_W + NSA_QW) // NSA_KVW
    return pl.pallas_call(
        _slc_body,
        grid=(B, n_tiles),
        in_specs=[pl.BlockSpec((1, TILE, NSA_QW), lambda b, i: (b, i, q_block)),
                  pl.BlockSpec((1, S, NSA_KVW), lambda b, i: (b, 0, ks_block)),
                  pl.BlockSpec((1, S, NSA_KVW), lambda b, i: (b, 0, ks_block + 1)),
                  pl.BlockSpec((1, NSA_KV_HEADS, n_sel, TILE), lambda b, i: (b, 0, 0, i)),
                  _resident(key_tags.shape), _resident(near.shape)],
        out_specs=pl.BlockSpec((1, TILE, NSA_QW), lambda b, i: (b, i, 0)),
        out_shape=jax.ShapeDtypeStruct((B, S, NSA_QW), F32),
        scratch_shapes=[pltpu.VMEM((NSA_KV_HEADS, S, NSA_KVW + LANES), BF16),
                        pltpu.VMEM((S, NSA_KVW + LANES), BF16)],
        compiler_params=_params(("arbitrary", "arbitrary"), VMEM_LIMIT_LARGE),
        name="nsa_selected",
    )(pb, pb, pb, sel_t, key_tags, near)


WIN_PREV_TILES = (WIN_SIZE - 1 + TILE - 1) // TILE
WIN_Q_TILES = 2


def _win_body(q_ref, k_ref, v_ref, wb_ref, gate_ref, eg_ref, ocmp_ref, oslc_ref, y_ref):
    i = pl.program_id(1)
    tq = q_ref.shape[1]
    n_span = WIN_PREV_TILES + WIN_Q_TILES
    first = i * WIN_Q_TILES - WIN_PREV_TILES
    lo_half = lax.broadcasted_iota(jnp.int32, (tq, LANES), 1) < NSA_HEAD_DIM

    def attend(early):
        if early:
            k_tiles, v_tiles, negs = [], [], []
            for t in range(n_span):
                sl = pl.ds(pl.multiple_of(jnp.maximum(first + t, 0) * TILE, TILE), TILE)
                k_tiles.append(k_ref[0, sl, :])
                v_tiles.append(v_ref[0, sl, :])
                negs.append(jnp.full((tq, TILE), jnp.where(first + t < 0, NEG_INF, 0.0), F32))
            k_all = jnp.concatenate(k_tiles, axis=0)
            v_all = jnp.concatenate(v_tiles, axis=0)
            missing = jnp.concatenate(negs, axis=1)[None]
        else:
            sl = pl.ds(pl.multiple_of(first * TILE, TILE), n_span * TILE)
            k_all = k_ref[0, sl, :]
            v_all = v_ref[0, sl, :]
        zero = jnp.zeros_like(k_all)
        k_lo = lax.broadcasted_iota(jnp.int32, k_all.shape, 1) < NSA_HEAD_DIM
        k_g = (jnp.where(k_lo, k_all, zero), jnp.where(k_lo, zero, k_all))
        v_ones = jnp.concatenate([v_all, jnp.ones_like(v_all)], axis=1)

        g_hi, g_lo = _split_bf16(jax.nn.sigmoid(gate_ref[0]))
        gates = _dot(g_hi, eg_ref[...]) + _dot(g_lo, eg_ref[...])
        q4 = jnp.concatenate([q_ref[0, :, r * LANES:(r + 1) * LANES] for r in range(NSA_GROUP)], axis=0)
        q4 = q4 * (NSA_HEAD_DIM ** -0.5)
        n_split = 2
        heads = NSA_GROUP // n_split
        chains = [(g, part) for g in range(NSA_KV_HEADS) for part in range(n_split)]

        def scores(g, part):
            s = _dot_nt(q4[part * heads * tq:(part + 1) * heads * tq], k_g[g]).reshape(heads, tq, n_span * TILE)
            s = s + wb_ref[g * NSA_GROUP + part * heads:g * NSA_GROUP + (part + 1) * heads]
            return s + missing if early else s

        def weights(s):
            m = jnp.max(s, axis=-1, keepdims=True)
            return jnp.exp2(s - m).astype(BF16).reshape(heads * tq, n_span * TILE)

        def values(p):
            acc = _dot(p, v_ones)
            return acc[:, :NSA_KVW] / acc[:, NSA_KVW:NSA_KVW + 1]

        s, p, o = {}, {}, {}
        for step in range(len(chains) + 2):
            if step < len(chains):
                s[step] = scores(*chains[step])
            if 0 <= step - 2 < len(chains):
                o[step - 2] = values(p.pop(step - 2))
            if 0 <= step - 1 < len(chains):
                p[step - 1] = weights(s.pop(step - 1))
        for r in range(NSA_GROUP):
            cols = slice(r * LANES, (r + 1) * LANES)
            part, rs = r // heads, slice((r % heads) * tq, (r % heads + 1) * tq)
            o_win = jnp.where(lo_half, o[part][rs], o[n_split + part][rs])
            y = (gates[:, r * LANES:(r + 1) * LANES] * ocmp_ref[0, :, cols]
                 + gates[:, NSA_QW + r * LANES:NSA_QW + (r + 1) * LANES] * oslc_ref[0, :, cols]
                 + gates[:, 2 * NSA_QW + r * LANES:2 * NSA_QW + (r + 1) * LANES] * o_win)
            y_ref[0, :, cols] = y.astype(y_ref.dtype)

    pl.when(first >= 0)(functools.partial(attend, False))
    pl.when(first < 0)(functools.partial(attend, True))


def _win_call(pb, pf, win_bias, gate_expand, o_cmp, o_slc):
    B, S, _ = pb.shape
    q_block = (3 * RET_W) // NSA_QW
    kw_block = (3 * RET_W + NSA_QW) // NSA_KVW + 2
    gate_block = (RET_W + 2 * NSA_KVW) // LANES
    tq = WIN_Q_TILES * TILE
    tile_spec = pl.BlockSpec((1, tq, NSA_QW), lambda b, i: (b, i, 0))
    return pl.pallas_call(
        _win_body,
        grid=(B, S // tq),
        in_specs=[pl.BlockSpec((1, tq, NSA_QW), lambda b, i: (b, i, q_block)),
                  pl.BlockSpec((1, S, NSA_KVW), lambda b, i: (b, 0, kw_block)),
                  pl.BlockSpec((1, S, NSA_KVW), lambda b, i: (b, 0, kw_block + 1)),
                  _resident(win_bias.shape),
                  pl.BlockSpec((1, tq, LANES), lambda b, i: (b, i, gate_block)),
                  _resident((LANES, 3 * NSA_QW)),
                  tile_spec, tile_spec],
        out_specs=tile_spec,
        out_shape=jax.ShapeDtypeStruct((B, S, NSA_QW), BF16),
        compiler_params=_params(("arbitrary", "arbitrary"), VMEM_LIMIT_LARGE),
        name="nsa_window_combine",
    )(pb, pb, pb, win_bias, pf, gate_expand, o_cmp, o_slc)


def _post_body(h_ref, ma_ref, mb_ref, mod_ref, woa_ref, wob_ref, lng_ref, lnb_ref,
               wg_ref, wu_ref, wd_ref, o_ref):
    y = _dot(ma_ref[0], woa_ref[...]) + _dot(mb_ref[0], wob_ref[...])
    h1 = _layer_norm(DEEPNORM_ALPHA * h_ref[0] + mod_ref[0, 2:3, :] * y, lng_ref[0:1, :], lnb_ref[0:1, :])
    u = (h1 * (1.0 + mod_ref[0, 4:5, :]) + mod_ref[0, 3:4, :]).astype(BF16)
    acc = jnp.zeros(h1.shape, F32)
    for c0, cn in FFN_CHUNKS:
        gate = _dot(u, wg_ref[:, c0:c0 + cn])
        up = _dot(u, wu_ref[:, c0:c0 + cn])
        acc = acc + _dot((_silu(gate) * up).astype(BF16), wd_ref[c0:c0 + cn, :])
    o_ref[0] = _layer_norm(DEEPNORM_ALPHA * h1 + mod_ref[0, 5:6, :] * acc, lng_ref[1:2, :], lnb_ref[1:2, :])


def _post_call(h, mix_a, mix_b, col_a, col_b, mod, wo_a, wo_b, ln_g, ln_b, w_gate, w_up, w_down, tm, name):
    B, S, _ = h.shape
    half = D_MODEL // 2
    tok = pl.BlockSpec((1, tm, D_MODEL), lambda b, t: (b, t, 0))
    return pl.pallas_call(
        _post_body,
        grid=(B, S // tm),
        in_specs=[tok,
                  pl.BlockSpec((1, tm, half), lambda b, t: (b, t, col_a)),
                  pl.BlockSpec((1, tm, half), lambda b, t: (b, t, col_b)),
                  pl.BlockSpec((1, 6, D_MODEL), lambda b, t: (b, 0, 0)),
                  _resident((half, D_MODEL)), _resident((half, D_MODEL)),
                  _resident((2, D_MODEL)), _resident((2, D_MODEL)),
                  _resident((D_MODEL, D_FF)), _resident((D_MODEL, D_FF)), _resident((D_FF, D_MODEL))],
        out_specs=tok,
        out_shape=jax.ShapeDtypeStruct((B, S, D_MODEL), F32),
        compiler_params=_params(("arbitrary", "arbitrary"), VMEM_LIMIT_LARGE),
        name=name,
    )(h, mix_a, mix_b, mod, wo_a, wo_b, ln_g, ln_b, w_gate, w_up, w_down)


DIL_STEP_TILES = 4


def _dil_body(q_ref, k_ref, kp_ref, v_ref, vp_ref, b_ref, o_ref, lse_ref, *, tiles_per_seg):
    t = pl.program_id(1)
    res_tiles = min(tiles_per_seg, DIL_STEP_TILES)
    with_prev = tiles_per_seg > 1
    if tiles_per_seg > DIL_STEP_TILES:
        first_prev = jnp.where((t * DIL_STEP_TILES) % tiles_per_seg == 0, NEG_INF, 0.0)
    else:
        first_prev = NEG_INF

    def span(ref, before_ref, j, cols):
        if not with_prev:
            return ref[0, j * TILE:(j + 1) * TILE, cols]
        if j > 0:
            return ref[0, (j - 1) * TILE:(j + 1) * TILE, cols]
        return jnp.concatenate([before_ref[0, :, cols], ref[0, :TILE, cols]], axis=0)

    def tile_bias(h, j):
        if not with_prev:
            return b_ref[h, TILE:, :]
        if j % res_tiles != 0:
            return b_ref[h]
        gone = first_prev if j == 0 else NEG_INF
        return jnp.concatenate([b_ref[h, :TILE, :] + gone, b_ref[h, TILE:, :]], axis=0)

    head_cols = lambda h: slice(h * DIL_HEAD_DIM, (h + 1) * DIL_HEAD_DIM)
    pad = jnp.zeros((LANES - DIL_HEADS, TILE), F32)

    def scores(j):
        return jnp.concatenate(
            [_dot_nt(span(k_ref, kp_ref, j, head_cols(h)), q_ref[0, j * TILE:(j + 1) * TILE, head_cols(h)])
             + tile_bias(h, j) for h in range(DIL_HEADS)], axis=1)

    def weights(j, s):
        m = jnp.max(s, axis=0, keepdims=True)
        e = jnp.exp2(s - m)
        den = jnp.sum(e, axis=0, keepdims=True)
        lse = (m + jnp.log2(den)) * math.log(2.0)
        by_head = [lse[:, h * TILE:(h + 1) * TILE] for h in range(DIL_HEADS)]
        lse_ref[0, j * TILE:(j + 1) * TILE, :] = jnp.concatenate(by_head + [pad], axis=0).T
        return (e * (1.0 / den)).astype(BF16)

    def values(j, p):
        for h in range(DIL_HEADS):
            o = _dot_tn(p[:, h * TILE:(h + 1) * TILE], span(v_ref, vp_ref, j, head_cols(h)))
            o_ref[0, j * TILE:(j + 1) * TILE, head_cols(h)] = o.astype(o_ref.dtype)

    s, p = {}, {}
    for step in range(DIL_STEP_TILES + 2):
        if step < DIL_STEP_TILES:
            s[step] = scores(step)
        if 0 <= step - 2 < DIL_STEP_TILES:
            values(step - 2, p.pop(step - 2))
        if 0 <= step - 1 < DIL_STEP_TILES:
            p[step - 1] = weights(step - 1, s.pop(step - 1))


def _dil_call(proj, bias, dilation, name):
    B, S, _ = proj.shape
    step = DIL_STEP_TILES * TILE
    own = lambda c: pl.BlockSpec((1, step, DIL_WIDTH), lambda b, i: (b, i, c))
    before = lambda c: pl.BlockSpec((1, TILE, DIL_WIDTH),
                                    lambda b, i: (b, jnp.maximum(i * DIL_STEP_TILES - 1, 0), c))
    tile = lambda width: pl.BlockSpec((1, step, width), lambda b, i: (b, i, 0))
    return pl.pallas_call(
        functools.partial(_dil_body, tiles_per_seg=(S // dilation) // TILE),
        grid=(B, S // step),
        in_specs=[own(0), own(1), before(1), own(2), before(2), _resident(bias.shape)],
        out_specs=[tile(DIL_WIDTH), tile(LANES)],
        out_shape=[jax.ShapeDtypeStruct((B, S, DIL_WIDTH), BF16), jax.ShapeDtypeStruct((B, S, LANES), F32)],
        compiler_params=_params(("arbitrary", "arbitrary"), VMEM_LIMIT_LARGE),
        name=name,
    )(proj, proj, proj, proj, proj, bias)


def _dil_mix_body(o0_ref, o1_ref, o2_ref, l0_ref, l1_ref, l2_ref, y_ref, o_scr, l_scr):
    o_refs = (o0_ref, o1_ref, o2_ref)
    l_refs = (l0_ref, l1_ref, l2_ref)
    tm = y_ref.shape[1]
    for gi, (o_ref, l_ref) in enumerate(zip(o_refs, l_refs)):
        dilation = o_ref.shape[1]
        for r in range(dilation):
            rows = pl.ds(r, tm // dilation, stride=dilation)
            l_scr[gi, rows, :] = l_ref[0, r]
            for h in range(DIL_HEADS):
                o_scr[gi, h, rows, :] = o_ref[0, r, :, h * DIL_HEAD_DIM:(h + 1) * DIL_HEAD_DIM].astype(F32)
    for h in range(DIL_HEADS):
        lses = [jnp.broadcast_to(l_scr[gi, :, h:h + 1], (tm, DIL_HEAD_DIM)) for gi in range(len(o_refs))]
        m = jnp.maximum(jnp.maximum(lses[0], lses[1]), lses[2])
        ws = [jnp.exp(l - m) for l in lses]
        den = ws[0] + ws[1] + ws[2]
        y = sum((w / den) * o_scr[gi, h] for gi, w in enumerate(ws))
        y_ref[0, :, h * DIL_HEAD_DIM:(h + 1) * DIL_HEAD_DIM] = y.astype(y_ref.dtype)


def _dil_mix_call(outs, lses, tm):
    B, S, _ = outs[0].shape
    dilations = [d for _, d in DIL_PATTERNS]
    by_residue = lambda t, d: t.reshape(B, d, S // d, t.shape[-1])
    spec = lambda d, width: pl.BlockSpec((1, d, tm // d, width), lambda b, t: (b, 0, t, 0))
    return pl.pallas_call(
        _dil_mix_body,
        grid=(B, S // tm),
        in_specs=[spec(d, DIL_WIDTH) for d in dilations] + [spec(d, LANES) for d in dilations],
        out_specs=pl.BlockSpec((1, tm, DIL_WIDTH), lambda b, t: (b, t, 0)),
        out_shape=jax.ShapeDtypeStruct((B, S, DIL_WIDTH), BF16),
        scratch_shapes=[pltpu.VMEM((len(dilations), DIL_HEADS, tm, DIL_HEAD_DIM), F32),
                        pltpu.VMEM((len(dilations), tm, LANES), F32)],
        compiler_params=_params(("arbitrary", "arbitrary"), VMEM_LIMIT_SMALL),
        name="dilated_mix",
    )(*[by_residue(o, d) for o, d in zip(outs, dilations)], *[by_residue(l, d) for l, d in zip(lses, dilations)])


def _t5_bucket_np(dist):
    n = np.maximum(dist, 0)
    max_exact = REL_BUCKETS // 2
    nf = np.maximum(n, 1).astype(np.float64)
    val = np.log(nf / max_exact) / math.log(REL_MAX_DIST / max_exact) * (REL_BUCKETS - max_exact)
    frac = np.abs(val - np.round(val))
    on_edge = (frac < 1e-9) & (n > max_exact) & (n < REL_MAX_DIST)
    assert not on_edge.any()
    large = np.minimum(max_exact + np.floor(val + 1e-9).astype(np.int64), REL_BUCKETS - 1)
    return np.where(n < max_exact, n, large).astype(np.int32)


def _shift_table(rel_bias, rows, cols, step, dist_fn, valid_fn):
    u = np.concatenate([np.arange(cols), np.arange(-(rows - 1) * step, 0)])
    period = u.size
    vals = jnp.take(rel_bias, jnp.asarray(_t5_bucket_np(dist_fn(u))), axis=0).T
    vals = jnp.where(jnp.asarray(valid_fn(u))[None], vals, NEG_INF)
    t = jnp.tile(vals, (1, rows))[:, :rows * (period - step)].reshape(vals.shape[0], rows, period - step)
    return t[:, :, :cols]


def _nsa_tables(rel_bias, S):
    rel_bias = rel_bias * LOG2_E
    always = lambda u: np.ones(u.shape, bool)
    win_dist = lambda u: WIN_PREV_TILES * TILE - u
    win_bias = _shift_table(rel_bias, WIN_Q_TILES * TILE, (WIN_PREV_TILES + WIN_Q_TILES) * TILE, 1, win_dist,
                            lambda u: (win_dist(u) >= 0) & (win_dist(u) <= WIN_SIZE - 1))
    d0 = _shift_table(rel_bias, TILE, TILE, 1, lambda u: -u, lambda u: u <= 0)
    d1 = _shift_table(rel_bias, TILE, TILE, 1, lambda u: TILE - u, always)
    far_bucket = _t5_bucket_np(np.arange(TILE + 1, S + TILE))
    assert (far_bucket == far_bucket[0]).all()
    far = rel_bias[int(far_bucket[0])][:, None, None]
    near = jnp.concatenate([d1 - far, d0 - far], axis=2)
    near = near.reshape(NSA_KV_HEADS, NSA_GROUP * TILE, 2 * TILE)
    cmp_bias = _shift_table(rel_bias, LANES, S, CMP_STRIDE, lambda u: u - (CMP_BLOCK - 1), always)
    cmp_bias = cmp_bias.transpose(0, 2, 1)
    n_cmp = (S - CMP_BLOCK) // CMP_STRIDE + 1
    n_sel = S // SEL_BLOCK
    cs = (np.arange(n_cmp) * CMP_STRIDE)[:, None]
    ss = (np.arange(n_sel) * SEL_BLOCK)[None, :]
    ov = np.clip(np.minimum(cs + CMP_BLOCK, ss + SEL_BLOCK) - np.maximum(cs, ss), 0, None) / CMP_BLOCK
    ov_t = np.zeros((n_sel, LANES), np.float32)
    ov_t[:, :n_cmp] = ov.T
    key = np.arange(S)[:, None]
    lane = np.arange(LANES)[None, :]
    tags = ((lane == key // SEL_BLOCK) | (lane == n_sel + key // TILE) | (lane == LANES - 1)).astype(np.float32)
    eg = np.zeros((LANES, 3 * NSA_QW), np.float32)
    for g in range(NSA_KV_HEADS):
        for r in range(NSA_GROUP):
            for j in range(3):
                base = j * NSA_QW + r * LANES + g * NSA_HEAD_DIM
                eg[g * NSA_GROUP * 3 + r * 3 + j, base:base + NSA_HEAD_DIM] = 1.0
    return win_bias, near, cmp_bias, jnp.asarray(ov_t, BF16), jnp.asarray(tags, BF16), jnp.asarray(eg, BF16)


def _dil_bias(rel_bias, dilation, max_dist):
    dist = lambda u: TILE + u
    return _shift_table(rel_bias * LOG2_E, 2 * TILE, TILE, 1, lambda u: dist(u) * dilation,
                        lambda u: (dist(u) >= 0) & (dist(u) <= max_dist))


def _nsa_head_perm():
    perm = np.zeros(NSA_QW, np.int64)
    for r in range(NSA_GROUP):
        for g in range(NSA_KV_HEADS):
            new = r * LANES + g * NSA_HEAD_DIM
            old = (g * NSA_GROUP + r) * NSA_HEAD_DIM
            perm[new:new + NSA_HEAD_DIM] = np.arange(old, old + NSA_HEAD_DIM)
    return perm


def _layer0_mixer(h, mod, ab_w_in, rel_bias, gn_g, gn_b, pos_k, pos_v, w1k, w2k, w1v, w2v):
    B, S, _ = h.shape
    o = np.cumsum((0, RET_W, RET_W, RET_W, RET_W, NSA_QW) + (NSA_KVW,) * 6 + (3 * NSA_HEADS,))
    seg = lambda a: ab_w_in[:, o[a]:o[a + 1]]
    gate_w = jnp.pad(seg(11), ((0, 0), (0, LANES - 3 * NSA_HEADS)))
    q_nsa = seg(4)[:, _nsa_head_perm()] * LOG2_E
    w = jnp.concatenate([seg(0), seg(1), seg(2), q_nsa, seg(7), seg(8), seg(9), seg(10),
                         seg(3), seg(5), seg(6), gate_w], axis=1).astype(BF16)
    pb, pf = _pre_call(h, mod, w, ((PRE0_BF16_COLS, BF16), (PRE0_F32_COLS, F32)), 512, "pre0")

    y_ret = _ret_call(pb, pf, gn_g, gn_b)

    win_bias, near, cmp_bias, ov_t, key_tags, eg = _nsa_tables(rel_bias, S)
    kc, vc = _compress_call(pf, pos_k, pos_v, w1k, w1v, w2k, w2v)
    o_cmp, sel_t = _cmp_attn_call(pb, kc, vc, cmp_bias, ov_t)
    o_slc = _slc_call(pb, sel_t, key_tags, near)
    y_nsa = _win_call(pb, pf, win_bias, eg, o_cmp, o_slc)
    return y_ret, y_nsa


def _layer1_mixer(h, mod, dil_w_in, rel_bias):
    B, S, _ = h.shape
    outs, lses = [], []
    for gi, (window, dilation) in enumerate(DIL_PATTERNS):
        w = dil_w_in[:, gi * 3 * DIL_WIDTH:(gi + 1) * 3 * DIL_WIDTH]
        w = jnp.concatenate([w[:, :DIL_WIDTH] * (DIL_HEAD_DIM ** -0.5 * LOG2_E), w[:, DIL_WIDTH:]], axis=1).astype(BF16)
        proj, = _pre_call(h, mod, w, ((3 * DIL_WIDTH, BF16),), 512, f"pre1_{gi}", dilation)
        bias = _dil_bias(rel_bias, dilation, window // dilation)
        o, lse = _dil_call(proj, bias, dilation, f"dilated_{gi}")
        outs.append(o)
        lses.append(lse)
    return _dil_mix_call(outs, lses, 256)


def kernel(x, c, rel_bias, ada_w, ada_b, ln_g, ln_b, ab_w_in, ab_w_out, ret_gn_g, ret_gn_b, cmp_pos_k, cmp_pos_v, cmp_k_w1, cmp_k_w2, cmp_v_w1, cmp_v_w2, dil_w_in, dil_w_out, ffn_w_gate, ffn_w_up, ffn_w_down):
    B = x.shape[0]
    mod = _ada_call(c, ada_w, ada_b).reshape(DEPTH, B, 6, D_MODEL)
    h = x
    for layer in range(DEPTH):
        i = layer // 2
        if layer % 2 == 0:
            mix_a, mix_b = _layer0_mixer(h, mod[layer], ab_w_in[i], rel_bias, ret_gn_g[i], ret_gn_b[i],
                                         cmp_pos_k[i], cmp_pos_v[i], cmp_k_w1[i], cmp_k_w2[i],
                                         cmp_v_w1[i], cmp_v_w2[i])
            col_a, col_b = 0, 0
            wo_a = ab_w_out[i, :RET_W]
            wo_b = ab_w_out[i, RET_W:][_nsa_head_perm()]
        else:
            mix_a = mix_b = _layer1_mixer(h, mod[layer], dil_w_in[i], rel_bias)
            col_a, col_b = 0, 1
            wo_a = dil_w_out[i, :D_MODEL // 2]
            wo_b = dil_w_out[i, D_MODEL // 2:]
        h = _post_call(h, mix_a, mix_b, col_a, col_b, mod[layer], wo_a.astype(BF16), wo_b.astype(BF16),
                       ln_g[layer], ln_b[layer], ffn_w_gate[layer].astype(BF16), ffn_w_up[layer].astype(BF16),
                       ffn_w_down[layer].astype(BF16), 512, f"post{layer}")
    return h
```

```python
import functools
import math

import numpy as np
import jax
import jax.numpy as jnp
from jax import lax
from jax.experimental import pallas as pl
from jax.experimental.pallas import tpu as pltpu

F32 = jnp.float32
BF16 = jnp.bfloat16

D_MODEL = 1024
DEPTH = 2
DEEPNORM_ALPHA = (2 * DEPTH) ** 0.25
LN_EPS = 1e-5
NEG_INF = -1e30
LOG2_E = math.log2(math.e)

RET_HEADS = 4
RET_HEAD_DIM = 128
RET_CHUNK = 128
ROPE_BASE = 10000.0
RET_W = RET_HEADS * RET_HEAD_DIM

NSA_HEADS = 8
NSA_KV_HEADS = 2
NSA_GROUP = 4
NSA_HEAD_DIM = 64
CMP_BLOCK = 32
CMP_STRIDE = 16
CMP_HIDDEN = 256
SEL_BLOCK = 64
SEL_TOP_N = 16
SEL_FORCE_SCORE = 1e4
WIN_SIZE = 512
NSA_QW = NSA_HEADS * NSA_HEAD_DIM
NSA_KVW = NSA_KV_HEADS * NSA_HEAD_DIM

DIL_PATTERNS = ((128, 1), (512, 4), (2048, 16))
DIL_HEADS = 8
DIL_HEAD_DIM = 128
DIL_WIDTH = DIL_HEADS * DIL_HEAD_DIM

REL_BUCKETS = 32
REL_MAX_DIST = 128
D_FF = 2816

LANES = 128
TILE = 128
VMEM_LIMIT_SMALL = 32 * 1024 * 1024
VMEM_LIMIT_LARGE = 56 * 1024 * 1024

PRE0_BF16_COLS = 3 * RET_W + NSA_QW + 4 * NSA_KVW
PRE0_F32_COLS = RET_W + 2 * NSA_KVW + LANES
FFN_CHUNKS = ((0, 768), (768, 768), (1536, 768), (2304, 512))


def _dot(a, b):
    return jnp.dot(a, b, preferred_element_type=F32)


def _dot_nt(a, b):
    return lax.dot_general(a, b, (((1,), (1,)), ((), ())), preferred_element_type=F32)


def _dot_tn(a, b):
    return lax.dot_general(a, b, (((0,), (0,)), ((), ())), preferred_element_type=F32)


def _split_bf16(x):
    hi = x.astype(BF16)
    lo = (x - hi.astype(F32)).astype(BF16)
    return hi, lo


def _silu(x):
    return x * jax.nn.sigmoid(x)


def _layer_norm(x, g, b):
    mu = jnp.mean(x, axis=-1, keepdims=True)
    xc = x - mu
    var = jnp.mean(xc * xc, axis=-1, keepdims=True)
    return xc * lax.rsqrt(var + LN_EPS) * g + b


def _resident(shape):
    return pl.BlockSpec(shape, lambda *_: (0,) * len(shape), pipeline_mode=pl.Buffered(1))


def _params(sem, vmem):
    return pltpu.CompilerParams(dimension_semantics=sem, vmem_limit_bytes=vmem)


def _ada_body(c_ref, w_ref, b_ref, o_ref):
    a_hi, a_lo = _split_bf16(_silu(c_ref[...]))
    w_hi, w_lo = _split_bf16(w_ref[0])
    o_ref[0] = _dot(a_hi, w_hi) + _dot(a_lo, w_hi) + _dot(a_hi, w_lo) + b_ref[0]


def _ada_call(c, ada_w, ada_b):
    B = c.shape[0]
    n_out = ada_w.shape[-1]
    tn = n_out // 4
    return pl.pallas_call(
        _ada_body,
        grid=(DEPTH, n_out // tn),
        in_specs=[pl.BlockSpec((B, D_MODEL), lambda l, n: (0, 0)),
                  pl.BlockSpec((1, D_MODEL, tn), lambda l, n: (l, 0, n)),
                  pl.BlockSpec((1, 1, tn), lambda l, n: (l, 0, n))],
        out_specs=pl.BlockSpec((1, B, tn), lambda l, n: (l, 0, n)),
        out_shape=jax.ShapeDtypeStruct((DEPTH, B, n_out), F32),
        compiler_params=_params(("arbitrary", "arbitrary"), VMEM_LIMIT_LARGE),
        name="ada_mod",
    )(c, ada_w, ada_b.reshape(DEPTH, 1, n_out))


def _pre_body(h_ref, mod_ref, w_ref, *refs, dilation):
    if dilation == 1:
        o_refs, h = refs, h_ref[0]
    else:
        o_refs, h_scr = refs[:-1], refs[-1]
        tm = h_ref.shape[1]
        for c in range(h_scr.shape[0]):
            h_scr[c] = h_ref[0, :, c * LANES:(c + 1) * LANES]
        h = jnp.concatenate(
            [jnp.concatenate([h_scr[c, pl.ds(r, tm // dilation, stride=dilation), :]
                              for c in range(h_scr.shape[0])], axis=1) for r in range(dilation)], axis=0)
    u = (h * (1.0 + mod_ref[0, 1:2, :]) + mod_ref[0, 0:1, :]).astype(BF16)
    off = 0
    for o_ref in o_refs:
        n = o_ref.shape[-1]
        o_ref[0] = _dot(u, w_ref[:, off:off + n]).astype(o_ref.dtype).reshape(o_ref.shape[1:])
        off += n


def _pre_call(h, mod, w, out_cols_dtypes, tm, name, dilation=1):
    B, S, _ = h.shape
    n_total = w.shape[1]
    assert sum(n for n, _ in out_cols_dtypes) == n_total
    if dilation == 1:
        out_specs = [pl.BlockSpec((1, tm, n), lambda b, t: (b, t, 0)) for n, _ in out_cols_dtypes]
        out_shape = [jax.ShapeDtypeStruct((B, S, n), dt) for n, dt in out_cols_dtypes]
        scratch = []
    else:
        out_specs = [pl.BlockSpec((1, dilation, tm // dilation, n), lambda b, t: (b, 0, t, 0))
                     for n, _ in out_cols_dtypes]
        out_shape = [jax.ShapeDtypeStruct((B, dilation, S // dilation, n), dt) for n, dt in out_cols_dtypes]
        scratch = [pltpu.VMEM((D_MODEL // LANES, tm, LANES), F32)]
    outs = pl.pallas_call(
        functools.partial(_pre_body, dilation=dilation),
        grid=(B, S // tm),
        in_specs=[pl.BlockSpec((1, tm, D_MODEL), lambda b, t: (b, t, 0)),
                  pl.BlockSpec((1, 6, D_MODEL), lambda b, t: (b, 0, 0)),
                  _resident((D_MODEL, n_total))],
        out_specs=out_specs,
        out_shape=out_shape,
        scratch_shapes=scratch,
        compiler_params=_params(("arbitrary", "arbitrary"), VMEM_LIMIT_LARGE),
        name=name,
    )(h, mod, w)
    return [o.reshape(B, S, o.shape[-1]) for o in outs]


def _ret_body(q_ref, k_ref, v_ref, g_ref, cos_ref, sin_ref, dec_ref, qd_ref, kd_ref, cd_ref,
              gng_ref, gnb_ref, o_ref):
    n_chunks = q_ref.shape[1] // RET_CHUNK
    state = jnp.zeros((RET_HEAD_DIM, RET_HEAD_DIM), F32)
    for n in range(n_chunks):
        sl = pl.ds(n * RET_CHUNK, RET_CHUNK)
        q = q_ref[0, sl, :].astype(F32)
        k = k_ref[0, sl, :].astype(F32)
        v = v_ref[0, sl, :]
        c2 = cos_ref[sl, :]
        s2 = sin_ref[sl, :]
        qr = (q * c2 + pltpu.roll(q, RET_HEAD_DIM // 2, 1) * s2) * (RET_HEAD_DIM ** -0.5)
        kr = k * c2 + pltpu.roll(k, RET_HEAD_DIM // 2, 1) * s2
        scores = _dot_nt(qr.astype(BF16), kr.astype(BF16)) * dec_ref[0]
        inner = _dot(scores.astype(BF16), v)
        cross = _dot((qr * qd_ref[0]).astype(BF16), state.astype(BF16))
        kv = _dot_tn((kr * kd_ref[0]).astype(BF16), v)
        state = state * cd_ref[0] + kv
        y = inner + cross
        mu = jnp.mean(y, axis=-1, keepdims=True)
        yc = y - mu
        var = jnp.mean(yc * yc, axis=-1, keepdims=True)
        yn = yc * lax.rsqrt(var + LN_EPS)
        gate = g_ref[0, sl, :]
        o_ref[0, sl, :] = ((yn * gng_ref[...] + gnb_ref[...]) * _silu(gate)).astype(o_ref.dtype)


def _ret_tables(S):
    d = RET_HEAD_DIM
    inv = ROPE_BASE ** (-jnp.arange(0, d, 2, dtype=F32) / d)
    ang = jnp.arange(S).astype(F32)[:, None] * inv[None, :]
    cos, sin = jnp.cos(ang), jnp.sin(ang)
    cos2 = jnp.concatenate([cos, cos], axis=-1)
    sin2 = jnp.concatenate([-sin, sin], axis=-1)
    C = RET_CHUNK
    log_gamma = jnp.log1p(-jnp.exp2(-5.0 - jnp.arange(RET_HEADS, dtype=F32)))
    idx = jnp.arange(C, dtype=F32)
    diff = idx[:, None] - idx[None, :]
    dec = jnp.where(diff >= 0, jnp.exp(log_gamma[:, None, None] * jnp.maximum(diff, 0.0)), 0.0)
    kd = jnp.exp(log_gamma[:, None] * (C - 1 - idx)[None, :])
    qd = jnp.exp(log_gamma[:, None] * (idx + 1.0)[None, :])
    cd = jnp.exp(log_gamma * C)
    bc = lambda t: jnp.broadcast_to(t[:, :, None], (RET_HEADS, C, d))
    cdb = jnp.broadcast_to(cd[:, None, None], (RET_HEADS, d, d))
    return cos2, sin2, dec, bc(qd), bc(kd), cdb


def _ret_call(pb, pf, gn_g, gn_b):
    B, S, _ = pb.shape
    cos2, sin2, dec, qd, kd, cd = _ret_tables(S)
    col = lambda off: pl.BlockSpec((1, S, RET_HEAD_DIM), lambda b, h: (b, 0, off + h))
    tab = pl.BlockSpec((1, RET_CHUNK, RET_HEAD_DIM), lambda b, h: (h, 0, 0))
    vec = pl.BlockSpec((1, RET_HEAD_DIM), lambda b, h: (0, h))
    return pl.pallas_call(
        _ret_body,
        grid=(B, RET_HEADS),
        in_specs=[col(0), col(RET_HEADS), col(2 * RET_HEADS), col(0),
                  _resident((S, RET_HEAD_DIM)), _resident((S, RET_HEAD_DIM)),
                  tab, tab, tab, tab, vec, vec],
        out_specs=pl.BlockSpec((1, S, RET_HEAD_DIM), lambda b, h: (b, 0, h)),
        out_shape=jax.ShapeDtypeStruct((B, S, RET_W), BF16),
        compiler_params=_params(("arbitrary", "arbitrary"), VMEM_LIMIT_SMALL),
        name="retention",
    )(pb, pb, pb, pf, cos2, sin2, dec, qd, kd, cd, gn_g.reshape(1, RET_W), gn_b.reshape(1, RET_W))


def _compress_body(xk_ref, xv_ref, pk_ref, pv_ref, w1k_ref, w1v_ref, w2k_ref, w2v_ref, kc_ref, vc_ref):
    n_blk = kc_ref.shape[1]
    for x_ref, p_ref, w1_ref, w2_ref, o_ref in ((xk_ref, pk_ref, w1k_ref, w2k_ref, kc_ref),
                                                (xv_ref, pv_ref, w1v_ref, w2v_ref, vc_ref)):
        first = jnp.zeros((n_blk, w1_ref.shape[-1]), F32)
        second = jnp.zeros((n_blk, w1_ref.shape[-1]), F32)
        for t in range(CMP_STRIDE):
            x = x_ref[0, pl.ds(t, n_blk, stride=CMP_STRIDE), :]
            first = first + _dot((x + p_ref[t:t + 1, :]).astype(BF16), w1_ref[t])
            second = second + _dot((x + p_ref[CMP_STRIDE + t:CMP_STRIDE + t + 1, :]).astype(BF16),
                                   w1_ref[CMP_STRIDE + t])
        hid = _silu(first + pltpu.roll(second, n_blk - 1, 0))
        o_ref[0] = _dot(hid.astype(BF16), w2_ref[...]).astype(o_ref.dtype)


def _compress_call(pf, pos_k, pos_v, w1k, w1v, w2k, w2v):
    B, S, _ = pf.shape
    n_blk = S // CMP_STRIDE
    assert CMP_BLOCK == 2 * CMP_STRIDE

    def both_heads(t):
        z = jnp.zeros_like(t)
        return jnp.concatenate([jnp.concatenate([t, z], -1), jnp.concatenate([z, t], -1)], -2).astype(BF16)

    pos = lambda t: jnp.concatenate([t, t], axis=-1)
    w1 = lambda t: both_heads(t.reshape(CMP_BLOCK, NSA_HEAD_DIM, CMP_HIDDEN))
    kc_block = RET_W // NSA_KVW
    xspec = lambda c: pl.BlockSpec((1, S, NSA_KVW), lambda b: (b, 0, c))
    ospec = pl.BlockSpec((1, n_blk, NSA_KVW), lambda b: (b, 0, 0))
    return pl.pallas_call(
        _compress_body,
        grid=(B,),
        in_specs=[xspec(kc_block), xspec(kc_block + 1),
                  _resident((CMP_BLOCK, NSA_KVW)), _resident((CMP_BLOCK, NSA_KVW)),
                  _resident((CMP_BLOCK, NSA_KVW, 2 * CMP_HIDDEN)), _resident((CMP_BLOCK, NSA_KVW, 2 * CMP_HIDDEN)),
                  _resident((2 * CMP_HIDDEN, NSA_KVW)), _resident((2 * CMP_HIDDEN, NSA_KVW))],
        out_specs=[ospec, ospec],
        out_shape=[jax.ShapeDtypeStruct((B, n_blk, NSA_KVW), BF16)] * 2,
        compiler_params=_params(("arbitrary",), VMEM_LIMIT_SMALL),
        name="nsa_compress",
    )(pf, pf, pos(pos_k), pos(pos_v), w1(w1k), w1(w1v), both_heads(w2k), both_heads(w2v))


CMP_Q_ROWS = 512


def _cmp_attn_body(q_ref, kc_ref, vc_ref, cb_ref, ov_ref, o_ref, sel_ref):
    i = pl.program_id(1)
    tq = q_ref.shape[1]
    lane = lax.broadcasted_iota(jnp.int32, (tq, LANES), 1)
    row = lax.broadcasted_iota(jnp.int32, (tq, LANES), 0)
    lo_half = lane < NSA_HEAD_DIM
    kc = kc_ref[0]
    vc = vc_ref[0]
    zero = jnp.zeros_like(kc)
    kc_lo = lax.broadcasted_iota(jnp.int32, kc.shape, 1) < NSA_HEAD_DIM
    kc_g = (jnp.where(kc_lo, kc, zero), jnp.where(kc_lo, zero, kc))
    valid = (lane * CMP_STRIDE + (CMP_BLOCK - 1) <= i * tq + row)[None]
    q4 = jnp.concatenate([q_ref[0, :, r * LANES:(r + 1) * LANES] for r in range(NSA_GROUP)], axis=0)
    q4 = q4 * (NSA_HEAD_DIM ** -0.5)
    psum, outs = [], []
    for g in range(NSA_KV_HEADS):
        s = _dot_nt(q4, kc_g[g]).reshape(NSA_GROUP, tq, LANES) + cb_ref[g * NSA_GROUP:(g + 1) * NSA_GROUP]
        s = jnp.where(valid, s, NEG_INF)
        m = jnp.max(s, axis=-1, keepdims=True)
        e = jnp.where(valid, jnp.exp2(s - m), 0.0)
        den = jnp.maximum(jnp.sum(e, axis=-1, keepdims=True), 1e-30)
        p = e / den
        psum.append(jnp.sum(p, axis=0))
        outs.append(_dot(p.reshape(NSA_GROUP * tq, LANES).astype(BF16), vc))
    for r in range(NSA_GROUP):
        rs = slice(r * tq, (r + 1) * tq)
        o_ref[0, :, r * LANES:(r + 1) * LANES] = jnp.where(lo_half, outs[0][rs], outs[1][rs])

    n_sel = sel_ref.shape[2]
    blk = lax.broadcasted_iota(jnp.int32, (n_sel, tq), 0)
    qblk = (i * tq + lax.broadcasted_iota(jnp.int32, (n_sel, tq), 1)) // SEL_BLOCK
    forced = jnp.where(blk == 0, 1.0, jnp.where(blk == qblk, 1.0, jnp.where(blk == qblk - 1, 1.0, 0.0)))
    for g in range(NSA_KV_HEADS):
        p_hi, p_lo = _split_bf16(psum[g])
        imp = _dot_nt(ov_ref[...], p_hi) + _dot_nt(ov_ref[...], p_lo)
        score = jnp.where(forced > 0.5, SEL_FORCE_SCORE, jnp.where(blk <= qblk, imp, -1.0))
        rank = jnp.zeros((n_sel, tq), F32)
        for other in range(n_sel):
            so = score[other:other + 1, :]
            tie = jnp.where(blk > other, 1.0, 0.0)
            rank = rank + jnp.where(so > score, 1.0, jnp.where(so == score, tie, 0.0))
        sel_ref[0, g] = jnp.where(rank < float(min(SEL_TOP_N, n_sel)), 1.0, 0.0)


def _cmp_attn_call(pb, kc, vc, cmp_bias, overlap_t):
    B, S, _ = pb.shape
    n_sel = S // SEL_BLOCK
    q_block = (3 * RET_W) // NSA_QW
    tq = CMP_Q_ROWS
    return pl.pallas_call(
        _cmp_attn_body,
        grid=(B, S // tq),
        in_specs=[pl.BlockSpec((1, tq, NSA_QW), lambda b, i: (b, i, q_block)),
                  pl.BlockSpec((1,) + kc.shape[1:], lambda b, i: (b, 0, 0)),
                  pl.BlockSpec((1,) + vc.shape[1:], lambda b, i: (b, 0, 0)),
                  pl.BlockSpec((NSA_HEADS, tq, LANES), lambda b, i: (0, i, 0)),
                  _resident((n_sel, LANES))],
        out_specs=[pl.BlockSpec((1, tq, NSA_QW), lambda b, i: (b, i, 0)),
                   pl.BlockSpec((1, NSA_KV_HEADS, n_sel, tq), lambda b, i: (b, 0, 0, i))],
        out_shape=[jax.ShapeDtypeStruct((B, S, NSA_QW), F32),
                   jax.ShapeDtypeStruct((B, NSA_KV_HEADS, n_sel, S), F32)],
        compiler_params=_params(("arbitrary", "arbitrary"), VMEM_LIMIT_SMALL),
        name="nsa_cmp_attn",
    )(pb, kc, vc, cmp_bias, overlap_t)


SLC_CLASS_TILES = 4


def _slc_far_tiles(cls, n_tiles):
    return min(SLC_CLASS_TILES * cls + 2, n_tiles - 2)


def _slc_body(q_ref, ks_ref, vs_ref, sel_ref, tag_ref, near_ref, o_ref, kaug_scr, vaug_scr):
    i = pl.program_id(1)
    n_tiles = ks_ref.shape[1] // TILE
    n_sel = sel_ref.shape[2]
    lane = lax.broadcasted_iota(jnp.int32, (TILE, LANES), 1)
    lo_half = lane < NSA_HEAD_DIM

    @pl.when(i == 0)
    def _():
        for t in range(n_tiles):
            sl = pl.ds(t * TILE, TILE)
            kt = ks_ref[0, sl, :]
            zero = jnp.zeros_like(kt)
            kaug_scr[0, sl, :] = jnp.concatenate([jnp.where(lo_half, kt, zero), tag_ref[sl, :]], axis=1)
            kaug_scr[1, sl, :] = jnp.concatenate([jnp.where(lo_half, zero, kt), tag_ref[sl, :]], axis=1)
            vaug_scr[sl, :] = jnp.concatenate([vs_ref[0, sl, :], tag_ref[sl, :]], axis=1)

    prev = jnp.maximum(i - 1, 0)
    own_sl = pl.ds(pl.multiple_of(i * TILE, TILE), TILE)
    prev_sl = pl.ds(pl.multiple_of(prev * TILE, TILE), TILE)
    no_prev = jnp.where(lax.broadcasted_iota(jnp.int32, (1, 2 * TILE), 1) < TILE,
                        jnp.where(i == 0, NEG_INF, 0.0), 0.0)
    tile_pen = jnp.where(lane - n_sel < i - 1, 0.0, NEG_INF)
    sel_pad = jnp.zeros((LANES - n_sel, TILE), F32)

    def tile_body(n_far):
        wf = n_far * TILE
        v_near = jnp.concatenate([vaug_scr[prev_sl, :], vaug_scr[own_sl, :]], axis=0)
        q4 = jnp.concatenate([q_ref[0, :, r * LANES:(r + 1) * LANES] for r in range(NSA_GROUP)], axis=0)
        q4 = q4 * (NSA_HEAD_DIM ** -0.5)
        q_far, q_near, k_near = [], [], []
        for g in range(NSA_KV_HEADS):
            sel_q = jnp.concatenate([sel_ref[0, g], sel_pad], axis=0).T
            blk_pen = (sel_q - 1.0) * (-NEG_INF)
            pen_near = jnp.where(lane < n_sel, blk_pen, 0.0).astype(BF16)
            pen_far = jnp.where(lane < n_sel, blk_pen,
                                jnp.where(lane < n_sel + n_tiles, tile_pen, 0.0)).astype(BF16)
            q_far.append(jnp.concatenate([q4, jnp.concatenate([pen_far] * NSA_GROUP, axis=0)], axis=1))
            q_near.append(jnp.concatenate([q4, jnp.concatenate([pen_near] * NSA_GROUP, axis=0)], axis=1))
            k_near.append(jnp.concatenate([kaug_scr[g, prev_sl, :], kaug_scr[g, own_sl, :]], axis=0))

        n_split = 2
        rows = NSA_GROUP * TILE // n_split
        chains = [(g, part) for g in range(NSA_KV_HEADS) for part in range(n_split)]

        def scores(g, part):
            rs = slice(part * rows, (part + 1) * rows)
            return (_dot_nt(q_far[g][rs], kaug_scr[g, :wf, :]),
                    _dot_nt(q_near[g][rs], k_near[g]) + (near_ref[g, rs, :] + no_prev))

        def weights(s_far, s_near):
            m = jnp.maximum(jnp.max(s_far, axis=-1, keepdims=True), jnp.max(s_near, axis=-1, keepdims=True))
            return jnp.exp2(s_far - m).astype(BF16), jnp.exp2(s_near - m).astype(BF16)

        def values(p_far, p_near):
            acc = _dot(p_far, vaug_scr[:wf, :]) + _dot(p_near, v_near)
            return acc[:, :NSA_KVW] / acc[:, NSA_KVW + LANES - 1:]

        s, p, o = {}, {}, {}
        for step in range(len(chains) + 2):
            if step < len(chains):
                s[step] = scores(*chains[step])
            if 0 <= step - 2 < len(chains):
                o[step - 2] = values(*p.pop(step - 2))
            if 0 <= step - 1 < len(chains):
                p[step - 1] = weights(*s.pop(step - 1))
        for r in range(NSA_GROUP):
            part, rs = r // (NSA_GROUP // n_split), slice((r % (NSA_GROUP // n_split)) * TILE,
                                                          (r % (NSA_GROUP // n_split) + 1) * TILE)
            o_ref[0, :, r * LANES:(r + 1) * LANES] = jnp.where(lo_half, o[part][rs], o[n_split + part][rs])

    n_classes = -(-n_tiles // SLC_CLASS_TILES)
    for cls in range(n_classes):
        pl.when(i // SLC_CLASS_TILES == cls)(functools.partial(tile_body, _slc_far_tiles(cls, n_tiles)))


def _slc_call(pb, sel_t, key_tags, near):
    B, S, _ = pb.shape
    n_sel = S // SEL_BLOCK
    n_tiles = S // TILE
    assert n_sel + n_tiles < LANES
    q_block = (3 * RET_W) // NSA_QW
    ks_block = (3 * RET_W + NSA_QW) // NSA_KVW
    return pl.pallas_call(
        _slc_body,
        grid=(B, n_tiles),
        in_specs=[pl.BlockSpec((1, TILE, NSA_QW), lambda b, i: (b, i, q_block)),
                  pl.BlockSpec((1, S, NSA_KVW), lambda b, i: (b, 0, ks_block)),
                  pl.BlockSpec((1, S, NSA_KVW), lambda b, i: (b, 0, ks_block + 1)),
                  pl.BlockSpec((1, NSA_KV_HEADS, n_sel, TILE), lambda b, i: (b, 0, 0, i)),
                  _resident(key_tags.shape), _resident(near.shape)],
        out_specs=pl.BlockSpec((1, TILE, NSA_QW), lambda b, i: (b, i, 0)),
        out_shape=jax.ShapeDtypeStruct((B, S, NSA_QW), F32),
        scratch_shapes=[pltpu.VMEM((NSA_KV_HEADS, S, NSA_KVW + LANES), BF16),
                        pltpu.VMEM((S, NSA_KVW + LANES), BF16)],
        compiler_params=_params(("arbitrary", "arbitrary"), VMEM_LIMIT_LARGE),
        name="nsa_selected",
    )(pb, pb, pb, sel_t, key_tags, near)


WIN_PREV_TILES = (WIN_SIZE - 1 + TILE - 1) // TILE
WIN_Q_TILES = 2


def _win_body(q_ref, k_ref, v_ref, wb_ref, gate_ref, eg_ref, ocmp_ref, oslc_ref, y_ref):
    i = pl.program_id(1)
    tq = q_ref.shape[1]
    n_span = WIN_PREV_TILES + WIN_Q_TILES
    first = i * WIN_Q_TILES - WIN_PREV_TILES
    lo_half = lax.broadcasted_iota(jnp.int32, (tq, LANES), 1) < NSA_HEAD_DIM

    def attend(early):
        if early:
            k_tiles, v_tiles, negs = [], [], []
            for t in range(n_span):
                sl = pl.ds(pl.multiple_of(jnp.maximum(first + t, 0) * TILE, TILE), TILE)
                k_tiles.append(k_ref[0, sl, :])
                v_tiles.append(v_ref[0, sl, :])
                negs.append(jnp.full((tq, TILE), jnp.where(first + t < 0, NEG_INF, 0.0), F32))
            k_all = jnp.concatenate(k_tiles, axis=0)
            v_all = jnp.concatenate(v_tiles, axis=0)
            missing = jnp.concatenate(negs, axis=1)[None]
        else:
            sl = pl.ds(pl.multiple_of(first * TILE, TILE), n_span * TILE)
            k_all = k_ref[0, sl, :]
            v_all = v_ref[0, sl, :]
        zero = jnp.zeros_like(k_all)
        k_lo = lax.broadcasted_iota(jnp.int32, k_all.shape, 1) < NSA_HEAD_DIM
        k_g = (jnp.where(k_lo, k_all, zero), jnp.where(k_lo, zero, k_all))
        v_ones = jnp.concatenate([v_all, jnp.ones_like(v_all)], axis=1)

        g_hi, g_lo = _split_bf16(jax.nn.sigmoid(gate_ref[0]))
        gates = _dot(g_hi, eg_ref[...]) + _dot(g_lo, eg_ref[...])
        q4 = jnp.concatenate([q_ref[0, :, r * LANES:(r + 1) * LANES] for r in range(NSA_GROUP)], axis=0)
        q4 = q4 * (NSA_HEAD_DIM ** -0.5)
        n_split = 2
        heads = NSA_GROUP // n_split
        chains = [(g, part) for g in range(NSA_KV_HEADS) for part in range(n_split)]

        def scores(g, part):
            s = _dot_nt(q4[part * heads * tq:(part + 1) * heads * tq], k_g[g]).reshape(heads, tq, n_span * TILE)
            s = s + wb_ref[g * NSA_GROUP + part * heads:g * NSA_GROUP + (part + 1) * heads]
            return s + missing if early else s

        def weights(s):
            m = jnp.max(s, axis=-1, keepdims=True)
            return jnp.exp2(s - m).astype(BF16).reshape(heads * tq, n_span * TILE)

        def values(p):
            acc = _dot(p, v_ones)
            return acc[:, :NSA_KVW] / acc[:, NSA_KVW:NSA_KVW + 1]

        s, p, o = {}, {}, {}
        for step in range(len(chains) + 2):
            if step < len(chains):
                s[step] = scores(*chains[step])
            if 0 <= step - 2 < len(chains):
                o[step - 2] = values(p.pop(step - 2))
            if 0 <= step - 1 < len(chains):
                p[step - 1] = weights(s.pop(step - 1))
        for r in range(NSA_GROUP):
            cols = slice(r * LANES, (r + 1) * LANES)
            part, rs = r // heads, slice((r % heads) * tq, (r % heads + 1) * tq)
            o_win = jnp.where(lo_half, o[part][rs], o[n_split + part][rs])
            y = (gates[:, r * LANES:(r + 1) * LANES] * ocmp_ref[0, :, cols]
                 + gates[:, NSA_QW + r * LANES:NSA_QW + (r + 1) * LANES] * oslc_ref[0, :, cols]
                 + gates[:, 2 * NSA_QW + r * LANES:2 * NSA_QW + (r + 1) * LANES] * o_win)
            y_ref[0, :, cols] = y.astype(y_ref.dtype)

    pl.when(first >= 0)(functools.partial(attend, False))
    pl.when(first < 0)(functools.partial(attend, True))


def _win_call(pb, pf, win_bias, gate_expand, o_cmp, o_slc):
    B, S, _ = pb.shape
    q_block = (3 * RET_W) // NSA_QW
    kw_block = (3 * RET_W + NSA_QW) // NSA_KVW + 2
    gate_block = (RET_W + 2 * NSA_KVW) // LANES
    tq = WIN_Q_TILES * TILE
    tile_spec = pl.BlockSpec((1, tq, NSA_QW), lambda b, i: (b, i, 0))
    return pl.pallas_call(
        _win_body,
        grid=(B, S // tq),
        in_specs=[pl.BlockSpec((1, tq, NSA_QW), lambda b, i: (b, i, q_block)),
                  pl.BlockSpec((1, S, NSA_KVW), lambda b, i: (b, 0, kw_block)),
                  pl.BlockSpec((1, S, NSA_KVW), lambda b, i: (b, 0, kw_block + 1)),
                  _resident(win_bias.shape),
                  pl.BlockSpec((1, tq, LANES), lambda b, i: (b, i, gate_block)),
                  _resident((LANES, 3 * NSA_QW)),
                  tile_spec, tile_spec],
        out_specs=tile_spec,
        out_shape=jax.ShapeDtypeStruct((B, S, NSA_QW), BF16),
        compiler_params=_params(("arbitrary", "arbitrary"), VMEM_LIMIT_LARGE),
        name="nsa_window_combine",
    )(pb, pb, pb, win_bias, pf, gate_expand, o_cmp, o_slc)


def _post_body(h_ref, ma_ref, mb_ref, mod_ref, woa_ref, wob_ref, lng_ref, lnb_ref,
               wg_ref, wu_ref, wd_ref, o_ref):
    y = _dot(ma_ref[0], woa_ref[...]) + _dot(mb_ref[0], wob_ref[...])
    h1 = _layer_norm(DEEPNORM_ALPHA * h_ref[0] + mod_ref[0, 2:3, :] * y, lng_ref[0:1, :], lnb_ref[0:1, :])
    u = (h1 * (1.0 + mod_ref[0, 4:5, :]) + mod_ref[0, 3:4, :]).astype(BF16)
    acc = jnp.zeros(h1.shape, F32)
    for c0, cn in FFN_CHUNKS:
        gate = _dot(u, wg_ref[:, c0:c0 + cn])
        up = _dot(u, wu_ref[:, c0:c0 + cn])
        acc = acc + _dot((_silu(gate) * up).astype(BF16), wd_ref[c0:c0 + cn, :])
    o_ref[0] = _layer_norm(DEEPNORM_ALPHA * h1 + mod_ref[0, 5:6, :] * acc, lng_ref[1:2, :], lnb_ref[1:2, :])


def _post_call(h, mix_a, mix_b, col_a, col_b, mod, wo_a, wo_b, ln_g, ln_b, w_gate, w_up, w_down, tm, name):
    B, S, _ = h.shape
    half = D_MODEL // 2
    tok = pl.BlockSpec((1, tm, D_MODEL), lambda b, t: (b, t, 0))
    return pl.pallas_call(
        _post_body,
        grid=(B, S // tm),
        in_specs=[tok,
                  pl.BlockSpec((1, tm, half), lambda b, t: (b, t, col_a)),
                  pl.BlockSpec((1, tm, half), lambda b, t: (b, t, col_b)),
                  pl.BlockSpec((1, 6, D_MODEL), lambda b, t: (b, 0, 0)),
                  _resident((half, D_MODEL)), _resident((half, D_MODEL)),
                  _resident((2, D_MODEL)), _resident((2, D_MODEL)),
                  _resident((D_MODEL, D_FF)), _resident((D_MODEL, D_FF)), _resident((D_FF, D_MODEL))],
        out_specs=tok,
        out_shape=jax.ShapeDtypeStruct((B, S, D_MODEL), F32),
        compiler_params=_params(("arbitrary", "arbitrary"), VMEM_LIMIT_LARGE),
        name=name,
    )(h, mix_a, mix_b, mod, wo_a, wo_b, ln_g, ln_b, w_gate, w_up, w_down)


DIL_STEP_TILES = 4


def _dil_body(q_ref, k_ref, kp_ref, v_ref, vp_ref, b_ref, o_ref, lse_ref, *, tiles_per_seg):
    t = pl.program_id(1)
    res_tiles = min(tiles_per_seg, DIL_STEP_TILES)
    with_prev = tiles_per_seg > 1
    if tiles_per_seg > DIL_STEP_TILES:
        first_prev = jnp.where((t * DIL_STEP_TILES) % tiles_per_seg == 0, NEG_INF, 0.0)
    else:
        first_prev = NEG_INF

    def span(ref, before_ref, j, cols):
        if not with_prev:
            return ref[0, j * TILE:(j + 1) * TILE, cols]
        if j > 0:
            return ref[0, (j - 1) * TILE:(j + 1) * TILE, cols]
        return jnp.concatenate([before_ref[0, :, cols], ref[0, :TILE, cols]], axis=0)

    def tile_bias(h, j):
        if not with_prev:
            return b_ref[h, TILE:, :]
        if j % res_tiles != 0:
            return b_ref[h]
        gone = first_prev if j == 0 else NEG_INF
        return jnp.concatenate([b_ref[h, :TILE, :] + gone, b_ref[h, TILE:, :]], axis=0)

    head_cols = lambda h: slice(h * DIL_HEAD_DIM, (h + 1) * DIL_HEAD_DIM)
    pad = jnp.zeros((LANES - DIL_HEADS, TILE), F32)

    def scores(j):
        return jnp.concatenate(
            [_dot_nt(span(k_ref, kp_ref, j, head_cols(h)), q_ref[0, j * TILE:(j + 1) * TILE, head_cols(h)])
             + tile_bias(h, j) for h in range(DIL_HEADS)], axis=1)

    def weights(j, s):
        m = jnp.max(s, axis=0, keepdims=True)
        e = jnp.exp2(s - m)
        den = jnp.sum(e, axis=0, keepdims=True)
        lse = (m + jnp.log2(den)) * math.log(2.0)
        by_head = [lse[:, h * TILE:(h + 1) * TILE] for h in range(DIL_HEADS)]
        lse_ref[0, j * TILE:(j + 1) * TILE, :] = jnp.concatenate(by_head + [pad], axis=0).T
        return (e * (1.0 / den)).astype(BF16)

    def values(j, p):
        for h in range(DIL_HEADS):
            o = _dot_tn(p[:, h * TILE:(h + 1) * TILE], span(v_ref, vp_ref, j, head_cols(h)))
            o_ref[0, j * TILE:(j + 1) * TILE, head_cols(h)] = o.astype(o_ref.dtype)

    s, p = {}, {}
    for step in range(DIL_STEP_TILES + 2):
        if step < DIL_STEP_TILES:
            s[step] = scores(step)
        if 0 <= step - 2 < DIL_STEP_TILES:
            values(step - 2, p.pop(step - 2))
        if 0 <= step - 1 < DIL_STEP_TILES:
            p[step - 1] = weights(step - 1, s.pop(step - 1))


def _dil_call(proj, bias, dilation, name):
    B, S, _ = proj.shape
    step = DIL_STEP_TILES * TILE
    own = lambda c: pl.BlockSpec((1, step, DIL_WIDTH), lambda b, i: (b, i, c))
    before = lambda c: pl.BlockSpec((1, TILE, DIL_WIDTH),
                                    lambda b, i: (b, jnp.maximum(i * DIL_STEP_TILES - 1, 0), c))
    tile = lambda width: pl.BlockSpec((1, step, width), lambda b, i: (b, i, 0))
    return pl.pallas_call(
        functools.partial(_dil_body, tiles_per_seg=(S // dilation) // TILE),
        grid=(B, S // step),
        in_specs=[own(0), own(1), before(1), own(2), before(2), _resident(bias.shape)],
        out_specs=[tile(DIL_WIDTH), tile(LANES)],
        out_shape=[jax.ShapeDtypeStruct((B, S, DIL_WIDTH), BF16), jax.ShapeDtypeStruct((B, S, LANES), F32)],
        compiler_params=_params(("arbitrary", "arbitrary"), VMEM_LIMIT_LARGE),
        name=name,
    )(proj, proj, proj, proj, proj, bias)


def _dil_mix_body(o0_ref, o1_ref, o2_ref, l0_ref, l1_ref, l2_ref, y_ref, o_scr, l_scr):
    o_refs = (o0_ref, o1_ref, o2_ref)
    l_refs = (l0_ref, l1_ref, l2_ref)
    tm = y_ref.shape[1]
    for gi, (o_ref, l_ref) in enumerate(zip(o_refs, l_refs)):
        dilation = o_ref.shape[1]
        for r in range(dilation):
            rows = pl.ds(r, tm // dilation, stride=dilation)
            l_scr[gi, rows, :] = l_ref[0, r]
            for h in range(DIL_HEADS):
                o_scr[gi, h, rows, :] = o_ref[0, r, :, h * DIL_HEAD_DIM:(h + 1) * DIL_HEAD_DIM].astype(F32)
    for h in range(DIL_HEADS):
        lses = [jnp.broadcast_to(l_scr[gi, :, h:h + 1], (tm, DIL_HEAD_DIM)) for gi in range(len(o_refs))]
        m = jnp.maximum(jnp.maximum(lses[0], lses[1]), lses[2])
        ws = [jnp.exp(l - m) for l in lses]
        den = ws[0] + ws[1] + ws[2]
        y = sum((w / den) * o_scr[gi, h] for gi, w in enumerate(ws))
        y_ref[0, :, h * DIL_HEAD_DIM:(h + 1) * DIL_HEAD_DIM] = y.astype(y_ref.dtype)


def _dil_mix_call(outs, lses, tm):
    B, S, _ = outs[0].shape
    dilations = [d for _, d in DIL_PATTERNS]
    by_residue = lambda t, d: t.reshape(B, d, S // d, t.shape[-1])
    spec = lambda d, width: pl.BlockSpec((1, d, tm // d, width), lambda b, t: (b, 0, t, 0))
    return pl.pallas_call(
        _dil_mix_body,
        grid=(B, S // tm),
        in_specs=[spec(d, DIL_WIDTH) for d in dilations] + [spec(d, LANES) for d in dilations],
        out_specs=pl.BlockSpec((1, tm, DIL_WIDTH), lambda b, t: (b, t, 0)),
        out_shape=jax.ShapeDtypeStruct((B, S, DIL_WIDTH), BF16),
        scratch_shapes=[pltpu.VMEM((len(dilations), DIL_HEADS, tm, DIL_HEAD_DIM), F32),
                        pltpu.VMEM((len(dilations), tm, LANES), F32)],
        compiler_params=_params(("arbitrary", "arbitrary"), VMEM_LIMIT_SMALL),
        name="dilated_mix",
    )(*[by_residue(o, d) for o, d in zip(outs, dilations)], *[by_residue(l, d) for l, d in zip(lses, dilations)])


def _post_mix_body(h_ref, o0_ref, o1_ref, o2_ref, l0_ref, l1_ref, l2_ref, mod_ref, wo_ref, lng_ref, lnb_ref,
                   wg_ref, wu_ref, wd_ref, out_ref, mix_a, mix_b, o_scr, l_scr):
    n = pl.program_id(0)
    o_refs = (o0_ref, o1_ref, o2_ref)
    l_refs = (l0_ref, l1_ref, l2_ref)
    tm = out_ref.shape[1]

    @pl.when(n == 0)
    def _():
        mix_a[...] = jnp.zeros_like(mix_a)
        mix_b[...] = jnp.zeros_like(mix_b)

    def unpermute():
        for gi, (o_ref, l_ref) in enumerate(zip(o_refs, l_refs)):
            dilation = o_ref.shape[1]
            for r in range(dilation):
                rows = pl.ds(r, tm // dilation, stride=dilation)
                l_scr[gi, rows, :] = l_ref[0, r]
                for h in range(DIL_HEADS):
                    o_scr[gi, h, rows, :] = o_ref[0, r, :, h * DIL_HEAD_DIM:(h + 1) * DIL_HEAD_DIM].astype(F32)

    def mix_head(h, mix_w):
        lses = [jnp.broadcast_to(l_scr[gi, :, h:h + 1], (tm, DIL_HEAD_DIM)) for gi in range(len(o_refs))]
        m = jnp.maximum(jnp.maximum(lses[0], lses[1]), lses[2])
        ws = [jnp.exp(l - m) for l in lses]
        den = ws[0] + ws[1] + ws[2]
        y = sum((w / den) * o_scr[gi, h] for gi, w in enumerate(ws))
        mix_w[:, h * DIL_HEAD_DIM:(h + 1) * DIL_HEAD_DIM] = y.astype(mix_w.dtype)

    def step(mix_w, mix_r):
        y = _dot(mix_r[...], wo_ref[...])
        unpermute()
        h1 = _layer_norm(DEEPNORM_ALPHA * h_ref[0] + mod_ref[0, 2:3, :] * y, lng_ref[0:1, :], lnb_ref[0:1, :])
        u = (h1 * (1.0 + mod_ref[0, 4:5, :]) + mod_ref[0, 3:4, :]).astype(BF16)
        acc = jnp.zeros(h1.shape, F32)
        heads_per_chunk = DIL_HEADS // len(FFN_CHUNKS)
        for ci, (c0, cn) in enumerate(FFN_CHUNKS):
            gate = _dot(u, wg_ref[:, c0:c0 + cn])
            up = _dot(u, wu_ref[:, c0:c0 + cn])
            for h in range(ci * heads_per_chunk, (ci + 1) * heads_per_chunk):
                mix_head(h, mix_w)
            acc = acc + _dot((_silu(gate) * up).astype(BF16), wd_ref[c0:c0 + cn, :])
        out_ref[0] = _layer_norm(DEEPNORM_ALPHA * h1 + mod_ref[0, 5:6, :] * acc, lng_ref[1:2, :], lnb_ref[1:2, :])

    pl.when(n % 2 == 0)(functools.partial(step, mix_a, mix_b))
    pl.when(n % 2 == 1)(functools.partial(step, mix_b, mix_a))


def _post_mix_call(h, outs, lses, mod, wo, ln_g, ln_b, w_gate, w_up, w_down, tm, name):
    B, S, _ = h.shape
    steps = S // tm
    last = B * steps - 1
    dilations = [d for _, d in DIL_PATTERNS]
    by_residue = lambda t, d: t.reshape(B, d, S // d, t.shape[-1])
    cur = lambda n: jnp.minimum(n, last)
    done = lambda n: jnp.maximum(n - 1, 0)
    src = lambda d, width: pl.BlockSpec((1, d, tm // d, width), lambda n: (cur(n) // steps, 0, cur(n) % steps, 0))
    tok = pl.BlockSpec((1, tm, D_MODEL), lambda n: (done(n) // steps, done(n) % steps, 0))
    return pl.pallas_call(
        _post_mix_body,
        grid=(B * steps + 1,),
        in_specs=[tok] + [src(d, DIL_WIDTH) for d in dilations] + [src(d, LANES) for d in dilations]
                 + [pl.BlockSpec((1, 6, D_MODEL), lambda n: (done(n) // steps, 0, 0)),
                    _resident((D_MODEL, D_MODEL)), _resident((2, D_MODEL)), _resident((2, D_MODEL)),
                    _resident((D_MODEL, D_FF)), _resident((D_MODEL, D_FF)), _resident((D_FF, D_MODEL))],
        out_specs=tok,
        out_shape=jax.ShapeDtypeStruct((B, S, D_MODEL), F32),
        scratch_shapes=[pltpu.VMEM((tm, D_MODEL), BF16), pltpu.VMEM((tm, D_MODEL), BF16),
                        pltpu.VMEM((len(dilations), DIL_HEADS, tm, DIL_HEAD_DIM), F32),
                        pltpu.VMEM((len(dilations), tm, LANES), F32)],
        compiler_params=_params(("arbitrary",), VMEM_LIMIT_LARGE),
        name=name,
    )(h, *[by_residue(o, d) for o, d in zip(outs, dilations)], *[by_residue(l, d) for l, d in zip(lses, dilations)],
      mod, wo, ln_g, ln_b, w_gate, w_up, w_down)


def _t5_bucket_np(dist):
    n = np.maximum(dist, 0)
    max_exact = REL_BUCKETS // 2
    nf = np.maximum(n, 1).astype(np.float64)
    val = np.log(nf / max_exact) / math.log(REL_MAX_DIST / max_exact) * (REL_BUCKETS - max_exact)
    frac = np.abs(val - np.round(val))
    on_edge = (frac < 1e-9) & (n > max_exact) & (n < REL_MAX_DIST)
    assert not on_edge.any()
    large = np.minimum(max_exact + np.floor(val + 1e-9).astype(np.int64), REL_BUCKETS - 1)
    return np.where(n < max_exact, n, large).astype(np.int32)


def _shift_table(rel_bias, rows, cols, step, dist_fn, valid_fn):
    u = np.concatenate([np.arange(cols), np.arange(-(rows - 1) * step, 0)])
    period = u.size
    vals = jnp.take(rel_bias, jnp.asarray(_t5_bucket_np(dist_fn(u))), axis=0).T
    vals = jnp.where(jnp.asarray(valid_fn(u))[None], vals, NEG_INF)
    t = jnp.tile(vals, (1, rows))[:, :rows * (period - step)].reshape(vals.shape[0], rows, period - step)
    return t[:, :, :cols]


def _nsa_tables(rel_bias, S):
    rel_bias = rel_bias * LOG2_E
    always = lambda u: np.ones(u.shape, bool)
    win_dist = lambda u: WIN_PREV_TILES * TILE - u
    win_bias = _shift_table(rel_bias, WIN_Q_TILES * TILE, (WIN_PREV_TILES + WIN_Q_TILES) * TILE, 1, win_dist,
                            lambda u: (win_dist(u) >= 0) & (win_dist(u) <= WIN_SIZE - 1))
    d0 = _shift_table(rel_bias, TILE, TILE, 1, lambda u: -u, lambda u: u <= 0)
    d1 = _shift_table(rel_bias, TILE, TILE, 1, lambda u: TILE - u, always)
    far_bucket = _t5_bucket_np(np.arange(TILE + 1, S + TILE))
    assert (far_bucket == far_bucket[0]).all()
    far = rel_bias[int(far_bucket[0])][:, None, None]
    near = jnp.concatenate([d1 - far, d0 - far], axis=2)
    near = near.reshape(NSA_KV_HEADS, NSA_GROUP * TILE, 2 * TILE)
    cmp_bias = _shift_table(rel_bias, LANES, S, CMP_STRIDE, lambda u: u - (CMP_BLOCK - 1), always)
    cmp_bias = cmp_bias.transpose(0, 2, 1)
    n_cmp = (S - CMP_BLOCK) // CMP_STRIDE + 1
    n_sel = S // SEL_BLOCK
    cs = (np.arange(n_cmp) * CMP_STRIDE)[:, None]
    ss = (np.arange(n_sel) * SEL_BLOCK)[None, :]
    ov = np.clip(np.minimum(cs + CMP_BLOCK, ss + SEL_BLOCK) - np.maximum(cs, ss), 0, None) / CMP_BLOCK
    ov_t = np.zeros((n_sel, LANES), np.float32)
    ov_t[:, :n_cmp] = ov.T
    key = np.arange(S)[:, None]
    lane = np.arange(LANES)[None, :]
    tags = ((lane == key // SEL_BLOCK) | (lane == n_sel + key // TILE) | (lane == LANES - 1)).astype(np.float32)
    eg = np.zeros((LANES, 3 * NSA_QW), np.float32)
    for g in range(NSA_KV_HEADS):
        for r in range(NSA_GROUP):
            for j in range(3):
                base = j * NSA_QW + r * LANES + g * NSA_HEAD_DIM
                eg[g * NSA_GROUP * 3 + r * 3 + j, base:base + NSA_HEAD_DIM] = 1.0
    return win_bias, near, cmp_bias, jnp.asarray(ov_t, BF16), jnp.asarray(tags, BF16), jnp.asarray(eg, BF16)


def _dil_bias(rel_bias, dilation, max_dist):
    dist = lambda u: TILE + u
    return _shift_table(rel_bias * LOG2_E, 2 * TILE, TILE, 1, lambda u: dist(u) * dilation,
                        lambda u: (dist(u) >= 0) & (dist(u) <= max_dist))


def _nsa_head_perm():
    perm = np.zeros(NSA_QW, np.int64)
    for r in range(NSA_GROUP):
        for g in range(NSA_KV_HEADS):
            new = r * LANES + g * NSA_HEAD_DIM
            old = (g * NSA_GROUP + r) * NSA_HEAD_DIM
            perm[new:new + NSA_HEAD_DIM] = np.arange(old, old + NSA_HEAD_DIM)
    return perm


def _layer0_mixer(h, mod, ab_w_in, rel_bias, gn_g, gn_b, pos_k, pos_v, w1k, w2k, w1v, w2v):
    B, S, _ = h.shape
    o = np.cumsum((0, RET_W, RET_W, RET_W, RET_W, NSA_QW) + (NSA_KVW,) * 6 + (3 * NSA_HEADS,))
    seg = lambda a: ab_w_in[:, o[a]:o[a + 1]]
    gate_w = jnp.pad(seg(11), ((0, 0), (0, LANES - 3 * NSA_HEADS)))
    q_nsa = seg(4)[:, _nsa_head_perm()] * LOG2_E
    w = jnp.concatenate([seg(0), seg(1), seg(2), q_nsa, seg(7), seg(8), seg(9), seg(10),
                         seg(3), seg(5), seg(6), gate_w], axis=1).astype(BF16)
    pb, pf = _pre_call(h, mod, w, ((PRE0_BF16_COLS, BF16), (PRE0_F32_COLS, F32)), 512, "pre0")

    y_ret = _ret_call(pb, pf, gn_g, gn_b)

    win_bias, near, cmp_bias, ov_t, key_tags, eg = _nsa_tables(rel_bias, S)
    kc, vc = _compress_call(pf, pos_k, pos_v, w1k, w1v, w2k, w2v)
    o_cmp, sel_t = _cmp_attn_call(pb, kc, vc, cmp_bias, ov_t)
    o_slc = _slc_call(pb, sel_t, key_tags, near)
    y_nsa = _win_call(pb, pf, win_bias, eg, o_cmp, o_slc)
    return y_ret, y_nsa


def _layer1_mixer(h, mod, dil_w_in, rel_bias):
    B, S, _ = h.shape
    outs, lses = [], []
    for gi, (window, dilation) in enumerate(DIL_PATTERNS):
        w = dil_w_in[:, gi * 3 * DIL_WIDTH:(gi + 1) * 3 * DIL_WIDTH]
        w = jnp.concatenate([w[:, :DIL_WIDTH] * (DIL_HEAD_DIM ** -0.5 * LOG2_E), w[:, DIL_WIDTH:]], axis=1).astype(BF16)
        proj, = _pre_call(h, mod, w, ((3 * DIL_WIDTH, BF16),), 512, f"pre1_{gi}", dilation)
        bias = _dil_bias(rel_bias, dilation, window // dilation)
        o, lse = _dil_call(proj, bias, dilation, f"dilated_{gi}")
        outs.append(o)
        lses.append(lse)
    return outs, lses


def kernel(x, c, rel_bias, ada_w, ada_b, ln_g, ln_b, ab_w_in, ab_w_out, ret_gn_g, ret_gn_b, cmp_pos_k, cmp_pos_v, cmp_k_w1, cmp_k_w2, cmp_v_w1, cmp_v_w2, dil_w_in, dil_w_out, ffn_w_gate, ffn_w_up, ffn_w_down):
    B = x.shape[0]
    mod = _ada_call(c, ada_w, ada_b).reshape(DEPTH, B, 6, D_MODEL)
    h = x
    for layer in range(DEPTH):
        i = layer // 2
        if layer % 2 == 0:
            mix_a, mix_b = _layer0_mixer(h, mod[layer], ab_w_in[i], rel_bias, ret_gn_g[i], ret_gn_b[i],
                                         cmp_pos_k[i], cmp_pos_v[i], cmp_k_w1[i], cmp_k_w2[i],
                                         cmp_v_w1[i], cmp_v_w2[i])
            wo_a = ab_w_out[i, :RET_W]
            wo_b = ab_w_out[i, RET_W:][_nsa_head_perm()]
            h = _post_call(h, mix_a, mix_b, 0, 0, mod[layer], wo_a.astype(BF16), wo_b.astype(BF16),
                           ln_g[layer], ln_b[layer], ffn_w_gate[layer].astype(BF16), ffn_w_up[layer].astype(BF16),
                           ffn_w_down[layer].astype(BF16), 512, f"post{layer}")
        else:
            outs, lses = _layer1_mixer(h, mod[layer], dil_w_in[i], rel_bias)
            h = _post_mix_call(h, outs, lses, mod[layer], dil_w_out[i].astype(BF16), ln_g[layer], ln_b[layer],
                               ffn_w_gate[layer].astype(BF16), ffn_w_up[layer].astype(BF16),
                               ffn_w_down[layer].astype(BF16), 512, f"post{layer}")
    return h
```

```python
import functools
import math

import numpy as np
import jax
import jax.numpy as jnp
from jax import lax
from jax.experimental import pallas as pl
from jax.experimental.pallas import tpu as pltpu

F32 = jnp.float32
BF16 = jnp.bfloat16

D_MODEL = 1024
DEPTH = 2
DEEPNORM_ALPHA = (2 * DEPTH) ** 0.25
LN_EPS = 1e-5
NEG_INF = -1e30
LOG2_E = math.log2(math.e)

RET_HEADS = 4
RET_HEAD_DIM = 128
RET_CHUNK = 128
ROPE_BASE = 10000.0
RET_W = RET_HEADS * RET_HEAD_DIM

NSA_HEADS = 8
NSA_KV_HEADS = 2
NSA_GROUP = 4
NSA_HEAD_DIM = 64
CMP_BLOCK = 32
CMP_STRIDE = 16
CMP_HIDDEN = 256
SEL_BLOCK = 64
SEL_TOP_N = 16
SEL_FORCE_SCORE = 1e4
WIN_SIZE = 512
NSA_QW = NSA_HEADS * NSA_HEAD_DIM
NSA_KVW = NSA_KV_HEADS * NSA_HEAD_DIM

DIL_PATTERNS = ((128, 1), (512, 4), (2048, 16))
DIL_HEADS = 8
DIL_HEAD_DIM = 128
DIL_WIDTH = DIL_HEADS * DIL_HEAD_DIM

REL_BUCKETS = 32
REL_MAX_DIST = 128
D_FF = 2816

LANES = 128
TILE = 128
VMEM_LIMIT_SMALL = 32 * 1024 * 1024
VMEM_LIMIT_LARGE = 56 * 1024 * 1024

PRE0_BF16_COLS = 3 * RET_W + NSA_QW + 4 * NSA_KVW
PRE0_F32_COLS = RET_W + 2 * NSA_KVW + LANES
FFN_CHUNKS = ((0, 768), (768, 768), (1536, 768), (2304, 512))


def _dot(a, b):
    return jnp.dot(a, b, preferred_element_type=F32)


def _dot_nt(a, b):
    return lax.dot_general(a, b, (((1,), (1,)), ((), ())), preferred_element_type=F32)


def _dot_tn(a, b):
    return lax.dot_general(a, b, (((0,), (0,)), ((), ())), preferred_element_type=F32)


def _split_bf16(x):
    hi = x.astype(BF16)
    lo = (x - hi.astype(F32)).astype(BF16)
    return hi, lo


def _silu(x):
    return x * jax.nn.sigmoid(x)


def _layer_norm(x, g, b):
    mu = jnp.mean(x, axis=-1, keepdims=True)
    xc = x - mu
    var = jnp.mean(xc * xc, axis=-1, keepdims=True)
    return xc * lax.rsqrt(var + LN_EPS) * g + b


def _resident(shape):
    return pl.BlockSpec(shape, lambda *_: (0,) * len(shape), pipeline_mode=pl.Buffered(1))


def _params(sem, vmem):
    return pltpu.CompilerParams(dimension_semantics=sem, vmem_limit_bytes=vmem)


def _ada_body(c_ref, w_ref, b_ref, o_ref):
    a_hi, a_lo = _split_bf16(_silu(c_ref[...]))
    w_hi, w_lo = _split_bf16(w_ref[0])
    o_ref[0] = _dot(a_hi, w_hi) + _dot(a_lo, w_hi) + _dot(a_hi, w_lo) + b_ref[0]


def _ada_call(c, ada_w, ada_b):
    B = c.shape[0]
    n_out = ada_w.shape[-1]
    tn = n_out // 4
    return pl.pallas_call(
        _ada_body,
        grid=(DEPTH, n_out // tn),
        in_specs=[pl.BlockSpec((B, D_MODEL), lambda l, n: (0, 0)),
                  pl.BlockSpec((1, D_MODEL, tn), lambda l, n: (l, 0, n)),
                  pl.BlockSpec((1, 1, tn), lambda l, n: (l, 0, n))],
        out_specs=pl.BlockSpec((1, B, tn), lambda l, n: (l, 0, n)),
        out_shape=jax.ShapeDtypeStruct((DEPTH, B, n_out), F32),
        compiler_params=_params(("arbitrary", "arbitrary"), VMEM_LIMIT_LARGE),
        name="ada_mod",
    )(c, ada_w, ada_b.reshape(DEPTH, 1, n_out))


def _pre_body(h_ref, mod_ref, w_ref, *refs, dilation):
    if dilation == 1:
        o_refs, h = refs, h_ref[0]
    else:
        o_refs, h_scr = refs[:-1], refs[-1]
        tm = h_ref.shape[1]
        for c in range(h_scr.shape[0]):
            h_scr[c] = h_ref[0, :, c * LANES:(c + 1) * LANES]
        h = jnp.concatenate(
            [jnp.concatenate([h_scr[c, pl.ds(r, tm // dilation, stride=dilation), :]
                              for c in range(h_scr.shape[0])], axis=1) for r in range(dilation)], axis=0)
    u = (h * (1.0 + mod_ref[0, 1:2, :]) + mod_ref[0, 0:1, :]).astype(BF16)
    off = 0
    for o_ref in o_refs:
        n = o_ref.shape[-1]
        o_ref[0] = _dot(u, w_ref[:, off:off + n]).astype(o_ref.dtype).reshape(o_ref.shape[1:])
        off += n


def _pre_call(h, mod, w, out_cols_dtypes, tm, name, dilation=1):
    B, S, _ = h.shape
    n_total = w.shape[1]
    assert sum(n for n, _ in out_cols_dtypes) == n_total
    if dilation == 1:
        out_specs = [pl.BlockSpec((1, tm, n), lambda b, t: (b, t, 0)) for n, _ in out_cols_dtypes]
        out_shape = [jax.ShapeDtypeStruct((B, S, n), dt) for n, dt in out_cols_dtypes]
        scratch = []
    else:
        out_specs = [pl.BlockSpec((1, dilation, tm // dilation, n), lambda b, t: (b, 0, t, 0))
                     for n, _ in out_cols_dtypes]
        out_shape = [jax.ShapeDtypeStruct((B, dilation, S // dilation, n), dt) for n, dt in out_cols_dtypes]
        scratch = [pltpu.VMEM((D_MODEL // LANES, tm, LANES), F32)]
    outs = pl.pallas_call(
        functools.partial(_pre_body, dilation=dilation),
        grid=(B, S // tm),
        in_specs=[pl.BlockSpec((1, tm, D_MODEL), lambda b, t: (b, t, 0)),
                  pl.BlockSpec((1, 6, D_MODEL), lambda b, t: (b, 0, 0)),
                  _resident((D_MODEL, n_total))],
        out_specs=out_specs,
        out_shape=out_shape,
        scratch_shapes=scratch,
        compiler_params=_params(("arbitrary", "arbitrary"), VMEM_LIMIT_LARGE),
        name=name,
    )(h, mod, w)
    return [o.reshape(B, S, o.shape[-1]) for o in outs]


def _ret_body(q_ref, k_ref, v_ref, g_ref, cos_ref, sin_ref, dec_ref, qd_ref, kd_ref, cd_ref,
              gng_ref, gnb_ref, o_ref):
    n_chunks = q_ref.shape[1] // RET_CHUNK
    state = jnp.zeros((RET_HEAD_DIM, RET_HEAD_DIM), F32)
    for n in range(n_chunks):
        sl = pl.ds(n * RET_CHUNK, RET_CHUNK)
        q = q_ref[0, sl, :].astype(F32)
        k = k_ref[0, sl, :].astype(F32)
        v = v_ref[0, sl, :]
        c2 = cos_ref[sl, :]
        s2 = sin_ref[sl, :]
        qr = (q * c2 + pltpu.roll(q, RET_HEAD_DIM // 2, 1) * s2) * (RET_HEAD_DIM ** -0.5)
        kr = k * c2 + pltpu.roll(k, RET_HEAD_DIM // 2, 1) * s2
        scores = _dot_nt(qr.astype(BF16), kr.astype(BF16)) * dec_ref[0]
        inner = _dot(scores.astype(BF16), v)
        cross = _dot((qr * qd_ref[0]).astype(BF16), state.astype(BF16))
        kv = _dot_tn((kr * kd_ref[0]).astype(BF16), v)
        state = state * cd_ref[0] + kv
        y = inner + cross
        mu = jnp.mean(y, axis=-1, keepdims=True)
        yc = y - mu
        var = jnp.mean(yc * yc, axis=-1, keepdims=True)
        yn = yc * lax.rsqrt(var + LN_EPS)
        gate = g_ref[0, sl, :]
        o_ref[0, sl, :] = ((yn * gng_ref[...] + gnb_ref[...]) * _silu(gate)).astype(o_ref.dtype)


def _ret_tables(S):
    d = RET_HEAD_DIM
    inv = ROPE_BASE ** (-jnp.arange(0, d, 2, dtype=F32) / d)
    ang = jnp.arange(S).astype(F32)[:, None] * inv[None, :]
    cos, sin = jnp.cos(ang), jnp.sin(ang)
    cos2 = jnp.concatenate([cos, cos], axis=-1)
    sin2 = jnp.concatenate([-sin, sin], axis=-1)
    C = RET_CHUNK
    log_gamma = jnp.log1p(-jnp.exp2(-5.0 - jnp.arange(RET_HEADS, dtype=F32)))
    idx = jnp.arange(C, dtype=F32)
    diff = idx[:, None] - idx[None, :]
    dec = jnp.where(diff >= 0, jnp.exp(log_gamma[:, None, None] * jnp.maximum(diff, 0.0)), 0.0)
    kd = jnp.exp(log_gamma[:, None] * (C - 1 - idx)[None, :])
    qd = jnp.exp(log_gamma[:, None] * (idx + 1.0)[None, :])
    cd = jnp.exp(log_gamma * C)
    bc = lambda t: jnp.broadcast_to(t[:, :, None], (RET_HEADS, C, d))
    cdb = jnp.broadcast_to(cd[:, None, None], (RET_HEADS, d, d))
    return cos2, sin2, dec, bc(qd), bc(kd), cdb


def _ret_call(pb, pf, gn_g, gn_b):
    B, S, _ = pb.shape
    cos2, sin2, dec, qd, kd, cd = _ret_tables(S)
    col = lambda off: pl.BlockSpec((1, S, RET_HEAD_DIM), lambda b, h: (b, 0, off + h))
    tab = pl.BlockSpec((1, RET_CHUNK, RET_HEAD_DIM), lambda b, h: (h, 0, 0))
    vec = pl.BlockSpec((1, RET_HEAD_DIM), lambda b, h: (0, h))
    return pl.pallas_call(
        _ret_body,
        grid=(B, RET_HEADS),
        in_specs=[col(0), col(RET_HEADS), col(2 * RET_HEADS), col(0),
                  _resident((S, RET_HEAD_DIM)), _resident((S, RET_HEAD_DIM)),
                  tab, tab, tab, tab, vec, vec],
        out_specs=pl.BlockSpec((1, S, RET_HEAD_DIM), lambda b, h: (b, 0, h)),
        out_shape=jax.ShapeDtypeStruct((B, S, RET_W), BF16),
        compiler_params=_params(("arbitrary", "arbitrary"), VMEM_LIMIT_SMALL),
        name="retention",
    )(pb, pb, pb, pf, cos2, sin2, dec, qd, kd, cd, gn_g.reshape(1, RET_W), gn_b.reshape(1, RET_W))


def _compress_body(xk_ref, xv_ref, pk_ref, pv_ref, w1k_ref, w1v_ref, w2k_ref, w2v_ref, kc_ref, vc_ref):
    n_blk = kc_ref.shape[1]
    for x_ref, p_ref, w1_ref, w2_ref, o_ref in ((xk_ref, pk_ref, w1k_ref, w2k_ref, kc_ref),
                                                (xv_ref, pv_ref, w1v_ref, w2v_ref, vc_ref)):
        first = jnp.zeros((n_blk, w1_ref.shape[-1]), F32)
        second = jnp.zeros((n_blk, w1_ref.shape[-1]), F32)
        for t in range(CMP_STRIDE):
            x = x_ref[0, pl.ds(t, n_blk, stride=CMP_STRIDE), :]
            first = first + _dot((x + p_ref[t:t + 1, :]).astype(BF16), w1_ref[t])
            second = second + _dot((x + p_ref[CMP_STRIDE + t:CMP_STRIDE + t + 1, :]).astype(BF16),
                                   w1_ref[CMP_STRIDE + t])
        hid = _silu(first + pltpu.roll(second, n_blk - 1, 0))
        o_ref[0] = _dot(hid.astype(BF16), w2_ref[...]).astype(o_ref.dtype)


def _compress_call(pf, pos_k, pos_v, w1k, w1v, w2k, w2v):
    B, S, _ = pf.shape
    n_blk = S // CMP_STRIDE
    assert CMP_BLOCK == 2 * CMP_STRIDE

    def both_heads(t):
        z = jnp.zeros_like(t)
        return jnp.concatenate([jnp.concatenate([t, z], -1), jnp.concatenate([z, t], -1)], -2).astype(BF16)

    pos = lambda t: jnp.concatenate([t, t], axis=-1)
    w1 = lambda t: both_heads(t.reshape(CMP_BLOCK, NSA_HEAD_DIM, CMP_HIDDEN))
    kc_block = RET_W // NSA_KVW
    xspec = lambda c: pl.BlockSpec((1, S, NSA_KVW), lambda b: (b, 0, c))
    ospec = pl.BlockSpec((1, n_blk, NSA_KVW), lambda b: (b, 0, 0))
    return pl.pallas_call(
        _compress_body,
        grid=(B,),
        in_specs=[xspec(kc_block), xspec(kc_block + 1),
                  _resident((CMP_BLOCK, NSA_KVW)), _resident((CMP_BLOCK, NSA_KVW)),
                  _resident((CMP_BLOCK, NSA_KVW, 2 * CMP_HIDDEN)), _resident((CMP_BLOCK, NSA_KVW, 2 * CMP_HIDDEN)),
                  _resident((2 * CMP_HIDDEN, NSA_KVW)), _resident((2 * CMP_HIDDEN, NSA_KVW))],
        out_specs=[ospec, ospec],
        out_shape=[jax.ShapeDtypeStruct((B, n_blk, NSA_KVW), BF16)] * 2,
        compiler_params=_params(("arbitrary",), VMEM_LIMIT_SMALL),
        name="nsa_compress",
    )(pf, pf, pos(pos_k), pos(pos_v), w1(w1k), w1(w1v), both_heads(w2k), both_heads(w2v))


CMP_Q_ROWS = 512


def _cmp_attn_body(q_ref, kc_ref, vc_ref, cb_ref, ov_ref, o_ref, sel_ref):
    i = pl.program_id(1)
    tq = q_ref.shape[1]
    lane = lax.broadcasted_iota(jnp.int32, (tq, LANES), 1)
    row = lax.broadcasted_iota(jnp.int32, (tq, LANES), 0)
    lo_half = lane < NSA_HEAD_DIM
    kc = kc_ref[0]
    vc = vc_ref[0]
    zero = jnp.zeros_like(kc)
    kc_lo = lax.broadcasted_iota(jnp.int32, kc.shape, 1) < NSA_HEAD_DIM
    kc_g = (jnp.where(kc_lo, kc, zero), jnp.where(kc_lo, zero, kc))
    valid = (lane * CMP_STRIDE + (CMP_BLOCK - 1) <= i * tq + row)[None]
    q4 = jnp.concatenate([q_ref[0, :, r * LANES:(r + 1) * LANES] for r in range(NSA_GROUP)], axis=0)
    q4 = q4 * (NSA_HEAD_DIM ** -0.5)
    psum, outs = [], []
    for g in range(NSA_KV_HEADS):
        s = _dot_nt(q4, kc_g[g]).reshape(NSA_GROUP, tq, LANES) + cb_ref[g * NSA_GROUP:(g + 1) * NSA_GROUP]
        s = jnp.where(valid, s, NEG_INF)
        m = jnp.max(s, axis=-1, keepdims=True)
        e = jnp.where(valid, jnp.exp2(s - m), 0.0)
        den = jnp.maximum(jnp.sum(e, axis=-1, keepdims=True), 1e-30)
        p = e / den
        psum.append(jnp.sum(p, axis=0))
        outs.append(_dot(p.reshape(NSA_GROUP * tq, LANES).astype(BF16), vc))
    for r in range(NSA_GROUP):
        rs = slice(r * tq, (r + 1) * tq)
        o_ref[0, :, r * LANES:(r + 1) * LANES] = jnp.where(lo_half, outs[0][rs], outs[1][rs])

    n_sel = sel_ref.shape[2]
    blk = lax.broadcasted_iota(jnp.int32, (n_sel, tq), 0)
    qblk = (i * tq + lax.broadcasted_iota(jnp.int32, (n_sel, tq), 1)) // SEL_BLOCK
    forced = jnp.where(blk == 0, 1.0, jnp.where(blk == qblk, 1.0, jnp.where(blk == qblk - 1, 1.0, 0.0)))
    for g in range(NSA_KV_HEADS):
        p_hi, p_lo = _split_bf16(psum[g])
        imp = _dot_nt(ov_ref[...], p_hi) + _dot_nt(ov_ref[...], p_lo)
        score = jnp.where(forced > 0.5, SEL_FORCE_SCORE, jnp.where(blk <= qblk, imp, -1.0))
        rank = jnp.zeros((n_sel, tq), F32)
        for other in range(n_sel):
            so = score[other:other + 1, :]
            tie = jnp.where(blk > other, 1.0, 0.0)
            rank = rank + jnp.where(so > score, 1.0, jnp.where(so == score, tie, 0.0))
        sel_ref[0, g] = jnp.where(rank < float(min(SEL_TOP_N, n_sel)), 1.0, 0.0)


def _cmp_attn_call(pb, kc, vc, cmp_bias, overlap_t):
    B, S, _ = pb.shape
    n_sel = S // SEL_BLOCK
    q_block = (3 * RET_W) // NSA_QW
    tq = CMP_Q_ROWS
    return pl.pallas_call(
        _cmp_attn_body,
        grid=(B, S // tq),
        in_specs=[pl.BlockSpec((1, tq, NSA_QW), lambda b, i: (b, i, q_block)),
                  pl.BlockSpec((1,) + kc.shape[1:], lambda b, i: (b, 0, 0)),
                  pl.BlockSpec((1,) + vc.shape[1:], lambda b, i: (b, 0, 0)),
                  pl.BlockSpec((NSA_HEADS, tq, LANES), lambda b, i: (0, i, 0)),
                  _resident((n_sel, LANES))],
        out_specs=[pl.BlockSpec((1, tq, NSA_QW), lambda b, i: (b, i, 0)),
                   pl.BlockSpec((1, NSA_KV_HEADS, n_sel, tq), lambda b, i: (b, 0, 0, i))],
        out_shape=[jax.ShapeDtypeStruct((B, S, NSA_QW), F32),
                   jax.ShapeDtypeStruct((B, NSA_KV_HEADS, n_sel, S), F32)],
        compiler_params=_params(("arbitrary", "arbitrary"), VMEM_LIMIT_SMALL),
        name="nsa_cmp_attn",
    )(pb, kc, vc, cmp_bias, overlap_t)


SLC_CLASS_TILES = 2


def _slc_far_tiles(cls, n_tiles):
    return max(min(SLC_CLASS_TILES * (cls + 1) - 2, n_tiles - 2), 1)


def _slc_body(q_ref, ks_ref, vs_ref, sel_ref, tag_ref, near_ref, o_ref, kaug_scr, vaug_scr):
    i = pl.program_id(1)
    n_tiles = ks_ref.shape[1] // TILE
    n_sel = sel_ref.shape[2]
    lane = lax.broadcasted_iota(jnp.int32, (TILE, LANES), 1)
    lo_half = lane < NSA_HEAD_DIM

    @pl.when(i == 0)
    def _():
        for t in range(n_tiles):
            sl = pl.ds(t * TILE, TILE)
            kt = ks_ref[0, sl, :]
            zero = jnp.zeros_like(kt)
            kaug_scr[0, sl, :] = jnp.concatenate([jnp.where(lo_half, kt, zero), tag_ref[sl, :]], axis=1)
            kaug_scr[1, sl, :] = jnp.concatenate([jnp.where(lo_half, zero, kt), tag_ref[sl, :]], axis=1)
            vaug_scr[sl, :] = jnp.concatenate([vs_ref[0, sl, :], tag_ref[sl, :]], axis=1)

    prev = jnp.maximum(i - 1, 0)
    own_sl = pl.ds(pl.multiple_of(i * TILE, TILE), TILE)
    prev_sl = pl.ds(pl.multiple_of(prev * TILE, TILE), TILE)
    no_prev = jnp.where(lax.broadcasted_iota(jnp.int32, (1, 2 * TILE), 1) < TILE,
                        jnp.where(i == 0, NEG_INF, 0.0), 0.0)
    tile_pen = jnp.where(lane - n_sel < i - 1, 0.0, NEG_INF)
    sel_pad = jnp.zeros((LANES - n_sel, TILE), F32)

    def tile_body(n_far):
        wf = n_far * TILE
        v_near = jnp.concatenate([vaug_scr[prev_sl, :], vaug_scr[own_sl, :]], axis=0)
        q4 = jnp.concatenate([q_ref[0, :, r * LANES:(r + 1) * LANES] for r in range(NSA_GROUP)], axis=0)
        q4 = q4 * (NSA_HEAD_DIM ** -0.5)
        q_far, q_near, k_near = [], [], []
        for g in range(NSA_KV_HEADS):
            sel_q = jnp.concatenate([sel_ref[0, g], sel_pad], axis=0).T
            blk_pen = (sel_q - 1.0) * (-NEG_INF)
            pen_near = jnp.where(lane < n_sel, blk_pen, 0.0).astype(BF16)
            pen_far = jnp.where(lane < n_sel, blk_pen,
                                jnp.where(lane < n_sel + n_tiles, tile_pen, 0.0)).astype(BF16)
            q_far.append(jnp.concatenate([q4, jnp.concatenate([pen_far] * NSA_GROUP, axis=0)], axis=1))
            q_near.append(jnp.concatenate([q4, jnp.concatenate([pen_near] * NSA_GROUP, axis=0)], axis=1))
            k_near.append(jnp.concatenate([kaug_scr[g, prev_sl, :], kaug_scr[g, own_sl, :]], axis=0))

        n_split = 2
        rows = NSA_GROUP * TILE // n_split
        chains = [(g, part) for g in range(NSA_KV_HEADS) for part in range(n_split)]

        def scores(g, part):
            rs = slice(part * rows, (part + 1) * rows)
            return (_dot_nt(q_far[g][rs], kaug_scr[g, :wf, :]),
                    _dot_nt(q_near[g][rs], k_near[g]) + (near_ref[g, rs, :] + no_prev))

        def weights(s_far, s_near):
            m = jnp.maximum(jnp.max(s_far, axis=-1, keepdims=True), jnp.max(s_near, axis=-1, keepdims=True))
            return jnp.exp2(s_far - m).astype(BF16), jnp.exp2(s_near - m).astype(BF16)

        def values(p_far, p_near):
            acc = _dot(p_far, vaug_scr[:wf, :]) + _dot(p_near, v_near)
            return acc[:, :NSA_KVW] / acc[:, NSA_KVW + LANES - 1:]

        s, p, o = {}, {}, {}
        for step in range(len(chains) + 2):
            if step < len(chains):
                s[step] = scores(*chains[step])
            if 0 <= step - 2 < len(chains):
                o[step - 2] = values(*p.pop(step - 2))
            if 0 <= step - 1 < len(chains):
                p[step - 1] = weights(*s.pop(step - 1))
        for r in range(NSA_GROUP):
            part, rs = r // (NSA_GROUP // n_split), slice((r % (NSA_GROUP // n_split)) * TILE,
                                                          (r % (NSA_GROUP // n_split) + 1) * TILE)
            o_ref[0, :, r * LANES:(r + 1) * LANES] = jnp.where(lo_half, o[part][rs], o[n_split + part][rs])

    n_classes = -(-n_tiles // SLC_CLASS_TILES)
    for cls in range(n_classes):
        pl.when(i // SLC_CLASS_TILES == cls)(functools.partial(tile_body, _slc_far_tiles(cls, n_tiles)))


def _slc_call(pb, sel_t, key_tags, near):
    B, S, _ = pb.shape
    n_sel = S // SEL_BLOCK
    n_tiles = S // TILE
    assert n_sel + n_tiles < LANES
    q_block = (3 * RET_W) // NSA_QW
    ks_block = (3 * RET_W + NSA_QW) // NSA_KVW
    return pl.pallas_call(
        _slc_body,
        grid=(B, n_tiles),
        in_specs=[pl.BlockSpec((1, TILE, NSA_QW), lambda b, i: (b, i, q_block)),
                  pl.BlockSpec((1, S, NSA_KVW), lambda b, i: (b, 0, ks_block)),
                  pl.BlockSpec((1, S, NSA_KVW), lambda b, i: (b, 0, ks_block + 1)),
                  pl.BlockSpec((1, NSA_KV_HEADS, n_sel, TILE), lambda b, i: (b, 0, 0, i)),
                  _resident(key_tags.shape), _resident(near.shape)],
        out_specs=pl.BlockSpec((1, TILE, NSA_QW), lambda b, i: (b, i, 0)),
        out_shape=jax.ShapeDtypeStruct((B, S, NSA_QW), F32),
        scratch_shapes=[pltpu.VMEM((NSA_KV_HEADS, S, NSA_KVW + LANES), BF16),
                        pltpu.VMEM((S, NSA_KVW + LANES), BF16)],
        compiler_params=_params(("arbitrary", "arbitrary"), VMEM_LIMIT_LARGE),
        name="nsa_selected",
    )(pb, pb, pb, sel_t, key_tags, near)


WIN_PREV_TILES = (WIN_SIZE - 1 + TILE - 1) // TILE
WIN_Q_TILES = 2


def _win_body(q_ref, k_ref, v_ref, wb_ref, gate_ref, eg_ref, ocmp_ref, oslc_ref, y_ref):
    i = pl.program_id(1)
    tq = q_ref.shape[1]
    n_span = WIN_PREV_TILES + WIN_Q_TILES
    first = i * WIN_Q_TILES - WIN_PREV_TILES
    lo_half = lax.broadcasted_iota(jnp.int32, (tq, LANES), 1) < NSA_HEAD_DIM

    def attend(early):
        if early:
            k_tiles, v_tiles, negs = [], [], []
            for t in range(n_span):
                sl = pl.ds(pl.multiple_of(jnp.maximum(first + t, 0) * TILE, TILE), TILE)
                k_tiles.append(k_ref[0, sl, :])
                v_tiles.append(v_ref[0, sl, :])
                negs.append(jnp.full((tq, TILE), jnp.where(first + t < 0, NEG_INF, 0.0), F32))
            k_all = jnp.concatenate(k_tiles, axis=0)
            v_all = jnp.concatenate(v_tiles, axis=0)
            missing = jnp.concatenate(negs, axis=1)[None]
        else:
            sl = pl.ds(pl.multiple_of(first * TILE, TILE), n_span * TILE)
            k_all = k_ref[0, sl, :]
            v_all = v_ref[0, sl, :]
        zero = jnp.zeros_like(k_all)
        k_lo = lax.broadcasted_iota(jnp.int32, k_all.shape, 1) < NSA_HEAD_DIM
        k_g = (jnp.where(k_lo, k_all, zero), jnp.where(k_lo, zero, k_all))
        v_ones = jnp.concatenate([v_all, jnp.ones_like(v_all)], axis=1)

        g_hi, g_lo = _split_bf16(jax.nn.sigmoid(gate_ref[0]))
        gates = _dot(g_hi, eg_ref[...]) + _dot(g_lo, eg_ref[...])
        q4 = jnp.concatenate([q_ref[0, :, r * LANES:(r + 1) * LANES] for r in range(NSA_GROUP)], axis=0)
        q4 = q4 * (NSA_HEAD_DIM ** -0.5)
        n_split = 2
        heads = NSA_GROUP // n_split
        chains = [(g, part) for g in range(NSA_KV_HEADS) for part in range(n_split)]

        def scores(g, part):
            s = _dot_nt(q4[part * heads * tq:(part + 1) * heads * tq], k_g[g]).reshape(heads, tq, n_span * TILE)
            s = s + wb_ref[g * NSA_GROUP + part * heads:g * NSA_GROUP + (part + 1) * heads]
            return s + missing if early else s

        def weights(s):
            m = jnp.max(s, axis=-1, keepdims=True)
            return jnp.exp2(s - m).astype(BF16).reshape(heads * tq, n_span * TILE)

        def values(p):
            acc = _dot(p, v_ones)
            return acc[:, :NSA_KVW] / acc[:, NSA_KVW:NSA_KVW + 1]

        s, p, o = {}, {}, {}
        for step in range(len(chains) + 2):
            if step < len(chains):
                s[step] = scores(*chains[step])
            if 0 <= step - 2 < len(chains):
                o[step - 2] = values(p.pop(step - 2))
            if 0 <= step - 1 < len(chains):
                p[step - 1] = weights(s.pop(step - 1))
        for r in range(NSA_GROUP):
            cols = slice(r * LANES, (r + 1) * LANES)
            part, rs = r // heads, slice((r % heads) * tq, (r % heads + 1) * tq)
            o_win = jnp.where(lo_half, o[part][rs], o[n_split + part][rs])
            y = (gates[:, r * LANES:(r + 1) * LANES] * ocmp_ref[0, :, cols]
                 + gates[:, NSA_QW + r * LANES:NSA_QW + (r + 1) * LANES] * oslc_ref[0, :, cols]
                 + gates[:, 2 * NSA_QW + r * LANES:2 * NSA_QW + (r + 1) * LANES] * o_win)
            y_ref[0, :, cols] = y.astype(y_ref.dtype)

    pl.when(first >= 0)(functools.partial(attend, False))
    pl.when(first < 0)(functools.partial(attend, True))


def _win_call(pb, pf, win_bias, gate_expand, o_cmp, o_slc):
    B, S, _ = pb.shape
    q_block = (3 * RET_W) // NSA_QW
    kw_block = (3 * RET_W + NSA_QW) // NSA_KVW + 2
    gate_block = (RET_W + 2 * NSA_KVW) // LANES
    tq = WIN_Q_TILES * TILE
    tile_spec = pl.BlockSpec((1, tq, NSA_QW), lambda b, i: (b, i, 0))
    return pl.pallas_call(
        _win_body,
        grid=(B, S // tq),
        in_specs=[pl.BlockSpec((1, tq, NSA_QW), lambda b, i: (b, i, q_block)),
                  pl.BlockSpec((1, S, NSA_KVW), lambda b, i: (b, 0, kw_block)),
                  pl.BlockSpec((1, S, NSA_KVW), lambda b, i: (b, 0, kw_block + 1)),
                  _resident(win_bias.shape),
                  pl.BlockSpec((1, tq, LANES), lambda b, i: (b, i, gate_block)),
                  _resident((LANES, 3 * NSA_QW)),
                  tile_spec, tile_spec],
        out_specs=tile_spec,
        out_shape=jax.ShapeDtypeStruct((B, S, NSA_QW), BF16),
        compiler_params=_params(("arbitrary", "arbitrary"), VMEM_LIMIT_LARGE),
        name="nsa_window_combine",
    )(pb, pb, pb, win_bias, pf, gate_expand, o_cmp, o_slc)


def _post_body(h_ref, ma_ref, mb_ref, mod_ref, woa_ref, wob_ref, lng_ref, lnb_ref,
               wg_ref, wu_ref, wd_ref, o_ref):
    y = _dot(ma_ref[0], woa_ref[...]) + _dot(mb_ref[0], wob_ref[...])
    h1 = _layer_norm(DEEPNORM_ALPHA * h_ref[0] + mod_ref[0, 2:3, :] * y, lng_ref[0:1, :], lnb_ref[0:1, :])
    u = (h1 * (1.0 + mod_ref[0, 4:5, :]) + mod_ref[0, 3:4, :]).astype(BF16)
    acc = jnp.zeros(h1.shape, F32)
    for c0, cn in FFN_CHUNKS:
        gate = _dot(u, wg_ref[:, c0:c0 + cn])
        up = _dot(u, wu_ref[:, c0:c0 + cn])
        acc = acc + _dot((_silu(gate) * up).astype(BF16), wd_ref[c0:c0 + cn, :])
    o_ref[0] = _layer_norm(DEEPNORM_ALPHA * h1 + mod_ref[0, 5:6, :] * acc, lng_ref[1:2, :], lnb_ref[1:2, :])


def _post_call(h, mix_a, mix_b, col_a, col_b, mod, wo_a, wo_b, ln_g, ln_b, w_gate, w_up, w_down, tm, name):
    B, S, _ = h.shape
    half = D_MODEL // 2
    tok = pl.BlockSpec((1, tm, D_MODEL), lambda b, t: (b, t, 0))
    return pl.pallas_call(
        _post_body,
        grid=(B, S // tm),
        in_specs=[tok,
                  pl.BlockSpec((1, tm, half), lambda b, t: (b, t, col_a)),
                  pl.BlockSpec((1, tm, half), lambda b, t: (b, t, col_b)),
                  pl.BlockSpec((1, 6, D_MODEL), lambda b, t: (b, 0, 0)),
                  _resident((half, D_MODEL)), _resident((half, D_MODEL)),
                  _resident((2, D_MODEL)), _resident((2, D_MODEL)),
                  _resident((D_MODEL, D_FF)), _resident((D_MODEL, D_FF)), _resident((D_FF, D_MODEL))],
        out_specs=tok,
        out_shape=jax.ShapeDtypeStruct((B, S, D_MODEL), F32),
        compiler_params=_params(("arbitrary", "arbitrary"), VMEM_LIMIT_LARGE),
        name=name,
    )(h, mix_a, mix_b, mod, wo_a, wo_b, ln_g, ln_b, w_gate, w_up, w_down)


DIL_STEP_TILES = 4


def _dil_body(q_ref, k_ref, kp_ref, v_ref, vp_ref, b_ref, o_ref, lse_ref, *, tiles_per_seg):
    t = pl.program_id(1)
    res_tiles = min(tiles_per_seg, DIL_STEP_TILES)
    with_prev = tiles_per_seg > 1
    if tiles_per_seg > DIL_STEP_TILES:
        first_prev = jnp.where((t * DIL_STEP_TILES) % tiles_per_seg == 0, NEG_INF, 0.0)
    else:
        first_prev = NEG_INF

    def span(ref, before_ref, j, cols):
        if not with_prev:
            return ref[0, j * TILE:(j + 1) * TILE, cols]
        if j > 0:
            return ref[0, (j - 1) * TILE:(j + 1) * TILE, cols]
        return jnp.concatenate([before_ref[0, :, cols], ref[0, :TILE, cols]], axis=0)

    def tile_bias(h, j):
        if not with_prev:
            return b_ref[h, TILE:, :]
        if j % res_tiles != 0:
            return b_ref[h]
        gone = first_prev if j == 0 else NEG_INF
        return jnp.concatenate([b_ref[h, :TILE, :] + gone, b_ref[h, TILE:, :]], axis=0)

    head_cols = lambda h: slice(h * DIL_HEAD_DIM, (h + 1) * DIL_HEAD_DIM)
    pad = jnp.zeros((LANES - DIL_HEADS, TILE), F32)

    def scores(j):
        return jnp.concatenate(
            [_dot_nt(span(k_ref, kp_ref, j, head_cols(h)), q_ref[0, j * TILE:(j + 1) * TILE, head_cols(h)])
             + tile_bias(h, j) for h in range(DIL_HEADS)], axis=1)

    def weights(j, s):
        m = jnp.max(s, axis=0, keepdims=True)
        e = jnp.exp2(s - m)
        den = jnp.sum(e, axis=0, keepdims=True)
        lse = (m + jnp.log2(den)) * math.log(2.0)
        by_head = [lse[:, h * TILE:(h + 1) * TILE] for h in range(DIL_HEADS)]
        lse_ref[0, j * TILE:(j + 1) * TILE, :] = jnp.concatenate(by_head + [pad], axis=0).T
        return (e * (1.0 / den)).astype(BF16)

    def values(j, p):
        for h in range(DIL_HEADS):
            o = _dot_tn(p[:, h * TILE:(h + 1) * TILE], span(v_ref, vp_ref, j, head_cols(h)))
            o_ref[0, j * TILE:(j + 1) * TILE, head_cols(h)] = o.astype(o_ref.dtype)

    s, p = {}, {}
    for step in range(DIL_STEP_TILES + 2):
        if step < DIL_STEP_TILES:
            s[step] = scores(step)
        if 0 <= step - 2 < DIL_STEP_TILES:
            values(step - 2, p.pop(step - 2))
        if 0 <= step - 1 < DIL_STEP_TILES:
            p[step - 1] = weights(step - 1, s.pop(step - 1))


def _dil_call(proj, bias, dilation, name):
    B, S, _ = proj.shape
    step = DIL_STEP_TILES * TILE
    own = lambda c: pl.BlockSpec((1, step, DIL_WIDTH), lambda b, i: (b, i, c))
    before = lambda c: pl.BlockSpec((1, TILE, DIL_WIDTH),
                                    lambda b, i: (b, jnp.maximum(i * DIL_STEP_TILES - 1, 0), c))
    tile = lambda width: pl.BlockSpec((1, step, width), lambda b, i: (b, i, 0))
    return pl.pallas_call(
        functools.partial(_dil_body, tiles_per_seg=(S // dilation) // TILE),
        grid=(B, S // step),
        in_specs=[own(0), own(1), before(1), own(2), before(2), _resident(bias.shape)],
        out_specs=[tile(DIL_WIDTH), tile(LANES)],
        out_shape=[jax.ShapeDtypeStruct((B, S, DIL_WIDTH), BF16), jax.ShapeDtypeStruct((B, S, LANES), F32)],
        compiler_params=_params(("arbitrary", "arbitrary"), VMEM_LIMIT_LARGE),
        name=name,
    )(proj, proj, proj, proj, proj, bias)


def _dil_mix_body(o0_ref, o1_ref, o2_ref, l0_ref, l1_ref, l2_ref, y_ref, o_scr, l_scr):
    o_refs = (o0_ref, o1_ref, o2_ref)
    l_refs = (l0_ref, l1_ref, l2_ref)
    tm = y_ref.shape[1]
    for gi, (o_ref, l_ref) in enumerate(zip(o_refs, l_refs)):
        dilation = o_ref.shape[1]
        for r in range(dilation):
            rows = pl.ds(r, tm // dilation, stride=dilation)
            l_scr[gi, rows, :] = l_ref[0, r]
            for h in range(DIL_HEADS):
                o_scr[gi, h, rows, :] = o_ref[0, r, :, h * DIL_HEAD_DIM:(h + 1) * DIL_HEAD_DIM].astype(F32)
    for h in range(DIL_HEADS):
        lses = [jnp.broadcast_to(l_scr[gi, :, h:h + 1], (tm, DIL_HEAD_DIM)) for gi in range(len(o_refs))]
        m = jnp.maximum(jnp.maximum(lses[0], lses[1]), lses[2])
        ws = [jnp.exp(l - m) for l in lses]
        den = ws[0] + ws[1] + ws[2]
        y = sum((w / den) * o_scr[gi, h] for gi, w in enumerate(ws))
        y_ref[0, :, h * DIL_HEAD_DIM:(h + 1) * DIL_HEAD_DIM] = y.astype(y_ref.dtype)


def _dil_mix_call(outs, lses, tm):
    B, S, _ = outs[0].shape
    dilations = [d for _, d in DIL_PATTERNS]
    by_residue = lambda t, d: t.reshape(B, d, S // d, t.shape[-1])
    spec = lambda d, width: pl.BlockSpec((1, d, tm // d, width), lambda b, t: (b, 0, t, 0))
    return pl.pallas_call(
        _dil_mix_body,
        grid=(B, S // tm),
        in_specs=[spec(d, DIL_WIDTH) for d in dilations] + [spec(d, LANES) for d in dilations],
        out_specs=pl.BlockSpec((1, tm, DIL_WIDTH), lambda b, t: (b, t, 0)),
        out_shape=jax.ShapeDtypeStruct((B, S, DIL_WIDTH), BF16),
        scratch_shapes=[pltpu.VMEM((len(dilations), DIL_HEADS, tm, DIL_HEAD_DIM), F32),
                        pltpu.VMEM((len(dilations), tm, LANES), F32)],
        compiler_params=_params(("arbitrary", "arbitrary"), VMEM_LIMIT_SMALL),
        name="dilated_mix",
    )(*[by_residue(o, d) for o, d in zip(outs, dilations)], *[by_residue(l, d) for l, d in zip(lses, dilations)])


def _post_mix_body(h_ref, o0_ref, o1_ref, o2_ref, l0_ref, l1_ref, l2_ref, mod_ref, wo_ref, lng_ref, lnb_ref,
                   wg_ref, wu_ref, wd_ref, out_ref, mix_a, mix_b, o_scr, l_scr):
    n = pl.program_id(0)
    o_refs = (o0_ref, o1_ref, o2_ref)
    l_refs = (l0_ref, l1_ref, l2_ref)
    tm = out_ref.shape[1]

    @pl.when(n == 0)
    def _():
        mix_a[...] = jnp.zeros_like(mix_a)
        mix_b[...] = jnp.zeros_like(mix_b)

    def unpermute():
        for gi, (o_ref, l_ref) in enumerate(zip(o_refs, l_refs)):
            dilation = o_ref.shape[1]
            for r in range(dilation):
                rows = pl.ds(r, tm // dilation, stride=dilation)
                l_scr[gi, rows, :] = l_ref[0, r]
                for h in range(DIL_HEADS):
                    o_scr[gi, h, rows, :] = o_ref[0, r, :, h * DIL_HEAD_DIM:(h + 1) * DIL_HEAD_DIM].astype(F32)

    def mix_head(h, mix_w):
        lses = [jnp.broadcast_to(l_scr[gi, :, h:h + 1], (tm, DIL_HEAD_DIM)) for gi in range(len(o_refs))]
        m = jnp.maximum(jnp.maximum(lses[0], lses[1]), lses[2])
        ws = [jnp.exp(l - m) for l in lses]
        den = ws[0] + ws[1] + ws[2]
        y = sum((w / den) * o_scr[gi, h] for gi, w in enumerate(ws))
        mix_w[:, h * DIL_HEAD_DIM:(h + 1) * DIL_HEAD_DIM] = y.astype(mix_w.dtype)

    def step(mix_w, mix_r):
        y = _dot(mix_r[...], wo_ref[...])
        unpermute()
        h1 = _layer_norm(DEEPNORM_ALPHA * h_ref[0] + mod_ref[0, 2:3, :] * y, lng_ref[0:1, :], lnb_ref[0:1, :])
        u = (h1 * (1.0 + mod_ref[0, 4:5, :]) + mod_ref[0, 3:4, :]).astype(BF16)
        acc = jnp.zeros(h1.shape, F32)
        heads_per_chunk = DIL_HEADS // len(FFN_CHUNKS)
        for ci, (c0, cn) in enumerate(FFN_CHUNKS):
            gate = _dot(u, wg_ref[:, c0:c0 + cn])
            up = _dot(u, wu_ref[:, c0:c0 + cn])
            for h in range(ci * heads_per_chunk, (ci + 1) * heads_per_chunk):
                mix_head(h, mix_w)
            acc = acc + _dot((_silu(gate) * up).astype(BF16), wd_ref[c0:c0 + cn, :])
        out_ref[0] = _layer_norm(DEEPNORM_ALPHA * h1 + mod_ref[0, 5:6, :] * acc, lng_ref[1:2, :], lnb_ref[1:2, :])

    pl.when(n % 2 == 0)(functools.partial(step, mix_a, mix_b))
    pl.when(n % 2 == 1)(functools.partial(step, mix_b, mix_a))


def _post_mix_call(h, outs, lses, mod, wo, ln_g, ln_b, w_gate, w_up, w_down, tm, name):
    B, S, _ = h.shape
    steps = S // tm
    last = B * steps - 1
    dilations = [d for _, d in DIL_PATTERNS]
    by_residue = lambda t, d: t.reshape(B, d, S // d, t.shape[-1])
    cur = lambda n: jnp.minimum(n, last)
    done = lambda n: jnp.maximum(n - 1, 0)
    src = lambda d, width: pl.BlockSpec((1, d, tm // d, width), lambda n: (cur(n) // steps, 0, cur(n) % steps, 0))
    tok = pl.BlockSpec((1, tm, D_MODEL), lambda n: (done(n) // steps, done(n) % steps, 0))
    return pl.pallas_call(
        _post_mix_body,
        grid=(B * steps + 1,),
        in_specs=[tok] + [src(d, DIL_WIDTH) for d in dilations] + [src(d, LANES) for d in dilations]
                 + [pl.BlockSpec((1, 6, D_MODEL), lambda n: (done(n) // steps, 0, 0)),
                    _resident((D_MODEL, D_MODEL)), _resident((2, D_MODEL)), _resident((2, D_MODEL)),
                    _resident((D_MODEL, D_FF)), _resident((D_MODEL, D_FF)), _resident((D_FF, D_MODEL))],
        out_specs=tok,
        out_shape=jax.ShapeDtypeStruct((B, S, D_MODEL), F32),
        scratch_shapes=[pltpu.VMEM((tm, D_MODEL), BF16), pltpu.VMEM((tm, D_MODEL), BF16),
                        pltpu.VMEM((len(dilations), DIL_HEADS, tm, DIL_HEAD_DIM), F32),
                        pltpu.VMEM((len(dilations), tm, LANES), F32)],
        compiler_params=_params(("arbitrary",), VMEM_LIMIT_LARGE),
        name=name,
    )(h, *[by_residue(o, d) for o, d in zip(outs, dilations)], *[by_residue(l, d) for l, d in zip(lses, dilations)],
      mod, wo, ln_g, ln_b, w_gate, w_up, w_down)


def _t5_bucket_np(dist):
    n = np.maximum(dist, 0)
    max_exact = REL_BUCKETS // 2
    nf = np.maximum(n, 1).astype(np.float64)
    val = np.log(nf / max_exact) / math.log(REL_MAX_DIST / max_exact) * (REL_BUCKETS - max_exact)
    frac = np.abs(val - np.round(val))
    on_edge = (frac < 1e-9) & (n > max_exact) & (n < REL_MAX_DIST)
    assert not on_edge.any()
    large = np.minimum(max_exact + np.floor(val + 1e-9).astype(np.int64), REL_BUCKETS - 1)
    return np.where(n < max_exact, n, large).astype(np.int32)


def _shift_table(rel_bias, rows, cols, step, dist_fn, valid_fn):
    u = np.concatenate([np.arange(cols), np.arange(-(rows - 1) * step, 0)])
    period = u.size
    vals = jnp.take(rel_bias, jnp.asarray(_t5_bucket_np(dist_fn(u))), axis=0).T
    vals = jnp.where(jnp.asarray(valid_fn(u))[None], vals, NEG_INF)
    t = jnp.tile(vals, (1, rows))[:, :rows * (period - step)].reshape(vals.shape[0], rows, period - step)
    return t[:, :, :cols]


def _nsa_tables(rel_bias, S):
    rel_bias = rel_bias * LOG2_E
    always = lambda u: np.ones(u.shape, bool)
    win_dist = lambda u: WIN_PREV_TILES * TILE - u
    win_bias = _shift_table(rel_bias, WIN_Q_TILES * TILE, (WIN_PREV_TILES + WIN_Q_TILES) * TILE, 1, win_dist,
                            lambda u: (win_dist(u) >= 0) & (win_dist(u) <= WIN_SIZE - 1))
    d0 = _shift_table(rel_bias, TILE, TILE, 1, lambda u: -u, lambda u: u <= 0)
    d1 = _shift_table(rel_bias, TILE, TILE, 1, lambda u: TILE - u, always)
    far_bucket = _t5_bucket_np(np.arange(TILE + 1, S + TILE))
    assert (far_bucket == far_bucket[0]).all()
    far = rel_bias[int(far_bucket[0])][:, None, None]
    near = jnp.concatenate([d1 - far, d0 - far], axis=2)
    near = near.reshape(NSA_KV_HEADS, NSA_GROUP * TILE, 2 * TILE)
    cmp_bias = _shift_table(rel_bias, LANES, S, CMP_STRIDE, lambda u: u - (CMP_BLOCK - 1), always)
    cmp_bias = cmp_bias.transpose(0, 2, 1)
    n_cmp = (S - CMP_BLOCK) // CMP_STRIDE + 1
    n_sel = S // SEL_BLOCK
    cs = (np.arange(n_cmp) * CMP_STRIDE)[:, None]
    ss = (np.arange(n_sel) * SEL_BLOCK)[None, :]
    ov = np.clip(np.minimum(cs + CMP_BLOCK, ss + SEL_BLOCK) - np.maximum(cs, ss), 0, None) / CMP_BLOCK
    ov_t = np.zeros((n_sel, LANES), np.float32)
    ov_t[:, :n_cmp] = ov.T
    key = np.arange(S)[:, None]
    lane = np.arange(LANES)[None, :]
    tags = ((lane == key // SEL_BLOCK) | (lane == n_sel + key // TILE) | (lane == LANES - 1)).astype(np.float32)
    eg = np.zeros((LANES, 3 * NSA_QW), np.float32)
    for g in range(NSA_KV_HEADS):
        for r in range(NSA_GROUP):
            for j in range(3):
                base = j * NSA_QW + r * LANES + g * NSA_HEAD_DIM
                eg[g * NSA_GROUP * 3 + r * 3 + j, base:base + NSA_HEAD_DIM] = 1.0
    return win_bias, near, cmp_bias, jnp.asarray(ov_t, BF16), jnp.asarray(tags, BF16), jnp.asarray(eg, BF16)


def _dil_bias(rel_bias, dilation, max_dist):
    dist = lambda u: TILE + u
    return _shift_table(rel_bias * LOG2_E, 2 * TILE, TILE, 1, lambda u: dist(u) * dilation,
                        lambda u: (dist(u) >= 0) & (dist(u) <= max_dist))


def _nsa_head_perm():
    perm = np.zeros(NSA_QW, np.int64)
    for r in range(NSA_GROUP):
        for g in range(NSA_KV_HEADS):
            new = r * LANES + g * NSA_HEAD_DIM
            old = (g * NSA_GROUP + r) * NSA_HEAD_DIM
            perm[new:new + NSA_HEAD_DIM] = np.arange(old, old + NSA_HEAD_DIM)
    return perm


def _layer0_mixer(h, mod, ab_w_in, rel_bias, gn_g, gn_b, pos_k, pos_v, w1k, w2k, w1v, w2v):
    B, S, _ = h.shape
    o = np.cumsum((0, RET_W, RET_W, RET_W, RET_W, NSA_QW) + (NSA_KVW,) * 6 + (3 * NSA_HEADS,))
    seg = lambda a: ab_w_in[:, o[a]:o[a + 1]]
    gate_w = jnp.pad(seg(11), ((0, 0), (0, LANES - 3 * NSA_HEADS)))
    q_nsa = seg(4)[:, _nsa_head_perm()] * LOG2_E
    w = jnp.concatenate([seg(0), seg(1), seg(2), q_nsa, seg(7), seg(8), seg(9), seg(10),
                         seg(3), seg(5), seg(6), gate_w], axis=1).astype(BF16)
    pb, pf = _pre_call(h, mod, w, ((PRE0_BF16_COLS, BF16), (PRE0_F32_COLS, F32)), 512, "pre0")

    y_ret = _ret_call(pb, pf, gn_g, gn_b)

    win_bias, near, cmp_bias, ov_t, key_tags, eg = _nsa_tables(rel_bias, S)
    kc, vc = _compress_call(pf, pos_k, pos_v, w1k, w1v, w2k, w2v)
    o_cmp, sel_t = _cmp_attn_call(pb, kc, vc, cmp_bias, ov_t)
    o_slc = _slc_call(pb, sel_t, key_tags, near)
    y_nsa = _win_call(pb, pf, win_bias, eg, o_cmp, o_slc)
    return y_ret, y_nsa


def _layer1_mixer(h, mod, dil_w_in, rel_bias):
    B, S, _ = h.shape
    outs, lses = [], []
    for gi, (window, dilation) in enumerate(DIL_PATTERNS):
        w = dil_w_in[:, gi * 3 * DIL_WIDTH:(gi + 1) * 3 * DIL_WIDTH]
        w = jnp.concatenate([w[:, :DIL_WIDTH] * (DIL_HEAD_DIM ** -0.5 * LOG2_E), w[:, DIL_WIDTH:]], axis=1).astype(BF16)
        proj, = _pre_call(h, mod, w, ((3 * DIL_WIDTH, BF16),), 512, f"pre1_{gi}", dilation)
        bias = _dil_bias(rel_bias, dilation, window // dilation)
        o, lse = _dil_call(proj, bias, dilation, f"dilated_{gi}")
        outs.append(o)
        lses.append(lse)
    return outs, lses


def kernel(x, c, rel_bias, ada_w, ada_b, ln_g, ln_b, ab_w_in, ab_w_out, ret_gn_g, ret_gn_b, cmp_pos_k, cmp_pos_v, cmp_k_w1, cmp_k_w2, cmp_v_w1, cmp_v_w2, dil_w_in, dil_w_out, ffn_w_gate, ffn_w_up, ffn_w_down):
    B = x.shape[0]
    mod = _ada_call(c, ada_w, ada_b).reshape(DEPTH, B, 6, D_MODEL)
    h = x
    for layer in range(DEPTH):
        i = layer // 2
        if layer % 2 == 0:
            mix_a, mix_b = _layer0_mixer(h, mod[layer], ab_w_in[i], rel_bias, ret_gn_g[i], ret_gn_b[i],
                                         cmp_pos_k[i], cmp_pos_v[i], cmp_k_w1[i], cmp_k_w2[i],
                                         cmp_v_w1[i], cmp_v_w2[i])
            wo_a = ab_w_out[i, :RET_W]
            wo_b = ab_w_out[i, RET_W:][_nsa_head_perm()]
            h = _post_call(h, mix_a, mix_b, 0, 0, mod[layer], wo_a.astype(BF16), wo_b.astype(BF16),
                           ln_g[layer], ln_b[layer], ffn_w_gate[layer].astype(BF16), ffn_w_up[layer].astype(BF16),
                           ffn_w_down[layer].astype(BF16), 512, f"post{layer}")
        else:
            outs, lses = _layer1_mixer(h, mod[layer], dil_w_in[i], rel_bias)
            h = _post_mix_call(h, outs, lses, mod[layer], dil_w_out[i].astype(BF16), ln_g[layer], ln_b[layer],
                               ffn_w_gate[layer].astype(BF16), ffn_w_up[layer].astype(BF16),
                               ffn_w_down[layer].astype(BF16), 512, f"post{layer}")
    return h
```

```python
import functools
import math

import numpy as np
import jax
import jax.numpy as jnp
from jax import lax
from jax.experimental import pallas as pl
from jax.experimental.pallas import tpu as pltpu

F32 = jnp.float32
BF16 = jnp.bfloat16

D_MODEL = 1024
DEPTH = 2
DEEPNORM_ALPHA = (2 * DEPTH) ** 0.25
LN_EPS = 1e-5
NEG_INF = -1e30
LOG2_E = math.log2(math.e)

RET_HEADS = 4
RET_HEAD_DIM = 128
RET_CHUNK = 128
ROPE_BASE = 10000.0
RET_W = RET_HEADS * RET_HEAD_DIM

NSA_HEADS = 8
NSA_KV_HEADS = 2
NSA_GROUP = 4
NSA_HEAD_DIM = 64
CMP_BLOCK = 32
CMP_STRIDE = 16
CMP_HIDDEN = 256
SEL_BLOCK = 64
SEL_TOP_N = 16
SEL_FORCE_SCORE = 1e4
WIN_SIZE = 512
NSA_QW = NSA_HEADS * NSA_HEAD_DIM
NSA_KVW = NSA_KV_HEADS * NSA_HEAD_DIM

DIL_PATTERNS = ((128, 1), (512, 4), (2048, 16))
DIL_HEADS = 8
DIL_HEAD_DIM = 128
DIL_WIDTH = DIL_HEADS * DIL_HEAD_DIM

REL_BUCKETS = 32
REL_MAX_DIST = 128
D_FF = 2816

LANES = 128
TILE = 128
VMEM_LIMIT_SMALL = 32 * 1024 * 1024
VMEM_LIMIT_LARGE = 56 * 1024 * 1024

PRE0_BF16_COLS = 3 * RET_W + NSA_QW + 4 * NSA_KVW
PRE0_F32_COLS = RET_W + 2 * NSA_KVW + LANES
FFN_CHUNKS = ((0, 768), (768, 768), (1536, 768), (2304, 512))
NSA_CHAIN_SPLIT = 2


def _dot(a, b):
    return jnp.dot(a, b, preferred_element_type=F32)


def _dot_nt(a, b):
    return lax.dot_general(a, b, (((1,), (1,)), ((), ())), preferred_element_type=F32)


def _dot_tn(a, b):
    return lax.dot_general(a, b, (((0,), (0,)), ((), ())), preferred_element_type=F32)


def _split_bf16(x):
    hi = x.astype(BF16)
    lo = (x - hi.astype(F32)).astype(BF16)
    return hi, lo


def _silu(x):
    return x * jax.nn.sigmoid(x)


def _layer_norm(x, g, b):
    mu = jnp.mean(x, axis=-1, keepdims=True)
    xc = x - mu
    var = jnp.mean(xc * xc, axis=-1, keepdims=True)
    return xc * lax.rsqrt(var + LN_EPS) * g + b


def _resident(shape):
    return pl.BlockSpec(shape, lambda *_: (0,) * len(shape), pipeline_mode=pl.Buffered(1))


def _params(sem, vmem):
    return pltpu.CompilerParams(dimension_semantics=sem, vmem_limit_bytes=vmem)


def _ada_body(c_ref, w_ref, b_ref, o_ref):
    a_hi, a_lo = _split_bf16(_silu(c_ref[...]))
    w_hi, w_lo = _split_bf16(w_ref[0])
    o_ref[0] = _dot(a_hi, w_hi) + _dot(a_lo, w_hi) + _dot(a_hi, w_lo) + b_ref[0]


def _ada_call(c, ada_w, ada_b):
    B = c.shape[0]
    n_out = ada_w.shape[-1]
    tn = n_out // 4
    return pl.pallas_call(
        _ada_body,
        grid=(DEPTH, n_out // tn),
        in_specs=[pl.BlockSpec((B, D_MODEL), lambda l, n: (0, 0)),
                  pl.BlockSpec((1, D_MODEL, tn), lambda l, n: (l, 0, n)),
                  pl.BlockSpec((1, 1, tn), lambda l, n: (l, 0, n))],
        out_specs=pl.BlockSpec((1, B, tn), lambda l, n: (l, 0, n)),
        out_shape=jax.ShapeDtypeStruct((DEPTH, B, n_out), F32),
        compiler_params=_params(("arbitrary", "arbitrary"), VMEM_LIMIT_LARGE),
        name="ada_mod",
    )(c, ada_w, ada_b.reshape(DEPTH, 1, n_out))


def _pre_body(h_ref, mod_ref, w_ref, *refs, dilation):
    if dilation == 1:
        o_refs, h = refs, h_ref[0]
    else:
        o_refs, h_scr = refs[:-1], refs[-1]
        tm = h_ref.shape[1]
        for c in range(h_scr.shape[0]):
            h_scr[c] = h_ref[0, :, c * LANES:(c + 1) * LANES]
        h = jnp.concatenate(
            [jnp.concatenate([h_scr[c, pl.ds(r, tm // dilation, stride=dilation), :]
                              for c in range(h_scr.shape[0])], axis=1) for r in range(dilation)], axis=0)
    u = (h * (1.0 + mod_ref[0, 1:2, :]) + mod_ref[0, 0:1, :]).astype(BF16)
    off = 0
    for o_ref in o_refs:
        n = o_ref.shape[-1]
        o_ref[0] = _dot(u, w_ref[:, off:off + n]).astype(o_ref.dtype).reshape(o_ref.shape[1:])
        off += n


def _pre_call(h, mod, w, out_cols_dtypes, tm, name, dilation=1):
    B, S, _ = h.shape
    n_total = w.shape[1]
    assert sum(n for n, _ in out_cols_dtypes) == n_total
    if dilation == 1:
        out_specs = [pl.BlockSpec((1, tm, n), lambda b, t: (b, t, 0)) for n, _ in out_cols_dtypes]
        out_shape = [jax.ShapeDtypeStruct((B, S, n), dt) for n, dt in out_cols_dtypes]
        scratch = []
    else:
        out_specs = [pl.BlockSpec((1, dilation, tm // dilation, n), lambda b, t: (b, 0, t, 0))
                     for n, _ in out_cols_dtypes]
        out_shape = [jax.ShapeDtypeStruct((B, dilation, S // dilation, n), dt) for n, dt in out_cols_dtypes]
        scratch = [pltpu.VMEM((D_MODEL // LANES, tm, LANES), F32)]
    outs = pl.pallas_call(
        functools.partial(_pre_body, dilation=dilation),
        grid=(B, S // tm),
        in_specs=[pl.BlockSpec((1, tm, D_MODEL), lambda b, t: (b, t, 0)),
                  pl.BlockSpec((1, 6, D_MODEL), lambda b, t: (b, 0, 0)),
                  _resident((D_MODEL, n_total))],
        out_specs=out_specs,
        out_shape=out_shape,
        scratch_shapes=scratch,
        compiler_params=_params(("arbitrary", "arbitrary"), VMEM_LIMIT_LARGE),
        name=name,
    )(h, mod, w)
    return [o.reshape(B, S, o.shape[-1]) for o in outs]


def _ret_body(q_ref, k_ref, v_ref, g_ref, cos_ref, sin_ref, dec_ref, qd_ref, kd_ref, cd_ref,
              gng_ref, gnb_ref, o_ref):
    n_chunks = q_ref.shape[1] // RET_CHUNK
    state = jnp.zeros((RET_HEAD_DIM, RET_HEAD_DIM), F32)
    for n in range(n_chunks):
        sl = pl.ds(n * RET_CHUNK, RET_CHUNK)
        q = q_ref[0, sl, :].astype(F32)
        k = k_ref[0, sl, :].astype(F32)
        v = v_ref[0, sl, :]
        c2 = cos_ref[sl, :]
        s2 = sin_ref[sl, :]
        qr = (q * c2 + pltpu.roll(q, RET_HEAD_DIM // 2, 1) * s2) * (RET_HEAD_DIM ** -0.5)
        kr = k * c2 + pltpu.roll(k, RET_HEAD_DIM // 2, 1) * s2
        scores = _dot_nt(qr.astype(BF16), kr.astype(BF16)) * dec_ref[0]
        inner = _dot(scores.astype(BF16), v)
        cross = _dot((qr * qd_ref[0]).astype(BF16), state.astype(BF16))
        kv = _dot_tn((kr * kd_ref[0]).astype(BF16), v)
        state = state * cd_ref[0] + kv
        y = inner + cross
        mu = jnp.mean(y, axis=-1, keepdims=True)
        yc = y - mu
        var = jnp.mean(yc * yc, axis=-1, keepdims=True)
        yn = yc * lax.rsqrt(var + LN_EPS)
        gate = g_ref[0, sl, :]
        o_ref[0, sl, :] = ((yn * gng_ref[...] + gnb_ref[...]) * _silu(gate)).astype(o_ref.dtype)


def _ret_tables(S):
    d = RET_HEAD_DIM
    inv = ROPE_BASE ** (-jnp.arange(0, d, 2, dtype=F32) / d)
    ang = jnp.arange(S).astype(F32)[:, None] * inv[None, :]
    cos, sin = jnp.cos(ang), jnp.sin(ang)
    cos2 = jnp.concatenate([cos, cos], axis=-1)
    sin2 = jnp.concatenate([-sin, sin], axis=-1)
    C = RET_CHUNK
    log_gamma = jnp.log1p(-jnp.exp2(-5.0 - jnp.arange(RET_HEADS, dtype=F32)))
    idx = jnp.arange(C, dtype=F32)
    diff = idx[:, None] - idx[None, :]
    dec = jnp.where(diff >= 0, jnp.exp(log_gamma[:, None, None] * jnp.maximum(diff, 0.0)), 0.0)
    kd = jnp.exp(log_gamma[:, None] * (C - 1 - idx)[None, :])
    qd = jnp.exp(log_gamma[:, None] * (idx + 1.0)[None, :])
    cd = jnp.exp(log_gamma * C)
    bc = lambda t: jnp.broadcast_to(t[:, :, None], (RET_HEADS, C, d))
    cdb = jnp.broadcast_to(cd[:, None, None], (RET_HEADS, d, d))
    return cos2, sin2, dec, bc(qd), bc(kd), cdb


def _ret_call(pb, pf, gn_g, gn_b):
    B, S, _ = pb.shape
    cos2, sin2, dec, qd, kd, cd = _ret_tables(S)
    col = lambda off: pl.BlockSpec((1, S, RET_HEAD_DIM), lambda b, h: (b, 0, off + h))
    tab = pl.BlockSpec((1, RET_CHUNK, RET_HEAD_DIM), lambda b, h: (h, 0, 0))
    vec = pl.BlockSpec((1, RET_HEAD_DIM), lambda b, h: (0, h))
    return pl.pallas_call(
        _ret_body,
        grid=(B, RET_HEADS),
        in_specs=[col(0), col(RET_HEADS), col(2 * RET_HEADS), col(0),
                  _resident((S, RET_HEAD_DIM)), _resident((S, RET_HEAD_DIM)),
                  tab, tab, tab, tab, vec, vec],
        out_specs=pl.BlockSpec((1, S, RET_HEAD_DIM), lambda b, h: (b, 0, h)),
        out_shape=jax.ShapeDtypeStruct((B, S, RET_W), BF16),
        compiler_params=_params(("arbitrary", "arbitrary"), VMEM_LIMIT_SMALL),
        name="retention",
    )(pb, pb, pb, pf, cos2, sin2, dec, qd, kd, cd, gn_g.reshape(1, RET_W), gn_b.reshape(1, RET_W))


def _compress_body(xk_ref, xv_ref, pk_ref, pv_ref, w1k_ref, w1v_ref, w2k_ref, w2v_ref, kc_ref, vc_ref):
    n_blk = kc_ref.shape[1]
    for x_ref, p_ref, w1_ref, w2_ref, o_ref in ((xk_ref, pk_ref, w1k_ref, w2k_ref, kc_ref),
                                                (xv_ref, pv_ref, w1v_ref, w2v_ref, vc_ref)):
        first = jnp.zeros((n_blk, w1_ref.shape[-1]), F32)
        second = jnp.zeros((n_blk, w1_ref.shape[-1]), F32)
        for t in range(CMP_STRIDE):
            x = x_ref[0, pl.ds(t, n_blk, stride=CMP_STRIDE), :]
            first = first + _dot((x + p_ref[t:t + 1, :]).astype(BF16), w1_ref[t])
            second = second + _dot((x + p_ref[CMP_STRIDE + t:CMP_STRIDE + t + 1, :]).astype(BF16),
                                   w1_ref[CMP_STRIDE + t])
        hid = _silu(first + pltpu.roll(second, n_blk - 1, 0))
        o_ref[0] = _dot(hid.astype(BF16), w2_ref[...]).astype(o_ref.dtype)


def _compress_call(pf, pos_k, pos_v, w1k, w1v, w2k, w2v):
    B, S, _ = pf.shape
    n_blk = S // CMP_STRIDE
    assert CMP_BLOCK == 2 * CMP_STRIDE

    def both_heads(t):
        z = jnp.zeros_like(t)
        return jnp.concatenate([jnp.concatenate([t, z], -1), jnp.concatenate([z, t], -1)], -2).astype(BF16)

    pos = lambda t: jnp.concatenate([t, t], axis=-1)
    w1 = lambda t: both_heads(t.reshape(CMP_BLOCK, NSA_HEAD_DIM, CMP_HIDDEN))
    kc_block = RET_W // NSA_KVW
    xspec = lambda c: pl.BlockSpec((1, S, NSA_KVW), lambda b: (b, 0, c))
    ospec = pl.BlockSpec((1, n_blk, NSA_KVW), lambda b: (b, 0, 0))
    return pl.pallas_call(
        _compress_body,
        grid=(B,),
        in_specs=[xspec(kc_block), xspec(kc_block + 1),
                  _resident((CMP_BLOCK, NSA_KVW)), _resident((CMP_BLOCK, NSA_KVW)),
                  _resident((CMP_BLOCK, NSA_KVW, 2 * CMP_HIDDEN)), _resident((CMP_BLOCK, NSA_KVW, 2 * CMP_HIDDEN)),
                  _resident((2 * CMP_HIDDEN, NSA_KVW)), _resident((2 * CMP_HIDDEN, NSA_KVW))],
        out_specs=[ospec, ospec],
        out_shape=[jax.ShapeDtypeStruct((B, n_blk, NSA_KVW), BF16)] * 2,
        compiler_params=_params(("arbitrary",), VMEM_LIMIT_SMALL),
        name="nsa_compress",
    )(pf, pf, pos(pos_k), pos(pos_v), w1(w1k), w1(w1v), both_heads(w2k), both_heads(w2v))


CMP_Q_ROWS = 512


def _cmp_attn_body(q_ref, kc_ref, vc_ref, cb_ref, ov_ref, o_ref, sel_ref):
    i = pl.program_id(1)
    tq = q_ref.shape[1]
    lane = lax.broadcasted_iota(jnp.int32, (tq, LANES), 1)
    row = lax.broadcasted_iota(jnp.int32, (tq, LANES), 0)
    lo_half = lane < NSA_HEAD_DIM
    kc = kc_ref[0]
    vc = vc_ref[0]
    zero = jnp.zeros_like(kc)
    kc_lo = lax.broadcasted_iota(jnp.int32, kc.shape, 1) < NSA_HEAD_DIM
    kc_g = (jnp.where(kc_lo, kc, zero), jnp.where(kc_lo, zero, kc))
    valid = (lane * CMP_STRIDE + (CMP_BLOCK - 1) <= i * tq + row)[None]
    q4 = jnp.concatenate([q_ref[0, :, r * LANES:(r + 1) * LANES] for r in range(NSA_GROUP)], axis=0)
    q4 = q4 * (NSA_HEAD_DIM ** -0.5)
    psum, outs = [], []
    for g in range(NSA_KV_HEADS):
        s = _dot_nt(q4, kc_g[g]).reshape(NSA_GROUP, tq, LANES) + cb_ref[g * NSA_GROUP:(g + 1) * NSA_GROUP]
        s = jnp.where(valid, s, NEG_INF)
        m = jnp.max(s, axis=-1, keepdims=True)
        e = jnp.where(valid, jnp.exp2(s - m), 0.0)
        den = jnp.maximum(jnp.sum(e, axis=-1, keepdims=True), 1e-30)
        p = e / den
        psum.append(jnp.sum(p, axis=0))
        outs.append(_dot(p.reshape(NSA_GROUP * tq, LANES).astype(BF16), vc))
    for r in range(NSA_GROUP):
        rs = slice(r * tq, (r + 1) * tq)
        o_ref[0, :, r * LANES:(r + 1) * LANES] = jnp.where(lo_half, outs[0][rs], outs[1][rs])

    n_sel = sel_ref.shape[2]
    blk = lax.broadcasted_iota(jnp.int32, (n_sel, tq), 0)
    qblk = (i * tq + lax.broadcasted_iota(jnp.int32, (n_sel, tq), 1)) // SEL_BLOCK
    forced = jnp.where(blk == 0, 1.0, jnp.where(blk == qblk, 1.0, jnp.where(blk == qblk - 1, 1.0, 0.0)))
    for g in range(NSA_KV_HEADS):
        p_hi, p_lo = _split_bf16(psum[g])
        imp = _dot_nt(ov_ref[...], p_hi) + _dot_nt(ov_ref[...], p_lo)
        score = jnp.where(forced > 0.5, SEL_FORCE_SCORE, jnp.where(blk <= qblk, imp, -1.0))
        rank = jnp.zeros((n_sel, tq), F32)
        for other in range(n_sel):
            so = score[other:other + 1, :]
            tie = jnp.where(blk > other, 1.0, 0.0)
            rank = rank + jnp.where(so > score, 1.0, jnp.where(so == score, tie, 0.0))
        sel_ref[0, g] = jnp.where(rank < float(min(SEL_TOP_N, n_sel)), 1.0, 0.0)


def _cmp_attn_call(pb, kc, vc, cmp_bias, overlap_t):
    B, S, _ = pb.shape
    n_sel = S // SEL_BLOCK
    q_block = (3 * RET_W) // NSA_QW
    tq = CMP_Q_ROWS
    return pl.pallas_call(
        _cmp_attn_body,
        grid=(B, S // tq),
        in_specs=[pl.BlockSpec((1, tq, NSA_QW), lambda b, i: (b, i, q_block)),
                  pl.BlockSpec((1,) + kc.shape[1:], lambda b, i: (b, 0, 0)),
                  pl.BlockSpec((1,) + vc.shape[1:], lambda b, i: (b, 0, 0)),
                  pl.BlockSpec((NSA_HEADS, tq, LANES), lambda b, i: (0, i, 0)),
                  _resident((n_sel, LANES))],
        out_specs=[pl.BlockSpec((1, tq, NSA_QW), lambda b, i: (b, i, 0)),
                   pl.BlockSpec((1, NSA_KV_HEADS, n_sel, tq), lambda b, i: (b, 0, 0, i))],
        out_shape=[jax.ShapeDtypeStruct((B, S, NSA_QW), F32),
                   jax.ShapeDtypeStruct((B, NSA_KV_HEADS, n_sel, S), F32)],
        compiler_params=_params(("arbitrary", "arbitrary"), VMEM_LIMIT_SMALL),
        name="nsa_cmp_attn",
    )(pb, kc, vc, cmp_bias, overlap_t)


SLC_CLASS_TILES = 2


def _slc_far_tiles(cls, n_tiles):
    return max(min(SLC_CLASS_TILES * (cls + 1) - 2, n_tiles - 2), 1)


def _slc_body(q_ref, ks_ref, vs_ref, sel_ref, tag_ref, near_ref, o_ref, kaug_scr, vaug_scr):
    i = pl.program_id(1)
    n_tiles = ks_ref.shape[1] // TILE
    n_sel = sel_ref.shape[2]
    lane = lax.broadcasted_iota(jnp.int32, (TILE, LANES), 1)
    lo_half = lane < NSA_HEAD_DIM

    @pl.when(i == 0)
    def _():
        for t in range(n_tiles):
            sl = pl.ds(t * TILE, TILE)
            kt = ks_ref[0, sl, :]
            zero = jnp.zeros_like(kt)
            kaug_scr[0, sl, :] = jnp.concatenate([jnp.where(lo_half, kt, zero), tag_ref[sl, :]], axis=1)
            kaug_scr[1, sl, :] = jnp.concatenate([jnp.where(lo_half, zero, kt), tag_ref[sl, :]], axis=1)
            vaug_scr[sl, :] = jnp.concatenate([vs_ref[0, sl, :], tag_ref[sl, :]], axis=1)

    prev = jnp.maximum(i - 1, 0)
    own_sl = pl.ds(pl.multiple_of(i * TILE, TILE), TILE)
    prev_sl = pl.ds(pl.multiple_of(prev * TILE, TILE), TILE)
    no_prev = jnp.where(lax.broadcasted_iota(jnp.int32, (1, 2 * TILE), 1) < TILE,
                        jnp.where(i == 0, NEG_INF, 0.0), 0.0)
    tile_pen = jnp.where(lane - n_sel < i - 1, 0.0, NEG_INF)
    sel_pad = jnp.zeros((LANES - n_sel, TILE), F32)

    def tile_body(n_far):
        wf = n_far * TILE
        v_near = jnp.concatenate([vaug_scr[prev_sl, :], vaug_scr[own_sl, :]], axis=0)
        q4 = jnp.concatenate([q_ref[0, :, r * LANES:(r + 1) * LANES] for r in range(NSA_GROUP)], axis=0)
        q4 = q4 * (NSA_HEAD_DIM ** -0.5)
        q_far, q_near, k_near = [], [], []
        for g in range(NSA_KV_HEADS):
            sel_q = jnp.concatenate([sel_ref[0, g], sel_pad], axis=0).T
            blk_pen = (sel_q - 1.0) * (-NEG_INF)
            pen_near = jnp.where(lane < n_sel, blk_pen, 0.0).astype(BF16)
            pen_far = jnp.where(lane < n_sel, blk_pen,
                                jnp.where(lane < n_sel + n_tiles, tile_pen, 0.0)).astype(BF16)
            q_far.append(jnp.concatenate([q4, jnp.concatenate([pen_far] * NSA_GROUP, axis=0)], axis=1))
            q_near.append(jnp.concatenate([q4, jnp.concatenate([pen_near] * NSA_GROUP, axis=0)], axis=1))
            k_near.append(jnp.concatenate([kaug_scr[g, prev_sl, :], kaug_scr[g, own_sl, :]], axis=0))

        n_split = NSA_CHAIN_SPLIT
        rows = NSA_GROUP * TILE // n_split
        chains = [(g, part) for g in range(NSA_KV_HEADS) for part in range(n_split)]

        def scores(g, part):
            rs = slice(part * rows, (part + 1) * rows)
            return (_dot_nt(q_far[g][rs], kaug_scr[g, :wf, :]),
                    _dot_nt(q_near[g][rs], k_near[g]) + (near_ref[g, rs, :] + no_prev))

        def weights(s_far, s_near):
            m = jnp.maximum(jnp.max(s_far, axis=-1, keepdims=True), jnp.max(s_near, axis=-1, keepdims=True))
            return jnp.exp2(s_far - m).astype(BF16), jnp.exp2(s_near - m).astype(BF16)

        def values(p_far, p_near):
            acc = _dot(p_far, vaug_scr[:wf, :]) + _dot(p_near, v_near)
            return acc[:, :NSA_KVW] / acc[:, NSA_KVW + LANES - 1:]

        s, p, o = {}, {}, {}
        for step in range(len(chains) + 2):
            if step < len(chains):
                s[step] = scores(*chains[step])
            if 0 <= step - 2 < len(chains):
                o[step - 2] = values(*p.pop(step - 2))
            if 0 <= step - 1 < len(chains):
                p[step - 1] = weights(*s.pop(step - 1))
        for r in range(NSA_GROUP):
            part, rs = r // (NSA_GROUP // n_split), slice((r % (NSA_GROUP // n_split)) * TILE,
                                                          (r % (NSA_GROUP // n_split) + 1) * TILE)
            o_ref[0, :, r * LANES:(r + 1) * LANES] = jnp.where(lo_half, o[part][rs], o[n_split + part][rs])

    n_classes = -(-n_tiles // SLC_CLASS_TILES)
    for cls in range(n_classes):
        pl.when(i // SLC_CLASS_TILES == cls)(functools.partial(tile_body, _slc_far_tiles(cls, n_tiles)))


def _slc_call(pb, sel_t, key_tags, near):
    B, S, _ = pb.shape
    n_sel = S // SEL_BLOCK
    n_tiles = S // TILE
    assert n_sel + n_tiles < LANES
    q_block = (3 * RET_W) // NSA_QW
    ks_block = (3 * RET_W + NSA_QW) // NSA_KVW
    return pl.pallas_call(
        _slc_body,
        grid=(B, n_tiles),
        in_specs=[pl.BlockSpec((1, TILE, NSA_QW), lambda b, i: (b, i, q_block)),
                  pl.BlockSpec((1, S, NSA_KVW), lambda b, i: (b, 0, ks_block)),
                  pl.BlockSpec((1, S, NSA_KVW), lambda b, i: (b, 0, ks_block + 1)),
                  pl.BlockSpec((1, NSA_KV_HEADS, n_sel, TILE), lambda b, i: (b, 0, 0, i)),
                  _resident(key_tags.shape), _resident(near.shape)],
        out_specs=pl.BlockSpec((1, TILE, NSA_QW), lambda b, i: (b, i, 0)),
        out_shape=jax.ShapeDtypeStruct((B, S, NSA_QW), F32),
        scratch_shapes=[pltpu.VMEM((NSA_KV_HEADS, S, NSA_KVW + LANES), BF16),
                        pltpu.VMEM((S, NSA_KVW + LANES), BF16)],
        compiler_params=_params(("arbitrary", "arbitrary"), VMEM_LIMIT_LARGE),
        name="nsa_selected",
    )(pb, pb, pb, sel_t, key_tags, near)


WIN_PREV_TILES = (WIN_SIZE - 1 + TILE - 1) // TILE
WIN_Q_TILES = 2


def _win_body(q_ref, k_ref, v_ref, wb_ref, gate_ref, eg_ref, ocmp_ref, oslc_ref, y_ref):
    i = pl.program_id(1)
    tq = q_ref.shape[1]
    n_span = WIN_PREV_TILES + WIN_Q_TILES
    first = i * WIN_Q_TILES - WIN_PREV_TILES
    lo_half = lax.broadcasted_iota(jnp.int32, (tq, LANES), 1) < NSA_HEAD_DIM

    def attend(early):
        if early:
            k_tiles, v_tiles, negs = [], [], []
            for t in range(n_span):
                sl = pl.ds(pl.multiple_of(jnp.maximum(first + t, 0) * TILE, TILE), TILE)
                k_tiles.append(k_ref[0, sl, :])
                v_tiles.append(v_ref[0, sl, :])
                negs.append(jnp.full((tq, TILE), jnp.where(first + t < 0, NEG_INF, 0.0), F32))
            k_all = jnp.concatenate(k_tiles, axis=0)
            v_all = jnp.concatenate(v_tiles, axis=0)
            missing = jnp.concatenate(negs, axis=1)[None]
        else:
            sl = pl.ds(pl.multiple_of(first * TILE, TILE), n_span * TILE)
            k_all = k_ref[0, sl, :]
            v_all = v_ref[0, sl, :]
        zero = jnp.zeros_like(k_all)
        k_lo = lax.broadcasted_iota(jnp.int32, k_all.shape, 1) < NSA_HEAD_DIM
        k_g = (jnp.where(k_lo, k_all, zero), jnp.where(k_lo, zero, k_all))
        v_ones = jnp.concatenate([v_all, jnp.ones_like(v_all)], axis=1)

        g_hi, g_lo = _split_bf16(jax.nn.sigmoid(gate_ref[0]))
        gates = _dot(g_hi, eg_ref[...]) + _dot(g_lo, eg_ref[...])
        q4 = jnp.concatenate([q_ref[0, :, r * LANES:(r + 1) * LANES] for r in range(NSA_GROUP)], axis=0)
        q4 = q4 * (NSA_HEAD_DIM ** -0.5)
        n_split = NSA_CHAIN_SPLIT
        heads = NSA_GROUP // n_split
        chains = [(g, part) for g in range(NSA_KV_HEADS) for part in range(n_split)]

        def scores(g, part):
            s = _dot_nt(q4[part * heads * tq:(part + 1) * heads * tq], k_g[g]).reshape(heads, tq, n_span * TILE)
            s = s + wb_ref[g * NSA_GROUP + part * heads:g * NSA_GROUP + (part + 1) * heads]
            return s + missing if early else s

        def weights(s):
            m = jnp.max(s, axis=-1, keepdims=True)
            return jnp.exp2(s - m).astype(BF16).reshape(heads * tq, n_span * TILE)

        def values(p):
            acc = _dot(p, v_ones)
            return acc[:, :NSA_KVW] / acc[:, NSA_KVW:NSA_KVW + 1]

        s, p, o = {}, {}, {}
        for step in range(len(chains) + 2):
            if step < len(chains):
                s[step] = scores(*chains[step])
            if 0 <= step - 2 < len(chains):
                o[step - 2] = values(p.pop(step - 2))
            if 0 <= step - 1 < len(chains):
                p[step - 1] = weights(s.pop(step - 1))
        for r in range(NSA_GROUP):
            cols = slice(r * LANES, (r + 1) * LANES)
            part, rs = r // heads, slice((r % heads) * tq, (r % heads + 1) * tq)
            o_win = jnp.where(lo_half, o[part][rs], o[n_split + part][rs])
            y = (gates[:, r * LANES:(r + 1) * LANES] * ocmp_ref[0, :, cols]
                 + gates[:, NSA_QW + r * LANES:NSA_QW + (r + 1) * LANES] * oslc_ref[0, :, cols]
                 + gates[:, 2 * NSA_QW + r * LANES:2 * NSA_QW + (r + 1) * LANES] * o_win)
            y_ref[0, :, cols] = y.astype(y_ref.dtype)

    pl.when(first >= 0)(functools.partial(attend, False))
    pl.when(first < 0)(functools.partial(attend, True))


def _win_call(pb, pf, win_bias, gate_expand, o_cmp, o_slc):
    B, S, _ = pb.shape
    q_block = (3 * RET_W) // NSA_QW
    kw_block = (3 * RET_W + NSA_QW) // NSA_KVW + 2
    gate_block = (RET_W + 2 * NSA_KVW) // LANES
    tq = WIN_Q_TILES * TILE
    tile_spec = pl.BlockSpec((1, tq, NSA_QW), lambda b, i: (b, i, 0))
    return pl.pallas_call(
        _win_body,
        grid=(B, S // tq),
        in_specs=[pl.BlockSpec((1, tq, NSA_QW), lambda b, i: (b, i, q_block)),
                  pl.BlockSpec((1, S, NSA_KVW), lambda b, i: (b, 0, kw_block)),
                  pl.BlockSpec((1, S, NSA_KVW), lambda b, i: (b, 0, kw_block + 1)),
                  _resident(win_bias.shape),
                  pl.BlockSpec((1, tq, LANES), lambda b, i: (b, i, gate_block)),
                  _resident((LANES, 3 * NSA_QW)),
                  tile_spec, tile_spec],
        out_specs=tile_spec,
        out_shape=jax.ShapeDtypeStruct((B, S, NSA_QW), BF16),
        compiler_params=_params(("arbitrary", "arbitrary"), VMEM_LIMIT_LARGE),
        name="nsa_window_combine",
    )(pb, pb, pb, win_bias, pf, gate_expand, o_cmp, o_slc)


def _post_body(h_ref, ma_ref, mb_ref, mod_ref, woa_ref, wob_ref, lng_ref, lnb_ref,
               wg_ref, wu_ref, wd_ref, o_ref):
    y = _dot(ma_ref[0], woa_ref[...]) + _dot(mb_ref[0], wob_ref[...])
    h1 = _layer_norm(DEEPNORM_ALPHA * h_ref[0] + mod_ref[0, 2:3, :] * y, lng_ref[0:1, :], lnb_ref[0:1, :])
    u = (h1 * (1.0 + mod_ref[0, 4:5, :]) + mod_ref[0, 3:4, :]).astype(BF16)
    acc = jnp.zeros(h1.shape, F32)
    for c0, cn in FFN_CHUNKS:
        gate = _dot(u, wg_ref[:, c0:c0 + cn])
        up = _dot(u, wu_ref[:, c0:c0 + cn])
        acc = acc + _dot((_silu(gate) * up).astype(BF16), wd_ref[c0:c0 + cn, :])
    o_ref[0] = _layer_norm(DEEPNORM_ALPHA * h1 + mod_ref[0, 5:6, :] * acc, lng_ref[1:2, :], lnb_ref[1:2, :])


def _post_call(h, mix_a, mix_b, col_a, col_b, mod, wo_a, wo_b, ln_g, ln_b, w_gate, w_up, w_down, tm, name):
    B, S, _ = h.shape
    half = D_MODEL // 2
    tok = pl.BlockSpec((1, tm, D_MODEL), lambda b, t: (b, t, 0))
    return pl.pallas_call(
        _post_body,
        grid=(B, S // tm),
        in_specs=[tok,
                  pl.BlockSpec((1, tm, half), lambda b, t: (b, t, col_a)),
                  pl.BlockSpec((1, tm, half), lambda b, t: (b, t, col_b)),
                  pl.BlockSpec((1, 6, D_MODEL), lambda b, t: (b, 0, 0)),
                  _resident((half, D_MODEL)), _resident((half, D_MODEL)),
                  _resident((2, D_MODEL)), _resident((2, D_MODEL)),
                  _resident((D_MODEL, D_FF)), _resident((D_MODEL, D_FF)), _resident((D_FF, D_MODEL))],
        out_specs=tok,
        out_shape=jax.ShapeDtypeStruct((B, S, D_MODEL), F32),
        compiler_params=_params(("arbitrary", "arbitrary"), VMEM_LIMIT_LARGE),
        name=name,
    )(h, mix_a, mix_b, mod, wo_a, wo_b, ln_g, ln_b, w_gate, w_up, w_down)


DIL_STEP_TILES = 4
DIL_WEIGHT_PARTS = 2


def _dil_body(q_ref, k_ref, kp_ref, v_ref, vp_ref, b_ref, o_ref, lse_ref, *, tiles_per_seg, between=None):
    t = pl.program_id(1)
    res_tiles = min(tiles_per_seg, DIL_STEP_TILES)
    with_prev = tiles_per_seg > 1
    if tiles_per_seg > DIL_STEP_TILES:
        first_prev = jnp.where((t * DIL_STEP_TILES) % tiles_per_seg == 0, NEG_INF, 0.0)
    else:
        first_prev = NEG_INF

    def span(ref, before_ref, j, cols):
        if not with_prev:
            return ref[0, j * TILE:(j + 1) * TILE, cols]
        if j > 0:
            return ref[0, (j - 1) * TILE:(j + 1) * TILE, cols]
        return jnp.concatenate([before_ref[0, :, cols], ref[0, :TILE, cols]], axis=0)

    def tile_bias(h, j):
        if not with_prev:
            return b_ref[h, TILE:, :]
        if j % res_tiles != 0:
            return b_ref[h]
        gone = first_prev if j == 0 else NEG_INF
        return jnp.concatenate([b_ref[h, :TILE, :] + gone, b_ref[h, TILE:, :]], axis=0)

    head_cols = lambda h: slice(h * DIL_HEAD_DIM, (h + 1) * DIL_HEAD_DIM)
    pad = jnp.zeros((LANES - DIL_HEADS, TILE), F32)

    def scores(j):
        return jnp.concatenate(
            [_dot_nt(span(k_ref, kp_ref, j, head_cols(h)), q_ref[0, j * TILE:(j + 1) * TILE, head_cols(h)])
             + tile_bias(h, j) for h in range(DIL_HEADS)], axis=1)

    def weights(s):
        m = jnp.max(s, axis=0, keepdims=True)
        e = jnp.exp2(s - m)
        den = jnp.sum(e, axis=0, keepdims=True)
        lse = (m + jnp.log2(den)) * math.log(2.0)
        return (e * (1.0 / den)).astype(BF16), lse

    def values(j, p):
        for h in range(DIL_HEADS):
            o = _dot_tn(p[:, h * TILE:(h + 1) * TILE], span(v_ref, vp_ref, j, head_cols(h)))
            o_ref[0, j * TILE:(j + 1) * TILE, head_cols(h)] = o.astype(o_ref.dtype)

    s, p = {}, {}
    for step in range(DIL_STEP_TILES + 2):
        if step < DIL_STEP_TILES:
            s[step] = scores(step)
        if 0 <= step - 2 < DIL_STEP_TILES:
            values(step - 2, p.pop(step - 2))
        j = step - 1
        parts = []
        for part in range(DIL_WEIGHT_PARTS):
            if between is not None:
                between[step * DIL_WEIGHT_PARTS + part]()
            if 0 <= j < DIL_STEP_TILES:
                slab = DIL_HEADS * TILE // DIL_WEIGHT_PARTS
                parts.append(weights(s[j][:, part * slab:(part + 1) * slab]))
        if parts:
            del s[j]
            p[j] = jnp.concatenate([pp for pp, _ in parts], axis=1)
            lse = jnp.concatenate([ll for _, ll in parts], axis=1)
            by_head = [lse[:, h * TILE:(h + 1) * TILE] for h in range(DIL_HEADS)]
            lse_ref[0, j * TILE:(j + 1) * TILE, :] = jnp.concatenate(by_head + [pad], axis=0).T


def _dil_proj_body(q_ref, k_ref, kp_ref, v_ref, vp_ref, b_ref, h_ref, mod_ref, w_ref,
                   o_ref, lse_ref, proj_ref, h_scr, *, tiles_per_seg, dilation):
    n_pieces = (DIL_STEP_TILES + 2) * DIL_WEIGHT_PARTS
    width = proj_ref.shape[-1] // n_pieces
    tm = h_ref.shape[1]
    state = {}

    def piece(c):
        if c == 0:
            for lc in range(h_scr.shape[0]):
                h_scr[lc] = h_ref[0, :, lc * LANES:(lc + 1) * LANES]
            h = jnp.concatenate(
                [jnp.concatenate([h_scr[lc, pl.ds(r, tm // dilation, stride=dilation), :]
                                  for lc in range(h_scr.shape[0])], axis=1) for r in range(dilation)], axis=0)
            state["u"] = (h * (1.0 + mod_ref[0, 1:2, :]) + mod_ref[0, 0:1, :]).astype(BF16)
        cols = slice(c * width, (c + 1) * width)
        res = _dot(state["u"], w_ref[:, cols]).astype(proj_ref.dtype)
        proj_ref[0, :, :, cols] = res.reshape(dilation, tm // dilation, width)

    _dil_body(q_ref, k_ref, kp_ref, v_ref, vp_ref, b_ref, o_ref, lse_ref, tiles_per_seg=tiles_per_seg,
              between=[functools.partial(piece, c) for c in range(n_pieces)])


def _dil_proj_call(proj, bias, dilation, h, mod, w_next, dilation_next, name):
    B, S, _ = proj.shape
    step = DIL_STEP_TILES * TILE
    n_next = w_next.shape[1]
    own = lambda c: pl.BlockSpec((1, step, DIL_WIDTH), lambda b, i: (b, i, c))
    before = lambda c: pl.BlockSpec((1, TILE, DIL_WIDTH),
                                    lambda b, i: (b, jnp.maximum(i * DIL_STEP_TILES - 1, 0), c))
    tile = lambda width: pl.BlockSpec((1, step, width), lambda b, i: (b, i, 0))
    o, lse, proj_next = pl.pallas_call(
        functools.partial(_dil_proj_body, tiles_per_seg=(S // dilation) // TILE, dilation=dilation_next),
        grid=(B, S // step),
        in_specs=[own(0), own(1), before(1), own(2), before(2), _resident(bias.shape),
                  tile(D_MODEL), pl.BlockSpec((1, 6, D_MODEL), lambda b, i: (b, 0, 0)),
                  _resident((D_MODEL, n_next))],
        out_specs=[tile(DIL_WIDTH), tile(LANES),
                   pl.BlockSpec((1, dilation_next, step // dilation_next, n_next), lambda b, i: (b, 0, i, 0))],
        out_shape=[jax.ShapeDtypeStruct((B, S, DIL_WIDTH), BF16), jax.ShapeDtypeStruct((B, S, LANES), F32),
                   jax.ShapeDtypeStruct((B, dilation_next, S // dilation_next, n_next), BF16)],
        scratch_shapes=[pltpu.VMEM((D_MODEL // LANES, step, LANES), F32)],
        compiler_params=_params(("arbitrary", "arbitrary"), VMEM_LIMIT_LARGE),
        name=name,
    )(proj, proj, proj, proj, proj, bias, h, mod, w_next)
    return o, lse, proj_next.reshape(B, S, n_next)


def _dil_call(proj, bias, dilation, name):
    B, S, _ = proj.shape
    step = DIL_STEP_TILES * TILE
    own = lambda c: pl.BlockSpec((1, step, DIL_WIDTH), lambda b, i: (b, i, c))
    before = lambda c: pl.BlockSpec((1, TILE, DIL_WIDTH),
                                    lambda b, i: (b, jnp.maximum(i * DIL_STEP_TILES - 1, 0), c))
    tile = lambda width: pl.BlockSpec((1, step, width), lambda b, i: (b, i, 0))
    return pl.pallas_call(
        functools.partial(_dil_body, tiles_per_seg=(S // dilation) // TILE),
        grid=(B, S // step),
        in_specs=[own(0), own(1), before(1), own(2), before(2), _resident(bias.shape)],
        out_specs=[tile(DIL_WIDTH), tile(LANES)],
        out_shape=[jax.ShapeDtypeStruct((B, S, DIL_WIDTH), BF16), jax.ShapeDtypeStruct((B, S, LANES), F32)],
        compiler_params=_params(("arbitrary", "arbitrary"), VMEM_LIMIT_LARGE),
        name=name,
    )(proj, proj, proj, proj, proj, bias)


def _post_mix_body(h_ref, o0_ref, o1_ref, o2_ref, l0_ref, l1_ref, l2_ref, mod_ref, wo_ref, lng_ref, lnb_ref,
                   wg_ref, wu_ref, wd_ref, out_ref, mix_a, mix_b, o_scr, l_scr):
    n = pl.program_id(0)
    o_refs = (o0_ref, o1_ref, o2_ref)
    l_refs = (l0_ref, l1_ref, l2_ref)
    tm = out_ref.shape[1]

    @pl.when(n == 0)
    def _():
        mix_a[...] = jnp.zeros_like(mix_a)
        mix_b[...] = jnp.zeros_like(mix_b)

    def unpermute():
        for gi, (o_ref, l_ref) in enumerate(zip(o_refs, l_refs)):
            dilation = o_ref.shape[1]
            for r in range(dilation):
                rows = pl.ds(r, tm // dilation, stride=dilation)
                l_scr[gi, rows, :] = l_ref[0, r]
                for h in range(DIL_HEADS):
                    o_scr[gi, h, rows, :] = o_ref[0, r, :, h * DIL_HEAD_DIM:(h + 1) * DIL_HEAD_DIM].astype(F32)

    def mix_head(h, mix_w):
        lses = [jnp.broadcast_to(l_scr[gi, :, h:h + 1], (tm, DIL_HEAD_DIM)) for gi in range(len(o_refs))]
        m = jnp.maximum(jnp.maximum(lses[0], lses[1]), lses[2])
        ws = [jnp.exp(l - m) for l in lses]
        den = ws[0] + ws[1] + ws[2]
        y = sum((w / den) * o_scr[gi, h] for gi, w in enumerate(ws))
        mix_w[:, h * DIL_HEAD_DIM:(h + 1) * DIL_HEAD_DIM] = y.astype(mix_w.dtype)

    def step(mix_w, mix_r):
        y = _dot(mix_r[...], wo_ref[...])
        unpermute()
        h1 = _layer_norm(DEEPNORM_ALPHA * h_ref[0] + mod_ref[0, 2:3, :] * y, lng_ref[0:1, :], lnb_ref[0:1, :])
        u = (h1 * (1.0 + mod_ref[0, 4:5, :]) + mod_ref[0, 3:4, :]).astype(BF16)
        acc = jnp.zeros(h1.shape, F32)
        for ci, (c0, cn) in enumerate(FFN_CHUNKS):
            gate = _dot(u, wg_ref[:, c0:c0 + cn])
            up = _dot(u, wu_ref[:, c0:c0 + cn])
            for h in range(DIL_HEADS):
                if h * len(FFN_CHUNKS) // DIL_HEADS == ci:
                    mix_head(h, mix_w)
            acc = acc + _dot((_silu(gate) * up).astype(BF16), wd_ref[c0:c0 + cn, :])
        out_ref[0] = _layer_norm(DEEPNORM_ALPHA * h1 + mod_ref[0, 5:6, :] * acc, lng_ref[1:2, :], lnb_ref[1:2, :])

    pl.when(n % 2 == 0)(functools.partial(step, mix_a, mix_b))
    pl.when(n % 2 == 1)(functools.partial(step, mix_b, mix_a))


def _post_mix_call(h, outs, lses, mod, wo, ln_g, ln_b, w_gate, w_up, w_down, tm, name):
    B, S, _ = h.shape
    steps = S // tm
    last = B * steps - 1
    dilations = [d for _, d in DIL_PATTERNS]
    by_residue = lambda t, d: t.reshape(B, d, S // d, t.shape[-1])
    cur = lambda n: jnp.minimum(n, last)
    done = lambda n: jnp.maximum(n - 1, 0)
    src = lambda d, width: pl.BlockSpec((1, d, tm // d, width), lambda n: (cur(n) // steps, 0, cur(n) % steps, 0))
    tok = pl.BlockSpec((1, tm, D_MODEL), lambda n: (done(n) // steps, done(n) % steps, 0))
    return pl.pallas_call(
        _post_mix_body,
        grid=(B * steps + 1,),
        in_specs=[tok] + [src(d, DIL_WIDTH) for d in dilations] + [src(d, LANES) for d in dilations]
                 + [pl.BlockSpec((1, 6, D_MODEL), lambda n: (done(n) // steps, 0, 0)),
                    _resident((D_MODEL, D_MODEL)), _resident((2, D_MODEL)), _resident((2, D_MODEL)),
                    _resident((D_MODEL, D_FF)), _resident((D_MODEL, D_FF)), _resident((D_FF, D_MODEL))],
        out_specs=tok,
        out_shape=jax.ShapeDtypeStruct((B, S, D_MODEL), F32),
        scratch_shapes=[pltpu.VMEM((tm, D_MODEL), BF16), pltpu.VMEM((tm, D_MODEL), BF16),
                        pltpu.VMEM((len(dilations), DIL_HEADS, tm, DIL_HEAD_DIM), F32),
                        pltpu.VMEM((len(dilations), tm, LANES), F32)],
        compiler_params=_params(("arbitrary",), VMEM_LIMIT_LARGE),
        name=name,
    )(h, *[by_residue(o, d) for o, d in zip(outs, dilations)], *[by_residue(l, d) for l, d in zip(lses, dilations)],
      mod, wo, ln_g, ln_b, w_gate, w_up, w_down)


def _t5_bucket_np(dist):
    n = np.maximum(dist, 0)
    max_exact = REL_BUCKETS // 2
    nf = np.maximum(n, 1).astype(np.float64)
    val = np.log(nf / max_exact) / math.log(REL_MAX_DIST / max_exact) * (REL_BUCKETS - max_exact)
    frac = np.abs(val - np.round(val))
    on_edge = (frac < 1e-9) & (n > max_exact) & (n < REL_MAX_DIST)
    assert not on_edge.any()
    large = np.minimum(max_exact + np.floor(val + 1e-9).astype(np.int64), REL_BUCKETS - 1)
    return np.where(n < max_exact, n, large).astype(np.int32)


def _shift_table(rel_bias, rows, cols, step, dist_fn, valid_fn):
    u = np.concatenate([np.arange(cols), np.arange(-(rows - 1) * step, 0)])
    period = u.size
    vals = jnp.take(rel_bias, jnp.asarray(_t5_bucket_np(dist_fn(u))), axis=0).T
    vals = jnp.where(jnp.asarray(valid_fn(u))[None], vals, NEG_INF)
    t = jnp.tile(vals, (1, rows))[:, :rows * (period - step)].reshape(vals.shape[0], rows, period - step)
    return t[:, :, :cols]


def _nsa_tables(rel_bias, S):
    rel_bias = rel_bias * LOG2_E
    always = lambda u: np.ones(u.shape, bool)
    win_dist = lambda u: WIN_PREV_TILES * TILE - u
    win_bias = _shift_table(rel_bias, WIN_Q_TILES * TILE, (WIN_PREV_TILES + WIN_Q_TILES) * TILE, 1, win_dist,
                            lambda u: (win_dist(u) >= 0) & (win_dist(u) <= WIN_SIZE - 1))
    d0 = _shift_table(rel_bias, TILE, TILE, 1, lambda u: -u, lambda u: u <= 0)
    d1 = _shift_table(rel_bias, TILE, TILE, 1, lambda u: TILE - u, always)
    far_bucket = _t5_bucket_np(np.arange(TILE + 1, S + TILE))
    assert (far_bucket == far_bucket[0]).all()
    far = rel_bias[int(far_bucket[0])][:, None, None]
    near = jnp.concatenate([d1 - far, d0 - far], axis=2)
    near = near.reshape(NSA_KV_HEADS, NSA_GROUP * TILE, 2 * TILE)
    cmp_bias = _shift_table(rel_bias, LANES, S, CMP_STRIDE, lambda u: u - (CMP_BLOCK - 1), always)
    cmp_bias = cmp_bias.transpose(0, 2, 1)
    n_cmp = (S - CMP_BLOCK) // CMP_STRIDE + 1
    n_sel = S // SEL_BLOCK
    cs = (np.arange(n_cmp) * CMP_STRIDE)[:, None]
    ss = (np.arange(n_sel) * SEL_BLOCK)[None, :]
    ov = np.clip(np.minimum(cs + CMP_BLOCK, ss + SEL_BLOCK) - np.maximum(cs, ss), 0, None) / CMP_BLOCK
    ov_t = np.zeros((n_sel, LANES), np.float32)
    ov_t[:, :n_cmp] = ov.T
    key = np.arange(S)[:, None]
    lane = np.arange(LANES)[None, :]
    tags = ((lane == key // SEL_BLOCK) | (lane == n_sel + key // TILE) | (lane == LANES - 1)).astype(np.float32)
    eg = np.zeros((LANES, 3 * NSA_QW), np.float32)
    for g in range(NSA_KV_HEADS):
        for r in range(NSA_GROUP):
            for j in range(3):
                base = j * NSA_QW + r * LANES + g * NSA_HEAD_DIM
                eg[g * NSA_GROUP * 3 + r * 3 + j, base:base + NSA_HEAD_DIM] = 1.0
    return win_bias, near, cmp_bias, jnp.asarray(ov_t, BF16), jnp.asarray(tags, BF16), jnp.asarray(eg, BF16)


def _dil_bias(rel_bias, dilation, max_dist):
    dist = lambda u: TILE + u
    return _shift_table(rel_bias * LOG2_E, 2 * TILE, TILE, 1, lambda u: dist(u) * dilation,
                        lambda u: (dist(u) >= 0) & (dist(u) <= max_dist))


def _nsa_head_perm():
    perm = np.zeros(NSA_QW, np.int64)
    for r in range(NSA_GROUP):
        for g in range(NSA_KV_HEADS):
            new = r * LANES + g * NSA_HEAD_DIM
            old = (g * NSA_GROUP + r) * NSA_HEAD_DIM
            perm[new:new + NSA_HEAD_DIM] = np.arange(old, old + NSA_HEAD_DIM)
    return perm


def _layer0_mixer(h, mod, ab_w_in, rel_bias, gn_g, gn_b, pos_k, pos_v, w1k, w2k, w1v, w2v):
    B, S, _ = h.shape
    o = np.cumsum((0, RET_W, RET_W, RET_W, RET_W, NSA_QW) + (NSA_KVW,) * 6 + (3 * NSA_HEADS,))
    seg = lambda a: ab_w_in[:, o[a]:o[a + 1]]
    gate_w = jnp.pad(seg(11), ((0, 0), (0, LANES - 3 * NSA_HEADS)))
    q_nsa = seg(4)[:, _nsa_head_perm()] * LOG2_E
    w = jnp.concatenate([seg(0), seg(1), seg(2), q_nsa, seg(7), seg(8), seg(9), seg(10),
                         seg(3), seg(5), seg(6), gate_w], axis=1).astype(BF16)
    pb, pf = _pre_call(h, mod, w, ((PRE0_BF16_COLS, BF16), (PRE0_F32_COLS, F32)), 512, "pre0")

    y_ret = _ret_call(pb, pf, gn_g, gn_b)

    win_bias, near, cmp_bias, ov_t, key_tags, eg = _nsa_tables(rel_bias, S)
    kc, vc = _compress_call(pf, pos_k, pos_v, w1k, w1v, w2k, w2v)
    o_cmp, sel_t = _cmp_attn_call(pb, kc, vc, cmp_bias, ov_t)
    o_slc = _slc_call(pb, sel_t, key_tags, near)
    y_nsa = _win_call(pb, pf, win_bias, eg, o_cmp, o_slc)
    return y_ret, y_nsa


def _layer1_mixer(h, mod, dil_w_in, rel_bias):
    B, S, _ = h.shape
    def group_weights(gi):
        w = dil_w_in[:, gi * 3 * DIL_WIDTH:(gi + 1) * 3 * DIL_WIDTH]
        return jnp.concatenate([w[:, :DIL_WIDTH] * (DIL_HEAD_DIM ** -0.5 * LOG2_E), w[:, DIL_WIDTH:]],
                               axis=1).astype(BF16)

    outs, lses = [], []
    proj, = _pre_call(h, mod, group_weights(0), ((3 * DIL_WIDTH, BF16),), 512, "pre1_0", DIL_PATTERNS[0][1])
    for gi, (window, dilation) in enumerate(DIL_PATTERNS):
        bias = _dil_bias(rel_bias, dilation, window // dilation)
        if gi + 1 < len(DIL_PATTERNS):
            o, lse, proj = _dil_proj_call(proj, bias, dilation, h, mod, group_weights(gi + 1),
                                          DIL_PATTERNS[gi + 1][1], f"dilated_{gi}_pre1_{gi + 1}")
        else:
            o, lse = _dil_call(proj, bias, dilation, f"dilated_{gi}")
        outs.append(o)
        lses.append(lse)
    return outs, lses


def kernel(x, c, rel_bias, ada_w, ada_b, ln_g, ln_b, ab_w_in, ab_w_out, ret_gn_g, ret_gn_b, cmp_pos_k, cmp_pos_v, cmp_k_w1, cmp_k_w2, cmp_v_w1, cmp_v_w2, dil_w_in, dil_w_out, ffn_w_gate, ffn_w_up, ffn_w_down):
    B = x.shape[0]
    mod = _ada_call(c, ada_w, ada_b).reshape(DEPTH, B, 6, D_MODEL)
    h = x
    for layer in range(DEPTH):
        i = layer // 2
        if layer % 2 == 0:
            mix_a, mix_b = _layer0_mixer(h, mod[layer], ab_w_in[i], rel_bias, ret_gn_g[i], ret_gn_b[i],
                                         cmp_pos_k[i], cmp_pos_v[i], cmp_k_w1[i], cmp_k_w2[i],
                                         cmp_v_w1[i], cmp_v_w2[i])
            wo_a = ab_w_out[i, :RET_W]
            wo_b = ab_w_out[i, RET_W:][_nsa_head_perm()]
            h = _post_call(h, mix_a, mix_b, 0, 0, mod[layer], wo_a.astype(BF16), wo_b.astype(BF16),
                           ln_g[layer], ln_b[layer], ffn_w_gate[layer].astype(BF16), ffn_w_up[layer].astype(BF16),
                           ffn_w_down[layer].astype(BF16), 512, f"post{layer}")
        else:
            outs, lses = _layer1_mixer(h, mod[layer], dil_w_in[i], rel_bias)
            h = _post_mix_call(h, outs, lses, mod[layer], dil_w_out[i].astype(BF16), ln_g[layer], ln_b[layer],
                               ffn_w_gate[layer].astype(BF16), ffn_w_up[layer].astype(BF16),
                               ffn_w_down[layer].astype(BF16), 512, f"post{layer}")
    return h
```

```python
import functools
import math

import numpy as np
import jax
import jax.numpy as jnp
from jax import lax
from jax.experimental import pallas as pl
from jax.experimental.pallas import tpu as pltpu

F32 = jnp.float32
BF16 = jnp.bfloat16

D_MODEL = 1024
DEPTH = 2
DEEPNORM_ALPHA = (2 * DEPTH) ** 0.25
LN_EPS = 1e-5
NEG_INF = -1e30
LOG2_E = math.log2(math.e)

RET_HEADS = 4
RET_HEAD_DIM = 128
RET_CHUNK = 128
ROPE_BASE = 10000.0
RET_W = RET_HEADS * RET_HEAD_DIM

NSA_HEADS = 8
NSA_KV_HEADS = 2
NSA_GROUP = 4
NSA_HEAD_DIM = 64
CMP_BLOCK = 32
CMP_STRIDE = 16
CMP_HIDDEN = 256
SEL_BLOCK = 64
SEL_TOP_N = 16
SEL_FORCE_SCORE = 1e4
WIN_SIZE = 512
NSA_QW = NSA_HEADS * NSA_HEAD_DIM
NSA_KVW = NSA_KV_HEADS * NSA_HEAD_DIM

DIL_PATTERNS = ((128, 1), (512, 4), (2048, 16))
DIL_HEADS = 8
DIL_HEAD_DIM = 128
DIL_WIDTH = DIL_HEADS * DIL_HEAD_DIM

REL_BUCKETS = 32
REL_MAX_DIST = 128
D_FF = 2816

LANES = 128
TILE = 128
VMEM_LIMIT_SMALL = 32 * 1024 * 1024
VMEM_LIMIT_LARGE = 56 * 1024 * 1024

PRE0_BF16_COLS = 3 * RET_W + NSA_QW + 4 * NSA_KVW
PRE0_F32_COLS = RET_W + 2 * NSA_KVW + LANES
FFN_CHUNKS = ((0, 768), (768, 768), (1536, 768), (2304, 512))
NSA_CHAIN_SPLIT = 2
DEN_LANE = (NSA_HEAD_DIM, 0)


def _dot(a, b):
    return jnp.dot(a, b, preferred_element_type=F32)


def _dot_nt(a, b):
    return lax.dot_general(a, b, (((1,), (1,)), ((), ())), preferred_element_type=F32)


def _dot_tn(a, b):
    return lax.dot_general(a, b, (((0,), (0,)), ((), ())), preferred_element_type=F32)


def _split_bf16(x):
    hi = x.astype(BF16)
    lo = (x - hi.astype(F32)).astype(BF16)
    return hi, lo


def _silu(x):
    return x * jax.nn.sigmoid(x)


def _layer_norm(x, g, b):
    mu = jnp.mean(x, axis=-1, keepdims=True)
    xc = x - mu
    var = jnp.mean(xc * xc, axis=-1, keepdims=True)
    return xc * lax.rsqrt(var + LN_EPS) * g + b


def _resident(shape):
    return pl.BlockSpec(shape, lambda *_: (0,) * len(shape), pipeline_mode=pl.Buffered(1))


def _params(sem, vmem):
    return pltpu.CompilerParams(dimension_semantics=sem, vmem_limit_bytes=vmem)


def _ada_body(c_ref, w_ref, b_ref, o_ref):
    a_hi, a_lo = _split_bf16(_silu(c_ref[...]))
    w_hi, w_lo = _split_bf16(w_ref[0])
    o_ref[0] = _dot(a_hi, w_hi) + _dot(a_lo, w_hi) + _dot(a_hi, w_lo) + b_ref[0]


def _ada_call(c, ada_w, ada_b):
    B = c.shape[0]
    n_out = ada_w.shape[-1]
    tn = n_out // 4
    return pl.pallas_call(
        _ada_body,
        grid=(DEPTH, n_out // tn),
        in_specs=[pl.BlockSpec((B, D_MODEL), lambda l, n: (0, 0)),
                  pl.BlockSpec((1, D_MODEL, tn), lambda l, n: (l, 0, n)),
                  pl.BlockSpec((1, 1, tn), lambda l, n: (l, 0, n))],
        out_specs=pl.BlockSpec((1, B, tn), lambda l, n: (l, 0, n)),
        out_shape=jax.ShapeDtypeStruct((DEPTH, B, n_out), F32),
        compiler_params=_params(("arbitrary", "arbitrary"), VMEM_LIMIT_LARGE),
        name="ada_mod",
    )(c, ada_w, ada_b.reshape(DEPTH, 1, n_out))


def _pre_body(h_ref, mod_ref, w_ref, *refs, dilation):
    if dilation == 1:
        o_refs, h = refs, h_ref[0]
    else:
        o_refs, h_scr = refs[:-1], refs[-1]
        tm = h_ref.shape[1]
        for c in range(h_scr.shape[0]):
            h_scr[c] = h_ref[0, :, c * LANES:(c + 1) * LANES]
        h = jnp.concatenate(
            [jnp.concatenate([h_scr[c, pl.ds(r, tm // dilation, stride=dilation), :]
                              for c in range(h_scr.shape[0])], axis=1) for r in range(dilation)], axis=0)
    u = (h * (1.0 + mod_ref[0, 1:2, :]) + mod_ref[0, 0:1, :]).astype(BF16)
    off = 0
    for o_ref in o_refs:
        n = o_ref.shape[-1]
        o_ref[0] = _dot(u, w_ref[:, off:off + n]).astype(o_ref.dtype).reshape(o_ref.shape[1:])
        off += n


def _pre_call(h, mod, w, out_cols_dtypes, tm, name, dilation=1):
    B, S, _ = h.shape
    n_total = w.shape[1]
    assert sum(n for n, _ in out_cols_dtypes) == n_total
    if dilation == 1:
        out_specs = [pl.BlockSpec((1, tm, n), lambda b, t: (b, t, 0)) for n, _ in out_cols_dtypes]
        out_shape = [jax.ShapeDtypeStruct((B, S, n), dt) for n, dt in out_cols_dtypes]
        scratch = []
    else:
        out_specs = [pl.BlockSpec((1, dilation, tm // dilation, n), lambda b, t: (b, 0, t, 0))
                     for n, _ in out_cols_dtypes]
        out_shape = [jax.ShapeDtypeStruct((B, dilation, S // dilation, n), dt) for n, dt in out_cols_dtypes]
        scratch = [pltpu.VMEM((D_MODEL // LANES, tm, LANES), F32)]
    outs = pl.pallas_call(
        functools.partial(_pre_body, dilation=dilation),
        grid=(B, S // tm),
        in_specs=[pl.BlockSpec((1, tm, D_MODEL), lambda b, t: (b, t, 0)),
                  pl.BlockSpec((1, 6, D_MODEL), lambda b, t: (b, 0, 0)),
                  _resident((D_MODEL, n_total))],
        out_specs=out_specs,
        out_shape=out_shape,
        scratch_shapes=scratch,
        compiler_params=_params(("arbitrary", "arbitrary"), VMEM_LIMIT_LARGE),
        name=name,
    )(h, mod, w)
    return [o.reshape(B, S, o.shape[-1]) for o in outs]


def _ret_body(q_ref, k_ref, v_ref, g_ref, cos_ref, sin_ref, dec_ref, qd_ref, kd_ref, cd_ref,
              gng_ref, gnb_ref, o_ref):
    n_chunks = q_ref.shape[1] // RET_CHUNK
    state = jnp.zeros((RET_HEAD_DIM, RET_HEAD_DIM), F32)
    for n in range(n_chunks):
        sl = pl.ds(n * RET_CHUNK, RET_CHUNK)
        q = q_ref[0, sl, :].astype(F32)
        k = k_ref[0, sl, :].astype(F32)
        v = v_ref[0, sl, :]
        c2 = cos_ref[sl, :]
        s2 = sin_ref[sl, :]
        qr = (q * c2 + pltpu.roll(q, RET_HEAD_DIM // 2, 1) * s2) * (RET_HEAD_DIM ** -0.5)
        kr = k * c2 + pltpu.roll(k, RET_HEAD_DIM // 2, 1) * s2
        scores = _dot_nt(qr.astype(BF16), kr.astype(BF16)) * dec_ref[0]
        inner = _dot(scores.astype(BF16), v)
        cross = _dot((qr * qd_ref[0]).astype(BF16), state.astype(BF16))
        kv = _dot_tn((kr * kd_ref[0]).astype(BF16), v)
        state = state * cd_ref[0] + kv
        y = inner + cross
        mu = jnp.mean(y, axis=-1, keepdims=True)
        yc = y - mu
        var = jnp.mean(yc * yc, axis=-1, keepdims=True)
        yn = yc * lax.rsqrt(var + LN_EPS)
        gate = g_ref[0, sl, :]
        o_ref[0, sl, :] = ((yn * gng_ref[...] + gnb_ref[...]) * _silu(gate)).astype(o_ref.dtype)


def _ret_tables(S):
    d = RET_HEAD_DIM
    inv = ROPE_BASE ** (-jnp.arange(0, d, 2, dtype=F32) / d)
    ang = jnp.arange(S).astype(F32)[:, None] * inv[None, :]
    cos, sin = jnp.cos(ang), jnp.sin(ang)
    cos2 = jnp.concatenate([cos, cos], axis=-1)
    sin2 = jnp.concatenate([-sin, sin], axis=-1)
    C = RET_CHUNK
    log_gamma = jnp.log1p(-jnp.exp2(-5.0 - jnp.arange(RET_HEADS, dtype=F32)))
    idx = jnp.arange(C, dtype=F32)
    diff = idx[:, None] - idx[None, :]
    dec = jnp.where(diff >= 0, jnp.exp(log_gamma[:, None, None] * jnp.maximum(diff, 0.0)), 0.0)
    kd = jnp.exp(log_gamma[:, None] * (C - 1 - idx)[None, :])
    qd = jnp.exp(log_gamma[:, None] * (idx + 1.0)[None, :])
    cd = jnp.exp(log_gamma * C)
    bc = lambda t: jnp.broadcast_to(t[:, :, None], (RET_HEADS, C, d))
    cdb = jnp.broadcast_to(cd[:, None, None], (RET_HEADS, d, d))
    return cos2, sin2, dec, bc(qd), bc(kd), cdb


def _ret_call(pb, pf, gn_g, gn_b):
    B, S, _ = pb.shape
    cos2, sin2, dec, qd, kd, cd = _ret_tables(S)
    col = lambda off: pl.BlockSpec((1, S, RET_HEAD_DIM), lambda b, h: (b, 0, off + h))
    tab = pl.BlockSpec((1, RET_CHUNK, RET_HEAD_DIM), lambda b, h: (h, 0, 0))
    vec = pl.BlockSpec((1, RET_HEAD_DIM), lambda b, h: (0, h))
    return pl.pallas_call(
        _ret_body,
        grid=(B, RET_HEADS),
        in_specs=[col(0), col(RET_HEADS), col(2 * RET_HEADS), col(0),
                  _resident((S, RET_HEAD_DIM)), _resident((S, RET_HEAD_DIM)),
                  tab, tab, tab, tab, vec, vec],
        out_specs=pl.BlockSpec((1, S, RET_HEAD_DIM), lambda b, h: (b, 0, h)),
        out_shape=jax.ShapeDtypeStruct((B, S, RET_W), BF16),
        compiler_params=_params(("arbitrary", "arbitrary"), VMEM_LIMIT_SMALL),
        name="retention",
    )(pb, pb, pb, pf, cos2, sin2, dec, qd, kd, cd, gn_g.reshape(1, RET_W), gn_b.reshape(1, RET_W))


def _compress_body(xk_ref, xv_ref, pk_ref, pv_ref, w1k_ref, w1v_ref, w2k_ref, w2v_ref, kc_ref, vc_ref):
    n_blk = kc_ref.shape[1]
    for x_ref, p_ref, w1_ref, w2_ref, o_ref in ((xk_ref, pk_ref, w1k_ref, w2k_ref, kc_ref),
                                                (xv_ref, pv_ref, w1v_ref, w2v_ref, vc_ref)):
        first = jnp.zeros((n_blk, w1_ref.shape[-1]), F32)
        second = jnp.zeros((n_blk, w1_ref.shape[-1]), F32)
        for t in range(CMP_STRIDE):
            x = x_ref[0, pl.ds(t, n_blk, stride=CMP_STRIDE), :]
            first = first + _dot((x + p_ref[t:t + 1, :]).astype(BF16), w1_ref[t])
            second = second + _dot((x + p_ref[CMP_STRIDE + t:CMP_STRIDE + t + 1, :]).astype(BF16),
                                   w1_ref[CMP_STRIDE + t])
        hid = _silu(first + pltpu.roll(second, n_blk - 1, 0))
        o_ref[0] = _dot(hid.astype(BF16), w2_ref[...]).astype(o_ref.dtype)


def _compress_call(pf, pos_k, pos_v, w1k, w1v, w2k, w2v):
    B, S, _ = pf.shape
    n_blk = S // CMP_STRIDE
    assert CMP_BLOCK == 2 * CMP_STRIDE

    def both_heads(t):
        z = jnp.zeros_like(t)
        return jnp.concatenate([jnp.concatenate([t, z], -1), jnp.concatenate([z, t], -1)], -2).astype(BF16)

    pos = lambda t: jnp.concatenate([t, t], axis=-1)
    w1 = lambda t: both_heads(t.reshape(CMP_BLOCK, NSA_HEAD_DIM, CMP_HIDDEN))
    kc_block = RET_W // NSA_KVW
    xspec = lambda c: pl.BlockSpec((1, S, NSA_KVW), lambda b: (b, 0, c))
    ospec = pl.BlockSpec((1, n_blk, NSA_KVW), lambda b: (b, 0, 0))
    return pl.pallas_call(
        _compress_body,
        grid=(B,),
        in_specs=[xspec(kc_block), xspec(kc_block + 1),
                  _resident((CMP_BLOCK, NSA_KVW)), _resident((CMP_BLOCK, NSA_KVW)),
                  _resident((CMP_BLOCK, NSA_KVW, 2 * CMP_HIDDEN)), _resident((CMP_BLOCK, NSA_KVW, 2 * CMP_HIDDEN)),
                  _resident((2 * CMP_HIDDEN, NSA_KVW)), _resident((2 * CMP_HIDDEN, NSA_KVW))],
        out_specs=[ospec, ospec],
        out_shape=[jax.ShapeDtypeStruct((B, n_blk, NSA_KVW), BF16)] * 2,
        compiler_params=_params(("arbitrary",), VMEM_LIMIT_SMALL),
        name="nsa_compress",
    )(pf, pf, pos(pos_k), pos(pos_v), w1(w1k), w1(w1v), both_heads(w2k), both_heads(w2v))


CMP_Q_ROWS = 512


def _cmp_attn_body(q_ref, kc_ref, vc_ref, cb_ref, ov_ref, o_ref, sel_ref):
    i = pl.program_id(1)
    tq = q_ref.shape[1]
    lane = lax.broadcasted_iota(jnp.int32, (tq, LANES), 1)
    row = lax.broadcasted_iota(jnp.int32, (tq, LANES), 0)
    lo_half = lane < NSA_HEAD_DIM
    kc = kc_ref[0]
    vc = vc_ref[0]
    zero = jnp.zeros_like(kc)
    kc_lo = lax.broadcasted_iota(jnp.int32, kc.shape, 1) < NSA_HEAD_DIM
    kc_g = (jnp.where(kc_lo, kc, zero), jnp.where(kc_lo, zero, kc))
    valid = (lane * CMP_STRIDE + (CMP_BLOCK - 1) <= i * tq + row)[None]
    q4 = jnp.concatenate([q_ref[0, :, r * LANES:(r + 1) * LANES] for r in range(NSA_GROUP)], axis=0)
    q4 = q4 * (NSA_HEAD_DIM ** -0.5)
    psum, outs = [], []
    for g in range(NSA_KV_HEADS):
        s = _dot_nt(q4, kc_g[g]).reshape(NSA_GROUP, tq, LANES) + cb_ref[g * NSA_GROUP:(g + 1) * NSA_GROUP]
        s = jnp.where(valid, s, NEG_INF)
        m = jnp.max(s, axis=-1, keepdims=True)
        e = jnp.where(valid, jnp.exp2(s - m), 0.0)
        den = jnp.maximum(jnp.sum(e, axis=-1, keepdims=True), 1e-30)
        p = e / den
        psum.append(jnp.sum(p, axis=0))
        outs.append(_dot(p.reshape(NSA_GROUP * tq, LANES).astype(BF16), vc))
    for r in range(NSA_GROUP):
        rs = slice(r * tq, (r + 1) * tq)
        o_ref[0, :, r * LANES:(r + 1) * LANES] = jnp.where(lo_half, outs[0][rs], outs[1][rs])

    n_sel = sel_ref.shape[2]
    blk = lax.broadcasted_iota(jnp.int32, (n_sel, tq), 0)
    qblk = (i * tq + lax.broadcasted_iota(jnp.int32, (n_sel, tq), 1)) // SEL_BLOCK
    forced = jnp.where(blk == 0, 1.0, jnp.where(blk == qblk, 1.0, jnp.where(blk == qblk - 1, 1.0, 0.0)))
    for g in range(NSA_KV_HEADS):
        p_hi, p_lo = _split_bf16(psum[g])
        imp = _dot_nt(ov_ref[...], p_hi) + _dot_nt(ov_ref[...], p_lo)
        score = jnp.where(forced > 0.5, SEL_FORCE_SCORE, jnp.where(blk <= qblk, imp, -1.0))
        rank = jnp.zeros((n_sel, tq), F32)
        for other in range(n_sel):
            so = score[other:other + 1, :]
            tie = jnp.where(blk > other, 1.0, 0.0)
            rank = rank + jnp.where(so > score, 1.0, jnp.where(so == score, tie, 0.0))
        sel_ref[0, g] = jnp.where(rank < float(min(SEL_TOP_N, n_sel)), 1.0, 0.0)


def _cmp_attn_call(pb, kc, vc, cmp_bias, overlap_t):
    B, S, _ = pb.shape
    n_sel = S // SEL_BLOCK
    q_block = (3 * RET_W) // NSA_QW
    tq = CMP_Q_ROWS
    return pl.pallas_call(
        _cmp_attn_body,
        grid=(B, S // tq),
        in_specs=[pl.BlockSpec((1, tq, NSA_QW), lambda b, i: (b, i, q_block)),
                  pl.BlockSpec((1,) + kc.shape[1:], lambda b, i: (b, 0, 0)),
                  pl.BlockSpec((1,) + vc.shape[1:], lambda b, i: (b, 0, 0)),
                  pl.BlockSpec((NSA_HEADS, tq, LANES), lambda b, i: (0, i, 0)),
                  _resident((n_sel, LANES))],
        out_specs=[pl.BlockSpec((1, tq, NSA_QW), lambda b, i: (b, i, 0)),
                   pl.BlockSpec((1, NSA_KV_HEADS, n_sel, tq), lambda b, i: (b, 0, 0, i))],
        out_shape=[jax.ShapeDtypeStruct((B, S, NSA_QW), F32),
                   jax.ShapeDtypeStruct((B, NSA_KV_HEADS, n_sel, S), F32)],
        compiler_params=_params(("arbitrary", "arbitrary"), VMEM_LIMIT_SMALL),
        name="nsa_cmp_attn",
    )(pb, kc, vc, cmp_bias, overlap_t)


SLC_CLASS_TILES = 2


def _slc_far_tiles(cls, n_tiles):
    return max(min(SLC_CLASS_TILES * (cls + 1) - 2, n_tiles - 2), 1)


def _slc_body(q_ref, ks_ref, vs_ref, sel_ref, tag_ref, near_ref, o_ref, kaug_scr, vaug_scr):
    i = pl.program_id(1)
    n_tiles = ks_ref.shape[1] // TILE
    n_sel = sel_ref.shape[2]
    lane = lax.broadcasted_iota(jnp.int32, (TILE, LANES), 1)
    lo_half = lane < NSA_HEAD_DIM

    @pl.when(i == 0)
    def _():
        for t in range(n_tiles):
            sl = pl.ds(t * TILE, TILE)
            kt = ks_ref[0, sl, :]
            zero = jnp.zeros_like(kt)
            kaug_scr[0, sl, :] = jnp.concatenate([jnp.where(lo_half, kt, zero), tag_ref[sl, :]], axis=1)
            kaug_scr[1, sl, :] = jnp.concatenate([jnp.where(lo_half, zero, kt), tag_ref[sl, :]], axis=1)
            vt = vs_ref[0, sl, :]
            vaug_scr[0, sl, :] = jnp.where(lo_half, vt, jnp.where(lane == DEN_LANE[0], 1.0, 0.0).astype(BF16))
            vaug_scr[1, sl, :] = jnp.where(lo_half, jnp.where(lane == DEN_LANE[1], 1.0, 0.0).astype(BF16), vt)

    prev = jnp.maximum(i - 1, 0)
    own_sl = pl.ds(pl.multiple_of(i * TILE, TILE), TILE)
    prev_sl = pl.ds(pl.multiple_of(prev * TILE, TILE), TILE)
    no_prev = jnp.where(lax.broadcasted_iota(jnp.int32, (1, 2 * TILE), 1) < TILE,
                        jnp.where(i == 0, NEG_INF, 0.0), 0.0)
    tile_pen = jnp.where(lane - n_sel < i - 1, 0.0, NEG_INF)
    sel_pad = jnp.zeros((LANES - n_sel, TILE), F32)

    def tile_body(n_far):
        wf = n_far * TILE
        q4 = jnp.concatenate([q_ref[0, :, r * LANES:(r + 1) * LANES] for r in range(NSA_GROUP)], axis=0)
        q4 = q4 * (NSA_HEAD_DIM ** -0.5)
        q_far, q_near, k_near, v_near = [], [], [], []
        for g in range(NSA_KV_HEADS):
            v_near.append(jnp.concatenate([vaug_scr[g, prev_sl, :], vaug_scr[g, own_sl, :]], axis=0))
            sel_q = jnp.concatenate([sel_ref[0, g], sel_pad], axis=0).T
            blk_pen = (sel_q - 1.0) * (-NEG_INF)
            pen_near = jnp.where(lane < n_sel, blk_pen, 0.0).astype(BF16)
            pen_far = jnp.where(lane < n_sel, blk_pen,
                                jnp.where(lane < n_sel + n_tiles, tile_pen, 0.0)).astype(BF16)
            q_far.append(jnp.concatenate([q4, jnp.concatenate([pen_far] * NSA_GROUP, axis=0)], axis=1))
            q_near.append(jnp.concatenate([q4, jnp.concatenate([pen_near] * NSA_GROUP, axis=0)], axis=1))
            k_near.append(jnp.concatenate([kaug_scr[g, prev_sl, :], kaug_scr[g, own_sl, :]], axis=0))

        n_split = NSA_CHAIN_SPLIT
        rows = NSA_GROUP * TILE // n_split
        chains = [(g, part) for g in range(NSA_KV_HEADS) for part in range(n_split)]

        def scores(g, part):
            rs = slice(part * rows, (part + 1) * rows)
            return (_dot_nt(q_far[g][rs], kaug_scr[g, :wf, :]),
                    _dot_nt(q_near[g][rs], k_near[g]) + (near_ref[g, rs, :] + no_prev))

        def weights(s_far, s_near):
            m = jnp.maximum(jnp.max(s_far, axis=-1, keepdims=True), jnp.max(s_near, axis=-1, keepdims=True))
            return jnp.exp2(s_far - m).astype(BF16), jnp.exp2(s_near - m).astype(BF16)

        def values(g, p_far, p_near):
            acc = _dot(p_far, vaug_scr[g, :wf, :]) + _dot(p_near, v_near[g])
            return acc / acc[:, DEN_LANE[g]:DEN_LANE[g] + 1]

        s, p, o = {}, {}, {}
        for step in range(len(chains) + 2):
            if step < len(chains):
                s[step] = scores(*chains[step])
            if 0 <= step - 2 < len(chains):
                o[step - 2] = values(chains[step - 2][0], *p.pop(step - 2))
            if 0 <= step - 1 < len(chains):
                p[step - 1] = weights(*s.pop(step - 1))
        for r in range(NSA_GROUP):
            part, rs = r // (NSA_GROUP // n_split), slice((r % (NSA_GROUP // n_split)) * TILE,
                                                          (r % (NSA_GROUP // n_split) + 1) * TILE)
            o_ref[0, :, r * LANES:(r + 1) * LANES] = jnp.where(lo_half, o[part][rs], o[n_split + part][rs])

    n_classes = -(-n_tiles // SLC_CLASS_TILES)
    for cls in range(n_classes):
        pl.when(i // SLC_CLASS_TILES == cls)(functools.partial(tile_body, _slc_far_tiles(cls, n_tiles)))


def _slc_call(pb, sel_t, key_tags, near):
    B, S, _ = pb.shape
    n_sel = S // SEL_BLOCK
    n_tiles = S // TILE
    assert n_sel + n_tiles <= LANES
    q_block = (3 * RET_W) // NSA_QW
    ks_block = (3 * RET_W + NSA_QW) // NSA_KVW
    return pl.pallas_call(
        _slc_body,
        grid=(B, n_tiles),
        in_specs=[pl.BlockSpec((1, TILE, NSA_QW), lambda b, i: (b, i, q_block)),
                  pl.BlockSpec((1, S, NSA_KVW), lambda b, i: (b, 0, ks_block)),
                  pl.BlockSpec((1, S, NSA_KVW), lambda b, i: (b, 0, ks_block + 1)),
                  pl.BlockSpec((1, NSA_KV_HEADS, n_sel, TILE), lambda b, i: (b, 0, 0, i)),
                  _resident(key_tags.shape), _resident(near.shape)],
        out_specs=pl.BlockSpec((1, TILE, NSA_QW), lambda b, i: (b, i, 0)),
        out_shape=jax.ShapeDtypeStruct((B, S, NSA_QW), F32),
        scratch_shapes=[pltpu.VMEM((NSA_KV_HEADS, S, NSA_KVW + LANES), BF16),
                        pltpu.VMEM((NSA_KV_HEADS, S, NSA_KVW), BF16)],
        compiler_params=_params(("arbitrary", "arbitrary"), VMEM_LIMIT_LARGE),
        name="nsa_selected",
    )(pb, pb, pb, sel_t, key_tags, near)


WIN_PREV_TILES = (WIN_SIZE - 1 + TILE - 1) // TILE
WIN_Q_TILES = 2


def _win_body(q_ref, k_ref, v_ref, wb_ref, gate_ref, eg_ref, ocmp_ref, oslc_ref, y_ref):
    i = pl.program_id(1)
    tq = q_ref.shape[1]
    n_span = WIN_PREV_TILES + WIN_Q_TILES
    first = i * WIN_Q_TILES - WIN_PREV_TILES
    lo_half = lax.broadcasted_iota(jnp.int32, (tq, LANES), 1) < NSA_HEAD_DIM

    def attend(early):
        if early:
            k_tiles, v_tiles, negs = [], [], []
            for t in range(n_span):
                sl = pl.ds(pl.multiple_of(jnp.maximum(first + t, 0) * TILE, TILE), TILE)
                k_tiles.append(k_ref[0, sl, :])
                v_tiles.append(v_ref[0, sl, :])
                negs.append(jnp.full((tq, TILE), jnp.where(first + t < 0, NEG_INF, 0.0), F32))
            k_all = jnp.concatenate(k_tiles, axis=0)
            v_all = jnp.concatenate(v_tiles, axis=0)
            missing = jnp.concatenate(negs, axis=1)[None]
        else:
            sl = pl.ds(pl.multiple_of(first * TILE, TILE), n_span * TILE)
            k_all = k_ref[0, sl, :]
            v_all = v_ref[0, sl, :]
        zero = jnp.zeros_like(k_all)
        k_lo = lax.broadcasted_iota(jnp.int32, k_all.shape, 1) < NSA_HEAD_DIM
        k_g = (jnp.where(k_lo, k_all, zero), jnp.where(k_lo, zero, k_all))
        v_lane = lax.broadcasted_iota(jnp.int32, v_all.shape, 1)
        v_g = (jnp.where(k_lo, v_all, jnp.where(v_lane == DEN_LANE[0], 1.0, 0.0).astype(BF16)),
               jnp.where(k_lo, jnp.where(v_lane == DEN_LANE[1], 1.0, 0.0).astype(BF16), v_all))

        g_hi, g_lo = _split_bf16(jax.nn.sigmoid(gate_ref[0]))
        gates = _dot(g_hi, eg_ref[...]) + _dot(g_lo, eg_ref[...])
        q4 = jnp.concatenate([q_ref[0, :, r * LANES:(r + 1) * LANES] for r in range(NSA_GROUP)], axis=0)
        q4 = q4 * (NSA_HEAD_DIM ** -0.5)
        n_split = NSA_CHAIN_SPLIT
        heads = NSA_GROUP // n_split
        chains = [(g, part) for g in range(NSA_KV_HEADS) for part in range(n_split)]

        def scores(g, part):
            s = _dot_nt(q4[part * heads * tq:(part + 1) * heads * tq], k_g[g]).reshape(heads, tq, n_span * TILE)
            s = s + wb_ref[g * NSA_GROUP + part * heads:g * NSA_GROUP + (part + 1) * heads]
            return s + missing if early else s

        def weights(s):
            m = jnp.max(s, axis=-1, keepdims=True)
            return jnp.exp2(s - m).astype(BF16).reshape(heads * tq, n_span * TILE)

        def values(g, p):
            acc = _dot(p, v_g[g])
            return acc / acc[:, DEN_LANE[g]:DEN_LANE[g] + 1]

        s, p, o = {}, {}, {}
        for step in range(len(chains) + 2):
            if step < len(chains):
                s[step] = scores(*chains[step])
            if 0 <= step - 2 < len(chains):
                o[step - 2] = values(chains[step - 2][0], p.pop(step - 2))
            if 0 <= step - 1 < len(chains):
                p[step - 1] = weights(s.pop(step - 1))
        for r in range(NSA_GROUP):
            cols = slice(r * LANES, (r + 1) * LANES)
            part, rs = r // heads, slice((r % heads) * tq, (r % heads + 1) * tq)
            o_win = jnp.where(lo_half, o[part][rs], o[n_split + part][rs])
            y = (gates[:, r * LANES:(r + 1) * LANES] * ocmp_ref[0, :, cols]
                 + gates[:, NSA_QW + r * LANES:NSA_QW + (r + 1) * LANES] * oslc_ref[0, :, cols]
                 + gates[:, 2 * NSA_QW + r * LANES:2 * NSA_QW + (r + 1) * LANES] * o_win)
            y_ref[0, :, cols] = y.astype(y_ref.dtype)

    pl.when(first >= 0)(functools.partial(attend, False))
    pl.when(first < 0)(functools.partial(attend, True))


def _win_call(pb, pf, win_bias, gate_expand, o_cmp, o_slc):
    B, S, _ = pb.shape
    q_block = (3 * RET_W) // NSA_QW
    kw_block = (3 * RET_W + NSA_QW) // NSA_KVW + 2
    gate_block = (RET_W + 2 * NSA_KVW) // LANES
    tq = WIN_Q_TILES * TILE
    tile_spec = pl.BlockSpec((1, tq, NSA_QW), lambda b, i: (b, i, 0))
    return pl.pallas_call(
        _win_body,
        grid=(B, S // tq),
        in_specs=[pl.BlockSpec((1, tq, NSA_QW), lambda b, i: (b, i, q_block)),
                  pl.BlockSpec((1, S, NSA_KVW), lambda b, i: (b, 0, kw_block)),
                  pl.BlockSpec((1, S, NSA_KVW), lambda b, i: (b, 0, kw_block + 1)),
                  _resident(win_bias.shape),
                  pl.BlockSpec((1, tq, LANES), lambda b, i: (b, i, gate_block)),
                  _resident((LANES, 3 * NSA_QW)),
                  tile_spec, tile_spec],
        out_specs=tile_spec,
        out_shape=jax.ShapeDtypeStruct((B, S, NSA_QW), BF16),
        compiler_params=_params(("arbitrary", "arbitrary"), VMEM_LIMIT_LARGE),
        name="nsa_window_combine",
    )(pb, pb, pb, win_bias, pf, gate_expand, o_cmp, o_slc)


def _post_body(h_ref, ma_ref, mb_ref, mod_ref, woa_ref, wob_ref, lng_ref, lnb_ref,
               wg_ref, wu_ref, wd_ref, o_ref):
    y = _dot(ma_ref[0], woa_ref[...]) + _dot(mb_ref[0], wob_ref[...])
    h1 = _layer_norm(DEEPNORM_ALPHA * h_ref[0] + mod_ref[0, 2:3, :] * y, lng_ref[0:1, :], lnb_ref[0:1, :])
    u = (h1 * (1.0 + mod_ref[0, 4:5, :]) + mod_ref[0, 3:4, :]).astype(BF16)
    acc = jnp.zeros(h1.shape, F32)
    for c0, cn in FFN_CHUNKS:
        gate = _dot(u, wg_ref[:, c0:c0 + cn])
        up = _dot(u, wu_ref[:, c0:c0 + cn])
        acc = acc + _dot((_silu(gate) * up).astype(BF16), wd_ref[c0:c0 + cn, :])
    o_ref[0] = _layer_norm(DEEPNORM_ALPHA * h1 + mod_ref[0, 5:6, :] * acc, lng_ref[1:2, :], lnb_ref[1:2, :])


def _post_call(h, mix_a, mix_b, col_a, col_b, mod, wo_a, wo_b, ln_g, ln_b, w_gate, w_up, w_down, tm, name):
    B, S, _ = h.shape
    half = D_MODEL // 2
    tok = pl.BlockSpec((1, tm, D_MODEL), lambda b, t: (b, t, 0))
    return pl.pallas_call(
        _post_body,
        grid=(B, S // tm),
        in_specs=[tok,
                  pl.BlockSpec((1, tm, half), lambda b, t: (b, t, col_a)),
                  pl.BlockSpec((1, tm, half), lambda b, t: (b, t, col_b)),
                  pl.BlockSpec((1, 6, D_MODEL), lambda b, t: (b, 0, 0)),
                  _resident((half, D_MODEL)), _resident((half, D_MODEL)),
                  _resident((2, D_MODEL)), _resident((2, D_MODEL)),
                  _resident((D_MODEL, D_FF)), _resident((D_MODEL, D_FF)), _resident((D_FF, D_MODEL))],
        out_specs=tok,
        out_shape=jax.ShapeDtypeStruct((B, S, D_MODEL), F32),
        compiler_params=_params(("arbitrary", "arbitrary"), VMEM_LIMIT_LARGE),
        name=name,
    )(h, mix_a, mix_b, mod, wo_a, wo_b, ln_g, ln_b, w_gate, w_up, w_down)


DIL_STEP_TILES = 4
DIL_WEIGHT_PARTS = 2


def _dil_body(q_ref, k_ref, kp_ref, v_ref, vp_ref, b_ref, o_ref, lse_ref, *, tiles_per_seg, between=None):
    t = pl.program_id(1)
    res_tiles = min(tiles_per_seg, DIL_STEP_TILES)
    with_prev = tiles_per_seg > 1
    if tiles_per_seg > DIL_STEP_TILES:
        first_prev = jnp.where((t * DIL_STEP_TILES) % tiles_per_seg == 0, NEG_INF, 0.0)
    else:
        first_prev = NEG_INF

    def span(ref, before_ref, j, cols):
        if not with_prev:
            return ref[0, j * TILE:(j + 1) * TILE, cols]
        if j > 0:
            return ref[0, (j - 1) * TILE:(j + 1) * TILE, cols]
        return jnp.concatenate([before_ref[0, :, cols], ref[0, :TILE, cols]], axis=0)

    def tile_bias(h, j):
        if not with_prev:
            return b_ref[h, TILE:, :]
        if j % res_tiles != 0:
            return b_ref[h]
        gone = first_prev if j == 0 else NEG_INF
        return jnp.concatenate([b_ref[h, :TILE, :] + gone, b_ref[h, TILE:, :]], axis=0)

    head_cols = lambda h: slice(h * DIL_HEAD_DIM, (h + 1) * DIL_HEAD_DIM)
    pad = jnp.zeros((LANES - DIL_HEADS, TILE), F32)

    def scores(j):
        return jnp.concatenate(
            [_dot_nt(span(k_ref, kp_ref, j, head_cols(h)), q_ref[0, j * TILE:(j + 1) * TILE, head_cols(h)])
             + tile_bias(h, j) for h in range(DIL_HEADS)], axis=1)

    def weights(s):
        m = jnp.max(s, axis=0, keepdims=True)
        e = jnp.exp2(s - m)
        den = jnp.sum(e, axis=0, keepdims=True)
        lse = (m + jnp.log2(den)) * math.log(2.0)
        return (e * (1.0 / den)).astype(BF16), lse

    def values(j, p):
        for h in range(DIL_HEADS):
            o = _dot_tn(p[:, h * TILE:(h + 1) * TILE], span(v_ref, vp_ref, j, head_cols(h)))
            o_ref[0, j * TILE:(j + 1) * TILE, head_cols(h)] = o.astype(o_ref.dtype)

    s, p = {}, {}
    for step in range(DIL_STEP_TILES + 2):
        if step < DIL_STEP_TILES:
            s[step] = scores(step)
        if 0 <= step - 2 < DIL_STEP_TILES:
            values(step - 2, p.pop(step - 2))
        j = step - 1
        parts = []
        for part in range(DIL_WEIGHT_PARTS):
            if between is not None:
                between[step * DIL_WEIGHT_PARTS + part]()
            if 0 <= j < DIL_STEP_TILES:
                slab = DIL_HEADS * TILE // DIL_WEIGHT_PARTS
                parts.append(weights(s[j][:, part * slab:(part + 1) * slab]))
        if parts:
            del s[j]
            p[j] = jnp.concatenate([pp for pp, _ in parts], axis=1)
            lse = jnp.concatenate([ll for _, ll in parts], axis=1)
            by_head = [lse[:, h * TILE:(h + 1) * TILE] for h in range(DIL_HEADS)]
            lse_ref[0, j * TILE:(j + 1) * TILE, :] = jnp.concatenate(by_head + [pad], axis=0).T


def _dil_proj_body(q_ref, k_ref, kp_ref, v_ref, vp_ref, b_ref, h_ref, mod_ref, w_ref,
                   o_ref, lse_ref, proj_ref, h_scr, *, tiles_per_seg, dilation):
    n_pieces = (DIL_STEP_TILES + 2) * DIL_WEIGHT_PARTS
    width = proj_ref.shape[-1] // n_pieces
    tm = h_ref.shape[1]
    state = {}

    def piece(c):
        if c == 0:
            for lc in range(h_scr.shape[0]):
                h_scr[lc] = h_ref[0, :, lc * LANES:(lc + 1) * LANES]
            h = jnp.concatenate(
                [jnp.concatenate([h_scr[lc, pl.ds(r, tm // dilation, stride=dilation), :]
                                  for lc in range(h_scr.shape[0])], axis=1) for r in range(dilation)], axis=0)
            state["u"] = (h * (1.0 + mod_ref[0, 1:2, :]) + mod_ref[0, 0:1, :]).astype(BF16)
        cols = slice(c * width, (c + 1) * width)
        res = _dot(state["u"], w_ref[:, cols]).astype(proj_ref.dtype)
        proj_ref[0, :, :, cols] = res.reshape(dilation, tm // dilation, width)

    _dil_body(q_ref, k_ref, kp_ref, v_ref, vp_ref, b_ref, o_ref, lse_ref, tiles_per_seg=tiles_per_seg,
              between=[functools.partial(piece, c) for c in range(n_pieces)])


def _dil_proj_call(proj, bias, dilation, h, mod, w_next, dilation_next, name):
    B, S, _ = proj.shape
    step = DIL_STEP_TILES * TILE
    n_next = w_next.shape[1]
    own = lambda c: pl.BlockSpec((1, step, DIL_WIDTH), lambda b, i: (b, i, c))
    before = lambda c: pl.BlockSpec((1, TILE, DIL_WIDTH),
                                    lambda b, i: (b, jnp.maximum(i * DIL_STEP_TILES - 1, 0), c))
    tile = lambda width: pl.BlockSpec((1, step, width), lambda b, i: (b, i, 0))
    o, lse, proj_next = pl.pallas_call(
        functools.partial(_dil_proj_body, tiles_per_seg=(S // dilation) // TILE, dilation=dilation_next),
        grid=(B, S // step),
        in_specs=[own(0), own(1), before(1), own(2), before(2), _resident(bias.shape),
                  tile(D_MODEL), pl.BlockSpec((1, 6, D_MODEL), lambda b, i: (b, 0, 0)),
                  _resident((D_MODEL, n_next))],
        out_specs=[tile(DIL_WIDTH), tile(LANES),
                   pl.BlockSpec((1, dilation_next, step // dilation_next, n_next), lambda b, i: (b, 0, i, 0))],
        out_shape=[jax.ShapeDtypeStruct((B, S, DIL_WIDTH), BF16), jax.ShapeDtypeStruct((B, S, LANES), F32),
                   jax.ShapeDtypeStruct((B, dilation_next, S // dilation_next, n_next), BF16)],
        scratch_shapes=[pltpu.VMEM((D_MODEL // LANES, step, LANES), F32)],
        compiler_params=_params(("arbitrary", "arbitrary"), VMEM_LIMIT_LARGE),
        name=name,
    )(proj, proj, proj, proj, proj, bias, h, mod, w_next)
    return o, lse, proj_next.reshape(B, S, n_next)


def _dil_call(proj, bias, dilation, name):
    B, S, _ = proj.shape
    step = DIL_STEP_TILES * TILE
    own = lambda c: pl.BlockSpec((1, step, DIL_WIDTH), lambda b, i: (b, i, c))
    before = lambda c: pl.BlockSpec((1, TILE, DIL_WIDTH),
                                    lambda b, i: (b, jnp.maximum(i * DIL_STEP_TILES - 1, 0), c))
    tile = lambda width: pl.BlockSpec((1, step, width), lambda b, i: (b, i, 0))
    return pl.pallas_call(
        functools.partial(_dil_body, tiles_per_seg=(S // dilation) // TILE),
        grid=(B, S // step),
        in_specs=[own(0), own(1), before(1), own(2), before(2), _resident(bias.shape)],
        out_specs=[tile(DIL_WIDTH), tile(LANES)],
        out_shape=[jax.ShapeDtypeStruct((B, S, DIL_WIDTH), BF16), jax.ShapeDtypeStruct((B, S, LANES), F32)],
        compiler_params=_params(("arbitrary", "arbitrary"), VMEM_LIMIT_LARGE),
        name=name,
    )(proj, proj, proj, proj, proj, bias)


def _post_mix_body(h_ref, o0_ref, o1_ref, o2_ref, l0_ref, l1_ref, l2_ref, mod_ref, wo_ref, lng_ref, lnb_ref,
                   wg_ref, wu_ref, wd_ref, out_ref, mix_a, mix_b, o_scr, l_scr):
    n = pl.program_id(0)
    o_refs = (o0_ref, o1_ref, o2_ref)
    l_refs = (l0_ref, l1_ref, l2_ref)
    tm = out_ref.shape[1]

    @pl.when(n == 0)
    def _():
        mix_a[...] = jnp.zeros_like(mix_a)
        mix_b[...] = jnp.zeros_like(mix_b)

    def unpermute():
        for gi, (o_ref, l_ref) in enumerate(zip(o_refs, l_refs)):
            dilation = o_ref.shape[1]
            for r in range(dilation):
                rows = pl.ds(r, tm // dilation, stride=dilation)
                l_scr[gi, rows, :] = l_ref[0, r]
                for h in range(DIL_HEADS):
                    o_scr[gi, h, rows, :] = o_ref[0, r, :, h * DIL_HEAD_DIM:(h + 1) * DIL_HEAD_DIM].astype(F32)

    def mix_head(h, mix_w):
        lses = [jnp.broadcast_to(l_scr[gi, :, h:h + 1], (tm, DIL_HEAD_DIM)) for gi in range(len(o_refs))]
        m = jnp.maximum(jnp.maximum(lses[0], lses[1]), lses[2])
        ws = [jnp.exp(l - m) for l in lses]
        den = ws[0] + ws[1] + ws[2]
        y = sum((w / den) * o_scr[gi, h] for gi, w in enumerate(ws))
        mix_w[:, h * DIL_HEAD_DIM:(h + 1) * DIL_HEAD_DIM] = y.astype(mix_w.dtype)

    def step(mix_w, mix_r):
        y = _dot(mix_r[...], wo_ref[...])
        unpermute()
        h1 = _layer_norm(DEEPNORM_ALPHA * h_ref[0] + mod_ref[0, 2:3, :] * y, lng_ref[0:1, :], lnb_ref[0:1, :])
        u = (h1 * (1.0 + mod_ref[0, 4:5, :]) + mod_ref[0, 3:4, :]).astype(BF16)
        acc = jnp.zeros(h1.shape, F32)
        for ci, (c0, cn) in enumerate(FFN_CHUNKS):
            gate = _dot(u, wg_ref[:, c0:c0 + cn])
            up = _dot(u, wu_ref[:, c0:c0 + cn])
            for h in range(DIL_HEADS):
                if h * len(FFN_CHUNKS) // DIL_HEADS == ci:
                    mix_head(h, mix_w)
            acc = acc + _dot((_silu(gate) * up).astype(BF16), wd_ref[c0:c0 + cn, :])
        out_ref[0] = _layer_norm(DEEPNORM_ALPHA * h1 + mod_ref[0, 5:6, :] * acc, lng_ref[1:2, :], lnb_ref[1:2, :])

    pl.when(n % 2 == 0)(functools.partial(step, mix_a, mix_b))
    pl.when(n % 2 == 1)(functools.partial(step, mix_b, mix_a))


def _post_mix_call(h, outs, lses, mod, wo, ln_g, ln_b, w_gate, w_up, w_down, tm, name):
    B, S, _ = h.shape
    steps = S // tm
    last = B * steps - 1
    dilations = [d for _, d in DIL_PATTERNS]
    by_residue = lambda t, d: t.reshape(B, d, S // d, t.shape[-1])
    cur = lambda n: jnp.minimum(n, last)
    done = lambda n: jnp.maximum(n - 1, 0)
    src = lambda d, width: pl.BlockSpec((1, d, tm // d, width), lambda n: (cur(n) // steps, 0, cur(n) % steps, 0))
    tok = pl.BlockSpec((1, tm, D_MODEL), lambda n: (done(n) // steps, done(n) % steps, 0))
    return pl.pallas_call(
        _post_mix_body,
        grid=(B * steps + 1,),
        in_specs=[tok] + [src(d, DIL_WIDTH) for d in dilations] + [src(d, LANES) for d in dilations]
                 + [pl.BlockSpec((1, 6, D_MODEL), lambda n: (done(n) // steps, 0, 0)),
                    _resident((D_MODEL, D_MODEL)), _resident((2, D_MODEL)), _resident((2, D_MODEL)),
                    _resident((D_MODEL, D_FF)), _resident((D_MODEL, D_FF)), _resident((D_FF, D_MODEL))],
        out_specs=tok,
        out_shape=jax.ShapeDtypeStruct((B, S, D_MODEL), F32),
        scratch_shapes=[pltpu.VMEM((tm, D_MODEL), BF16), pltpu.VMEM((tm, D_MODEL), BF16),
                        pltpu.VMEM((len(dilations), DIL_HEADS, tm, DIL_HEAD_DIM), F32),
                        pltpu.VMEM((len(dilations), tm, LANES), F32)],
        compiler_params=_params(("arbitrary",), VMEM_LIMIT_LARGE),
        name=name,
    )(h, *[by_residue(o, d) for o, d in zip(outs, dilations)], *[by_residue(l, d) for l, d in zip(lses, dilations)],
      mod, wo, ln_g, ln_b, w_gate, w_up, w_down)


def _t5_bucket_np(dist):
    n = np.maximum(dist, 0)
    max_exact = REL_BUCKETS // 2
    nf = np.maximum(n, 1).astype(np.float64)
    val = np.log(nf / max_exact) / math.log(REL_MAX_DIST / max_exact) * (REL_BUCKETS - max_exact)
    frac = np.abs(val - np.round(val))
    on_edge = (frac < 1e-9) & (n > max_exact) & (n < REL_MAX_DIST)
    assert not on_edge.any()
    large = np.minimum(max_exact + np.floor(val + 1e-9).astype(np.int64), REL_BUCKETS - 1)
    return np.where(n < max_exact, n, large).astype(np.int32)


def _shift_table(rel_bias, rows, cols, step, dist_fn, valid_fn):
    u = np.concatenate([np.arange(cols), np.arange(-(rows - 1) * step, 0)])
    period = u.size
    vals = jnp.take(rel_bias, jnp.asarray(_t5_bucket_np(dist_fn(u))), axis=0).T
    vals = jnp.where(jnp.asarray(valid_fn(u))[None], vals, NEG_INF)
    t = jnp.tile(vals, (1, rows))[:, :rows * (period - step)].reshape(vals.shape[0], rows, period - step)
    return t[:, :, :cols]


def _nsa_tables(rel_bias, S):
    rel_bias = rel_bias * LOG2_E
    always = lambda u: np.ones(u.shape, bool)
    win_dist = lambda u: WIN_PREV_TILES * TILE - u
    win_bias = _shift_table(rel_bias, WIN_Q_TILES * TILE, (WIN_PREV_TILES + WIN_Q_TILES) * TILE, 1, win_dist,
                            lambda u: (win_dist(u) >= 0) & (win_dist(u) <= WIN_SIZE - 1))
    d0 = _shift_table(rel_bias, TILE, TILE, 1, lambda u: -u, lambda u: u <= 0)
    d1 = _shift_table(rel_bias, TILE, TILE, 1, lambda u: TILE - u, always)
    far_bucket = _t5_bucket_np(np.arange(TILE + 1, S + TILE))
    assert (far_bucket == far_bucket[0]).all()
    far = rel_bias[int(far_bucket[0])][:, None, None]
    near = jnp.concatenate([d1 - far, d0 - far], axis=2)
    near = near.reshape(NSA_KV_HEADS, NSA_GROUP * TILE, 2 * TILE)
    cmp_bias = _shift_table(rel_bias, LANES, S, CMP_STRIDE, lambda u: u - (CMP_BLOCK - 1), always)
    cmp_bias = cmp_bias.transpose(0, 2, 1)
    n_cmp = (S - CMP_BLOCK) // CMP_STRIDE + 1
    n_sel = S // SEL_BLOCK
    cs = (np.arange(n_cmp) * CMP_STRIDE)[:, None]
    ss = (np.arange(n_sel) * SEL_BLOCK)[None, :]
    ov = np.clip(np.minimum(cs + CMP_BLOCK, ss + SEL_BLOCK) - np.maximum(cs, ss), 0, None) / CMP_BLOCK
    ov_t = np.zeros((n_sel, LANES), np.float32)
    ov_t[:, :n_cmp] = ov.T
    key = np.arange(S)[:, None]
    lane = np.arange(LANES)[None, :]
    tags = ((lane == key // SEL_BLOCK) | (lane == n_sel + key // TILE)).astype(np.float32)
    eg = np.zeros((LANES, 3 * NSA_QW), np.float32)
    for g in range(NSA_KV_HEADS):
        for r in range(NSA_GROUP):
            for j in range(3):
                base = j * NSA_QW + r * LANES + g * NSA_HEAD_DIM
                eg[g * NSA_GROUP * 3 + r * 3 + j, base:base + NSA_HEAD_DIM] = 1.0
    return win_bias, near, cmp_bias, jnp.asarray(ov_t, BF16), jnp.asarray(tags, BF16), jnp.asarray(eg, BF16)


def _dil_bias(rel_bias, dilation, max_dist):
    dist = lambda u: TILE + u
    return _shift_table(rel_bias * LOG2_E, 2 * TILE, TILE, 1, lambda u: dist(u) * dilation,
                        lambda u: (dist(u) >= 0) & (dist(u) <= max_dist))


def _nsa_head_perm():
    perm = np.zeros(NSA_QW, np.int64)
    for r in range(NSA_GROUP):
        for g in range(NSA_KV_HEADS):
            new = r * LANES + g * NSA_HEAD_DIM
            old = (g * NSA_GROUP + r) * NSA_HEAD_DIM
            perm[new:new + NSA_HEAD_DIM] = np.arange(old, old + NSA_HEAD_DIM)
    return perm


def _layer0_mixer(h, mod, ab_w_in, rel_bias, gn_g, gn_b, pos_k, pos_v, w1k, w2k, w1v, w2v):
    B, S, _ = h.shape
    o = np.cumsum((0, RET_W, RET_W, RET_W, RET_W, NSA_QW) + (NSA_KVW,) * 6 + (3 * NSA_HEADS,))
    seg = lambda a: ab_w_in[:, o[a]:o[a + 1]]
    gate_w = jnp.pad(seg(11), ((0, 0), (0, LANES - 3 * NSA_HEADS)))
    q_nsa = seg(4)[:, _nsa_head_perm()] * LOG2_E
    w = jnp.concatenate([seg(0), seg(1), seg(2), q_nsa, seg(7), seg(8), seg(9), seg(10),
                         seg(3), seg(5), seg(6), gate_w], axis=1).astype(BF16)
    pb, pf = _pre_call(h, mod, w, ((PRE0_BF16_COLS, BF16), (PRE0_F32_COLS, F32)), 512, "pre0")

    y_ret = _ret_call(pb, pf, gn_g, gn_b)

    win_bias, near, cmp_bias, ov_t, key_tags, eg = _nsa_tables(rel_bias, S)
    kc, vc = _compress_call(pf, pos_k, pos_v, w1k, w1v, w2k, w2v)
    o_cmp, sel_t = _cmp_attn_call(pb, kc, vc, cmp_bias, ov_t)
    o_slc = _slc_call(pb, sel_t, key_tags, near)
    y_nsa = _win_call(pb, pf, win_bias, eg, o_cmp, o_slc)
    return y_ret, y_nsa


def _layer1_mixer(h, mod, dil_w_in, rel_bias):
    B, S, _ = h.shape
    def group_weights(gi):
        w = dil_w_in[:, gi * 3 * DIL_WIDTH:(gi + 1) * 3 * DIL_WIDTH]
        return jnp.concatenate([w[:, :DIL_WIDTH] * (DIL_HEAD_DIM ** -0.5 * LOG2_E), w[:, DIL_WIDTH:]],
                               axis=1).astype(BF16)

    outs, lses = [], []
    proj, = _pre_call(h, mod, group_weights(0), ((3 * DIL_WIDTH, BF16),), 512, "pre1_0", DIL_PATTERNS[0][1])
    for gi, (window, dilation) in enumerate(DIL_PATTERNS):
        bias = _dil_bias(rel_bias, dilation, window // dilation)
        if gi + 1 < len(DIL_PATTERNS):
            o, lse, proj = _dil_proj_call(proj, bias, dilation, h, mod, group_weights(gi + 1),
                                          DIL_PATTERNS[gi + 1][1], f"dilated_{gi}_pre1_{gi + 1}")
        else:
            o, lse = _dil_call(proj, bias, dilation, f"dilated_{gi}")
        outs.append(o)
        lses.append(lse)
    return outs, lses


def kernel(x, c, rel_bias, ada_w, ada_b, ln_g, ln_b, ab_w_in, ab_w_out, ret_gn_g, ret_gn_b, cmp_pos_k, cmp_pos_v, cmp_k_w1, cmp_k_w2, cmp_v_w1, cmp_v_w2, dil_w_in, dil_w_out, ffn_w_gate, ffn_w_up, ffn_w_down):
    B = x.shape[0]
    mod = _ada_call(c, ada_w, ada_b).reshape(DEPTH, B, 6, D_MODEL)
    h = x
    for layer in range(DEPTH):
        i = layer // 2
        if layer % 2 == 0:
            mix_a, mix_b = _layer0_mixer(h, mod[layer], ab_w_in[i], rel_bias, ret_gn_g[i], ret_gn_b[i],
                                         cmp_pos_k[i], cmp_pos_v[i], cmp_k_w1[i], cmp_k_w2[i],
                                         cmp_v_w1[i], cmp_v_w2[i])
            wo_a = ab_w_out[i, :RET_W]
            wo_b = ab_w_out[i, RET_W:][_nsa_head_perm()]
            h = _post_call(h, mix_a, mix_b, 0, 0, mod[layer], wo_a.astype(BF16), wo_b.astype(BF16),
                           ln_g[layer], ln_b[layer], ffn_w_gate[layer].astype(BF16), ffn_w_up[layer].astype(BF16),
                           ffn_w_down[layer].astype(BF16), 512, f"post{layer}")
        else:
            outs, lses = _layer1_mixer(h, mod[layer], dil_w_in[i], rel_bias)
            h = _post_mix_call(h, outs, lses, mod[layer], dil_w_out[i].astype(BF16), ln_g[layer], ln_b[layer],
                               ffn_w_gate[layer].astype(BF16), ffn_w_up[layer].astype(BF16),
                               ffn_w_down[layer].astype(BF16), 512, f"post{layer}")
    return h
```

```python
import functools
import math

import numpy as np
import jax
import jax.numpy as jnp
from jax import lax
from jax.experimental import pallas as pl
from jax.experimental.pallas import tpu as pltpu

F32 = jnp.float32
BF16 = jnp.bfloat16

D_MODEL = 1024
DEPTH = 2
DEEPNORM_ALPHA = (2 * DEPTH) ** 0.25
LN_EPS = 1e-5
NEG_INF = -1e30
LOG2_E = math.log2(math.e)

RET_HEADS = 4
RET_HEAD_DIM = 128
RET_CHUNK = 128
ROPE_BASE = 10000.0
RET_W = RET_HEADS * RET_HEAD_DIM

NSA_HEADS = 8
NSA_KV_HEADS = 2
NSA_GROUP = 4
NSA_HEAD_DIM = 64
CMP_BLOCK = 32
CMP_STRIDE = 16
CMP_HIDDEN = 256
SEL_BLOCK = 64
SEL_TOP_N = 16
SEL_FORCE_SCORE = 1e4
WIN_SIZE = 512
NSA_QW = NSA_HEADS * NSA_HEAD_DIM
NSA_KVW = NSA_KV_HEADS * NSA_HEAD_DIM

DIL_PATTERNS = ((128, 1), (512, 4), (2048, 16))
DIL_HEADS = 8
DIL_HEAD_DIM = 128
DIL_WIDTH = DIL_HEADS * DIL_HEAD_DIM

REL_BUCKETS = 32
REL_MAX_DIST = 128
D_FF = 2816

LANES = 128
TILE = 128
VMEM_LIMIT_SMALL = 32 * 1024 * 1024
VMEM_LIMIT_LARGE = 56 * 1024 * 1024

PRE0_BF16_COLS = 3 * RET_W + NSA_QW + 4 * NSA_KVW
PRE0_F32_COLS = RET_W + 2 * NSA_KVW + LANES
FFN_CHUNKS = ((0, 768), (768, 768), (1536, 768), (2304, 512))
NSA_CHAIN_SPLIT = 2
DEN_LANE = (NSA_HEAD_DIM, 0)


def _dot(a, b):
    return jnp.dot(a, b, preferred_element_type=F32)


def _dot_nt(a, b):
    return lax.dot_general(a, b, (((1,), (1,)), ((), ())), preferred_element_type=F32)


def _dot_tn(a, b):
    return lax.dot_general(a, b, (((0,), (0,)), ((), ())), preferred_element_type=F32)


def _split_bf16(x):
    hi = x.astype(BF16)
    lo = (x - hi.astype(F32)).astype(BF16)
    return hi, lo


def _silu(x):
    return x * jax.nn.sigmoid(x)


def _layer_norm(x, g, b):
    mu = jnp.mean(x, axis=-1, keepdims=True)
    xc = x - mu
    var = jnp.mean(xc * xc, axis=-1, keepdims=True)
    return xc * lax.rsqrt(var + LN_EPS) * g + b


def _resident(shape):
    return pl.BlockSpec(shape, lambda *_: (0,) * len(shape), pipeline_mode=pl.Buffered(1))


def _params(sem, vmem):
    return pltpu.CompilerParams(dimension_semantics=sem, vmem_limit_bytes=vmem)


def _ada_body(c_ref, w_ref, b_ref, o_ref):
    a_hi, a_lo = _split_bf16(_silu(c_ref[...]))
    w_hi, w_lo = _split_bf16(w_ref[0])
    o_ref[0] = _dot(a_hi, w_hi) + _dot(a_lo, w_hi) + _dot(a_hi, w_lo) + b_ref[0]


def _ada_call(c, ada_w, ada_b):
    B = c.shape[0]
    n_out = ada_w.shape[-1]
    tn = n_out // 4
    return pl.pallas_call(
        _ada_body,
        grid=(DEPTH, n_out // tn),
        in_specs=[pl.BlockSpec((B, D_MODEL), lambda l, n: (0, 0)),
                  pl.BlockSpec((1, D_MODEL, tn), lambda l, n: (l, 0, n)),
                  pl.BlockSpec((1, 1, tn), lambda l, n: (l, 0, n))],
        out_specs=pl.BlockSpec((1, B, tn), lambda l, n: (l, 0, n)),
        out_shape=jax.ShapeDtypeStruct((DEPTH, B, n_out), F32),
        compiler_params=_params(("arbitrary", "arbitrary"), VMEM_LIMIT_LARGE),
        name="ada_mod",
    )(c, ada_w, ada_b.reshape(DEPTH, 1, n_out))


def _pre_body(h_ref, mod_ref, w_ref, *refs, dilation):
    if dilation == 1:
        o_refs, h = refs, h_ref[0]
    else:
        o_refs, h_scr = refs[:-1], refs[-1]
        tm = h_ref.shape[1]
        for c in range(h_scr.shape[0]):
            h_scr[c] = h_ref[0, :, c * LANES:(c + 1) * LANES]
        h = jnp.concatenate(
            [jnp.concatenate([h_scr[c, pl.ds(r, tm // dilation, stride=dilation), :]
                              for c in range(h_scr.shape[0])], axis=1) for r in range(dilation)], axis=0)
    u = (h * (1.0 + mod_ref[0, 1:2, :]) + mod_ref[0, 0:1, :]).astype(BF16)
    off = 0
    for o_ref in o_refs:
        n = o_ref.shape[-1]
        o_ref[0] = _dot(u, w_ref[:, off:off + n]).astype(o_ref.dtype).reshape(o_ref.shape[1:])
        off += n


def _pre_call(h, mod, w, out_cols_dtypes, tm, name, dilation=1):
    B, S, _ = h.shape
    n_total = w.shape[1]
    assert sum(n for n, _ in out_cols_dtypes) == n_total
    if dilation == 1:
        out_specs = [pl.BlockSpec((1, tm, n), lambda b, t: (b, t, 0)) for n, _ in out_cols_dtypes]
        out_shape = [jax.ShapeDtypeStruct((B, S, n), dt) for n, dt in out_cols_dtypes]
        scratch = []
    else:
        out_specs = [pl.BlockSpec((1, dilation, tm // dilation, n), lambda b, t: (b, 0, t, 0))
                     for n, _ in out_cols_dtypes]
        out_shape = [jax.ShapeDtypeStruct((B, dilation, S // dilation, n), dt) for n, dt in out_cols_dtypes]
        scratch = [pltpu.VMEM((D_MODEL // LANES, tm, LANES), F32)]
    outs = pl.pallas_call(
        functools.partial(_pre_body, dilation=dilation),
        grid=(B, S // tm),
        in_specs=[pl.BlockSpec((1, tm, D_MODEL), lambda b, t: (b, t, 0)),
                  pl.BlockSpec((1, 6, D_MODEL), lambda b, t: (b, 0, 0)),
                  _resident((D_MODEL, n_total))],
        out_specs=out_specs,
        out_shape=out_shape,
        scratch_shapes=scratch,
        compiler_params=_params(("arbitrary", "arbitrary"), VMEM_LIMIT_LARGE),
        name=name,
    )(h, mod, w)
    return [o.reshape(B, S, o.shape[-1]) for o in outs]


def _ret_body(q_ref, k_ref, v_ref, g_ref, cos_ref, sin_ref, dec_ref, qd_ref, kd_ref, cd_ref,
              gng_ref, gnb_ref, o_ref):
    n_chunks = q_ref.shape[1] // RET_CHUNK
    state = jnp.zeros((RET_HEAD_DIM, RET_HEAD_DIM), F32)
    for n in range(n_chunks):
        sl = pl.ds(n * RET_CHUNK, RET_CHUNK)
        q = q_ref[0, sl, :].astype(F32)
        k = k_ref[0, sl, :].astype(F32)
        v = v_ref[0, sl, :]
        c2 = cos_ref[sl, :]
        s2 = sin_ref[sl, :]
        qr = (q * c2 + pltpu.roll(q, RET_HEAD_DIM // 2, 1) * s2) * (RET_HEAD_DIM ** -0.5)
        kr = k * c2 + pltpu.roll(k, RET_HEAD_DIM // 2, 1) * s2
        scores = _dot_nt(qr.astype(BF16), kr.astype(BF16)) * dec_ref[0]
        inner = _dot(scores.astype(BF16), v)
        cross = _dot((qr * qd_ref[0]).astype(BF16), state.astype(BF16))
        kv = _dot_tn((kr * kd_ref[0]).astype(BF16), v)
        state = state * cd_ref[0] + kv
        y = inner + cross
        mu = jnp.mean(y, axis=-1, keepdims=True)
        yc = y - mu
        var = jnp.mean(yc * yc, axis=-1, keepdims=True)
        yn = yc * lax.rsqrt(var + LN_EPS)
        gate = g_ref[0, sl, :]
        o_ref[0, sl, :] = ((yn * gng_ref[...] + gnb_ref[...]) * _silu(gate)).astype(o_ref.dtype)


def _ret_tables(S):
    d = RET_HEAD_DIM
    inv = ROPE_BASE ** (-jnp.arange(0, d, 2, dtype=F32) / d)
    ang = jnp.arange(S).astype(F32)[:, None] * inv[None, :]
    cos, sin = jnp.cos(ang), jnp.sin(ang)
    cos2 = jnp.concatenate([cos, cos], axis=-1)
    sin2 = jnp.concatenate([-sin, sin], axis=-1)
    C = RET_CHUNK
    log_gamma = jnp.log1p(-jnp.exp2(-5.0 - jnp.arange(RET_HEADS, dtype=F32)))
    idx = jnp.arange(C, dtype=F32)
    diff = idx[:, None] - idx[None, :]
    dec = jnp.where(diff >= 0, jnp.exp(log_gamma[:, None, None] * jnp.maximum(diff, 0.0)), 0.0)
    kd = jnp.exp(log_gamma[:, None] * (C - 1 - idx)[None, :])
    qd = jnp.exp(log_gamma[:, None] * (idx + 1.0)[None, :])
    cd = jnp.exp(log_gamma * C)
    bc = lambda t: jnp.broadcast_to(t[:, :, None], (RET_HEADS, C, d))
    cdb = jnp.broadcast_to(cd[:, None, None], (RET_HEADS, d, d))
    return cos2, sin2, dec, bc(qd), bc(kd), cdb


def _ret_call(pb, pf, gn_g, gn_b):
    B, S, _ = pb.shape
    cos2, sin2, dec, qd, kd, cd = _ret_tables(S)
    col = lambda off: pl.BlockSpec((1, S, RET_HEAD_DIM), lambda b, h: (b, 0, off + h))
    tab = pl.BlockSpec((1, RET_CHUNK, RET_HEAD_DIM), lambda b, h: (h, 0, 0))
    vec = pl.BlockSpec((1, RET_HEAD_DIM), lambda b, h: (0, h))
    return pl.pallas_call(
        _ret_body,
        grid=(B, RET_HEADS),
        in_specs=[col(0), col(RET_HEADS), col(2 * RET_HEADS), col(0),
                  _resident((S, RET_HEAD_DIM)), _resident((S, RET_HEAD_DIM)),
                  tab, tab, tab, tab, vec, vec],
        out_specs=pl.BlockSpec((1, S, RET_HEAD_DIM), lambda b, h: (b, 0, h)),
        out_shape=jax.ShapeDtypeStruct((B, S, RET_W), BF16),
        compiler_params=_params(("arbitrary", "arbitrary"), VMEM_LIMIT_SMALL),
        name="retention",
    )(pb, pb, pb, pf, cos2, sin2, dec, qd, kd, cd, gn_g.reshape(1, RET_W), gn_b.reshape(1, RET_W))


def _compress_body(xk_ref, xv_ref, pk_ref, pv_ref, w1k_ref, w1v_ref, w2k_ref, w2v_ref, kc_ref, vc_ref):
    n_blk = kc_ref.shape[1]
    for x_ref, p_ref, w1_ref, w2_ref, o_ref in ((xk_ref, pk_ref, w1k_ref, w2k_ref, kc_ref),
                                                (xv_ref, pv_ref, w1v_ref, w2v_ref, vc_ref)):
        first = jnp.zeros((n_blk, w1_ref.shape[-1]), F32)
        second = jnp.zeros((n_blk, w1_ref.shape[-1]), F32)
        for t in range(CMP_STRIDE):
            x = x_ref[0, pl.ds(t, n_blk, stride=CMP_STRIDE), :]
            first = first + _dot((x + p_ref[t:t + 1, :]).astype(BF16), w1_ref[t])
            second = second + _dot((x + p_ref[CMP_STRIDE + t:CMP_STRIDE + t + 1, :]).astype(BF16),
                                   w1_ref[CMP_STRIDE + t])
        hid = _silu(first + pltpu.roll(second, n_blk - 1, 0))
        o_ref[0] = _dot(hid.astype(BF16), w2_ref[...]).astype(o_ref.dtype)


def _compress_call(pf, pos_k, pos_v, w1k, w1v, w2k, w2v):
    B, S, _ = pf.shape
    n_blk = S // CMP_STRIDE
    assert CMP_BLOCK == 2 * CMP_STRIDE

    def both_heads(t):
        z = jnp.zeros_like(t)
        return jnp.concatenate([jnp.concatenate([t, z], -1), jnp.concatenate([z, t], -1)], -2).astype(BF16)

    pos = lambda t: jnp.concatenate([t, t], axis=-1)
    w1 = lambda t: both_heads(t.reshape(CMP_BLOCK, NSA_HEAD_DIM, CMP_HIDDEN))
    kc_block = RET_W // NSA_KVW
    xspec = lambda c: pl.BlockSpec((1, S, NSA_KVW), lambda b: (b, 0, c))
    ospec = pl.BlockSpec((1, n_blk, NSA_KVW), lambda b: (b, 0, 0))
    return pl.pallas_call(
        _compress_body,
        grid=(B,),
        in_specs=[xspec(kc_block), xspec(kc_block + 1),
                  _resident((CMP_BLOCK, NSA_KVW)), _resident((CMP_BLOCK, NSA_KVW)),
                  _resident((CMP_BLOCK, NSA_KVW, 2 * CMP_HIDDEN)), _resident((CMP_BLOCK, NSA_KVW, 2 * CMP_HIDDEN)),
                  _resident((2 * CMP_HIDDEN, NSA_KVW)), _resident((2 * CMP_HIDDEN, NSA_KVW))],
        out_specs=[ospec, ospec],
        out_shape=[jax.ShapeDtypeStruct((B, n_blk, NSA_KVW), BF16)] * 2,
        compiler_params=_params(("arbitrary",), VMEM_LIMIT_SMALL),
        name="nsa_compress",
    )(pf, pf, pos(pos_k), pos(pos_v), w1(w1k), w1(w1v), both_heads(w2k), both_heads(w2v))


CMP_Q_ROWS = 512


def _cmp_attn_body(q_ref, kc_ref, vc_ref, cb_ref, ov_ref, o_ref, sel_ref):
    i = pl.program_id(1)
    tq = q_ref.shape[1]
    lane = lax.broadcasted_iota(jnp.int32, (tq, LANES), 1)
    row = lax.broadcasted_iota(jnp.int32, (tq, LANES), 0)
    lo_half = lane < NSA_HEAD_DIM
    kc = kc_ref[0]
    vc = vc_ref[0]
    zero = jnp.zeros_like(kc)
    kc_lo = lax.broadcasted_iota(jnp.int32, kc.shape, 1) < NSA_HEAD_DIM
    kc_g = (jnp.where(kc_lo, kc, zero), jnp.where(kc_lo, zero, kc))
    valid = (lane * CMP_STRIDE + (CMP_BLOCK - 1) <= i * tq + row)[None]
    q4 = jnp.concatenate([q_ref[0, :, r * LANES:(r + 1) * LANES] for r in range(NSA_GROUP)], axis=0)
    q4 = q4 * (NSA_HEAD_DIM ** -0.5)
    psum, outs = [], []
    for g in range(NSA_KV_HEADS):
        s = _dot_nt(q4, kc_g[g]).reshape(NSA_GROUP, tq, LANES) + cb_ref[g * NSA_GROUP:(g + 1) * NSA_GROUP]
        s = jnp.where(valid, s, NEG_INF)
        m = jnp.max(s, axis=-1, keepdims=True)
        e = jnp.where(valid, jnp.exp2(s - m), 0.0)
        den = jnp.maximum(jnp.sum(e, axis=-1, keepdims=True), 1e-30)
        p = e / den
        psum.append(jnp.sum(p, axis=0))
        outs.append(_dot(p.reshape(NSA_GROUP * tq, LANES).astype(BF16), vc))
    for r in range(NSA_GROUP):
        rs = slice(r * tq, (r + 1) * tq)
        o_ref[0, :, r * LANES:(r + 1) * LANES] = jnp.where(lo_half, outs[0][rs], outs[1][rs])

    n_sel = sel_ref.shape[2]
    blk = lax.broadcasted_iota(jnp.int32, (n_sel, tq), 0)
    qblk = (i * tq + lax.broadcasted_iota(jnp.int32, (n_sel, tq), 1)) // SEL_BLOCK
    forced = jnp.where(blk == 0, 1.0, jnp.where(blk == qblk, 1.0, jnp.where(blk == qblk - 1, 1.0, 0.0)))
    for g in range(NSA_KV_HEADS):
        p_hi, p_lo = _split_bf16(psum[g])
        imp = _dot_nt(ov_ref[...], p_hi) + _dot_nt(ov_ref[...], p_lo)
        score = jnp.where(forced > 0.5, SEL_FORCE_SCORE, jnp.where(blk <= qblk, imp, -1.0))
        rank = jnp.zeros((n_sel, tq), F32)
        for other in range(n_sel):
            so = score[other:other + 1, :]
            tie = jnp.where(blk > other, 1.0, 0.0)
            rank = rank + jnp.where(so > score, 1.0, jnp.where(so == score, tie, 0.0))
        sel_ref[0, g] = jnp.where(rank < float(min(SEL_TOP_N, n_sel)), 1.0, 0.0)


def _cmp_attn_call(pb, kc, vc, cmp_bias, overlap_t):
    B, S, _ = pb.shape
    n_sel = S // SEL_BLOCK
    q_block = (3 * RET_W) // NSA_QW
    tq = CMP_Q_ROWS
    return pl.pallas_call(
        _cmp_attn_body,
        grid=(B, S // tq),
        in_specs=[pl.BlockSpec((1, tq, NSA_QW), lambda b, i: (b, i, q_block)),
                  pl.BlockSpec((1,) + kc.shape[1:], lambda b, i: (b, 0, 0)),
                  pl.BlockSpec((1,) + vc.shape[1:], lambda b, i: (b, 0, 0)),
                  pl.BlockSpec((NSA_HEADS, tq, LANES), lambda b, i: (0, i, 0)),
                  _resident((n_sel, LANES))],
        out_specs=[pl.BlockSpec((1, tq, NSA_QW), lambda b, i: (b, i, 0)),
                   pl.BlockSpec((1, NSA_KV_HEADS, n_sel, tq), lambda b, i: (b, 0, 0, i))],
        out_shape=[jax.ShapeDtypeStruct((B, S, NSA_QW), F32),
                   jax.ShapeDtypeStruct((B, NSA_KV_HEADS, n_sel, S), F32)],
        compiler_params=_params(("arbitrary", "arbitrary"), VMEM_LIMIT_SMALL),
        name="nsa_cmp_attn",
    )(pb, kc, vc, cmp_bias, overlap_t)


SLC_CLASS_TILES = 2


def _slc_far_tiles(cls, n_tiles):
    return max(min(SLC_CLASS_TILES * (cls + 1) - 2, n_tiles - 2), 1)


def _slc_body(q_ref, ks_ref, vs_ref, sel_ref, tag_ref, near_ref, o_ref, kaug_scr, vaug_scr):
    i = pl.program_id(1)
    n_tiles = ks_ref.shape[1] // TILE
    n_sel = sel_ref.shape[2]
    lane = lax.broadcasted_iota(jnp.int32, (TILE, LANES), 1)
    lo_half = lane < NSA_HEAD_DIM

    @pl.when(i == 0)
    def _():
        for t in range(n_tiles):
            sl = pl.ds(t * TILE, TILE)
            kt = ks_ref[0, sl, :]
            zero = jnp.zeros_like(kt)
            kaug_scr[0, sl, :] = jnp.concatenate([jnp.where(lo_half, kt, zero), tag_ref[sl, :]], axis=1)
            kaug_scr[1, sl, :] = jnp.concatenate([jnp.where(lo_half, zero, kt), tag_ref[sl, :]], axis=1)
            vt = vs_ref[0, sl, :]
            vaug_scr[0, sl, :] = jnp.where(lo_half, vt, jnp.where(lane == DEN_LANE[0], 1.0, 0.0).astype(BF16))
            vaug_scr[1, sl, :] = jnp.where(lo_half, jnp.where(lane == DEN_LANE[1], 1.0, 0.0).astype(BF16), vt)

    prev = jnp.maximum(i - 1, 0)
    own_sl = pl.ds(pl.multiple_of(i * TILE, TILE), TILE)
    prev_sl = pl.ds(pl.multiple_of(prev * TILE, TILE), TILE)
    no_prev = jnp.where(lax.broadcasted_iota(jnp.int32, (1, 2 * TILE), 1) < TILE,
                        jnp.where(i == 0, NEG_INF, 0.0), 0.0)
    tile_pen = jnp.where(lane - n_sel < i - 1, 0.0, NEG_INF)
    sel_pad = jnp.zeros((LANES - n_sel, TILE), F32)

    def tile_body(n_far):
        wf = n_far * TILE
        q4 = jnp.concatenate([q_ref[0, :, r * LANES:(r + 1) * LANES] for r in range(NSA_GROUP)], axis=0)
        q4 = q4 * (NSA_HEAD_DIM ** -0.5)
        q_far, q_near, k_near, v_near = [], [], [], []
        for g in range(NSA_KV_HEADS):
            v_near.append(jnp.concatenate([vaug_scr[g, prev_sl, :], vaug_scr[g, own_sl, :]], axis=0))
            sel_q = jnp.concatenate([sel_ref[0, g], sel_pad], axis=0).T
            blk_pen = (sel_q - 1.0) * (-NEG_INF)
            pen_near = jnp.where(lane < n_sel, blk_pen, 0.0).astype(BF16)
            pen_far = jnp.where(lane < n_sel, blk_pen,
                                jnp.where(lane < n_sel + n_tiles, tile_pen, 0.0)).astype(BF16)
            q_far.append(jnp.concatenate([q4, jnp.concatenate([pen_far] * NSA_GROUP, axis=0)], axis=1))
            q_near.append(jnp.concatenate([q4, jnp.concatenate([pen_near] * NSA_GROUP, axis=0)], axis=1))
            k_near.append(jnp.concatenate([kaug_scr[g, prev_sl, :], kaug_scr[g, own_sl, :]], axis=0))

        n_split = NSA_CHAIN_SPLIT
        rows = NSA_GROUP * TILE // n_split
        chains = [(g, part) for g in range(NSA_KV_HEADS) for part in range(n_split)]

        def scores(g, part):
            rs = slice(part * rows, (part + 1) * rows)
            return (_dot_nt(q_far[g][rs], kaug_scr[g, :wf, :]),
                    _dot_nt(q_near[g][rs], k_near[g]) + (near_ref[g, rs, :] + no_prev))

        def weights(s_far, s_near):
            m = jnp.maximum(jnp.max(s_far, axis=-1, keepdims=True), jnp.max(s_near, axis=-1, keepdims=True))
            return jnp.exp2(s_far - m).astype(BF16), jnp.exp2(s_near - m).astype(BF16)

        def values(g, p_far, p_near):
            acc = _dot(p_far, vaug_scr[g, :wf, :]) + _dot(p_near, v_near[g])
            return acc / acc[:, DEN_LANE[g]:DEN_LANE[g] + 1]

        s, p, o = {}, {}, {}
        for step in range(len(chains) + 2):
            if step < len(chains):
                s[step] = scores(*chains[step])
            if 0 <= step - 2 < len(chains):
                o[step - 2] = values(chains[step - 2][0], *p.pop(step - 2))
            if 0 <= step - 1 < len(chains):
                p[step - 1] = weights(*s.pop(step - 1))
        for r in range(NSA_GROUP):
            part, rs = r // (NSA_GROUP // n_split), slice((r % (NSA_GROUP // n_split)) * TILE,
                                                          (r % (NSA_GROUP // n_split) + 1) * TILE)
            o_ref[0, :, r * LANES:(r + 1) * LANES] = jnp.where(lo_half, o[part][rs], o[n_split + part][rs])

    n_classes = -(-n_tiles // SLC_CLASS_TILES)
    for cls in range(n_classes):
        pl.when(i // SLC_CLASS_TILES == cls)(functools.partial(tile_body, _slc_far_tiles(cls, n_tiles)))


def _slc_call(pb, sel_t, key_tags, near):
    B, S, _ = pb.shape
    n_sel = S // SEL_BLOCK
    n_tiles = S // TILE
    assert n_sel + n_tiles <= LANES
    q_block = (3 * RET_W) // NSA_QW
    ks_block = (3 * RET_W + NSA_QW) // NSA_KVW
    return pl.pallas_call(
        _slc_body,
        grid=(B, n_tiles),
        in_specs=[pl.BlockSpec((1, TILE, NSA_QW), lambda b, i: (b, i, q_block)),
                  pl.BlockSpec((1, S, NSA_KVW), lambda b, i: (b, 0, ks_block)),
                  pl.BlockSpec((1, S, NSA_KVW), lambda b, i: (b, 0, ks_block + 1)),
                  pl.BlockSpec((1, NSA_KV_HEADS, n_sel, TILE), lambda b, i: (b, 0, 0, i)),
                  _resident(key_tags.shape), _resident(near.shape)],
        out_specs=pl.BlockSpec((1, TILE, NSA_QW), lambda b, i: (b, i, 0)),
        out_shape=jax.ShapeDtypeStruct((B, S, NSA_QW), F32),
        scratch_shapes=[pltpu.VMEM((NSA_KV_HEADS, S, NSA_KVW + LANES), BF16),
                        pltpu.VMEM((NSA_KV_HEADS, S, NSA_KVW), BF16)],
        compiler_params=_params(("arbitrary", "arbitrary"), VMEM_LIMIT_LARGE),
        name="nsa_selected",
    )(pb, pb, pb, sel_t, key_tags, near)


WIN_PREV_TILES = (WIN_SIZE - 1 + TILE - 1) // TILE
WIN_Q_TILES = 2


def _win_body(q_ref, k_ref, v_ref, wb_ref, gate_ref, eg_ref, ocmp_ref, oslc_ref, y_ref):
    i = pl.program_id(1)
    tq = q_ref.shape[1]
    n_span = WIN_PREV_TILES + WIN_Q_TILES
    first = i * WIN_Q_TILES - WIN_PREV_TILES
    lo_half = lax.broadcasted_iota(jnp.int32, (tq, LANES), 1) < NSA_HEAD_DIM

    def attend(early):
        if early:
            k_tiles, v_tiles, negs = [], [], []
            for t in range(n_span):
                sl = pl.ds(pl.multiple_of(jnp.maximum(first + t, 0) * TILE, TILE), TILE)
                k_tiles.append(k_ref[0, sl, :])
                v_tiles.append(v_ref[0, sl, :])
                negs.append(jnp.full((tq, TILE), jnp.where(first + t < 0, NEG_INF, 0.0), F32))
            k_all = jnp.concatenate(k_tiles, axis=0)
            v_all = jnp.concatenate(v_tiles, axis=0)
            missing = jnp.concatenate(negs, axis=1)[None]
        else:
            sl = pl.ds(pl.multiple_of(first * TILE, TILE), n_span * TILE)
            k_all = k_ref[0, sl, :]
            v_all = v_ref[0, sl, :]
        zero = jnp.zeros_like(k_all)
        k_lo = lax.broadcasted_iota(jnp.int32, k_all.shape, 1) < NSA_HEAD_DIM
        k_g = (jnp.where(k_lo, k_all, zero), jnp.where(k_lo, zero, k_all))
        v_lane = lax.broadcasted_iota(jnp.int32, v_all.shape, 1)
        v_g = (jnp.where(k_lo, v_all, jnp.where(v_lane == DEN_LANE[0], 1.0, 0.0).astype(BF16)),
               jnp.where(k_lo, jnp.where(v_lane == DEN_LANE[1], 1.0, 0.0).astype(BF16), v_all))

        g_hi, g_lo = _split_bf16(jax.nn.sigmoid(gate_ref[0]))
        gates = _dot(g_hi, eg_ref[...]) + _dot(g_lo, eg_ref[...])
        q4 = jnp.concatenate([q_ref[0, :, r * LANES:(r + 1) * LANES] for r in range(NSA_GROUP)], axis=0)
        q4 = q4 * (NSA_HEAD_DIM ** -0.5)
        n_split = NSA_CHAIN_SPLIT
        heads = NSA_GROUP // n_split
        chains = [(g, part) for g in range(NSA_KV_HEADS) for part in range(n_split)]

        def scores(g, part):
            s = _dot_nt(q4[part * heads * tq:(part + 1) * heads * tq], k_g[g]).reshape(heads, tq, n_span * TILE)
            s = s + wb_ref[g * NSA_GROUP + part * heads:g * NSA_GROUP + (part + 1) * heads]
            return s + missing if early else s

        def weights(s):
            m = jnp.max(s, axis=-1, keepdims=True)
            return jnp.exp2(s - m).astype(BF16).reshape(heads * tq, n_span * TILE)

        def values(g, p):
            acc = _dot(p, v_g[g])
            return acc / acc[:, DEN_LANE[g]:DEN_LANE[g] + 1]

        s, p, o = {}, {}, {}
        for step in range(len(chains) + 2):
            if step < len(chains):
                s[step] = scores(*chains[step])
            if 0 <= step - 2 < len(chains):
                o[step - 2] = values(chains[step - 2][0], p.pop(step - 2))
            if 0 <= step - 1 < len(chains):
                p[step - 1] = weights(s.pop(step - 1))
        for r in range(NSA_GROUP):
            cols = slice(r * LANES, (r + 1) * LANES)
            part, rs = r // heads, slice((r % heads) * tq, (r % heads + 1) * tq)
            o_win = jnp.where(lo_half, o[part][rs], o[n_split + part][rs])
            y = (gates[:, r * LANES:(r + 1) * LANES] * ocmp_ref[0, :, cols]
                 + gates[:, NSA_QW + r * LANES:NSA_QW + (r + 1) * LANES] * oslc_ref[0, :, cols]
                 + gates[:, 2 * NSA_QW + r * LANES:2 * NSA_QW + (r + 1) * LANES] * o_win)
            y_ref[0, :, cols] = y.astype(y_ref.dtype)

    pl.when(first >= 0)(functools.partial(attend, False))
    pl.when(first < 0)(functools.partial(attend, True))


def _win_call(pb, pf, win_bias, gate_expand, o_cmp, o_slc):
    B, S, _ = pb.shape
    q_block = (3 * RET_W) // NSA_QW
    kw_block = (3 * RET_W + NSA_QW) // NSA_KVW + 2
    gate_block = (RET_W + 2 * NSA_KVW) // LANES
    tq = WIN_Q_TILES * TILE
    tile_spec = pl.BlockSpec((1, tq, NSA_QW), lambda b, i: (b, i, 0))
    return pl.pallas_call(
        _win_body,
        grid=(B, S // tq),
        in_specs=[pl.BlockSpec((1, tq, NSA_QW), lambda b, i: (b, i, q_block)),
                  pl.BlockSpec((1, S, NSA_KVW), lambda b, i: (b, 0, kw_block)),
                  pl.BlockSpec((1, S, NSA_KVW), lambda b, i: (b, 0, kw_block + 1)),
                  _resident(win_bias.shape),
                  pl.BlockSpec((1, tq, LANES), lambda b, i: (b, i, gate_block)),
                  _resident((LANES, 3 * NSA_QW)),
                  tile_spec, tile_spec],
        out_specs=tile_spec,
        out_shape=jax.ShapeDtypeStruct((B, S, NSA_QW), BF16),
        compiler_params=_params(("arbitrary", "arbitrary"), VMEM_LIMIT_LARGE),
        name="nsa_window_combine",
    )(pb, pb, pb, win_bias, pf, gate_expand, o_cmp, o_slc)


def _post_body(h_ref, ma_ref, mb_ref, mod_ref, woa_ref, wob_ref, lng_ref, lnb_ref,
               wg_ref, wu_ref, wd_ref, o_ref):
    y = _dot(ma_ref[0], woa_ref[...]) + _dot(mb_ref[0], wob_ref[...])
    h1 = _layer_norm(DEEPNORM_ALPHA * h_ref[0] + mod_ref[0, 2:3, :] * y, lng_ref[0:1, :], lnb_ref[0:1, :])
    u = (h1 * (1.0 + mod_ref[0, 4:5, :]) + mod_ref[0, 3:4, :]).astype(BF16)
    acc = jnp.zeros(h1.shape, F32)
    for c0, cn in FFN_CHUNKS:
        gate = _dot(u, wg_ref[:, c0:c0 + cn])
        up = _dot(u, wu_ref[:, c0:c0 + cn])
        acc = acc + _dot((_silu(gate) * up).astype(BF16), wd_ref[c0:c0 + cn, :])
    o_ref[0] = _layer_norm(DEEPNORM_ALPHA * h1 + mod_ref[0, 5:6, :] * acc, lng_ref[1:2, :], lnb_ref[1:2, :])


def _post_call(h, mix_a, mix_b, col_a, col_b, mod, wo_a, wo_b, ln_g, ln_b, w_gate, w_up, w_down, tm, name):
    B, S, _ = h.shape
    half = D_MODEL // 2
    tok = pl.BlockSpec((1, tm, D_MODEL), lambda b, t: (b, t, 0))
    return pl.pallas_call(
        _post_body,
        grid=(B, S // tm),
        in_specs=[tok,
                  pl.BlockSpec((1, tm, half), lambda b, t: (b, t, col_a)),
                  pl.BlockSpec((1, tm, half), lambda b, t: (b, t, col_b)),
                  pl.BlockSpec((1, 6, D_MODEL), lambda b, t: (b, 0, 0)),
                  _resident((half, D_MODEL)), _resident((half, D_MODEL)),
                  _resident((2, D_MODEL)), _resident((2, D_MODEL)),
                  _resident((D_MODEL, D_FF)), _resident((D_MODEL, D_FF)), _resident((D_FF, D_MODEL))],
        out_specs=tok,
        out_shape=jax.ShapeDtypeStruct((B, S, D_MODEL), F32),
        compiler_params=_params(("arbitrary", "arbitrary"), VMEM_LIMIT_LARGE),
        name=name,
    )(h, mix_a, mix_b, mod, wo_a, wo_b, ln_g, ln_b, w_gate, w_up, w_down)


DIL_STEP_TILES = 4
DIL_WEIGHT_PARTS = 2


def _dil_body(q_ref, k_ref, kp_ref, v_ref, vp_ref, b_ref, o_ref, lse_ref, *, tiles_per_seg, between=None):
    t = pl.program_id(1)
    res_tiles = min(tiles_per_seg, DIL_STEP_TILES)
    with_prev = tiles_per_seg > 1
    if tiles_per_seg > DIL_STEP_TILES:
        first_prev = jnp.where((t * DIL_STEP_TILES) % tiles_per_seg == 0, NEG_INF, 0.0)
    else:
        first_prev = NEG_INF

    def span(ref, before_ref, j, cols):
        if not with_prev:
            return ref[0, j * TILE:(j + 1) * TILE, cols]
        if j > 0:
            return ref[0, (j - 1) * TILE:(j + 1) * TILE, cols]
        return jnp.concatenate([before_ref[0, :, cols], ref[0, :TILE, cols]], axis=0)

    def tile_bias(h, j):
        if not with_prev:
            return b_ref[h, TILE:, :]
        if j % res_tiles != 0:
            return b_ref[h]
        gone = first_prev if j == 0 else NEG_INF
        return jnp.concatenate([b_ref[h, :TILE, :] + gone, b_ref[h, TILE:, :]], axis=0)

    head_cols = lambda h: slice(h * DIL_HEAD_DIM, (h + 1) * DIL_HEAD_DIM)
    pad = jnp.zeros((LANES - DIL_HEADS, TILE), F32)

    def scores(j):
        return jnp.concatenate(
            [_dot_nt(span(k_ref, kp_ref, j, head_cols(h)), q_ref[0, j * TILE:(j + 1) * TILE, head_cols(h)])
             + tile_bias(h, j) for h in range(DIL_HEADS)], axis=1)

    def weights(s):
        m = jnp.max(s, axis=0, keepdims=True)
        e = jnp.exp2(s - m)
        den = jnp.sum(e, axis=0, keepdims=True)
        lse = (m + jnp.log2(den)) * math.log(2.0)
        return (e * (1.0 / den)).astype(BF16), lse

    def values(j, p):
        for h in range(DIL_HEADS):
            o = _dot_tn(p[:, h * TILE:(h + 1) * TILE], span(v_ref, vp_ref, j, head_cols(h)))
            o_ref[0, j * TILE:(j + 1) * TILE, head_cols(h)] = o.astype(o_ref.dtype)

    s, p = {}, {}
    for step in range(DIL_STEP_TILES + 2):
        if step < DIL_STEP_TILES:
            s[step] = scores(step)
        if 0 <= step - 2 < DIL_STEP_TILES:
            values(step - 2, p.pop(step - 2))
        j = step - 1
        parts = []
        for part in range(DIL_WEIGHT_PARTS):
            if between is not None:
                between[step * DIL_WEIGHT_PARTS + part]()
            if 0 <= j < DIL_STEP_TILES:
                slab = DIL_HEADS * TILE // DIL_WEIGHT_PARTS
                parts.append(weights(s[j][:, part * slab:(part + 1) * slab]))
        if parts:
            del s[j]
            p[j] = jnp.concatenate([pp for pp, _ in parts], axis=1)
            lse = jnp.concatenate([ll for _, ll in parts], axis=1)
            by_head = [lse[:, h * TILE:(h + 1) * TILE] for h in range(DIL_HEADS)]
            lse_ref[0, j // res_tiles, (j % res_tiles) * TILE:(j % res_tiles + 1) * TILE, :] = (
                jnp.concatenate(by_head + [pad], axis=0).T)


def _dil_proj_body(q_ref, k_ref, kp_ref, v_ref, vp_ref, b_ref, h_ref, mod_ref, w_ref,
                   o_ref, lse_ref, proj_ref, h_scr, *, tiles_per_seg, dilation):
    n_pieces = (DIL_STEP_TILES + 2) * DIL_WEIGHT_PARTS
    width = proj_ref.shape[-1] // n_pieces
    tm = h_ref.shape[1]
    state = {}

    def piece(c):
        if c == 0:
            for lc in range(h_scr.shape[0]):
                h_scr[lc] = h_ref[0, :, lc * LANES:(lc + 1) * LANES]
            h = jnp.concatenate(
                [jnp.concatenate([h_scr[lc, pl.ds(r, tm // dilation, stride=dilation), :]
                                  for lc in range(h_scr.shape[0])], axis=1) for r in range(dilation)], axis=0)
            state["u"] = (h * (1.0 + mod_ref[0, 1:2, :]) + mod_ref[0, 0:1, :]).astype(BF16)
        cols = slice(c * width, (c + 1) * width)
        res = _dot(state["u"], w_ref[:, cols]).astype(proj_ref.dtype)
        proj_ref[0, :, :, cols] = res.reshape(dilation, tm // dilation, width)

    _dil_body(q_ref, k_ref, kp_ref, v_ref, vp_ref, b_ref, o_ref, lse_ref, tiles_per_seg=tiles_per_seg,
              between=[functools.partial(piece, c) for c in range(n_pieces)])


def _lse_out(B, S, dilation):
    step = DIL_STEP_TILES * TILE
    seg = S // dilation
    n_res = max(1, step // seg)
    rows = step // n_res
    spec = pl.BlockSpec((1, n_res, rows, LANES),
                        lambda b, i: (b, (i * step // seg) // n_res, (i * step % seg) // rows, 0))
    return spec, jax.ShapeDtypeStruct((B, dilation, seg, LANES), F32)


def _dil_proj_call(proj, bias, dilation, h, mod, w_next, dilation_next, name):
    B, S, _ = proj.shape
    step = DIL_STEP_TILES * TILE
    n_next = w_next.shape[1]
    own = lambda c: pl.BlockSpec((1, step, DIL_WIDTH), lambda b, i: (b, i, c))
    before = lambda c: pl.BlockSpec((1, TILE, DIL_WIDTH),
                                    lambda b, i: (b, jnp.maximum(i * DIL_STEP_TILES - 1, 0), c))
    tile = lambda width: pl.BlockSpec((1, step, width), lambda b, i: (b, i, 0))
    lse_spec, lse_shape = _lse_out(B, S, dilation)
    o, lse, proj_next = pl.pallas_call(
        functools.partial(_dil_proj_body, tiles_per_seg=(S // dilation) // TILE, dilation=dilation_next),
        grid=(B, S // step),
        in_specs=[own(0), own(1), before(1), own(2), before(2), _resident(bias.shape),
                  tile(D_MODEL), pl.BlockSpec((1, 6, D_MODEL), lambda b, i: (b, 0, 0)),
                  _resident((D_MODEL, n_next))],
        out_specs=[tile(DIL_WIDTH), lse_spec,
                   pl.BlockSpec((1, dilation_next, step // dilation_next, n_next), lambda b, i: (b, 0, i, 0))],
        out_shape=[jax.ShapeDtypeStruct((B, S, DIL_WIDTH), BF16), lse_shape,
                   jax.ShapeDtypeStruct((B, dilation_next, S // dilation_next, n_next), BF16)],
        scratch_shapes=[pltpu.VMEM((D_MODEL // LANES, step, LANES), F32)],
        compiler_params=_params(("arbitrary", "arbitrary"), VMEM_LIMIT_LARGE),
        name=name,
    )(proj, proj, proj, proj, proj, bias, h, mod, w_next)
    return o, lse, proj_next.reshape(B, S, n_next)


def _dil_call(proj, bias, dilation, name):
    B, S, _ = proj.shape
    step = DIL_STEP_TILES * TILE
    own = lambda c: pl.BlockSpec((1, step, DIL_WIDTH), lambda b, i: (b, i, c))
    before = lambda c: pl.BlockSpec((1, TILE, DIL_WIDTH),
                                    lambda b, i: (b, jnp.maximum(i * DIL_STEP_TILES - 1, 0), c))
    tile = lambda width: pl.BlockSpec((1, step, width), lambda b, i: (b, i, 0))
    return pl.pallas_call(
        functools.partial(_dil_body, tiles_per_seg=(S // dilation) // TILE),
        grid=(B, S // step),
        in_specs=[own(0), own(1), before(1), own(2), before(2), _resident(bias.shape)],
        out_specs=[tile(DIL_WIDTH), _lse_out(B, S, dilation)[0]],
        out_shape=[jax.ShapeDtypeStruct((B, S, DIL_WIDTH), BF16), _lse_out(B, S, dilation)[1]],
        compiler_params=_params(("arbitrary", "arbitrary"), VMEM_LIMIT_LARGE),
        name=name,
    )(proj, proj, proj, proj, proj, bias)


def _post_mix_body(h_ref, o0_ref, o1_ref, o2_ref, l0_ref, l1_ref, l2_ref, mod_ref, wo_ref, lng_ref, lnb_ref,
                   wg_ref, wu_ref, wd_ref, out_ref, mix_a, mix_b, o_scr, l_scr):
    n = pl.program_id(0)
    o_refs = (o0_ref, o1_ref, o2_ref)
    l_refs = (l0_ref, l1_ref, l2_ref)
    tm = out_ref.shape[1]

    @pl.when(n == 0)
    def _():
        mix_a[...] = jnp.zeros_like(mix_a)
        mix_b[...] = jnp.zeros_like(mix_b)

    def unpermute():
        for gi, (o_ref, l_ref) in enumerate(zip(o_refs, l_refs)):
            dilation = o_ref.shape[1]
            for r in range(dilation):
                rows = pl.ds(r, tm // dilation, stride=dilation)
                l_scr[gi, rows, :] = l_ref[0, r]
                for h in range(DIL_HEADS):
                    o_scr[gi, h, rows, :] = o_ref[0, r, :, h * DIL_HEAD_DIM:(h + 1) * DIL_HEAD_DIM].astype(F32)

    def mix_head(h, mix_w):
        lses = [jnp.broadcast_to(l_scr[gi, :, h:h + 1], (tm, DIL_HEAD_DIM)) for gi in range(len(o_refs))]
        m = jnp.maximum(jnp.maximum(lses[0], lses[1]), lses[2])
        ws = [jnp.exp(l - m) for l in lses]
        den = ws[0] + ws[1] + ws[2]
        y = sum((w / den) * o_scr[gi, h] for gi, w in enumerate(ws))
        mix_w[:, h * DIL_HEAD_DIM:(h + 1) * DIL_HEAD_DIM] = y.astype(mix_w.dtype)

    def step(mix_w, mix_r):
        y = _dot(mix_r[...], wo_ref[...])
        unpermute()
        h1 = _layer_norm(DEEPNORM_ALPHA * h_ref[0] + mod_ref[0, 2:3, :] * y, lng_ref[0:1, :], lnb_ref[0:1, :])
        u = (h1 * (1.0 + mod_ref[0, 4:5, :]) + mod_ref[0, 3:4, :]).astype(BF16)
        acc = jnp.zeros(h1.shape, F32)
        for ci, (c0, cn) in enumerate(FFN_CHUNKS):
            gate = _dot(u, wg_ref[:, c0:c0 + cn])
            up = _dot(u, wu_ref[:, c0:c0 + cn])
            for h in range(DIL_HEADS):
                if h * len(FFN_CHUNKS) // DIL_HEADS == ci:
                    mix_head(h, mix_w)
            acc = acc + _dot((_silu(gate) * up).astype(BF16), wd_ref[c0:c0 + cn, :])
        out_ref[0] = _layer_norm(DEEPNORM_ALPHA * h1 + mod_ref[0, 5:6, :] * acc, lng_ref[1:2, :], lnb_ref[1:2, :])

    pl.when(n % 2 == 0)(functools.partial(step, mix_a, mix_b))
    pl.when(n % 2 == 1)(functools.partial(step, mix_b, mix_a))


def _post_mix_call(h, outs, lses, mod, wo, ln_g, ln_b, w_gate, w_up, w_down, tm, name):
    B, S, _ = h.shape
    steps = S // tm
    last = B * steps - 1
    dilations = [d for _, d in DIL_PATTERNS]
    by_residue = lambda t, d: t.reshape(B, d, S // d, t.shape[-1])
    cur = lambda n: jnp.minimum(n, last)
    done = lambda n: jnp.maximum(n - 1, 0)
    src = lambda d, width: pl.BlockSpec((1, d, tm // d, width), lambda n: (cur(n) // steps, 0, cur(n) % steps, 0))
    tok = pl.BlockSpec((1, tm, D_MODEL), lambda n: (done(n) // steps, done(n) % steps, 0))
    return pl.pallas_call(
        _post_mix_body,
        grid=(B * steps + 1,),
        in_specs=[tok] + [src(d, DIL_WIDTH) for d in dilations] + [src(d, LANES) for d in dilations]
                 + [pl.BlockSpec((1, 6, D_MODEL), lambda n: (done(n) // steps, 0, 0)),
                    _resident((D_MODEL, D_MODEL)), _resident((2, D_MODEL)), _resident((2, D_MODEL)),
                    _resident((D_MODEL, D_FF)), _resident((D_MODEL, D_FF)), _resident((D_FF, D_MODEL))],
        out_specs=tok,
        out_shape=jax.ShapeDtypeStruct((B, S, D_MODEL), F32),
        scratch_shapes=[pltpu.VMEM((tm, D_MODEL), BF16), pltpu.VMEM((tm, D_MODEL), BF16),
                        pltpu.VMEM((len(dilations), DIL_HEADS, tm, DIL_HEAD_DIM), F32),
                        pltpu.VMEM((len(dilations), tm, LANES), F32)],
        compiler_params=_params(("arbitrary",), VMEM_LIMIT_LARGE),
        name=name,
    )(h, *[by_residue(o, d) for o, d in zip(outs, dilations)], *lses,
      mod, wo, ln_g, ln_b, w_gate, w_up, w_down)


def _t5_bucket_np(dist):
    n = np.maximum(dist, 0)
    max_exact = REL_BUCKETS // 2
    nf = np.maximum(n, 1).astype(np.float64)
    val = np.log(nf / max_exact) / math.log(REL_MAX_DIST / max_exact) * (REL_BUCKETS - max_exact)
    frac = np.abs(val - np.round(val))
    on_edge = (frac < 1e-9) & (n > max_exact) & (n < REL_MAX_DIST)
    assert not on_edge.any()
    large = np.minimum(max_exact + np.floor(val + 1e-9).astype(np.int64), REL_BUCKETS - 1)
    return np.where(n < max_exact, n, large).astype(np.int32)


def _shift_table(rel_bias, rows, cols, step, dist_fn, valid_fn):
    u = np.concatenate([np.arange(cols), np.arange(-(rows - 1) * step, 0)])
    period = u.size
    vals = jnp.take(rel_bias, jnp.asarray(_t5_bucket_np(dist_fn(u))), axis=0).T
    vals = jnp.where(jnp.asarray(valid_fn(u))[None], vals, NEG_INF)
    t = jnp.tile(vals, (1, rows))[:, :rows * (period - step)].reshape(vals.shape[0], rows, period - step)
    return t[:, :, :cols]


def _nsa_tables(rel_bias, S):
    rel_bias = rel_bias * LOG2_E
    always = lambda u: np.ones(u.shape, bool)
    win_dist = lambda u: WIN_PREV_TILES * TILE - u
    win_bias = _shift_table(rel_bias, WIN_Q_TILES * TILE, (WIN_PREV_TILES + WIN_Q_TILES) * TILE, 1, win_dist,
                            lambda u: (win_dist(u) >= 0) & (win_dist(u) <= WIN_SIZE - 1))
    d0 = _shift_table(rel_bias, TILE, TILE, 1, lambda u: -u, lambda u: u <= 0)
    d1 = _shift_table(rel_bias, TILE, TILE, 1, lambda u: TILE - u, always)
    far_bucket = _t5_bucket_np(np.arange(TILE + 1, S + TILE))
    assert (far_bucket == far_bucket[0]).all()
    far = rel_bias[int(far_bucket[0])][:, None, None]
    near = jnp.concatenate([d1 - far, d0 - far], axis=2)
    near = near.reshape(NSA_KV_HEADS, NSA_GROUP * TILE, 2 * TILE)
    cmp_bias = _shift_table(rel_bias, LANES, S, CMP_STRIDE, lambda u: u - (CMP_BLOCK - 1), always)
    cmp_bias = cmp_bias.transpose(0, 2, 1)
    n_cmp = (S - CMP_BLOCK) // CMP_STRIDE + 1
    n_sel = S // SEL_BLOCK
    cs = (np.arange(n_cmp) * CMP_STRIDE)[:, None]
    ss = (np.arange(n_sel) * SEL_BLOCK)[None, :]
    ov = np.clip(np.minimum(cs + CMP_BLOCK, ss + SEL_BLOCK) - np.maximum(cs, ss), 0, None) / CMP_BLOCK
    ov_t = np.zeros((n_sel, LANES), np.float32)
    ov_t[:, :n_cmp] = ov.T
    key = np.arange(S)[:, None]
    lane = np.arange(LANES)[None, :]
    tags = ((lane == key // SEL_BLOCK) | (lane == n_sel + key // TILE)).astype(np.float32)
    eg = np.zeros((LANES, 3 * NSA_QW), np.float32)
    for g in range(NSA_KV_HEADS):
        for r in range(NSA_GROUP):
            for j in range(3):
                base = j * NSA_QW + r * LANES + g * NSA_HEAD_DIM
                eg[g * NSA_GROUP * 3 + r * 3 + j, base:base + NSA_HEAD_DIM] = 1.0
    return win_bias, near, cmp_bias, jnp.asarray(ov_t, BF16), jnp.asarray(tags, BF16), jnp.asarray(eg, BF16)


def _dil_bias(rel_bias, dilation, max_dist):
    dist = lambda u: TILE + u
    return _shift_table(rel_bias * LOG2_E, 2 * TILE, TILE, 1, lambda u: dist(u) * dilation,
                        lambda u: (dist(u) >= 0) & (dist(u) <= max_dist))


def _nsa_head_perm():
    perm = np.zeros(NSA_QW, np.int64)
    for r in range(NSA_GROUP):
        for g in range(NSA_KV_HEADS):
            new = r * LANES + g * NSA_HEAD_DIM
            old = (g * NSA_GROUP + r) * NSA_HEAD_DIM
            perm[new:new + NSA_HEAD_DIM] = np.arange(old, old + NSA_HEAD_DIM)
    return perm


def _layer0_mixer(h, mod, ab_w_in, rel_bias, gn_g, gn_b, pos_k, pos_v, w1k, w2k, w1v, w2v):
    B, S, _ = h.shape
    o = np.cumsum((0, RET_W, RET_W, RET_W, RET_W, NSA_QW) + (NSA_KVW,) * 6 + (3 * NSA_HEADS,))
    seg = lambda a: ab_w_in[:, o[a]:o[a + 1]]
    gate_w = jnp.pad(seg(11), ((0, 0), (0, LANES - 3 * NSA_HEADS)))
    q_nsa = seg(4)[:, _nsa_head_perm()] * LOG2_E
    w = jnp.concatenate([seg(0), seg(1), seg(2), q_nsa, seg(7), seg(8), seg(9), seg(10),
                         seg(3), seg(5), seg(6), gate_w], axis=1).astype(BF16)
    pb, pf = _pre_call(h, mod, w, ((PRE0_BF16_COLS, BF16), (PRE0_F32_COLS, F32)), 1024, "pre0")

    y_ret = _ret_call(pb, pf, gn_g, gn_b)

    win_bias, near, cmp_bias, ov_t, key_tags, eg = _nsa_tables(rel_bias, S)
    kc, vc = _compress_call(pf, pos_k, pos_v, w1k, w1v, w2k, w2v)
    o_cmp, sel_t = _cmp_attn_call(pb, kc, vc, cmp_bias, ov_t)
    o_slc = _slc_call(pb, sel_t, key_tags, near)
    y_nsa = _win_call(pb, pf, win_bias, eg, o_cmp, o_slc)
    return y_ret, y_nsa


def _layer1_mixer(h, mod, dil_w_in, rel_bias):
    B, S, _ = h.shape
    def group_weights(gi):
        w = dil_w_in[:, gi * 3 * DIL_WIDTH:(gi + 1) * 3 * DIL_WIDTH]
        return jnp.concatenate([w[:, :DIL_WIDTH] * (DIL_HEAD_DIM ** -0.5 * LOG2_E), w[:, DIL_WIDTH:]],
                               axis=1).astype(BF16)

    outs, lses = [], []
    proj, = _pre_call(h, mod, group_weights(0), ((3 * DIL_WIDTH, BF16),), 1024, "pre1_0", DIL_PATTERNS[0][1])
    for gi, (window, dilation) in enumerate(DIL_PATTERNS):
        bias = _dil_bias(rel_bias, dilation, window // dilation)
        if gi + 1 < len(DIL_PATTERNS):
            o, lse, proj = _dil_proj_call(proj, bias, dilation, h, mod, group_weights(gi + 1),
                                          DIL_PATTERNS[gi + 1][1], f"dilated_{gi}_pre1_{gi + 1}")
        else:
            o, lse = _dil_call(proj, bias, dilation, f"dilated_{gi}")
        outs.append(o)
        lses.append(lse)
    return outs, lses


def kernel(x, c, rel_bias, ada_w, ada_b, ln_g, ln_b, ab_w_in, ab_w_out, ret_gn_g, ret_gn_b, cmp_pos_k, cmp_pos_v, cmp_k_w1, cmp_k_w2, cmp_v_w1, cmp_v_w2, dil_w_in, dil_w_out, ffn_w_gate, ffn_w_up, ffn_w_down):
    B = x.shape[0]
    mod = _ada_call(c, ada_w, ada_b).reshape(DEPTH, B, 6, D_MODEL)
    h = x
    for layer in range(DEPTH):
        i = layer // 2
        if layer % 2 == 0:
            mix_a, mix_b = _layer0_mixer(h, mod[layer], ab_w_in[i], rel_bias, ret_gn_g[i], ret_gn_b[i],
                                         cmp_pos_k[i], cmp_pos_v[i], cmp_k_w1[i], cmp_k_w2[i],
                                         cmp_v_w1[i], cmp_v_w2[i])
            wo_a = ab_w_out[i, :RET_W]
            wo_b = ab_w_out[i, RET_W:][_nsa_head_perm()]
            h = _post_call(h, mix_a, mix_b, 0, 0, mod[layer], wo_a.astype(BF16), wo_b.astype(BF16),
                           ln_g[layer], ln_b[layer], ffn_w_gate[layer].astype(BF16), ffn_w_up[layer].astype(BF16),
                           ffn_w_down[layer].astype(BF16), 512, f"post{layer}")
        else:
            outs, lses = _layer1_mixer(h, mod[layer], dil_w_in[i], rel_bias)
            h = _post_mix_call(h, outs, lses, mod[layer], dil_w_out[i].astype(BF16), ln_g[layer], ln_b[layer],
                               ffn_w_gate[layer].astype(BF16), ffn_w_up[layer].astype(BF16),
                               ffn_w_down[layer].astype(BF16), 512, f"post{layer}")
    return h
```

```python
import functools
import math

import numpy as np
import jax
import jax.numpy as jnp
from jax import lax
from jax.experimental import pallas as pl
from jax.experimental.pallas import tpu as pltpu

F32 = jnp.float32
BF16 = jnp.bfloat16

D_MODEL = 1024
DEPTH = 2
DEEPNORM_ALPHA = (2 * DEPTH) ** 0.25
LN_EPS = 1e-5
NEG_INF = -1e30
LOG2_E = math.log2(math.e)

RET_HEADS = 4
RET_HEAD_DIM = 128
RET_CHUNK = 128
ROPE_BASE = 10000.0
RET_W = RET_HEADS * RET_HEAD_DIM

NSA_HEADS = 8
NSA_KV_HEADS = 2
NSA_GROUP = 4
NSA_HEAD_DIM = 64
CMP_BLOCK = 32
CMP_STRIDE = 16
CMP_HIDDEN = 256
SEL_BLOCK = 64
SEL_TOP_N = 16
SEL_FORCE_SCORE = 1e4
WIN_SIZE = 512
NSA_QW = NSA_HEADS * NSA_HEAD_DIM
NSA_KVW = NSA_KV_HEADS * NSA_HEAD_DIM

DIL_PATTERNS = ((128, 1), (512, 4), (2048, 16))
DIL_HEADS = 8
DIL_HEAD_DIM = 128
DIL_WIDTH = DIL_HEADS * DIL_HEAD_DIM

REL_BUCKETS = 32
REL_MAX_DIST = 128
D_FF = 2816

LANES = 128
TILE = 128
VMEM_LIMIT_SMALL = 32 * 1024 * 1024
VMEM_LIMIT_LARGE = 56 * 1024 * 1024

PRE0_BF16_COLS = 3 * RET_W + NSA_QW + 4 * NSA_KVW
PRE0_F32_COLS = RET_W + 2 * NSA_KVW + LANES
FFN_CHUNKS = ((0, 768), (768, 768), (1536, 768), (2304, 512))
NSA_CHAIN_SPLIT = 2
DEN_LANE = (NSA_HEAD_DIM, 0)


def _dot(a, b):
    return jnp.dot(a, b, preferred_element_type=F32)


def _dot_nt(a, b):
    return lax.dot_general(a, b, (((1,), (1,)), ((), ())), preferred_element_type=F32)


def _dot_tn(a, b):
    return lax.dot_general(a, b, (((0,), (0,)), ((), ())), preferred_element_type=F32)


def _split_bf16(x):
    hi = x.astype(BF16)
    lo = (x - hi.astype(F32)).astype(BF16)
    return hi, lo


def _silu(x):
    return x * jax.nn.sigmoid(x)


def _layer_norm(x, g, b):
    mu = jnp.mean(x, axis=-1, keepdims=True)
    xc = x - mu
    var = jnp.mean(xc * xc, axis=-1, keepdims=True)
    return xc * lax.rsqrt(var + LN_EPS) * g + b


def _resident(shape):
    return pl.BlockSpec(shape, lambda *_: (0,) * len(shape), pipeline_mode=pl.Buffered(1))


def _params(sem, vmem):
    return pltpu.CompilerParams(dimension_semantics=sem, vmem_limit_bytes=vmem)


def _ada_body(c_ref, w_ref, b_ref, o_ref):
    a_hi, a_lo = _split_bf16(_silu(c_ref[...]))
    w_hi, w_lo = _split_bf16(w_ref[0])
    o_ref[0] = _dot(a_hi, w_hi) + _dot(a_lo, w_hi) + _dot(a_hi, w_lo) + b_ref[0]


def _ada_call(c, ada_w, ada_b):
    B = c.shape[0]
    n_out = ada_w.shape[-1]
    tn = n_out // 4
    return pl.pallas_call(
        _ada_body,
        grid=(DEPTH, n_out // tn),
        in_specs=[pl.BlockSpec((B, D_MODEL), lambda l, n: (0, 0)),
                  pl.BlockSpec((1, D_MODEL, tn), lambda l, n: (l, 0, n)),
                  pl.BlockSpec((1, 1, tn), lambda l, n: (l, 0, n))],
        out_specs=pl.BlockSpec((1, B, tn), lambda l, n: (l, 0, n)),
        out_shape=jax.ShapeDtypeStruct((DEPTH, B, n_out), F32),
        compiler_params=_params(("arbitrary", "arbitrary"), VMEM_LIMIT_LARGE),
        name="ada_mod",
    )(c, ada_w, ada_b.reshape(DEPTH, 1, n_out))


def _pre_body(h_ref, mod_ref, w_ref, *refs, dilation):
    if dilation == 1:
        o_refs, h = refs, h_ref[0]
    else:
        o_refs, h_scr = refs[:-1], refs[-1]
        tm = h_ref.shape[1]
        for c in range(h_scr.shape[0]):
            h_scr[c] = h_ref[0, :, c * LANES:(c + 1) * LANES]
        h = jnp.concatenate(
            [jnp.concatenate([h_scr[c, pl.ds(r, tm // dilation, stride=dilation), :]
                              for c in range(h_scr.shape[0])], axis=1) for r in range(dilation)], axis=0)
    u = (h * (1.0 + mod_ref[0, 1:2, :]) + mod_ref[0, 0:1, :]).astype(BF16)
    off = 0
    for o_ref in o_refs:
        n = o_ref.shape[-1]
        o_ref[0] = _dot(u, w_ref[:, off:off + n]).astype(o_ref.dtype).reshape(o_ref.shape[1:])
        off += n


def _pre_call(h, mod, w, out_cols_dtypes, tm, name, dilation=1):
    B, S, _ = h.shape
    n_total = w.shape[1]
    assert sum(n for n, _ in out_cols_dtypes) == n_total
    if dilation == 1:
        out_specs = [pl.BlockSpec((1, tm, n), lambda b, t: (b, t, 0)) for n, _ in out_cols_dtypes]
        out_shape = [jax.ShapeDtypeStruct((B, S, n), dt) for n, dt in out_cols_dtypes]
        scratch = []
    else:
        out_specs = [pl.BlockSpec((1, dilation, tm // dilation, n), lambda b, t: (b, 0, t, 0))
                     for n, _ in out_cols_dtypes]
        out_shape = [jax.ShapeDtypeStruct((B, dilation, S // dilation, n), dt) for n, dt in out_cols_dtypes]
        scratch = [pltpu.VMEM((D_MODEL // LANES, tm, LANES), F32)]
    outs = pl.pallas_call(
        functools.partial(_pre_body, dilation=dilation),
        grid=(B, S // tm),
        in_specs=[pl.BlockSpec((1, tm, D_MODEL), lambda b, t: (b, t, 0)),
                  pl.BlockSpec((1, 6, D_MODEL), lambda b, t: (b, 0, 0)),
                  _resident((D_MODEL, n_total))],
        out_specs=out_specs,
        out_shape=out_shape,
        scratch_shapes=scratch,
        compiler_params=_params(("arbitrary", "arbitrary"), VMEM_LIMIT_LARGE),
        name=name,
    )(h, mod, w)
    return [o.reshape(B, S, o.shape[-1]) for o in outs]


def _ret_body(q_ref, k_ref, v_ref, g_ref, cos_ref, sin_ref, dec_ref, qd_ref, kd_ref, cd_ref,
              gng_ref, gnb_ref, o_ref):
    n_chunks = q_ref.shape[1] // RET_CHUNK
    state = jnp.zeros((RET_HEAD_DIM, RET_HEAD_DIM), F32)
    for n in range(n_chunks):
        sl = pl.ds(n * RET_CHUNK, RET_CHUNK)
        q = q_ref[0, sl, :].astype(F32)
        k = k_ref[0, sl, :].astype(F32)
        v = v_ref[0, sl, :]
        c2 = cos_ref[sl, :]
        s2 = sin_ref[sl, :]
        qr = (q * c2 + pltpu.roll(q, RET_HEAD_DIM // 2, 1) * s2) * (RET_HEAD_DIM ** -0.5)
        kr = k * c2 + pltpu.roll(k, RET_HEAD_DIM // 2, 1) * s2
        scores = _dot_nt(qr.astype(BF16), kr.astype(BF16)) * dec_ref[0]
        inner = _dot(scores.astype(BF16), v)
        cross = _dot((qr * qd_ref[0]).astype(BF16), state.astype(BF16))
        kv = _dot_tn((kr * kd_ref[0]).astype(BF16), v)
        state = state * cd_ref[0] + kv
        y = inner + cross
        mu = jnp.mean(y, axis=-1, keepdims=True)
        yc = y - mu
        var = jnp.mean(yc * yc, axis=-1, keepdims=True)
        yn = yc * lax.rsqrt(var + LN_EPS)
        gate = g_ref[0, sl, :]
        o_ref[0, sl, :] = ((yn * gng_ref[...] + gnb_ref[...]) * _silu(gate)).astype(o_ref.dtype)


def _ret_tables(S):
    d = RET_HEAD_DIM
    inv = ROPE_BASE ** (-jnp.arange(0, d, 2, dtype=F32) / d)
    ang = jnp.arange(S).astype(F32)[:, None] * inv[None, :]
    cos, sin = jnp.cos(ang), jnp.sin(ang)
    cos2 = jnp.concatenate([cos, cos], axis=-1)
    sin2 = jnp.concatenate([-sin, sin], axis=-1)
    C = RET_CHUNK
    log_gamma = jnp.log1p(-jnp.exp2(-5.0 - jnp.arange(RET_HEADS, dtype=F32)))
    idx = jnp.arange(C, dtype=F32)
    diff = idx[:, None] - idx[None, :]
    dec = jnp.where(diff >= 0, jnp.exp(log_gamma[:, None, None] * jnp.maximum(diff, 0.0)), 0.0)
    kd = jnp.exp(log_gamma[:, None] * (C - 1 - idx)[None, :])
    qd = jnp.exp(log_gamma[:, None] * (idx + 1.0)[None, :])
    cd = jnp.exp(log_gamma * C)
    bc = lambda t: jnp.broadcast_to(t[:, :, None], (RET_HEADS, C, d))
    cdb = jnp.broadcast_to(cd[:, None, None], (RET_HEADS, d, d))
    return cos2, sin2, dec, bc(qd), bc(kd), cdb


def _ret_call(pb, pf, gn_g, gn_b):
    B, S, _ = pb.shape
    cos2, sin2, dec, qd, kd, cd = _ret_tables(S)
    col = lambda off: pl.BlockSpec((1, S, RET_HEAD_DIM), lambda b, h: (b, 0, off + h))
    tab = pl.BlockSpec((1, RET_CHUNK, RET_HEAD_DIM), lambda b, h: (h, 0, 0))
    vec = pl.BlockSpec((1, RET_HEAD_DIM), lambda b, h: (0, h))
    return pl.pallas_call(
        _ret_body,
        grid=(B, RET_HEADS),
        in_specs=[col(0), col(RET_HEADS), col(2 * RET_HEADS), col(0),
                  _resident((S, RET_HEAD_DIM)), _resident((S, RET_HEAD_DIM)),
                  tab, tab, tab, tab, vec, vec],
        out_specs=pl.BlockSpec((1, S, RET_HEAD_DIM), lambda b, h: (b, 0, h)),
        out_shape=jax.ShapeDtypeStruct((B, S, RET_W), BF16),
        compiler_params=_params(("arbitrary", "arbitrary"), VMEM_LIMIT_SMALL),
        name="retention",
    )(pb, pb, pb, pf, cos2, sin2, dec, qd, kd, cd, gn_g.reshape(1, RET_W), gn_b.reshape(1, RET_W))


def _compress_body(xk_ref, xv_ref, pk_ref, pv_ref, w1k_ref, w1v_ref, w2k_ref, w2v_ref, kc_ref, vc_ref):
    n_blk = kc_ref.shape[1]
    for x_ref, p_ref, w1_ref, w2_ref, o_ref in ((xk_ref, pk_ref, w1k_ref, w2k_ref, kc_ref),
                                                (xv_ref, pv_ref, w1v_ref, w2v_ref, vc_ref)):
        first = jnp.zeros((n_blk, w1_ref.shape[-1]), F32)
        second = jnp.zeros((n_blk, w1_ref.shape[-1]), F32)
        for t in range(CMP_STRIDE):
            x = x_ref[0, pl.ds(t, n_blk, stride=CMP_STRIDE), :]
            first = first + _dot((x + p_ref[t:t + 1, :]).astype(BF16), w1_ref[t])
            second = second + _dot((x + p_ref[CMP_STRIDE + t:CMP_STRIDE + t + 1, :]).astype(BF16),
                                   w1_ref[CMP_STRIDE + t])
        hid = _silu(first + pltpu.roll(second, n_blk - 1, 0))
        o_ref[0] = _dot(hid.astype(BF16), w2_ref[...]).astype(o_ref.dtype)


def _compress_call(pf, pos_k, pos_v, w1k, w1v, w2k, w2v):
    B, S, _ = pf.shape
    n_blk = S // CMP_STRIDE
    assert CMP_BLOCK == 2 * CMP_STRIDE

    def both_heads(t):
        z = jnp.zeros_like(t)
        return jnp.concatenate([jnp.concatenate([t, z], -1), jnp.concatenate([z, t], -1)], -2).astype(BF16)

    pos = lambda t: jnp.concatenate([t, t], axis=-1)
    w1 = lambda t: both_heads(t.reshape(CMP_BLOCK, NSA_HEAD_DIM, CMP_HIDDEN))
    kc_block = RET_W // NSA_KVW
    xspec = lambda c: pl.BlockSpec((1, S, NSA_KVW), lambda b: (b, 0, c))
    ospec = pl.BlockSpec((1, n_blk, NSA_KVW), lambda b: (b, 0, 0))
    return pl.pallas_call(
        _compress_body,
        grid=(B,),
        in_specs=[xspec(kc_block), xspec(kc_block + 1),
                  _resident((CMP_BLOCK, NSA_KVW)), _resident((CMP_BLOCK, NSA_KVW)),
                  _resident((CMP_BLOCK, NSA_KVW, 2 * CMP_HIDDEN)), _resident((CMP_BLOCK, NSA_KVW, 2 * CMP_HIDDEN)),
                  _resident((2 * CMP_HIDDEN, NSA_KVW)), _resident((2 * CMP_HIDDEN, NSA_KVW))],
        out_specs=[ospec, ospec],
        out_shape=[jax.ShapeDtypeStruct((B, n_blk, NSA_KVW), BF16)] * 2,
        compiler_params=_params(("arbitrary",), VMEM_LIMIT_SMALL),
        name="nsa_compress",
    )(pf, pf, pos(pos_k), pos(pos_v), w1(w1k), w1(w1v), both_heads(w2k), both_heads(w2v))


CMP_Q_ROWS = 1024


def _cmp_attn_body(q_ref, kc_ref, vc_ref, cb_ref, ov_ref, o_ref, sel_ref):
    i = pl.program_id(1)
    tq = q_ref.shape[1]
    lane = lax.broadcasted_iota(jnp.int32, (tq, LANES), 1)
    row = lax.broadcasted_iota(jnp.int32, (tq, LANES), 0)
    lo_half = lane < NSA_HEAD_DIM
    kc = kc_ref[0]
    vc = vc_ref[0]
    zero = jnp.zeros_like(kc)
    kc_lo = lax.broadcasted_iota(jnp.int32, kc.shape, 1) < NSA_HEAD_DIM
    kc_g = (jnp.where(kc_lo, kc, zero), jnp.where(kc_lo, zero, kc))
    valid = (lane * CMP_STRIDE + (CMP_BLOCK - 1) <= i * tq + row)[None]
    q4 = jnp.concatenate([q_ref[0, :, r * LANES:(r + 1) * LANES] for r in range(NSA_GROUP)], axis=0)
    q4 = q4 * (NSA_HEAD_DIM ** -0.5)
    psum, outs = [], []
    for g in range(NSA_KV_HEADS):
        s = _dot_nt(q4, kc_g[g]).reshape(NSA_GROUP, tq, LANES) + cb_ref[g * NSA_GROUP:(g + 1) * NSA_GROUP]
        s = jnp.where(valid, s, NEG_INF)
        m = jnp.max(s, axis=-1, keepdims=True)
        e = jnp.where(valid, jnp.exp2(s - m), 0.0)
        den = jnp.maximum(jnp.sum(e, axis=-1, keepdims=True), 1e-30)
        p = e / den
        psum.append(jnp.sum(p, axis=0))
        outs.append(_dot(p.reshape(NSA_GROUP * tq, LANES).astype(BF16), vc))
    for r in range(NSA_GROUP):
        rs = slice(r * tq, (r + 1) * tq)
        o_ref[0, :, r * LANES:(r + 1) * LANES] = jnp.where(lo_half, outs[0][rs], outs[1][rs])

    n_sel = sel_ref.shape[2]
    blk = lax.broadcasted_iota(jnp.int32, (n_sel, tq), 0)
    qblk = (i * tq + lax.broadcasted_iota(jnp.int32, (n_sel, tq), 1)) // SEL_BLOCK
    forced = jnp.where(blk == 0, 1.0, jnp.where(blk == qblk, 1.0, jnp.where(blk == qblk - 1, 1.0, 0.0)))
    for g in range(NSA_KV_HEADS):
        p_hi, p_lo = _split_bf16(psum[g])
        imp = _dot_nt(ov_ref[...], p_hi) + _dot_nt(ov_ref[...], p_lo)
        score = jnp.where(forced > 0.5, SEL_FORCE_SCORE, jnp.where(blk <= qblk, imp, -1.0))
        rank = jnp.zeros((n_sel, tq), F32)
        for other in range(n_sel):
            so = score[other:other + 1, :]
            tie = jnp.where(blk > other, 1.0, 0.0)
            rank = rank + jnp.where(so > score, 1.0, jnp.where(so == score, tie, 0.0))
        sel_ref[0, g] = jnp.where(rank < float(min(SEL_TOP_N, n_sel)), 1.0, 0.0)


def _cmp_attn_call(pb, kc, vc, cmp_bias, overlap_t):
    B, S, _ = pb.shape
    n_sel = S // SEL_BLOCK
    q_block = (3 * RET_W) // NSA_QW
    tq = CMP_Q_ROWS
    return pl.pallas_call(
        _cmp_attn_body,
        grid=(B, S // tq),
        in_specs=[pl.BlockSpec((1, tq, NSA_QW), lambda b, i: (b, i, q_block)),
                  pl.BlockSpec((1,) + kc.shape[1:], lambda b, i: (b, 0, 0)),
                  pl.BlockSpec((1,) + vc.shape[1:], lambda b, i: (b, 0, 0)),
                  pl.BlockSpec((NSA_HEADS, tq, LANES), lambda b, i: (0, i, 0)),
                  _resident((n_sel, LANES))],
        out_specs=[pl.BlockSpec((1, tq, NSA_QW), lambda b, i: (b, i, 0)),
                   pl.BlockSpec((1, NSA_KV_HEADS, n_sel, tq), lambda b, i: (b, 0, 0, i))],
        out_shape=[jax.ShapeDtypeStruct((B, S, NSA_QW), F32),
                   jax.ShapeDtypeStruct((B, NSA_KV_HEADS, n_sel, S), F32)],
        compiler_params=_params(("arbitrary", "arbitrary"), VMEM_LIMIT_SMALL),
        name="nsa_cmp_attn",
    )(pb, kc, vc, cmp_bias, overlap_t)


SLC_CLASS_TILES = 2


def _slc_far_tiles(cls, n_tiles):
    return max(min(SLC_CLASS_TILES * (cls + 1) - 2, n_tiles - 2), 1)


def _slc_body(q_ref, ks_ref, vs_ref, sel_ref, tag_ref, near_ref, o_ref, kaug_scr, vaug_scr):
    i = pl.program_id(1)
    n_tiles = ks_ref.shape[1] // TILE
    n_sel = sel_ref.shape[2]
    lane = lax.broadcasted_iota(jnp.int32, (TILE, LANES), 1)
    lo_half = lane < NSA_HEAD_DIM

    @pl.when(i == 0)
    def _():
        for t in range(n_tiles):
            sl = pl.ds(t * TILE, TILE)
            kt = ks_ref[0, sl, :]
            zero = jnp.zeros_like(kt)
            kaug_scr[0, sl, :] = jnp.concatenate([jnp.where(lo_half, kt, zero), tag_ref[sl, :]], axis=1)
            kaug_scr[1, sl, :] = jnp.concatenate([jnp.where(lo_half, zero, kt), tag_ref[sl, :]], axis=1)
            vt = vs_ref[0, sl, :]
            vaug_scr[0, sl, :] = jnp.where(lo_half, vt, jnp.where(lane == DEN_LANE[0], 1.0, 0.0).astype(BF16))
            vaug_scr[1, sl, :] = jnp.where(lo_half, jnp.where(lane == DEN_LANE[1], 1.0, 0.0).astype(BF16), vt)

    prev = jnp.maximum(i - 1, 0)
    own_sl = pl.ds(pl.multiple_of(i * TILE, TILE), TILE)
    prev_sl = pl.ds(pl.multiple_of(prev * TILE, TILE), TILE)
    no_prev = jnp.where(lax.broadcasted_iota(jnp.int32, (1, 2 * TILE), 1) < TILE,
                        jnp.where(i == 0, NEG_INF, 0.0), 0.0)
    tile_pen = jnp.where(lane - n_sel < i - 1, 0.0, NEG_INF)
    sel_pad = jnp.zeros((LANES - n_sel, TILE), F32)

    def tile_body(n_far):
        wf = n_far * TILE
        q4 = jnp.concatenate([q_ref[0, :, r * LANES:(r + 1) * LANES] for r in range(NSA_GROUP)], axis=0)
        q4 = q4 * (NSA_HEAD_DIM ** -0.5)
        q_far, q_near, k_near, v_near = [], [], [], []
        for g in range(NSA_KV_HEADS):
            v_near.append(jnp.concatenate([vaug_scr[g, prev_sl, :], vaug_scr[g, own_sl, :]], axis=0))
            sel_q = jnp.concatenate([sel_ref[0, g], sel_pad], axis=0).T
            blk_pen = (sel_q - 1.0) * (-NEG_INF)
            pen_near = jnp.where(lane < n_sel, blk_pen, 0.0).astype(BF16)
            pen_far = jnp.where(lane < n_sel, blk_pen,
                                jnp.where(lane < n_sel + n_tiles, tile_pen, 0.0)).astype(BF16)
            q_far.append(jnp.concatenate([q4, jnp.concatenate([pen_far] * NSA_GROUP, axis=0)], axis=1))
            q_near.append(jnp.concatenate([q4, jnp.concatenate([pen_near] * NSA_GROUP, axis=0)], axis=1))
            k_near.append(jnp.concatenate([kaug_scr[g, prev_sl, :], kaug_scr[g, own_sl, :]], axis=0))

        n_split = NSA_CHAIN_SPLIT
        rows = NSA_GROUP * TILE // n_split
        chains = [(g, part) for g in range(NSA_KV_HEADS) for part in range(n_split)]

        def scores(g, part):
            rs = slice(part * rows, (part + 1) * rows)
            return (_dot_nt(q_far[g][rs], kaug_scr[g, :wf, :]),
                    _dot_nt(q_near[g][rs], k_near[g]) + (near_ref[g, rs, :] + no_prev))

        def weights(s_far, s_near):
            m = jnp.maximum(jnp.max(s_far, axis=-1, keepdims=True), jnp.max(s_near, axis=-1, keepdims=True))
            return jnp.exp2(s_far - m).astype(BF16), jnp.exp2(s_near - m).astype(BF16)

        def values(g, p_far, p_near):
            acc = _dot(p_far, vaug_scr[g, :wf, :]) + _dot(p_near, v_near[g])
            return acc / acc[:, DEN_LANE[g]:DEN_LANE[g] + 1]

        s, p, o = {}, {}, {}
        for step in range(len(chains) + 2):
            if step < len(chains):
                s[step] = scores(*chains[step])
            if 0 <= step - 2 < len(chains):
                o[step - 2] = values(chains[step - 2][0], *p.pop(step - 2))
            if 0 <= step - 1 < len(chains):
                p[step - 1] = weights(*s.pop(step - 1))
        for r in range(NSA_GROUP):
            part, rs = r // (NSA_GROUP // n_split), slice((r % (NSA_GROUP // n_split)) * TILE,
                                                          (r % (NSA_GROUP // n_split) + 1) * TILE)
            o_ref[0, :, r * LANES:(r + 1) * LANES] = jnp.where(lo_half, o[part][rs], o[n_split + part][rs])

    n_classes = -(-n_tiles // SLC_CLASS_TILES)
    for cls in range(n_classes):
        pl.when(i // SLC_CLASS_TILES == cls)(functools.partial(tile_body, _slc_far_tiles(cls, n_tiles)))


def _slc_call(pb, sel_t, key_tags, near):
    B, S, _ = pb.shape
    n_sel = S // SEL_BLOCK
    n_tiles = S // TILE
    assert n_sel + n_tiles <= LANES
    q_block = (3 * RET_W) // NSA_QW
    ks_block = (3 * RET_W + NSA_QW) // NSA_KVW
    return pl.pallas_call(
        _slc_body,
        grid=(B, n_tiles),
        in_specs=[pl.BlockSpec((1, TILE, NSA_QW), lambda b, i: (b, i, q_block)),
                  pl.BlockSpec((1, S, NSA_KVW), lambda b, i: (b, 0, ks_block)),
                  pl.BlockSpec((1, S, NSA_KVW), lambda b, i: (b, 0, ks_block + 1)),
                  pl.BlockSpec((1, NSA_KV_HEADS, n_sel, TILE), lambda b, i: (b, 0, 0, i)),
                  _resident(key_tags.shape), _resident(near.shape)],
        out_specs=pl.BlockSpec((1, TILE, NSA_QW), lambda b, i: (b, i, 0)),
        out_shape=jax.ShapeDtypeStruct((B, S, NSA_QW), F32),
        scratch_shapes=[pltpu.VMEM((NSA_KV_HEADS, S, NSA_KVW + LANES), BF16),
                        pltpu.VMEM((NSA_KV_HEADS, S, NSA_KVW), BF16)],
        compiler_params=_params(("arbitrary", "arbitrary"), VMEM_LIMIT_LARGE),
        name="nsa_selected",
    )(pb, pb, pb, sel_t, key_tags, near)


WIN_PREV_TILES = (WIN_SIZE - 1 + TILE - 1) // TILE
WIN_Q_TILES = 2


def _win_body(q_ref, k_ref, v_ref, wb_ref, gate_ref, eg_ref, ocmp_ref, oslc_ref, y_ref):
    i = pl.program_id(1)
    tq = q_ref.shape[1]
    n_span = WIN_PREV_TILES + WIN_Q_TILES
    first = i * WIN_Q_TILES - WIN_PREV_TILES
    lo_half = lax.broadcasted_iota(jnp.int32, (tq, LANES), 1) < NSA_HEAD_DIM

    def attend(early):
        if early:
            k_tiles, v_tiles, negs = [], [], []
            for t in range(n_span):
                sl = pl.ds(pl.multiple_of(jnp.maximum(first + t, 0) * TILE, TILE), TILE)
                k_tiles.append(k_ref[0, sl, :])
                v_tiles.append(v_ref[0, sl, :])
                negs.append(jnp.full((tq, TILE), jnp.where(first + t < 0, NEG_INF, 0.0), F32))
            k_all = jnp.concatenate(k_tiles, axis=0)
            v_all = jnp.concatenate(v_tiles, axis=0)
            missing = jnp.concatenate(negs, axis=1)[None]
        else:
            sl = pl.ds(pl.multiple_of(first * TILE, TILE), n_span * TILE)
            k_all = k_ref[0, sl, :]
            v_all = v_ref[0, sl, :]
        zero = jnp.zeros_like(k_all)
        k_lo = lax.broadcasted_iota(jnp.int32, k_all.shape, 1) < NSA_HEAD_DIM
        k_g = (jnp.where(k_lo, k_all, zero), jnp.where(k_lo, zero, k_all))
        v_lane = lax.broadcasted_iota(jnp.int32, v_all.shape, 1)
        v_g = (jnp.where(k_lo, v_all, jnp.where(v_lane == DEN_LANE[0], 1.0, 0.0).astype(BF16)),
               jnp.where(k_lo, jnp.where(v_lane == DEN_LANE[1], 1.0, 0.0).astype(BF16), v_all))

        g_hi, g_lo = _split_bf16(jax.nn.sigmoid(gate_ref[0]))
        gates = _dot(g_hi, eg_ref[...]) + _dot(g_lo, eg_ref[...])
        q4 = jnp.concatenate([q_ref[0, :, r * LANES:(r + 1) * LANES] for r in range(NSA_GROUP)], axis=0)
        q4 = q4 * (NSA_HEAD_DIM ** -0.5)
        n_split = NSA_CHAIN_SPLIT
        heads = NSA_GROUP // n_split
        chains = [(g, part) for g in range(NSA_KV_HEADS) for part in range(n_split)]

        def scores(g, part):
            s = _dot_nt(q4[part * heads * tq:(part + 1) * heads * tq], k_g[g]).reshape(heads, tq, n_span * TILE)
            s = s + wb_ref[g * NSA_GROUP + part * heads:g * NSA_GROUP + (part + 1) * heads]
            return s + missing if early else s

        def weights(s):
            m = jnp.max(s, axis=-1, keepdims=True)
            return jnp.exp2(s - m).astype(BF16).reshape(heads * tq, n_span * TILE)

        def values(g, p):
            acc = _dot(p, v_g[g])
            return acc / acc[:, DEN_LANE[g]:DEN_LANE[g] + 1]

        s, p, o = {}, {}, {}
        for step in range(len(chains) + 2):
            if step < len(chains):
                s[step] = scores(*chains[step])
            if 0 <= step - 2 < len(chains):
                o[step - 2] = values(chains[step - 2][0], p.pop(step - 2))
            if 0 <= step - 1 < len(chains):
                p[step - 1] = weights(s.pop(step - 1))
        for r in range(NSA_GROUP):
            cols = slice(r * LANES, (r + 1) * LANES)
            part, rs = r // heads, slice((r % heads) * tq, (r % heads + 1) * tq)
            o_win = jnp.where(lo_half, o[part][rs], o[n_split + part][rs])
            y = (gates[:, r * LANES:(r + 1) * LANES] * ocmp_ref[0, :, cols]
                 + gates[:, NSA_QW + r * LANES:NSA_QW + (r + 1) * LANES] * oslc_ref[0, :, cols]
                 + gates[:, 2 * NSA_QW + r * LANES:2 * NSA_QW + (r + 1) * LANES] * o_win)
            y_ref[0, :, cols] = y.astype(y_ref.dtype)

    pl.when(first >= 0)(functools.partial(attend, False))
    pl.when(first < 0)(functools.partial(attend, True))


def _win_call(pb, pf, win_bias, gate_expand, o_cmp, o_slc):
    B, S, _ = pb.shape
    q_block = (3 * RET_W) // NSA_QW
    kw_block = (3 * RET_W + NSA_QW) // NSA_KVW + 2
    gate_block = (RET_W + 2 * NSA_KVW) // LANES
    tq = WIN_Q_TILES * TILE
    tile_spec = pl.BlockSpec((1, tq, NSA_QW), lambda b, i: (b, i, 0))
    return pl.pallas_call(
        _win_body,
        grid=(B, S // tq),
        in_specs=[pl.BlockSpec((1, tq, NSA_QW), lambda b, i: (b, i, q_block)),
                  pl.BlockSpec((1, S, NSA_KVW), lambda b, i: (b, 0, kw_block)),
                  pl.BlockSpec((1, S, NSA_KVW), lambda b, i: (b, 0, kw_block + 1)),
                  _resident(win_bias.shape),
                  pl.BlockSpec((1, tq, LANES), lambda b, i: (b, i, gate_block)),
                  _resident((LANES, 3 * NSA_QW)),
                  tile_spec, tile_spec],
        out_specs=tile_spec,
        out_shape=jax.ShapeDtypeStruct((B, S, NSA_QW), BF16),
        compiler_params=_params(("arbitrary", "arbitrary"), VMEM_LIMIT_LARGE),
        name="nsa_window_combine",
    )(pb, pb, pb, win_bias, pf, gate_expand, o_cmp, o_slc)


def _post_body(h_ref, ma_ref, mb_ref, mod_ref, woa_ref, wob_ref, lng_ref, lnb_ref,
               wg_ref, wu_ref, wd_ref, o_ref):
    y = _dot(ma_ref[0], woa_ref[...]) + _dot(mb_ref[0], wob_ref[...])
    h1 = _layer_norm(DEEPNORM_ALPHA * h_ref[0] + mod_ref[0, 2:3, :] * y, lng_ref[0:1, :], lnb_ref[0:1, :])
    u = (h1 * (1.0 + mod_ref[0, 4:5, :]) + mod_ref[0, 3:4, :]).astype(BF16)
    acc = jnp.zeros(h1.shape, F32)
    for c0, cn in FFN_CHUNKS:
        gate = _dot(u, wg_ref[:, c0:c0 + cn])
        up = _dot(u, wu_ref[:, c0:c0 + cn])
        acc = acc + _dot((_silu(gate) * up).astype(BF16), wd_ref[c0:c0 + cn, :])
    o_ref[0] = _layer_norm(DEEPNORM_ALPHA * h1 + mod_ref[0, 5:6, :] * acc, lng_ref[1:2, :], lnb_ref[1:2, :])


def _post_call(h, mix_a, mix_b, col_a, col_b, mod, wo_a, wo_b, ln_g, ln_b, w_gate, w_up, w_down, tm, name):
    B, S, _ = h.shape
    half = D_MODEL // 2
    tok = pl.BlockSpec((1, tm, D_MODEL), lambda b, t: (b, t, 0))
    return pl.pallas_call(
        _post_body,
        grid=(B, S // tm),
        in_specs=[tok,
                  pl.BlockSpec((1, tm, half), lambda b, t: (b, t, col_a)),
                  pl.BlockSpec((1, tm, half), lambda b, t: (b, t, col_b)),
                  pl.BlockSpec((1, 6, D_MODEL), lambda b, t: (b, 0, 0)),
                  _resident((half, D_MODEL)), _resident((half, D_MODEL)),
                  _resident((2, D_MODEL)), _resident((2, D_MODEL)),
                  _resident((D_MODEL, D_FF)), _resident((D_MODEL, D_FF)), _resident((D_FF, D_MODEL))],
        out_specs=tok,
        out_shape=jax.ShapeDtypeStruct((B, S, D_MODEL), F32),
        compiler_params=_params(("arbitrary", "arbitrary"), VMEM_LIMIT_LARGE),
        name=name,
    )(h, mix_a, mix_b, mod, wo_a, wo_b, ln_g, ln_b, w_gate, w_up, w_down)


DIL_STEP_TILES = 4
DIL_WEIGHT_PARTS = 2


def _dil_body(q_ref, k_ref, kp_ref, v_ref, vp_ref, b_ref, o_ref, lse_ref, *, tiles_per_seg, between=None):
    t = pl.program_id(1)
    res_tiles = min(tiles_per_seg, DIL_STEP_TILES)
    with_prev = tiles_per_seg > 1
    if tiles_per_seg > DIL_STEP_TILES:
        first_prev = jnp.where((t * DIL_STEP_TILES) % tiles_per_seg == 0, NEG_INF, 0.0)
    else:
        first_prev = NEG_INF

    def span(ref, before_ref, j, cols):
        if not with_prev:
            return ref[0, j * TILE:(j + 1) * TILE, cols]
        if j > 0:
            return ref[0, (j - 1) * TILE:(j + 1) * TILE, cols]
        return jnp.concatenate([before_ref[0, :, cols], ref[0, :TILE, cols]], axis=0)

    def tile_bias(h, j):
        if not with_prev:
            return b_ref[h, TILE:, :]
        if j % res_tiles != 0:
            return b_ref[h]
        gone = first_prev if j == 0 else NEG_INF
        return jnp.concatenate([b_ref[h, :TILE, :] + gone, b_ref[h, TILE:, :]], axis=0)

    head_cols = lambda h: slice(h * DIL_HEAD_DIM, (h + 1) * DIL_HEAD_DIM)
    pad = jnp.zeros((LANES - DIL_HEADS, TILE), F32)

    def scores(j):
        return jnp.concatenate(
            [_dot_nt(span(k_ref, kp_ref, j, head_cols(h)), q_ref[0, j * TILE:(j + 1) * TILE, head_cols(h)])
             + tile_bias(h, j) for h in range(DIL_HEADS)], axis=1)

    def weights(s):
        m = jnp.max(s, axis=0, keepdims=True)
        e = jnp.exp2(s - m)
        den = jnp.sum(e, axis=0, keepdims=True)
        lse = (m + jnp.log2(den)) * math.log(2.0)
        return (e * (1.0 / den)).astype(BF16), lse

    def values(j, p):
        for h in range(DIL_HEADS):
            o = _dot_tn(p[:, h * TILE:(h + 1) * TILE], span(v_ref, vp_ref, j, head_cols(h)))
            o_ref[0, j * TILE:(j + 1) * TILE, head_cols(h)] = o.astype(o_ref.dtype)

    s, p = {}, {}
    for step in range(DIL_STEP_TILES + 2):
        if step < DIL_STEP_TILES:
            s[step] = scores(step)
        if 0 <= step - 2 < DIL_STEP_TILES:
            values(step - 2, p.pop(step - 2))
        j = step - 1
        parts = []
        for part in range(DIL_WEIGHT_PARTS):
            if between is not None:
                between[step * DIL_WEIGHT_PARTS + part]()
            if 0 <= j < DIL_STEP_TILES:
                slab = DIL_HEADS * TILE // DIL_WEIGHT_PARTS
                parts.append(weights(s[j][:, part * slab:(part + 1) * slab]))
        if parts:
            del s[j]
            p[j] = jnp.concatenate([pp for pp, _ in parts], axis=1)
            lse = jnp.concatenate([ll for _, ll in parts], axis=1)
            by_head = [lse[:, h * TILE:(h + 1) * TILE] for h in range(DIL_HEADS)]
            lse_ref[0, j // res_tiles, (j % res_tiles) * TILE:(j % res_tiles + 1) * TILE, :] = (
                jnp.concatenate(by_head + [pad], axis=0).T)


def _dil_proj_body(q_ref, k_ref, kp_ref, v_ref, vp_ref, b_ref, h_ref, mod_ref, w_ref,
                   o_ref, lse_ref, proj_ref, h_scr, *, tiles_per_seg, dilation):
    n_pieces = (DIL_STEP_TILES + 2) * DIL_WEIGHT_PARTS
    width = proj_ref.shape[-1] // n_pieces
    tm = h_ref.shape[1]
    state = {}

    def piece(c):
        if c == 0:
            for lc in range(h_scr.shape[0]):
                h_scr[lc] = h_ref[0, :, lc * LANES:(lc + 1) * LANES]
            h = jnp.concatenate(
                [jnp.concatenate([h_scr[lc, pl.ds(r, tm // dilation, stride=dilation), :]
                                  for lc in range(h_scr.shape[0])], axis=1) for r in range(dilation)], axis=0)
            state["u"] = (h * (1.0 + mod_ref[0, 1:2, :]) + mod_ref[0, 0:1, :]).astype(BF16)
        cols = slice(c * width, (c + 1) * width)
        res = _dot(state["u"], w_ref[:, cols]).astype(proj_ref.dtype)
        proj_ref[0, :, :, cols] = res.reshape(dilation, tm // dilation, width)

    _dil_body(q_ref, k_ref, kp_ref, v_ref, vp_ref, b_ref, o_ref, lse_ref, tiles_per_seg=tiles_per_seg,
              between=[functools.partial(piece, c) for c in range(n_pieces)])


def _lse_out(B, S, dilation):
    step = DIL_STEP_TILES * TILE
    seg = S // dilation
    n_res = max(1, step // seg)
    rows = step // n_res
    spec = pl.BlockSpec((1, n_res, rows, LANES),
                        lambda b, i: (b, (i * step // seg) // n_res, (i * step % seg) // rows, 0))
    return spec, jax.ShapeDtypeStruct((B, dilation, seg, LANES), F32)


def _dil_proj_call(proj, bias, dilation, h, mod, w_next, dilation_next, name):
    B, S, _ = proj.shape
    step = DIL_STEP_TILES * TILE
    n_next = w_next.shape[1]
    own = lambda c: pl.BlockSpec((1, step, DIL_WIDTH), lambda b, i: (b, i, c))
    before = lambda c: pl.BlockSpec((1, TILE, DIL_WIDTH),
                                    lambda b, i: (b, jnp.maximum(i * DIL_STEP_TILES - 1, 0), c))
    tile = lambda width: pl.BlockSpec((1, step, width), lambda b, i: (b, i, 0))
    lse_spec, lse_shape = _lse_out(B, S, dilation)
    o, lse, proj_next = pl.pallas_call(
        functools.partial(_dil_proj_body, tiles_per_seg=(S // dilation) // TILE, dilation=dilation_next),
        grid=(B, S // step),
        in_specs=[own(0), own(1), before(1), own(2), before(2), _resident(bias.shape),
                  tile(D_MODEL), pl.BlockSpec((1, 6, D_MODEL), lambda b, i: (b, 0, 0)),
                  _resident((D_MODEL, n_next))],
        out_specs=[tile(DIL_WIDTH), lse_spec,
                   pl.BlockSpec((1, dilation_next, step // dilation_next, n_next), lambda b, i: (b, 0, i, 0))],
        out_shape=[jax.ShapeDtypeStruct((B, S, DIL_WIDTH), BF16), lse_shape,
                   jax.ShapeDtypeStruct((B, dilation_next, S // dilation_next, n_next), BF16)],
        scratch_shapes=[pltpu.VMEM((D_MODEL // LANES, step, LANES), F32)],
        compiler_params=_params(("arbitrary", "arbitrary"), VMEM_LIMIT_LARGE),
        name=name,
    )(proj, proj, proj, proj, proj, bias, h, mod, w_next)
    return o, lse, proj_next.reshape(B, S, n_next)


def _dil_call(proj, bias, dilation, name):
    B, S, _ = proj.shape
    step = DIL_STEP_TILES * TILE
    own = lambda c: pl.BlockSpec((1, step, DIL_WIDTH), lambda b, i: (b, i, c))
    before = lambda c: pl.BlockSpec((1, TILE, DIL_WIDTH),
                                    lambda b, i: (b, jnp.maximum(i * DIL_STEP_TILES - 1, 0), c))
    tile = lambda width: pl.BlockSpec((1, step, width), lambda b, i: (b, i, 0))
    return pl.pallas_call(
        functools.partial(_dil_body, tiles_per_seg=(S // dilation) // TILE),
        grid=(B, S // step),
        in_specs=[own(0), own(1), before(1), own(2), before(2), _resident(bias.shape)],
        out_specs=[tile(DIL_WIDTH), _lse_out(B, S, dilation)[0]],
        out_shape=[jax.ShapeDtypeStruct((B, S, DIL_WIDTH), BF16), _lse_out(B, S, dilation)[1]],
        compiler_params=_params(("arbitrary", "arbitrary"), VMEM_LIMIT_LARGE),
        name=name,
    )(proj, proj, proj, proj, proj, bias)


def _post_mix_body(h_ref, o0_ref, o1_ref, o2_ref, l0_ref, l1_ref, l2_ref, mod_ref, wo_ref, lng_ref, lnb_ref,
                   wg_ref, wu_ref, wd_ref, out_ref, mix_a, mix_b, o_scr, l_scr):
    n = pl.program_id(0)
    o_refs = (o0_ref, o1_ref, o2_ref)
    l_refs = (l0_ref, l1_ref, l2_ref)
    tm = out_ref.shape[1]

    @pl.when(n == 0)
    def _():
        mix_a[...] = jnp.zeros_like(mix_a)
        mix_b[...] = jnp.zeros_like(mix_b)

    def unpermute():
        for gi, (o_ref, l_ref) in enumerate(zip(o_refs, l_refs)):
            dilation = o_ref.shape[1]
            for r in range(dilation):
                rows = pl.ds(r, tm // dilation, stride=dilation)
                l_scr[gi, rows, :] = l_ref[0, r]
                for h in range(DIL_HEADS):
                    o_scr[gi, h, rows, :] = o_ref[0, r, :, h * DIL_HEAD_DIM:(h + 1) * DIL_HEAD_DIM].astype(F32)

    def mix_head(h, mix_w):
        lses = [jnp.broadcast_to(l_scr[gi, :, h:h + 1], (tm, DIL_HEAD_DIM)) for gi in range(len(o_refs))]
        m = jnp.maximum(jnp.maximum(lses[0], lses[1]), lses[2])
        ws = [jnp.exp(l - m) for l in lses]
        den = ws[0] + ws[1] + ws[2]
        y = sum((w / den) * o_scr[gi, h] for gi, w in enumerate(ws))
        mix_w[:, h * DIL_HEAD_DIM:(h + 1) * DIL_HEAD_DIM] = y.astype(mix_w.dtype)

    def step(mix_w, mix_r):
        y = _dot(mix_r[...], wo_ref[...])
        unpermute()
        h1 = _layer_norm(DEEPNORM_ALPHA * h_ref[0] + mod_ref[0, 2:3, :] * y, lng_ref[0:1, :], lnb_ref[0:1, :])
        u = (h1 * (1.0 + mod_ref[0, 4:5, :]) + mod_ref[0, 3:4, :]).astype(BF16)
        acc = jnp.zeros(h1.shape, F32)
        for ci, (c0, cn) in enumerate(FFN_CHUNKS):
            gate = _dot(u, wg_ref[:, c0:c0 + cn])
            up = _dot(u, wu_ref[:, c0:c0 + cn])
            for h in range(DIL_HEADS):
                if h * len(FFN_CHUNKS) // DIL_HEADS == ci:
                    mix_head(h, mix_w)
            acc = acc + _dot((_silu(gate) * up).astype(BF16), wd_ref[c0:c0 + cn, :])
        out_ref[0] = _layer_norm(DEEPNORM_ALPHA * h1 + mod_ref[0, 5:6, :] * acc, lng_ref[1:2, :], lnb_ref[1:2, :])

    pl.when(n % 2 == 0)(functools.partial(step, mix_a, mix_b))
    pl.when(n % 2 == 1)(functools.partial(step, mix_b, mix_a))


def _post_mix_call(h, outs, lses, mod, wo, ln_g, ln_b, w_gate, w_up, w_down, tm, name):
    B, S, _ = h.shape
    steps = S // tm
    last = B * steps - 1
    dilations = [d for _, d in DIL_PATTERNS]
    by_residue = lambda t, d: t.reshape(B, d, S // d, t.shape[-1])
    cur = lambda n: jnp.minimum(n, last)
    done = lambda n: jnp.maximum(n - 1, 0)
    src = lambda d, width: pl.BlockSpec((1, d, tm // d, width), lambda n: (cur(n) // steps, 0, cur(n) % steps, 0))
    tok = pl.BlockSpec((1, tm, D_MODEL), lambda n: (done(n) // steps, done(n) % steps, 0))
    return pl.pallas_call(
        _post_mix_body,
        grid=(B * steps + 1,),
        in_specs=[tok] + [src(d, DIL_WIDTH) for d in dilations] + [src(d, LANES) for d in dilations]
                 + [pl.BlockSpec((1, 6, D_MODEL), lambda n: (done(n) // steps, 0, 0)),
                    _resident((D_MODEL, D_MODEL)), _resident((2, D_MODEL)), _resident((2, D_MODEL)),
                    _resident((D_MODEL, D_FF)), _resident((D_MODEL, D_FF)), _resident((D_FF, D_MODEL))],
        out_specs=tok,
        out_shape=jax.ShapeDtypeStruct((B, S, D_MODEL), F32),
        scratch_shapes=[pltpu.VMEM((tm, D_MODEL), BF16), pltpu.VMEM((tm, D_MODEL), BF16),
                        pltpu.VMEM((len(dilations), DIL_HEADS, tm, DIL_HEAD_DIM), F32),
                        pltpu.VMEM((len(dilations), tm, LANES), F32)],
        compiler_params=_params(("arbitrary",), VMEM_LIMIT_LARGE),
        name=name,
    )(h, *[by_residue(o, d) for o, d in zip(outs, dilations)], *lses,
      mod, wo, ln_g, ln_b, w_gate, w_up, w_down)


def _t5_bucket_np(dist):
    n = np.maximum(dist, 0)
    max_exact = REL_BUCKETS // 2
    nf = np.maximum(n, 1).astype(np.float64)
    val = np.log(nf / max_exact) / math.log(REL_MAX_DIST / max_exact) * (REL_BUCKETS - max_exact)
    frac = np.abs(val - np.round(val))
    on_edge = (frac < 1e-9) & (n > max_exact) & (n < REL_MAX_DIST)
    assert not on_edge.any()
    large = np.minimum(max_exact + np.floor(val + 1e-9).astype(np.int64), REL_BUCKETS - 1)
    return np.where(n < max_exact, n, large).astype(np.int32)


def _shift_table(rel_bias, rows, cols, step, dist_fn, valid_fn):
    u = np.concatenate([np.arange(cols), np.arange(-(rows - 1) * step, 0)])
    period = u.size
    vals = jnp.take(rel_bias, jnp.asarray(_t5_bucket_np(dist_fn(u))), axis=0).T
    vals = jnp.where(jnp.asarray(valid_fn(u))[None], vals, NEG_INF)
    t = jnp.tile(vals, (1, rows))[:, :rows * (period - step)].reshape(vals.shape[0], rows, period - step)
    return t[:, :, :cols]


def _nsa_tables(rel_bias, S):
    rel_bias = rel_bias * LOG2_E
    always = lambda u: np.ones(u.shape, bool)
    win_dist = lambda u: WIN_PREV_TILES * TILE - u
    win_bias = _shift_table(rel_bias, WIN_Q_TILES * TILE, (WIN_PREV_TILES + WIN_Q_TILES) * TILE, 1, win_dist,
                            lambda u: (win_dist(u) >= 0) & (win_dist(u) <= WIN_SIZE - 1))
    d0 = _shift_table(rel_bias, TILE, TILE, 1, lambda u: -u, lambda u: u <= 0)
    d1 = _shift_table(rel_bias, TILE, TILE, 1, lambda u: TILE - u, always)
    far_bucket = _t5_bucket_np(np.arange(TILE + 1, S + TILE))
    assert (far_bucket == far_bucket[0]).all()
    far = rel_bias[int(far_bucket[0])][:, None, None]
    near = jnp.concatenate([d1 - far, d0 - far], axis=2)
    near = near.reshape(NSA_KV_HEADS, NSA_GROUP * TILE, 2 * TILE)
    cmp_bias = _shift_table(rel_bias, LANES, S, CMP_STRIDE, lambda u: u - (CMP_BLOCK - 1), always)
    cmp_bias = cmp_bias.transpose(0, 2, 1)
    n_cmp = (S - CMP_BLOCK) // CMP_STRIDE + 1
    n_sel = S // SEL_BLOCK
    cs = (np.arange(n_cmp) * CMP_STRIDE)[:, None]
    ss = (np.arange(n_sel) * SEL_BLOCK)[None, :]
    ov = np.clip(np.minimum(cs + CMP_BLOCK, ss + SEL_BLOCK) - np.maximum(cs, ss), 0, None) / CMP_BLOCK
    ov_t = np.zeros((n_sel, LANES), np.float32)
    ov_t[:, :n_cmp] = ov.T
    key = np.arange(S)[:, None]
    lane = np.arange(LANES)[None, :]
    tags = ((lane == key // SEL_BLOCK) | (lane == n_sel + key // TILE)).astype(np.float32)
    eg = np.zeros((LANES, 3 * NSA_QW), np.float32)
    for g in range(NSA_KV_HEADS):
        for r in range(NSA_GROUP):
            for j in range(3):
                base = j * NSA_QW + r * LANES + g * NSA_HEAD_DIM
                eg[g * NSA_GROUP * 3 + r * 3 + j, base:base + NSA_HEAD_DIM] = 1.0
    return win_bias, near, cmp_bias, jnp.asarray(ov_t, BF16), jnp.asarray(tags, BF16), jnp.asarray(eg, BF16)


def _dil_bias(rel_bias, dilation, max_dist):
    dist = lambda u: TILE + u
    return _shift_table(rel_bias * LOG2_E, 2 * TILE, TILE, 1, lambda u: dist(u) * dilation,
                        lambda u: (dist(u) >= 0) & (dist(u) <= max_dist))


def _nsa_head_perm():
    perm = np.zeros(NSA_QW, np.int64)
    for r in range(NSA_GROUP):
        for g in range(NSA_KV_HEADS):
            new = r * LANES + g * NSA_HEAD_DIM
            old = (g * NSA_GROUP + r) * NSA_HEAD_DIM
            perm[new:new + NSA_HEAD_DIM] = np.arange(old, old + NSA_HEAD_DIM)
    return perm


def _layer0_mixer(h, mod, ab_w_in, rel_bias, gn_g, gn_b, pos_k, pos_v, w1k, w2k, w1v, w2v):
    B, S, _ = h.shape
    o = np.cumsum((0, RET_W, RET_W, RET_W, RET_W, NSA_QW) + (NSA_KVW,) * 6 + (3 * NSA_HEADS,))
    seg = lambda a: ab_w_in[:, o[a]:o[a + 1]]
    gate_w = jnp.pad(seg(11), ((0, 0), (0, LANES - 3 * NSA_HEADS)))
    q_nsa = seg(4)[:, _nsa_head_perm()] * LOG2_E
    w = jnp.concatenate([seg(0), seg(1), seg(2), q_nsa, seg(7), seg(8), seg(9), seg(10),
                         seg(3), seg(5), seg(6), gate_w], axis=1).astype(BF16)
    pb, pf = _pre_call(h, mod, w, ((PRE0_BF16_COLS, BF16), (PRE0_F32_COLS, F32)), 1024, "pre0")

    y_ret = _ret_call(pb, pf, gn_g, gn_b)

    win_bias, near, cmp_bias, ov_t, key_tags, eg = _nsa_tables(rel_bias, S)
    kc, vc = _compress_call(pf, pos_k, pos_v, w1k, w1v, w2k, w2v)
    o_cmp, sel_t = _cmp_attn_call(pb, kc, vc, cmp_bias, ov_t)
    o_slc = _slc_call(pb, sel_t, key_tags, near)
    y_nsa = _win_call(pb, pf, win_bias, eg, o_cmp, o_slc)
    return y_ret, y_nsa


def _layer1_mixer(h, mod, dil_w_in, rel_bias):
    B, S, _ = h.shape
    def group_weights(gi):
        w = dil_w_in[:, gi * 3 * DIL_WIDTH:(gi + 1) * 3 * DIL_WIDTH]
        return jnp.concatenate([w[:, :DIL_WIDTH] * (DIL_HEAD_DIM ** -0.5 * LOG2_E), w[:, DIL_WIDTH:]],
                               axis=1).astype(BF16)

    outs, lses = [], []
    proj, = _pre_call(h, mod, group_weights(0), ((3 * DIL_WIDTH, BF16),), 1024, "pre1_0", DIL_PATTERNS[0][1])
    for gi, (window, dilation) in enumerate(DIL_PATTERNS):
        bias = _dil_bias(rel_bias, dilation, window // dilation)
        if gi + 1 < len(DIL_PATTERNS):
            o, lse, proj = _dil_proj_call(proj, bias, dilation, h, mod, group_weights(gi + 1),
                                          DIL_PATTERNS[gi + 1][1], f"dilated_{gi}_pre1_{gi + 1}")
        else:
            o, lse = _dil_call(proj, bias, dilation, f"dilated_{gi}")
        outs.append(o)
        lses.append(lse)
    return outs, lses


def kernel(x, c, rel_bias, ada_w, ada_b, ln_g, ln_b, ab_w_in, ab_w_out, ret_gn_g, ret_gn_b, cmp_pos_k, cmp_pos_v, cmp_k_w1, cmp_k_w2, cmp_v_w1, cmp_v_w2, dil_w_in, dil_w_out, ffn_w_gate, ffn_w_up, ffn_w_down):
    B = x.shape[0]
    mod = _ada_call(c, ada_w, ada_b).reshape(DEPTH, B, 6, D_MODEL)
    h = x
    for layer in range(DEPTH):
        i = layer // 2
        if layer % 2 == 0:
            mix_a, mix_b = _layer0_mixer(h, mod[layer], ab_w_in[i], rel_bias, ret_gn_g[i], ret_gn_b[i],
                                         cmp_pos_k[i], cmp_pos_v[i], cmp_k_w1[i], cmp_k_w2[i],
                                         cmp_v_w1[i], cmp_v_w2[i])
            wo_a = ab_w_out[i, :RET_W]
            wo_b = ab_w_out[i, RET_W:][_nsa_head_perm()]
            h = _post_call(h, mix_a, mix_b, 0, 0, mod[layer], wo_a.astype(BF16), wo_b.astype(BF16),
                           ln_g[layer], ln_b[layer], ffn_w_gate[layer].astype(BF16), ffn_w_up[layer].astype(BF16),
                           ffn_w_down[layer].astype(BF16), 512, f"post{layer}")
        else:
            outs, lses = _layer1_mixer(h, mod[layer], dil_w_in[i], rel_bias)
            h = _post_mix_call(h, outs, lses, mod[layer], dil_w_out[i].astype(BF16), ln_g[layer], ln_b[layer],
                               ffn_w_gate[layer].astype(BF16), ffn_w_up[layer].astype(BF16),
                               ffn_w_down[layer].astype(BF16), 512, f"post{layer}")
    return h
```

```python
import functools
import math

import numpy as np
import jax
import jax.numpy as jnp
from jax import lax
from jax.experimental import pallas as pl
from jax.experimental.pallas import tpu as pltpu

F32 = jnp.float32
BF16 = jnp.bfloat16

D_MODEL = 1024
DEPTH = 2
DEEPNORM_ALPHA = (2 * DEPTH) ** 0.25
LN_EPS = 1e-5
NEG_INF = -1e30
LOG2_E = math.log2(math.e)

RET_HEADS = 4
RET_HEAD_DIM = 128
RET_CHUNK = 128
ROPE_BASE = 10000.0
RET_W = RET_HEADS * RET_HEAD_DIM

NSA_HEADS = 8
NSA_KV_HEADS = 2
NSA_GROUP = 4
NSA_HEAD_DIM = 64
CMP_BLOCK = 32
CMP_STRIDE = 16
CMP_HIDDEN = 256
SEL_BLOCK = 64
SEL_TOP_N = 16
SEL_FORCE_SCORE = 1e4
WIN_SIZE = 512
NSA_QW = NSA_HEADS * NSA_HEAD_DIM
NSA_KVW = NSA_KV_HEADS * NSA_HEAD_DIM

DIL_PATTERNS = ((128, 1), (512, 4), (2048, 16))
DIL_HEADS = 8
DIL_HEAD_DIM = 128
DIL_WIDTH = DIL_HEADS * DIL_HEAD_DIM

REL_BUCKETS = 32
REL_MAX_DIST = 128
D_FF = 2816

LANES = 128
TILE = 128
VMEM_LIMIT_SMALL = 32 * 1024 * 1024
VMEM_LIMIT_LARGE = 56 * 1024 * 1024

PRE0_BF16_COLS = 3 * RET_W + NSA_QW + 4 * NSA_KVW
PRE0_F32_COLS = RET_W + 2 * NSA_KVW + LANES
FFN_CHUNKS = ((0, 768), (768, 768), (1536, 768), (2304, 512))
NSA_CHAIN_SPLIT = 2
WIN_CHAIN_SPLIT = 4
DEN_LANE = (NSA_HEAD_DIM, 0)


def _dot(a, b):
    return jnp.dot(a, b, preferred_element_type=F32)


def _dot_nt(a, b):
    return lax.dot_general(a, b, (((1,), (1,)), ((), ())), preferred_element_type=F32)


def _dot_tn(a, b):
    return lax.dot_general(a, b, (((0,), (0,)), ((), ())), preferred_element_type=F32)


def _split_bf16(x):
    hi = x.astype(BF16)
    lo = (x - hi.astype(F32)).astype(BF16)
    return hi, lo


def _silu(x):
    return x * jax.nn.sigmoid(x)


def _layer_norm(x, g, b):
    mu = jnp.mean(x, axis=-1, keepdims=True)
    xc = x - mu
    var = jnp.mean(xc * xc, axis=-1, keepdims=True)
    return xc * lax.rsqrt(var + LN_EPS) * g + b


def _resident(shape):
    return pl.BlockSpec(shape, lambda *_: (0,) * len(shape), pipeline_mode=pl.Buffered(1))


def _params(sem, vmem):
    return pltpu.CompilerParams(dimension_semantics=sem, vmem_limit_bytes=vmem)


def _ada_body(c_ref, w_ref, b_ref, o_ref):
    a_hi, a_lo = _split_bf16(_silu(c_ref[...]))
    w_hi, w_lo = _split_bf16(w_ref[0])
    o_ref[0] = _dot(a_hi, w_hi) + _dot(a_lo, w_hi) + _dot(a_hi, w_lo) + b_ref[0]


def _ada_call(c, ada_w, ada_b):
    B = c.shape[0]
    n_out = ada_w.shape[-1]
    tn = n_out // 4
    return pl.pallas_call(
        _ada_body,
        grid=(DEPTH, n_out // tn),
        in_specs=[pl.BlockSpec((B, D_MODEL), lambda l, n: (0, 0)),
                  pl.BlockSpec((1, D_MODEL, tn), lambda l, n: (l, 0, n)),
                  pl.BlockSpec((1, 1, tn), lambda l, n: (l, 0, n))],
        out_specs=pl.BlockSpec((1, B, tn), lambda l, n: (l, 0, n)),
        out_shape=jax.ShapeDtypeStruct((DEPTH, B, n_out), F32),
        compiler_params=_params(("arbitrary", "arbitrary"), VMEM_LIMIT_LARGE),
        name="ada_mod",
    )(c, ada_w, ada_b.reshape(DEPTH, 1, n_out))


def _pre_body(h_ref, mod_ref, w_ref, *refs, dilation):
    if dilation == 1:
        o_refs, h = refs, h_ref[0]
    else:
        o_refs, h_scr = refs[:-1], refs[-1]
        tm = h_ref.shape[1]
        for c in range(h_scr.shape[0]):
            h_scr[c] = h_ref[0, :, c * LANES:(c + 1) * LANES]
        h = jnp.concatenate(
            [jnp.concatenate([h_scr[c, pl.ds(r, tm // dilation, stride=dilation), :]
                              for c in range(h_scr.shape[0])], axis=1) for r in range(dilation)], axis=0)
    u = (h * (1.0 + mod_ref[0, 1:2, :]) + mod_ref[0, 0:1, :]).astype(BF16)
    off = 0
    for o_ref in o_refs:
        n = o_ref.shape[-1]
        o_ref[0] = _dot(u, w_ref[:, off:off + n]).astype(o_ref.dtype).reshape(o_ref.shape[1:])
        off += n


def _pre_call(h, mod, w, out_cols_dtypes, tm, name, dilation=1):
    B, S, _ = h.shape
    n_total = w.shape[1]
    assert sum(n for n, _ in out_cols_dtypes) == n_total
    if dilation == 1:
        out_specs = [pl.BlockSpec((1, tm, n), lambda b, t: (b, t, 0)) for n, _ in out_cols_dtypes]
        out_shape = [jax.ShapeDtypeStruct((B, S, n), dt) for n, dt in out_cols_dtypes]
        scratch = []
    else:
        out_specs = [pl.BlockSpec((1, dilation, tm // dilation, n), lambda b, t: (b, 0, t, 0))
                     for n, _ in out_cols_dtypes]
        out_shape = [jax.ShapeDtypeStruct((B, dilation, S // dilation, n), dt) for n, dt in out_cols_dtypes]
        scratch = [pltpu.VMEM((D_MODEL // LANES, tm, LANES), F32)]
    outs = pl.pallas_call(
        functools.partial(_pre_body, dilation=dilation),
        grid=(B, S // tm),
        in_specs=[pl.BlockSpec((1, tm, D_MODEL), lambda b, t: (b, t, 0)),
                  pl.BlockSpec((1, 6, D_MODEL), lambda b, t: (b, 0, 0)),
                  _resident((D_MODEL, n_total))],
        out_specs=out_specs,
        out_shape=out_shape,
        scratch_shapes=scratch,
        compiler_params=_params(("arbitrary", "arbitrary"), VMEM_LIMIT_LARGE),
        name=name,
    )(h, mod, w)
    return [o.reshape(B, S, o.shape[-1]) for o in outs]


def _ret_body(q_ref, k_ref, v_ref, g_ref, cos_ref, sin_ref, dec_ref, qd_ref, kd_ref, cd_ref,
              gng_ref, gnb_ref, o_ref):
    n_chunks = q_ref.shape[1] // RET_CHUNK
    state = jnp.zeros((RET_HEAD_DIM, RET_HEAD_DIM), F32)
    for n in range(n_chunks):
        sl = pl.ds(n * RET_CHUNK, RET_CHUNK)
        q = q_ref[0, sl, :].astype(F32)
        k = k_ref[0, sl, :].astype(F32)
        v = v_ref[0, sl, :]
        c2 = cos_ref[sl, :]
        s2 = sin_ref[sl, :]
        qr = (q * c2 + pltpu.roll(q, RET_HEAD_DIM // 2, 1) * s2) * (RET_HEAD_DIM ** -0.5)
        kr = k * c2 + pltpu.roll(k, RET_HEAD_DIM // 2, 1) * s2
        scores = _dot_nt(qr.astype(BF16), kr.astype(BF16)) * dec_ref[0]
        inner = _dot(scores.astype(BF16), v)
        cross = _dot((qr * qd_ref[0]).astype(BF16), state.astype(BF16))
        kv = _dot_tn((kr * kd_ref[0]).astype(BF16), v)
        state = state * cd_ref[0] + kv
        y = inner + cross
        mu = jnp.mean(y, axis=-1, keepdims=True)
        yc = y - mu
        var = jnp.mean(yc * yc, axis=-1, keepdims=True)
        yn = yc * lax.rsqrt(var + LN_EPS)
        gate = g_ref[0, sl, :]
        o_ref[0, sl, :] = ((yn * gng_ref[...] + gnb_ref[...]) * _silu(gate)).astype(o_ref.dtype)


def _ret_tables(S):
    d = RET_HEAD_DIM
    inv = ROPE_BASE ** (-jnp.arange(0, d, 2, dtype=F32) / d)
    ang = jnp.arange(S).astype(F32)[:, None] * inv[None, :]
    cos, sin = jnp.cos(ang), jnp.sin(ang)
    cos2 = jnp.concatenate([cos, cos], axis=-1)
    sin2 = jnp.concatenate([-sin, sin], axis=-1)
    C = RET_CHUNK
    log_gamma = jnp.log1p(-jnp.exp2(-5.0 - jnp.arange(RET_HEADS, dtype=F32)))
    idx = jnp.arange(C, dtype=F32)
    diff = idx[:, None] - idx[None, :]
    dec = jnp.where(diff >= 0, jnp.exp(log_gamma[:, None, None] * jnp.maximum(diff, 0.0)), 0.0)
    kd = jnp.exp(log_gamma[:, None] * (C - 1 - idx)[None, :])
    qd = jnp.exp(log_gamma[:, None] * (idx + 1.0)[None, :])
    cd = jnp.exp(log_gamma * C)
    bc = lambda t: jnp.broadcast_to(t[:, :, None], (RET_HEADS, C, d))
    cdb = jnp.broadcast_to(cd[:, None, None], (RET_HEADS, d, d))
    return cos2, sin2, dec, bc(qd), bc(kd), cdb


def _ret_call(pb, pf, gn_g, gn_b):
    B, S, _ = pb.shape
    cos2, sin2, dec, qd, kd, cd = _ret_tables(S)
    col = lambda off: pl.BlockSpec((1, S, RET_HEAD_DIM), lambda b, h: (b, 0, off + h))
    tab = pl.BlockSpec((1, RET_CHUNK, RET_HEAD_DIM), lambda b, h: (h, 0, 0))
    vec = pl.BlockSpec((1, RET_HEAD_DIM), lambda b, h: (0, h))
    return pl.pallas_call(
        _ret_body,
        grid=(B, RET_HEADS),
        in_specs=[col(0), col(RET_HEADS), col(2 * RET_HEADS), col(0),
                  _resident((S, RET_HEAD_DIM)), _resident((S, RET_HEAD_DIM)),
                  tab, tab, tab, tab, vec, vec],
        out_specs=pl.BlockSpec((1, S, RET_HEAD_DIM), lambda b, h: (b, 0, h)),
        out_shape=jax.ShapeDtypeStruct((B, S, RET_W), BF16),
        compiler_params=_params(("arbitrary", "arbitrary"), VMEM_LIMIT_SMALL),
        name="retention",
    )(pb, pb, pb, pf, cos2, sin2, dec, qd, kd, cd, gn_g.reshape(1, RET_W), gn_b.reshape(1, RET_W))


def _compress_body(xk_ref, xv_ref, pk_ref, pv_ref, w1k_ref, w1v_ref, w2k_ref, w2v_ref, kc_ref, vc_ref):
    n_blk = kc_ref.shape[1]
    for x_ref, p_ref, w1_ref, w2_ref, o_ref in ((xk_ref, pk_ref, w1k_ref, w2k_ref, kc_ref),
                                                (xv_ref, pv_ref, w1v_ref, w2v_ref, vc_ref)):
        first = jnp.zeros((n_blk, w1_ref.shape[-1]), F32)
        second = jnp.zeros((n_blk, w1_ref.shape[-1]), F32)
        for t in range(CMP_STRIDE):
            x = x_ref[0, pl.ds(t, n_blk, stride=CMP_STRIDE), :]
            first = first + _dot((x + p_ref[t:t + 1, :]).astype(BF16), w1_ref[t])
            second = second + _dot((x + p_ref[CMP_STRIDE + t:CMP_STRIDE + t + 1, :]).astype(BF16),
                                   w1_ref[CMP_STRIDE + t])
        hid = _silu(first + pltpu.roll(second, n_blk - 1, 0))
        o_ref[0] = _dot(hid.astype(BF16), w2_ref[...]).astype(o_ref.dtype)


def _compress_call(pf, pos_k, pos_v, w1k, w1v, w2k, w2v):
    B, S, _ = pf.shape
    n_blk = S // CMP_STRIDE
    assert CMP_BLOCK == 2 * CMP_STRIDE

    def both_heads(t):
        z = jnp.zeros_like(t)
        return jnp.concatenate([jnp.concatenate([t, z], -1), jnp.concatenate([z, t], -1)], -2).astype(BF16)

    pos = lambda t: jnp.concatenate([t, t], axis=-1)
    w1 = lambda t: both_heads(t.reshape(CMP_BLOCK, NSA_HEAD_DIM, CMP_HIDDEN))
    kc_block = RET_W // NSA_KVW
    xspec = lambda c: pl.BlockSpec((1, S, NSA_KVW), lambda b: (b, 0, c))
    ospec = pl.BlockSpec((1, n_blk, NSA_KVW), lambda b: (b, 0, 0))
    return pl.pallas_call(
        _compress_body,
        grid=(B,),
        in_specs=[xspec(kc_block), xspec(kc_block + 1),
                  _resident((CMP_BLOCK, NSA_KVW)), _resident((CMP_BLOCK, NSA_KVW)),
                  _resident((CMP_BLOCK, NSA_KVW, 2 * CMP_HIDDEN)), _resident((CMP_BLOCK, NSA_KVW, 2 * CMP_HIDDEN)),
                  _resident((2 * CMP_HIDDEN, NSA_KVW)), _resident((2 * CMP_HIDDEN, NSA_KVW))],
        out_specs=[ospec, ospec],
        out_shape=[jax.ShapeDtypeStruct((B, n_blk, NSA_KVW), BF16)] * 2,
        compiler_params=_params(("arbitrary",), VMEM_LIMIT_SMALL),
        name="nsa_compress",
    )(pf, pf, pos(pos_k), pos(pos_v), w1(w1k), w1(w1v), both_heads(w2k), both_heads(w2v))


CMP_Q_ROWS = 1024


def _cmp_attn_body(q_ref, kc_ref, vc_ref, cb_ref, ov_ref, o_ref, sel_ref):
    i = pl.program_id(1)
    tq = q_ref.shape[1]
    lane = lax.broadcasted_iota(jnp.int32, (tq, LANES), 1)
    row = lax.broadcasted_iota(jnp.int32, (tq, LANES), 0)
    lo_half = lane < NSA_HEAD_DIM
    kc = kc_ref[0]
    vc = vc_ref[0]
    zero = jnp.zeros_like(kc)
    kc_lo = lax.broadcasted_iota(jnp.int32, kc.shape, 1) < NSA_HEAD_DIM
    kc_g = (jnp.where(kc_lo, kc, zero), jnp.where(kc_lo, zero, kc))
    valid = (lane * CMP_STRIDE + (CMP_BLOCK - 1) <= i * tq + row)[None]
    q4 = jnp.concatenate([q_ref[0, :, r * LANES:(r + 1) * LANES] for r in range(NSA_GROUP)], axis=0)
    q4 = q4 * (NSA_HEAD_DIM ** -0.5)
    psum, outs = [], []
    for g in range(NSA_KV_HEADS):
        s = _dot_nt(q4, kc_g[g]).reshape(NSA_GROUP, tq, LANES) + cb_ref[g * NSA_GROUP:(g + 1) * NSA_GROUP]
        s = jnp.where(valid, s, NEG_INF)
        m = jnp.max(s, axis=-1, keepdims=True)
        e = jnp.where(valid, jnp.exp2(s - m), 0.0)
        den = jnp.maximum(jnp.sum(e, axis=-1, keepdims=True), 1e-30)
        p = e / den
        psum.append(jnp.sum(p, axis=0))
        outs.append(_dot(p.reshape(NSA_GROUP * tq, LANES).astype(BF16), vc))
    for r in range(NSA_GROUP):
        rs = slice(r * tq, (r + 1) * tq)
        o_ref[0, :, r * LANES:(r + 1) * LANES] = jnp.where(lo_half, outs[0][rs], outs[1][rs])

    n_sel = sel_ref.shape[2]
    blk = lax.broadcasted_iota(jnp.int32, (n_sel, tq), 0)
    qblk = (i * tq + lax.broadcasted_iota(jnp.int32, (n_sel, tq), 1)) // SEL_BLOCK
    forced = jnp.where(blk == 0, 1.0, jnp.where(blk == qblk, 1.0, jnp.where(blk == qblk - 1, 1.0, 0.0)))
    for g in range(NSA_KV_HEADS):
        p_hi, p_lo = _split_bf16(psum[g])
        imp = _dot_nt(ov_ref[...], p_hi) + _dot_nt(ov_ref[...], p_lo)
        score = jnp.where(forced > 0.5, SEL_FORCE_SCORE, jnp.where(blk <= qblk, imp, -1.0))
        rank = jnp.zeros((n_sel, tq), F32)
        for other in range(n_sel):
            so = score[other:other + 1, :]
            tie = jnp.where(blk > other, 1.0, 0.0)
            rank = rank + jnp.where(so > score, 1.0, jnp.where(so == score, tie, 0.0))
        sel_ref[0, g] = jnp.where(rank < float(min(SEL_TOP_N, n_sel)), 1.0, 0.0)


def _cmp_attn_call(pb, kc, vc, cmp_bias, overlap_t):
    B, S, _ = pb.shape
    n_sel = S // SEL_BLOCK
    q_block = (3 * RET_W) // NSA_QW
    tq = CMP_Q_ROWS
    return pl.pallas_call(
        _cmp_attn_body,
        grid=(B, S // tq),
        in_specs=[pl.BlockSpec((1, tq, NSA_QW), lambda b, i: (b, i, q_block)),
                  pl.BlockSpec((1,) + kc.shape[1:], lambda b, i: (b, 0, 0)),
                  pl.BlockSpec((1,) + vc.shape[1:], lambda b, i: (b, 0, 0)),
                  pl.BlockSpec((NSA_HEADS, tq, LANES), lambda b, i: (0, i, 0)),
                  _resident((n_sel, LANES))],
        out_specs=[pl.BlockSpec((1, tq, NSA_QW), lambda b, i: (b, i, 0)),
                   pl.BlockSpec((1, NSA_KV_HEADS, n_sel, tq), lambda b, i: (b, 0, 0, i))],
        out_shape=[jax.ShapeDtypeStruct((B, S, NSA_QW), F32),
                   jax.ShapeDtypeStruct((B, NSA_KV_HEADS, n_sel, S), F32)],
        compiler_params=_params(("arbitrary", "arbitrary"), VMEM_LIMIT_SMALL),
        name="nsa_cmp_attn",
    )(pb, kc, vc, cmp_bias, overlap_t)


SLC_CLASS_TILES = 2


def _slc_far_tiles(cls, n_tiles):
    return max(min(SLC_CLASS_TILES * (cls + 1) - 2, n_tiles - 2), 1)


def _slc_body(q_ref, ks_ref, vs_ref, sel_ref, tag_ref, near_ref, o_ref, kaug_scr, vaug_scr):
    i = pl.program_id(1)
    n_tiles = ks_ref.shape[1] // TILE
    n_sel = sel_ref.shape[2]
    lane = lax.broadcasted_iota(jnp.int32, (TILE, LANES), 1)
    lo_half = lane < NSA_HEAD_DIM

    @pl.when(i == 0)
    def _():
        for t in range(n_tiles):
            sl = pl.ds(t * TILE, TILE)
            kt = ks_ref[0, sl, :]
            zero = jnp.zeros_like(kt)
            kaug_scr[0, sl, :] = jnp.concatenate([jnp.where(lo_half, kt, zero), tag_ref[sl, :]], axis=1)
            kaug_scr[1, sl, :] = jnp.concatenate([jnp.where(lo_half, zero, kt), tag_ref[sl, :]], axis=1)
            vt = vs_ref[0, sl, :]
            vaug_scr[0, sl, :] = jnp.where(lo_half, vt, jnp.where(lane == DEN_LANE[0], 1.0, 0.0).astype(BF16))
            vaug_scr[1, sl, :] = jnp.where(lo_half, jnp.where(lane == DEN_LANE[1], 1.0, 0.0).astype(BF16), vt)

    prev = jnp.maximum(i - 1, 0)
    own_sl = pl.ds(pl.multiple_of(i * TILE, TILE), TILE)
    prev_sl = pl.ds(pl.multiple_of(prev * TILE, TILE), TILE)
    no_prev = jnp.where(lax.broadcasted_iota(jnp.int32, (1, 2 * TILE), 1) < TILE,
                        jnp.where(i == 0, NEG_INF, 0.0), 0.0)
    tile_pen = jnp.where(lane - n_sel < i - 1, 0.0, NEG_INF)
    sel_pad = jnp.zeros((LANES - n_sel, TILE), F32)

    def tile_body(n_far):
        wf = n_far * TILE
        q4 = jnp.concatenate([q_ref[0, :, r * LANES:(r + 1) * LANES] for r in range(NSA_GROUP)], axis=0)
        q4 = q4 * (NSA_HEAD_DIM ** -0.5)
        q_far, q_near, k_near, v_near = [], [], [], []
        for g in range(NSA_KV_HEADS):
            v_near.append(jnp.concatenate([vaug_scr[g, prev_sl, :], vaug_scr[g, own_sl, :]], axis=0))
            sel_q = jnp.concatenate([sel_ref[0, g], sel_pad], axis=0).T
            blk_pen = (sel_q - 1.0) * (-NEG_INF)
            pen_near = jnp.where(lane < n_sel, blk_pen, 0.0).astype(BF16)
            pen_far = jnp.where(lane < n_sel, blk_pen,
                                jnp.where(lane < n_sel + n_tiles, tile_pen, 0.0)).astype(BF16)
            q_far.append(jnp.concatenate([q4, jnp.concatenate([pen_far] * NSA_GROUP, axis=0)], axis=1))
            q_near.append(jnp.concatenate([q4, jnp.concatenate([pen_near] * NSA_GROUP, axis=0)], axis=1))
            k_near.append(jnp.concatenate([kaug_scr[g, prev_sl, :], kaug_scr[g, own_sl, :]], axis=0))

        n_split = NSA_CHAIN_SPLIT
        rows = NSA_GROUP * TILE // n_split
        chains = [(g, part) for g in range(NSA_KV_HEADS) for part in range(n_split)]

        def scores(g, part):
            rs = slice(part * rows, (part + 1) * rows)
            return (_dot_nt(q_far[g][rs], kaug_scr[g, :wf, :]),
                    _dot_nt(q_near[g][rs], k_near[g]) + (near_ref[g, rs, :] + no_prev))

        def weights(s_far, s_near):
            m = jnp.maximum(jnp.max(s_far, axis=-1, keepdims=True), jnp.max(s_near, axis=-1, keepdims=True))
            return jnp.exp2(s_far - m).astype(BF16), jnp.exp2(s_near - m).astype(BF16)

        def values(g, p_far, p_near):
            acc = _dot(p_far, vaug_scr[g, :wf, :]) + _dot(p_near, v_near[g])
            return acc / acc[:, DEN_LANE[g]:DEN_LANE[g] + 1]

        s, p, o = {}, {}, {}
        for step in range(len(chains) + 2):
            if step < len(chains):
                s[step] = scores(*chains[step])
            if 0 <= step - 2 < len(chains):
                o[step - 2] = values(chains[step - 2][0], *p.pop(step - 2))
            if 0 <= step - 1 < len(chains):
                p[step - 1] = weights(*s.pop(step - 1))
        for r in range(NSA_GROUP):
            part, rs = r // (NSA_GROUP // n_split), slice((r % (NSA_GROUP // n_split)) * TILE,
                                                          (r % (NSA_GROUP // n_split) + 1) * TILE)
            o_ref[0, :, r * LANES:(r + 1) * LANES] = jnp.where(lo_half, o[part][rs], o[n_split + part][rs])

    n_classes = -(-n_tiles // SLC_CLASS_TILES)
    for cls in range(n_classes):
        pl.when(i // SLC_CLASS_TILES == cls)(functools.partial(tile_body, _slc_far_tiles(cls, n_tiles)))


def _slc_call(pb, sel_t, key_tags, near):
    B, S, _ = pb.shape
    n_sel = S // SEL_BLOCK
    n_tiles = S // TILE
    assert n_sel + n_tiles <= LANES
    q_block = (3 * RET_W) // NSA_QW
    ks_block = (3 * RET_W + NSA_QW) // NSA_KVW
    return pl.pallas_call(
        _slc_body,
        grid=(B, n_tiles),
        in_specs=[pl.BlockSpec((1, TILE, NSA_QW), lambda b, i: (b, i, q_block)),
                  pl.BlockSpec((1, S, NSA_KVW), lambda b, i: (b, 0, ks_block)),
                  pl.BlockSpec((1, S, NSA_KVW), lambda b, i: (b, 0, ks_block + 1)),
                  pl.BlockSpec((1, NSA_KV_HEADS, n_sel, TILE), lambda b, i: (b, 0, 0, i)),
                  _resident(key_tags.shape), _resident(near.shape)],
        out_specs=pl.BlockSpec((1, TILE, NSA_QW), lambda b, i: (b, i, 0)),
        out_shape=jax.ShapeDtypeStruct((B, S, NSA_QW), F32),
        scratch_shapes=[pltpu.VMEM((NSA_KV_HEADS, S, NSA_KVW + LANES), BF16),
                        pltpu.VMEM((NSA_KV_HEADS, S, NSA_KVW), BF16)],
        compiler_params=_params(("arbitrary", "arbitrary"), VMEM_LIMIT_LARGE),
        name="nsa_selected",
    )(pb, pb, pb, sel_t, key_tags, near)


WIN_PREV_TILES = (WIN_SIZE - 1 + TILE - 1) // TILE
WIN_Q_TILES = 2


def _win_body(q_ref, k_ref, v_ref, wb_ref, gate_ref, eg_ref, ocmp_ref, oslc_ref, y_ref):
    i = pl.program_id(1)
    tq = q_ref.shape[1]
    n_span = WIN_PREV_TILES + WIN_Q_TILES
    first = i * WIN_Q_TILES - WIN_PREV_TILES
    lo_half = lax.broadcasted_iota(jnp.int32, (tq, LANES), 1) < NSA_HEAD_DIM

    def attend(early):
        if early:
            k_tiles, v_tiles, negs = [], [], []
            for t in range(n_span):
                sl = pl.ds(pl.multiple_of(jnp.maximum(first + t, 0) * TILE, TILE), TILE)
                k_tiles.append(k_ref[0, sl, :])
                v_tiles.append(v_ref[0, sl, :])
                negs.append(jnp.full((tq, TILE), jnp.where(first + t < 0, NEG_INF, 0.0), F32))
            k_all = jnp.concatenate(k_tiles, axis=0)
            v_all = jnp.concatenate(v_tiles, axis=0)
            missing = jnp.concatenate(negs, axis=1)[None]
        else:
            sl = pl.ds(pl.multiple_of(first * TILE, TILE), n_span * TILE)
            k_all = k_ref[0, sl, :]
            v_all = v_ref[0, sl, :]
        zero = jnp.zeros_like(k_all)
        k_lo = lax.broadcasted_iota(jnp.int32, k_all.shape, 1) < NSA_HEAD_DIM
        k_g = (jnp.where(k_lo, k_all, zero), jnp.where(k_lo, zero, k_all))
        v_lane = lax.broadcasted_iota(jnp.int32, v_all.shape, 1)
        v_g = (jnp.where(k_lo, v_all, jnp.where(v_lane == DEN_LANE[0], 1.0, 0.0).astype(BF16)),
               jnp.where(k_lo, jnp.where(v_lane == DEN_LANE[1], 1.0, 0.0).astype(BF16), v_all))

        g_hi, g_lo = _split_bf16(jax.nn.sigmoid(gate_ref[0]))
        gates = _dot(g_hi, eg_ref[...]) + _dot(g_lo, eg_ref[...])
        q4 = jnp.concatenate([q_ref[0, :, r * LANES:(r + 1) * LANES] for r in range(NSA_GROUP)], axis=0)
        q4 = q4 * (NSA_HEAD_DIM ** -0.5)
        n_split = WIN_CHAIN_SPLIT
        heads = NSA_GROUP // n_split
        chains = [(g, part) for g in range(NSA_KV_HEADS) for part in range(n_split)]

        def scores(g, part):
            s = _dot_nt(q4[part * heads * tq:(part + 1) * heads * tq], k_g[g]).reshape(heads, tq, n_span * TILE)
            s = s + wb_ref[g * NSA_GROUP + part * heads:g * NSA_GROUP + (part + 1) * heads]
            return s + missing if early else s

        def weights(s):
            m = jnp.max(s, axis=-1, keepdims=True)
            return jnp.exp2(s - m).astype(BF16).reshape(heads * tq, n_span * TILE)

        def values(g, p):
            acc = _dot(p, v_g[g])
            return acc / acc[:, DEN_LANE[g]:DEN_LANE[g] + 1]

        s, p, o = {}, {}, {}
        for step in range(len(chains) + 2):
            if step < len(chains):
                s[step] = scores(*chains[step])
            if 0 <= step - 2 < len(chains):
                o[step - 2] = values(chains[step - 2][0], p.pop(step - 2))
            if 0 <= step - 1 < len(chains):
                p[step - 1] = weights(s.pop(step - 1))
        for r in range(NSA_GROUP):
            cols = slice(r * LANES, (r + 1) * LANES)
            part, rs = r // heads, slice((r % heads) * tq, (r % heads + 1) * tq)
            o_win = jnp.where(lo_half, o[part][rs], o[n_split + part][rs])
            y = (gates[:, r * LANES:(r + 1) * LANES] * ocmp_ref[0, :, cols]
                 + gates[:, NSA_QW + r * LANES:NSA_QW + (r + 1) * LANES] * oslc_ref[0, :, cols]
                 + gates[:, 2 * NSA_QW + r * LANES:2 * NSA_QW + (r + 1) * LANES] * o_win)
            y_ref[0, :, cols] = y.astype(y_ref.dtype)

    pl.when(first >= 0)(functools.partial(attend, False))
    pl.when(first < 0)(functools.partial(attend, True))


def _win_call(pb, pf, win_bias, gate_expand, o_cmp, o_slc):
    B, S, _ = pb.shape
    q_block = (3 * RET_W) // NSA_QW
    kw_block = (3 * RET_W + NSA_QW) // NSA_KVW + 2
    gate_block = (RET_W + 2 * NSA_KVW) // LANES
    tq = WIN_Q_TILES * TILE
    tile_spec = pl.BlockSpec((1, tq, NSA_QW), lambda b, i: (b, i, 0))
    return pl.pallas_call(
        _win_body,
        grid=(B, S // tq),
        in_specs=[pl.BlockSpec((1, tq, NSA_QW), lambda b, i: (b, i, q_block)),
                  pl.BlockSpec((1, S, NSA_KVW), lambda b, i: (b, 0, kw_block)),
                  pl.BlockSpec((1, S, NSA_KVW), lambda b, i: (b, 0, kw_block + 1)),
                  _resident(win_bias.shape),
                  pl.BlockSpec((1, tq, LANES), lambda b, i: (b, i, gate_block)),
                  _resident((LANES, 3 * NSA_QW)),
                  tile_spec, tile_spec],
        out_specs=tile_spec,
        out_shape=jax.ShapeDtypeStruct((B, S, NSA_QW), BF16),
        compiler_params=_params(("arbitrary", "arbitrary"), VMEM_LIMIT_LARGE),
        name="nsa_window_combine",
    )(pb, pb, pb, win_bias, pf, gate_expand, o_cmp, o_slc)


def _post_body(h_ref, ma_ref, mb_ref, mod_ref, woa_ref, wob_ref, lng_ref, lnb_ref,
               wg_ref, wu_ref, wd_ref, o_ref):
    y = _dot(ma_ref[0], woa_ref[...]) + _dot(mb_ref[0], wob_ref[...])
    h1 = _layer_norm(DEEPNORM_ALPHA * h_ref[0] + mod_ref[0, 2:3, :] * y, lng_ref[0:1, :], lnb_ref[0:1, :])
    u = (h1 * (1.0 + mod_ref[0, 4:5, :]) + mod_ref[0, 3:4, :]).astype(BF16)
    acc = jnp.zeros(h1.shape, F32)
    for c0, cn in FFN_CHUNKS:
        gate = _dot(u, wg_ref[:, c0:c0 + cn])
        up = _dot(u, wu_ref[:, c0:c0 + cn])
        acc = acc + _dot((_silu(gate) * up).astype(BF16), wd_ref[c0:c0 + cn, :])
    o_ref[0] = _layer_norm(DEEPNORM_ALPHA * h1 + mod_ref[0, 5:6, :] * acc, lng_ref[1:2, :], lnb_ref[1:2, :])


def _post_call(h, mix_a, mix_b, col_a, col_b, mod, wo_a, wo_b, ln_g, ln_b, w_gate, w_up, w_down, tm, name):
    B, S, _ = h.shape
    half = D_MODEL // 2
    tok = pl.BlockSpec((1, tm, D_MODEL), lambda b, t: (b, t, 0))
    return pl.pallas_call(
        _post_body,
        grid=(B, S // tm),
        in_specs=[tok,
                  pl.BlockSpec((1, tm, half), lambda b, t: (b, t, col_a)),
                  pl.BlockSpec((1, tm, half), lambda b, t: (b, t, col_b)),
                  pl.BlockSpec((1, 6, D_MODEL), lambda b, t: (b, 0, 0)),
                  _resident((half, D_MODEL)), _resident((half, D_MODEL)),
                  _resident((2, D_MODEL)), _resident((2, D_MODEL)),
                  _resident((D_MODEL, D_FF)), _resident((D_MODEL, D_FF)), _resident((D_FF, D_MODEL))],
        out_specs=tok,
        out_shape=jax.ShapeDtypeStruct((B, S, D_MODEL), F32),
        compiler_params=_params(("arbitrary", "arbitrary"), VMEM_LIMIT_LARGE),
        name=name,
    )(h, mix_a, mix_b, mod, wo_a, wo_b, ln_g, ln_b, w_gate, w_up, w_down)


DIL_STEP_TILES = 4
DIL_WEIGHT_PARTS = 2


def _dil_body(q_ref, k_ref, kp_ref, v_ref, vp_ref, b_ref, o_ref, lse_ref, *, tiles_per_seg, between=None):
    t = pl.program_id(1)
    res_tiles = min(tiles_per_seg, DIL_STEP_TILES)
    with_prev = tiles_per_seg > 1
    if tiles_per_seg > DIL_STEP_TILES:
        first_prev = jnp.where((t * DIL_STEP_TILES) % tiles_per_seg == 0, NEG_INF, 0.0)
    else:
        first_prev = NEG_INF

    def span(ref, before_ref, j, cols):
        if not with_prev:
            return ref[0, j * TILE:(j + 1) * TILE, cols]
        if j > 0:
            return ref[0, (j - 1) * TILE:(j + 1) * TILE, cols]
        return jnp.concatenate([before_ref[0, :, cols], ref[0, :TILE, cols]], axis=0)

    def tile_bias(h, j):
        if not with_prev:
            return b_ref[h, TILE:, :]
        if j % res_tiles != 0:
            return b_ref[h]
        gone = first_prev if j == 0 else NEG_INF
        return jnp.concatenate([b_ref[h, :TILE, :] + gone, b_ref[h, TILE:, :]], axis=0)

    head_cols = lambda h: slice(h * DIL_HEAD_DIM, (h + 1) * DIL_HEAD_DIM)
    pad = jnp.zeros((LANES - DIL_HEADS, TILE), F32)

    def scores(j):
        return jnp.concatenate(
            [_dot_nt(span(k_ref, kp_ref, j, head_cols(h)), q_ref[0, j * TILE:(j + 1) * TILE, head_cols(h)])
             + tile_bias(h, j) for h in range(DIL_HEADS)], axis=1)

    def weights(s):
        m = jnp.max(s, axis=0, keepdims=True)
        e = jnp.exp2(s - m)
        den = jnp.sum(e, axis=0, keepdims=True)
        lse = (m + jnp.log2(den)) * math.log(2.0)
        return (e * (1.0 / den)).astype(BF16), lse

    def values(j, p):
        for h in range(DIL_HEADS):
            o = _dot_tn(p[:, h * TILE:(h + 1) * TILE], span(v_ref, vp_ref, j, head_cols(h)))
            o_ref[0, j * TILE:(j + 1) * TILE, head_cols(h)] = o.astype(o_ref.dtype)

    s, p = {}, {}
    for step in range(DIL_STEP_TILES + 2):
        if step < DIL_STEP_TILES:
            s[step] = scores(step)
        if 0 <= step - 2 < DIL_STEP_TILES:
            values(step - 2, p.pop(step - 2))
        j = step - 1
        parts = []
        for part in range(DIL_WEIGHT_PARTS):
            if between is not None:
                between[step * DIL_WEIGHT_PARTS + part]()
            if 0 <= j < DIL_STEP_TILES:
                slab = DIL_HEADS * TILE // DIL_WEIGHT_PARTS
                parts.append(weights(s[j][:, part * slab:(part + 1) * slab]))
        if parts:
            del s[j]
            p[j] = jnp.concatenate([pp for pp, _ in parts], axis=1)
            lse = jnp.concatenate([ll for _, ll in parts], axis=1)
            by_head = [lse[:, h * TILE:(h + 1) * TILE] for h in range(DIL_HEADS)]
            lse_ref[0, j // res_tiles, (j % res_tiles) * TILE:(j % res_tiles + 1) * TILE, :] = (
                jnp.concatenate(by_head + [pad], axis=0).T)


def _dil_proj_body(q_ref, k_ref, kp_ref, v_ref, vp_ref, b_ref, h_ref, mod_ref, w_ref,
                   o_ref, lse_ref, proj_ref, h_scr, *, tiles_per_seg, dilation):
    n_pieces = (DIL_STEP_TILES + 2) * DIL_WEIGHT_PARTS
    width = proj_ref.shape[-1] // n_pieces
    tm = h_ref.shape[1]
    state = {}

    def piece(c):
        if c == 0:
            for lc in range(h_scr.shape[0]):
                h_scr[lc] = h_ref[0, :, lc * LANES:(lc + 1) * LANES]
            h = jnp.concatenate(
                [jnp.concatenate([h_scr[lc, pl.ds(r, tm // dilation, stride=dilation), :]
                                  for lc in range(h_scr.shape[0])], axis=1) for r in range(dilation)], axis=0)
            state["u"] = (h * (1.0 + mod_ref[0, 1:2, :]) + mod_ref[0, 0:1, :]).astype(BF16)
        cols = slice(c * width, (c + 1) * width)
        res = _dot(state["u"], w_ref[:, cols]).astype(proj_ref.dtype)
        proj_ref[0, :, :, cols] = res.reshape(dilation, tm // dilation, width)

    _dil_body(q_ref, k_ref, kp_ref, v_ref, vp_ref, b_ref, o_ref, lse_ref, tiles_per_seg=tiles_per_seg,
              between=[functools.partial(piece, c) for c in range(n_pieces)])


def _lse_out(B, S, dilation):
    step = DIL_STEP_TILES * TILE
    seg = S // dilation
    n_res = max(1, step // seg)
    rows = step // n_res
    spec = pl.BlockSpec((1, n_res, rows, LANES),
                        lambda b, i: (b, (i * step // seg) // n_res, (i * step % seg) // rows, 0))
    return spec, jax.ShapeDtypeStruct((B, dilation, seg, LANES), F32)


def _dil_proj_call(proj, bias, dilation, h, mod, w_next, dilation_next, name):
    B, S, _ = proj.shape
    step = DIL_STEP_TILES * TILE
    n_next = w_next.shape[1]
    own = lambda c: pl.BlockSpec((1, step, DIL_WIDTH), lambda b, i: (b, i, c))
    before = lambda c: pl.BlockSpec((1, TILE, DIL_WIDTH),
                                    lambda b, i: (b, jnp.maximum(i * DIL_STEP_TILES - 1, 0), c))
    tile = lambda width: pl.BlockSpec((1, step, width), lambda b, i: (b, i, 0))
    lse_spec, lse_shape = _lse_out(B, S, dilation)
    o, lse, proj_next = pl.pallas_call(
        functools.partial(_dil_proj_body, tiles_per_seg=(S // dilation) // TILE, dilation=dilation_next),
        grid=(B, S // step),
        in_specs=[own(0), own(1), before(1), own(2), before(2), _resident(bias.shape),
                  tile(D_MODEL), pl.BlockSpec((1, 6, D_MODEL), lambda b, i: (b, 0, 0)),
                  _resident((D_MODEL, n_next))],
        out_specs=[tile(DIL_WIDTH), lse_spec,
                   pl.BlockSpec((1, dilation_next, step // dilation_next, n_next), lambda b, i: (b, 0, i, 0))],
        out_shape=[jax.ShapeDtypeStruct((B, S, DIL_WIDTH), BF16), lse_shape,
                   jax.ShapeDtypeStruct((B, dilation_next, S // dilation_next, n_next), BF16)],
        scratch_shapes=[pltpu.VMEM((D_MODEL // LANES, step, LANES), F32)],
        compiler_params=_params(("arbitrary", "arbitrary"), VMEM_LIMIT_LARGE),
        name=name,
    )(proj, proj, proj, proj, proj, bias, h, mod, w_next)
    return o, lse, proj_next.reshape(B, S, n_next)


def _dil_call(proj, bias, dilation, name):
    B, S, _ = proj.shape
    step = DIL_STEP_TILES * TILE
    own = lambda c: pl.BlockSpec((1, step, DIL_WIDTH), lambda b, i: (b, i, c))
    before = lambda c: pl.BlockSpec((1, TILE, DIL_WIDTH),
                                    lambda b, i: (b, jnp.maximum(i * DIL_STEP_TILES - 1, 0), c))
    tile = lambda width: pl.BlockSpec((1, step, width), lambda b, i: (b, i, 0))
    return pl.pallas_call(
        functools.partial(_dil_body, tiles_per_seg=(S // dilation) // TILE),
        grid=(B, S // step),
        in_specs=[own(0), own(1), before(1), own(2), before(2), _resident(bias.shape)],
        out_specs=[tile(DIL_WIDTH), _lse_out(B, S, dilation)[0]],
        out_shape=[jax.ShapeDtypeStruct((B, S, DIL_WIDTH), BF16), _lse_out(B, S, dilation)[1]],
        compiler_params=_params(("arbitrary", "arbitrary"), VMEM_LIMIT_LARGE),
        name=name,
    )(proj, proj, proj, proj, proj, bias)


def _post_mix_body(h_ref, o0_ref, o1_ref, o2_ref, l0_ref, l1_ref, l2_ref, mod_ref, wo_ref, lng_ref, lnb_ref,
                   wg_ref, wu_ref, wd_ref, out_ref, mix_a, mix_b, o_scr, l_scr):
    n = pl.program_id(0)
    o_refs = (o0_ref, o1_ref, o2_ref)
    l_refs = (l0_ref, l1_ref, l2_ref)
    tm = out_ref.shape[1]

    @pl.when(n == 0)
    def _():
        mix_a[...] = jnp.zeros_like(mix_a)
        mix_b[...] = jnp.zeros_like(mix_b)

    def unpermute():
        for gi, (o_ref, l_ref) in enumerate(zip(o_refs, l_refs)):
            dilation = o_ref.shape[1]
            for r in range(dilation):
                rows = pl.ds(r, tm // dilation, stride=dilation)
                l_scr[gi, rows, :] = l_ref[0, r]
                for h in range(DIL_HEADS):
                    o_scr[gi, h, rows, :] = o_ref[0, r, :, h * DIL_HEAD_DIM:(h + 1) * DIL_HEAD_DIM].astype(F32)

    def mix_head(h, mix_w):
        lses = [jnp.broadcast_to(l_scr[gi, :, h:h + 1], (tm, DIL_HEAD_DIM)) for gi in range(len(o_refs))]
        m = jnp.maximum(jnp.maximum(lses[0], lses[1]), lses[2])
        ws = [jnp.exp(l - m) for l in lses]
        den = ws[0] + ws[1] + ws[2]
        y = sum((w / den) * o_scr[gi, h] for gi, w in enumerate(ws))
        mix_w[:, h * DIL_HEAD_DIM:(h + 1) * DIL_HEAD_DIM] = y.astype(mix_w.dtype)

    def step(mix_w, mix_r):
        y = _dot(mix_r[...], wo_ref[...])
        unpermute()
        h1 = _layer_norm(DEEPNORM_ALPHA * h_ref[0] + mod_ref[0, 2:3, :] * y, lng_ref[0:1, :], lnb_ref[0:1, :])
        u = (h1 * (1.0 + mod_ref[0, 4:5, :]) + mod_ref[0, 3:4, :]).astype(BF16)
        acc = jnp.zeros(h1.shape, F32)
        for ci, (c0, cn) in enumerate(FFN_CHUNKS):
            gate = _dot(u, wg_ref[:, c0:c0 + cn])
            up = _dot(u, wu_ref[:, c0:c0 + cn])
            for h in range(DIL_HEADS):
                if h * len(FFN_CHUNKS) // DIL_HEADS == ci:
                    mix_head(h, mix_w)
            acc = acc + _dot((_silu(gate) * up).astype(BF16), wd_ref[c0:c0 + cn, :])
        out_ref[0] = _layer_norm(DEEPNORM_ALPHA * h1 + mod_ref[0, 5:6, :] * acc, lng_ref[1:2, :], lnb_ref[1:2, :])

    pl.when(n % 2 == 0)(functools.partial(step, mix_a, mix_b))
    pl.when(n % 2 == 1)(functools.partial(step, mix_b, mix_a))


def _post_mix_call(h, outs, lses, mod, wo, ln_g, ln_b, w_gate, w_up, w_down, tm, name):
    B, S, _ = h.shape
    steps = S // tm
    last = B * steps - 1
    dilations = [d for _, d in DIL_PATTERNS]
    by_residue = lambda t, d: t.reshape(B, d, S // d, t.shape[-1])
    cur = lambda n: jnp.minimum(n, last)
    done = lambda n: jnp.maximum(n - 1, 0)
    src = lambda d, width: pl.BlockSpec((1, d, tm // d, width), lambda n: (cur(n) // steps, 0, cur(n) % steps, 0))
    tok = pl.BlockSpec((1, tm, D_MODEL), lambda n: (done(n) // steps, done(n) % steps, 0))
    return pl.pallas_call(
        _post_mix_body,
        grid=(B * steps + 1,),
        in_specs=[tok] + [src(d, DIL_WIDTH) for d in dilations] + [src(d, LANES) for d in dilations]
                 + [pl.BlockSpec((1, 6, D_MODEL), lambda n: (done(n) // steps, 0, 0)),
                    _resident((D_MODEL, D_MODEL)), _resident((2, D_MODEL)), _resident((2, D_MODEL)),
                    _resident((D_MODEL, D_FF)), _resident((D_MODEL, D_FF)), _resident((D_FF, D_MODEL))],
        out_specs=tok,
        out_shape=jax.ShapeDtypeStruct((B, S, D_MODEL), F32),
        scratch_shapes=[pltpu.VMEM((tm, D_MODEL), BF16), pltpu.VMEM((tm, D_MODEL), BF16),
                        pltpu.VMEM((len(dilations), DIL_HEADS, tm, DIL_HEAD_DIM), F32),
                        pltpu.VMEM((len(dilations), tm, LANES), F32)],
        compiler_params=_params(("arbitrary",), VMEM_LIMIT_LARGE),
        name=name,
    )(h, *[by_residue(o, d) for o, d in zip(outs, dilations)], *lses,
      mod, wo, ln_g, ln_b, w_gate, w_up, w_down)


def _t5_bucket_np(dist):
    n = np.maximum(dist, 0)
    max_exact = REL_BUCKETS // 2
    nf = np.maximum(n, 1).astype(np.float64)
    val = np.log(nf / max_exact) / math.log(REL_MAX_DIST / max_exact) * (REL_BUCKETS - max_exact)
    frac = np.abs(val - np.round(val))
    on_edge = (frac < 1e-9) & (n > max_exact) & (n < REL_MAX_DIST)
    assert not on_edge.any()
    large = np.minimum(max_exact + np.floor(val + 1e-9).astype(np.int64), REL_BUCKETS - 1)
    return np.where(n < max_exact, n, large).astype(np.int32)


def _shift_table(rel_bias, rows, cols, step, dist_fn, valid_fn):
    u = np.concatenate([np.arange(cols), np.arange(-(rows - 1) * step, 0)])
    period = u.size
    vals = jnp.take(rel_bias, jnp.asarray(_t5_bucket_np(dist_fn(u))), axis=0).T
    vals = jnp.where(jnp.asarray(valid_fn(u))[None], vals, NEG_INF)
    t = jnp.tile(vals, (1, rows))[:, :rows * (period - step)].reshape(vals.shape[0], rows, period - step)
    return t[:, :, :cols]


def _nsa_tables(rel_bias, S):
    rel_bias = rel_bias * LOG2_E
    always = lambda u: np.ones(u.shape, bool)
    win_dist = lambda u: WIN_PREV_TILES * TILE - u
    win_bias = _shift_table(rel_bias, WIN_Q_TILES * TILE, (WIN_PREV_TILES + WIN_Q_TILES) * TILE, 1, win_dist,
                            lambda u: (win_dist(u) >= 0) & (win_dist(u) <= WIN_SIZE - 1))
    d0 = _shift_table(rel_bias, TILE, TILE, 1, lambda u: -u, lambda u: u <= 0)
    d1 = _shift_table(rel_bias, TILE, TILE, 1, lambda u: TILE - u, always)
    far_bucket = _t5_bucket_np(np.arange(TILE + 1, S + TILE))
    assert (far_bucket == far_bucket[0]).all()
    far = rel_bias[int(far_bucket[0])][:, None, None]
    near = jnp.concatenate([d1 - far, d0 - far], axis=2)
    near = near.reshape(NSA_KV_HEADS, NSA_GROUP * TILE, 2 * TILE)
    cmp_bias = _shift_table(rel_bias, LANES, S, CMP_STRIDE, lambda u: u - (CMP_BLOCK - 1), always)
    cmp_bias = cmp_bias.transpose(0, 2, 1)
    n_cmp = (S - CMP_BLOCK) // CMP_STRIDE + 1
    n_sel = S // SEL_BLOCK
    cs = (np.arange(n_cmp) * CMP_STRIDE)[:, None]
    ss = (np.arange(n_sel) * SEL_BLOCK)[None, :]
    ov = np.clip(np.minimum(cs + CMP_BLOCK, ss + SEL_BLOCK) - np.maximum(cs, ss), 0, None) / CMP_BLOCK
    ov_t = np.zeros((n_sel, LANES), np.float32)
    ov_t[:, :n_cmp] = ov.T
    key = np.arange(S)[:, None]
    lane = np.arange(LANES)[None, :]
    tags = ((lane == key // SEL_BLOCK) | (lane == n_sel + key // TILE)).astype(np.float32)
    eg = np.zeros((LANES, 3 * NSA_QW), np.float32)
    for g in range(NSA_KV_HEADS):
        for r in range(NSA_GROUP):
            for j in range(3):
                base = j * NSA_QW + r * LANES + g * NSA_HEAD_DIM
                eg[g * NSA_GROUP * 3 + r * 3 + j, base:base + NSA_HEAD_DIM] = 1.0
    return win_bias, near, cmp_bias, jnp.asarray(ov_t, BF16), jnp.asarray(tags, BF16), jnp.asarray(eg, BF16)


def _dil_bias(rel_bias, dilation, max_dist):
    dist = lambda u: TILE + u
    return _shift_table(rel_bias * LOG2_E, 2 * TILE, TILE, 1, lambda u: dist(u) * dilation,
                        lambda u: (dist(u) >= 0) & (dist(u) <= max_dist))


def _nsa_head_perm():
    perm = np.zeros(NSA_QW, np.int64)
    for r in range(NSA_GROUP):
        for g in range(NSA_KV_HEADS):
            new = r * LANES + g * NSA_HEAD_DIM
            old = (g * NSA_GROUP + r) * NSA_HEAD_DIM
            perm[new:new + NSA_HEAD_DIM] = np.arange(old, old + NSA_HEAD_DIM)
    return perm


def _layer0_mixer(h, mod, ab_w_in, rel_bias, gn_g, gn_b, pos_k, pos_v, w1k, w2k, w1v, w2v):
    B, S, _ = h.shape
    o = np.cumsum((0, RET_W, RET_W, RET_W, RET_W, NSA_QW) + (NSA_KVW,) * 6 + (3 * NSA_HEADS,))
    seg = lambda a: ab_w_in[:, o[a]:o[a + 1]]
    gate_w = jnp.pad(seg(11), ((0, 0), (0, LANES - 3 * NSA_HEADS)))
    q_nsa = seg(4)[:, _nsa_head_perm()] * LOG2_E
    w = jnp.concatenate([seg(0), seg(1), seg(2), q_nsa, seg(7), seg(8), seg(9), seg(10),
                         seg(3), seg(5), seg(6), gate_w], axis=1).astype(BF16)
    pb, pf = _pre_call(h, mod, w, ((PRE0_BF16_COLS, BF16), (PRE0_F32_COLS, F32)), 1024, "pre0")

    y_ret = _ret_call(pb, pf, gn_g, gn_b)

    win_bias, near, cmp_bias, ov_t, key_tags, eg = _nsa_tables(rel_bias, S)
    kc, vc = _compress_call(pf, pos_k, pos_v, w1k, w1v, w2k, w2v)
    o_cmp, sel_t = _cmp_attn_call(pb, kc, vc, cmp_bias, ov_t)
    o_slc = _slc_call(pb, sel_t, key_tags, near)
    y_nsa = _win_call(pb, pf, win_bias, eg, o_cmp, o_slc)
    return y_ret, y_nsa


def _layer1_mixer(h, mod, dil_w_in, rel_bias):
    B, S, _ = h.shape
    def group_weights(gi):
        w = dil_w_in[:, gi * 3 * DIL_WIDTH:(gi + 1) * 3 * DIL_WIDTH]
        return jnp.concatenate([w[:, :DIL_WIDTH] * (DIL_HEAD_DIM ** -0.5 * LOG2_E), w[:, DIL_WIDTH:]],
                               axis=1).astype(BF16)

    outs, lses = [], []
    proj, = _pre_call(h, mod, group_weights(0), ((3 * DIL_WIDTH, BF16),), 1024, "pre1_0", DIL_PATTERNS[0][1])
    for gi, (window, dilation) in enumerate(DIL_PATTERNS):
        bias = _dil_bias(rel_bias, dilation, window // dilation)
        if gi + 1 < len(DIL_PATTERNS):
            o, lse, proj = _dil_proj_call(proj, bias, dilation, h, mod, group_weights(gi + 1),
                                          DIL_PATTERNS[gi + 1][1], f"dilated_{gi}_pre1_{gi + 1}")
        else:
            o, lse = _dil_call(proj, bias, dilation, f"dilated_{gi}")
        outs.append(o)
        lses.append(lse)
    return outs, lses


def kernel(x, c, rel_bias, ada_w, ada_b, ln_g, ln_b, ab_w_in, ab_w_out, ret_gn_g, ret_gn_b, cmp_pos_k, cmp_pos_v, cmp_k_w1, cmp_k_w2, cmp_v_w1, cmp_v_w2, dil_w_in, dil_w_out, ffn_w_gate, ffn_w_up, ffn_w_down):
    B = x.shape[0]
    mod = _ada_call(c, ada_w, ada_b).reshape(DEPTH, B, 6, D_MODEL)
    h = x
    for layer in range(DEPTH):
        i = layer // 2
        if layer % 2 == 0:
            mix_a, mix_b = _layer0_mixer(h, mod[layer], ab_w_in[i], rel_bias, ret_gn_g[i], ret_gn_b[i],
                                         cmp_pos_k[i], cmp_pos_v[i], cmp_k_w1[i], cmp_k_w2[i],
                                         cmp_v_w1[i], cmp_v_w2[i])
            wo_a = ab_w_out[i, :RET_W]
            wo_b = ab_w_out[i, RET_W:][_nsa_head_perm()]
            h = _post_call(h, mix_a, mix_b, 0, 0, mod[layer], wo_a.astype(BF16), wo_b.astype(BF16),
                           ln_g[layer], ln_b[layer], ffn_w_gate[layer].astype(BF16), ffn_w_up[layer].astype(BF16),
                           ffn_w_down[layer].astype(BF16), 512, f"post{layer}")
        else:
            outs, lses = _layer1_mixer(h, mod[layer], dil_w_in[i], rel_bias)
            h = _post_mix_call(h, outs, lses, mod[layer], dil_w_out[i].astype(BF16), ln_g[layer], ln_b[layer],
                               ffn_w_gate[layer].astype(BF16), ffn_w_up[layer].astype(BF16),
                               ffn_w_down[layer].astype(BF16), 512, f"post{layer}")
    return h
```
